```python
import jax, jax.numpy as jnp
from jax import lax
import numpy as np

D_MODEL = 2048
BATCH = 4
SEQ = 2048
DEPTH = 1
DEC_BATCH = 128
DEC_SEQ = 8
PAST_LEN = 16384
PAGE_SIZE = 128

N_META = 16
D_HALF = D_MODEL // 2
DK_A = 128
DV_A = 128
H_A = D_HALF // DV_A
H_B = 4
DV_B = D_HALF // H_B
DK_B = DV_B // 2
ALPHA_RANK = 16
GATE_TEMP = 16.0
D_MIX = H_A * DV_A + H_B * DV_B
D_FF = 5632
CHUNK = 64
EPS = 1e-6
PROJ_WIDTHS = (H_A * DK_A, H_A * DK_A, H_A * DV_A, H_A * DV_A,
               H_B * DK_B, H_B * DK_B, H_B * DV_B, H_B * DV_B, ALPHA_RANK)
D_IN_PROJ = sum(PROJ_WIDTHS)
SPLIT_IDX = [int(v) for v in np.cumsum(PROJ_WIDTHS)[:-1]]

kernel_name = "hymba_hgrn2_gla_macaron_step"


def rmsnorm(x, w):
    xf = x.astype(jnp.float32)
    y = xf * lax.rsqrt(jnp.mean(xf * xf, axis=-1, keepdims=True) + EPS)
    return (y * w.astype(jnp.float32)).astype(x.dtype)


def head_rmsnorm(o, w):
    y = o * lax.rsqrt(jnp.mean(o * o, axis=-1, keepdims=True) + EPS)
    b, t = o.shape[:2]
    return y.reshape(b, t, -1) * w.astype(jnp.float32)


def swiglu(h, norm_w, w_in, w_out):
    u = rmsnorm(h, norm_w)
    g, up = jnp.split(u @ w_in, 2, axis=-1)
    return (jax.nn.silu(g) * up) @ w_out


def gated_linear_scan(q, k, v, logg, S0):
    b, t, h, dk = q.shape
    dv = v.shape[-1]
    c = min(CHUNK, t)
    n = -(-t // c)
    pad = n * c - t
    padf = lambda a: jnp.pad(a, ((0, 0), (0, pad), (0, 0), (0, 0)))
    to_chunks = lambda a: padf(a).reshape(b, n, c, h, a.shape[-1]).transpose(1, 0, 2, 3, 4)
    mask = jnp.tril(jnp.ones((c, c), dtype=bool))[None, :, :, None, None]

    def step(S, inp):
        qc, kc, vc, gc = inp
        Bc = jnp.cumsum(gc, axis=1)
        o_inter = jnp.einsum('bthk,bhkv->bthv', qc * jnp.exp(Bc), S)
        diff = Bc[:, :, None] - Bc[:, None, :]
        decay = jnp.exp(jnp.where(mask, diff, -jnp.inf))
        A = jnp.einsum('bthk,bshk,btshk->bhts', qc, kc, decay)
        o_intra = jnp.einsum('bhts,bshv->bthv', A, vc)
        Blast = Bc[:, -1]
        kdec = kc * jnp.exp(Blast[:, None] - Bc)
        S_new = jnp.exp(Blast)[..., None] * S + jnp.einsum('bshk,bshv->bhkv', kdec, vc)
        return S_new, o_inter + o_intra

    S, o = lax.scan(step, S0.astype(jnp.float32), (to_chunks(q), to_chunks(k), to_chunks(v), to_chunks(logg)))
    o = o.transpose(1, 0, 2, 3, 4).reshape(b, n * c, h, dv)[:, :t]
    return o, S


def segmented_scan(q, k, v, logg, S0, seg_lens):
    outs, S, start = [], S0, 0
    for L in seg_lens:
        sl = slice(start, start + L)
        o, S = gated_linear_scan(q[:, sl], k[:, sl], v[:, sl], logg[:, sl], S)
        outs.append(o)
        start += L
    return jnp.concatenate(outs, axis=1), S


def token_mix(u, S_a0, S_b0, seg_lens, lb, w_in, w_alpha_up, b_alpha, gn_a, gn_b, w_out):
    b, t, _ = u.shape
    f32 = jnp.float32
    q_a, f_a, i_a, g_a, q_b, k_b, v_b, r_b, a_low = jnp.split(u @ w_in, SPLIT_IDX, axis=-1)
    heads = lambda a, h: a.astype(f32).reshape(b, t, h, -1)
    lbh = lb.reshape(H_A, DK_A)
    fg = lbh + (1.0 - lbh) * jax.nn.sigmoid(heads(f_a, H_A))
    qa = heads(jax.nn.silu(q_a), H_A) * (DK_A ** -0.5)
    o_a, S_a = segmented_scan(qa, 1.0 - fg, heads(i_a, H_A), jnp.log(fg), S_a0, seg_lens)
    logg_b = heads(jax.nn.log_sigmoid((a_low @ w_alpha_up + b_alpha).astype(f32)) / GATE_TEMP, H_B)
    qb = heads(q_b, H_B) * (DK_B ** -0.5)
    o_b, S_b = segmented_scan(qb, heads(k_b, H_B), heads(v_b, H_B), logg_b, S_b0, seg_lens)
    o_a = head_rmsnorm(o_a, gn_a) * jax.nn.silu(g_a.astype(f32))
    o_b = head_rmsnorm(o_b, gn_b) * jax.nn.silu(r_b.astype(f32))
    o = jnp.concatenate([o_a, o_b], axis=-1).astype(u.dtype)
    return o @ w_out, S_a, S_b


def trunk(h, states_a, states_b, seg_lens, lb_logits,
          ffn1_norm, w_ffn1_in, w_ffn1_out, mix_norm, w_in, w_alpha_up, b_alpha, gnorm_a, gnorm_b, w_out,
          ffn2_norm, w_ffn2_in, w_ffn2_out, final_norm):
    lbs = jnp.cumsum(jax.nn.softmax(lb_logits.astype(jnp.float32), axis=0), axis=0)
    new_a, new_b = [], []
    for l in range(DEPTH):
        h = h + 0.5 * swiglu(h, ffn1_norm[l], w_ffn1_in[l], w_ffn1_out[l])
        m, Sa, Sb = token_mix(rmsnorm(h, mix_norm[l]), states_a[l], states_b[l], seg_lens, lbs[l],
                              w_in[l], w_alpha_up[l], b_alpha[l], gnorm_a[l], gnorm_b[l], w_out[l])
        h = h + m
        h = h + 0.5 * swiglu(h, ffn2_norm[l], w_ffn2_in[l], w_ffn2_out[l])
        new_a.append(Sa)
        new_b.append(Sb)
    return rmsnorm(h, final_norm), jnp.stack(new_a), jnp.stack(new_b)


def setup_inputs(seed: int = 0) -> dict:
    key = jax.random.key(seed)
    ks = jax.random.split(key, 24)
    nrm = lambda k, shape, s: jax.random.normal(k, shape, jnp.float32) * s
    gain = lambda k, shape: 1.0 + nrm(k, shape, 0.02)
    return {
        "x_prompt": nrm(ks[0], (BATCH, SEQ, D_MODEL), 1.0),
        "x_sample": nrm(ks[1], (DEC_BATCH, DEC_SEQ, D_MODEL), 1.0),
        "state_hgrn": nrm(ks[2], (DEPTH, DEC_BATCH, H_A, DK_A, DV_A), 0.5),
        "state_gla": nrm(ks[3], (DEPTH, DEC_BATCH, H_B, DK_B, DV_B), 0.5),
        "meta_tokens": nrm(ks[4], (N_META, D_MODEL), 1.0),
        "lb_logits": nrm(ks[5], (DEPTH + 1, H_A * DK_A), 0.5),
        "ffn1_norm": gain(ks[6], (DEPTH, D_MODEL)),
        "w_ffn1_in": nrm(ks[7], (DEPTH, D_MODEL, 2 * D_FF), D_MODEL ** -0.5),
        "w_ffn1_out": nrm(ks[8], (DEPTH, D_FF, D_MODEL), D_FF ** -0.5),
        "mix_norm": gain(ks[9], (DEPTH, D_MODEL)),
        "w_in": nrm(ks[10], (DEPTH, D_MODEL, D_IN_PROJ), D_MODEL ** -0.5),
        "w_alpha_up": nrm(ks[11], (DEPTH, ALPHA_RANK, H_B * DK_B), ALPHA_RANK ** -0.5),
        "b_alpha": nrm(ks[12], (DEPTH, H_B * DK_B), 0.1),
        "gnorm_a": gain(ks[13], (DEPTH, H_A * DV_A)),
        "gnorm_b": gain(ks[14], (DEPTH, H_B * DV_B)),
        "w_out": nrm(ks[15], (DEPTH, D_MIX, D_MODEL), D_MIX ** -0.5),
        "ffn2_norm": gain(ks[16], (DEPTH, D_MODEL)),
        "w_ffn2_in": nrm(ks[17], (DEPTH, D_MODEL, 2 * D_FF), D_MODEL ** -0.5),
        "w_ffn2_out": nrm(ks[18], (DEPTH, D_FF, D_MODEL), D_FF ** -0.5),
        "final_norm": gain(ks[19], (D_MODEL,)),
    }


def reference(x_prompt, x_sample, state_hgrn, state_gla, meta_tokens, lb_logits,
              ffn1_norm, w_ffn1_in, w_ffn1_out, mix_norm, w_in, w_alpha_up, b_alpha, gnorm_a, gnorm_b, w_out,
              ffn2_norm, w_ffn2_in, w_ffn2_out, final_norm):
    weights = (lb_logits, ffn1_norm, w_ffn1_in, w_ffn1_out, mix_norm, w_in, w_alpha_up, b_alpha,
               gnorm_a, gnorm_b, w_out, ffn2_norm, w_ffn2_in, w_ffn2_out, final_norm)
    b = x_prompt.shape[0]
    meta = jnp.broadcast_to(meta_tokens.astype(x_prompt.dtype)[None], (b, N_META, D_MODEL))
    h_p = jnp.concatenate([meta, x_prompt], axis=1)
    za = jnp.zeros((DEPTH, b, H_A, DK_A, DV_A), jnp.float32)
    zb = jnp.zeros((DEPTH, b, H_B, DK_B, DV_B), jnp.float32)
    out_p, new_a_p, new_b_p = trunk(h_p, za, zb, (N_META, x_prompt.shape[1]), *weights)
    y_prompt = out_p[:, N_META:]
    y_sample, new_a_s, new_b_s = trunk(x_sample, state_hgrn, state_gla, (x_sample.shape[1],), *weights)
    return (y_prompt, y_sample, new_a_p, new_b_p, new_a_s, new_b_s)
```

```python
import functools

import jax
import jax.numpy as jnp
from jax import lax
from jax.experimental import pallas as pl
from jax.experimental.pallas import tpu as pltpu

F32 = jnp.float32
BF16 = jnp.bfloat16
HIGHEST = lax.Precision.HIGHEST

D_MODEL = 2048
N_META = 16
DK = 128
H_A, DV_A = 8, 128
H_B, DV_B = 4, 256
D_HALF = H_A * DV_A
ALPHA_RANK = 16
GATE_TEMP = 16.0
D_FF = 5632
CHUNK = 64
EPS = 1e-6
D_IN_PROJ = 4 * D_HALF + 2 * H_B * DK + 2 * D_HALF + ALPHA_RANK

LANES = 128
VMEM_LIMIT_BYTES = 56 * 1024 * 1024

COL_QA, COL_FA, COL_IA, COL_GA = 0, 8, 16, 24
COL_QB, COL_KB, COL_VB, COL_RB, COL_AL = 32, 36, 40, 48, 56
NP_PAD = 7680

TF = 512
TN_PROJ = 512
DIAG_BLK = 16
SAMPLE_SEQS = 8


def _rms(x, w):
    return x * lax.rsqrt(jnp.mean(x * x, axis=-1, keepdims=True) + EPS) * w


def _silu(x):
    return x * jax.nn.sigmoid(x)


def _params(*sem):
    return pltpu.CompilerParams(dimension_semantics=sem, vmem_limit_bytes=VMEM_LIMIT_BYTES)


def _ffn_kernel(h_ref, nw_ref, wg_ref, wu_ref, wo_ref, fnw_ref, out_ref, u_ref, acc_ref, *,
                apply_final_norm):
    j = pl.program_id(1)

    @pl.when(j == 0)
    def _init():
        u_ref[...] = _rms(h_ref[...], nw_ref[...]).astype(BF16)
        acc_ref[...] = jnp.zeros_like(acc_ref)

    u = u_ref[...]
    g = jnp.dot(u, wg_ref[...], preferred_element_type=F32)
    up = jnp.dot(u, wu_ref[...], preferred_element_type=F32)
    a = (_silu(g) * up).astype(BF16)
    acc_ref[...] += jnp.dot(a, wo_ref[...], preferred_element_type=F32)

    @pl.when(j == pl.num_programs(1) - 1)
    def _finish():
        y = h_ref[...] + 0.5 * acc_ref[...]
        if apply_final_norm:
            y = _rms(y, fnw_ref[...])
        out_ref[...] = y


def _ffn(h, norm_w, w_in, w_out, final_norm_w, *, tm, apply_final_norm, name):
    m = h.shape[0]
    nf = D_FF // TF
    return pl.pallas_call(
        functools.partial(_ffn_kernel, apply_final_norm=apply_final_norm),
        grid=(m // tm, nf),
        in_specs=[
            pl.BlockSpec((tm, D_MODEL), lambda i, j: (i, 0)),
            pl.BlockSpec((1, D_MODEL), lambda i, j: (0, 0)),
            pl.BlockSpec((D_MODEL, TF), lambda i, j: (0, j)),
            pl.BlockSpec((D_MODEL, TF), lambda i, j: (0, j + nf)),
            pl.BlockSpec((TF, D_MODEL), lambda i, j: (j, 0)),
            pl.BlockSpec((1, D_MODEL), lambda i, j: (0, 0)),
        ],
        out_specs=pl.BlockSpec((tm, D_MODEL), lambda i, j: (i, 0)),
        out_shape=jax.ShapeDtypeStruct((m, D_MODEL), F32),
        scratch_shapes=[pltpu.VMEM((tm, D_MODEL), BF16), pltpu.VMEM((tm, D_MODEL), F32)],
        compiler_params=_params("parallel", "arbitrary"),
        name=name,
    )(h, norm_w, w_in, w_in, w_out, final_norm_w)


def _proj_kernel(h_ref, nw_ref, w_ref, out_ref, u_ref):
    @pl.when(pl.program_id(1) == 0)
    def _init():
        u_ref[...] = _rms(h_ref[...], nw_ref[...]).astype(BF16)

    out_ref[...] = jnp.dot(u_ref[...], w_ref[...], preferred_element_type=F32)


def _in_proj(h, norm_w, w, *, tm, name):
    m = h.shape[0]
    return pl.pallas_call(
        _proj_kernel,
        grid=(m // tm, NP_PAD // TN_PROJ),
        in_specs=[
            pl.BlockSpec((tm, D_MODEL), lambda i, j: (i, 0)),
            pl.BlockSpec((1, D_MODEL), lambda i, j: (0, 0)),
            pl.BlockSpec((D_MODEL, TN_PROJ), lambda i, j: (0, j)),
        ],
        out_specs=pl.BlockSpec((tm, TN_PROJ), lambda i, j: (i, j)),
        out_shape=jax.ShapeDtypeStruct((m, NP_PAD), F32),
        scratch_shapes=[pltpu.VMEM((tm, D_MODEL), BF16)],
        compiler_params=_params("parallel", "arbitrary"),
        name=name,
    )(h, norm_w, w)


def _out_proj_kernel(oa_ref, ob_ref, h_ref, wa_ref, wb_ref, out_ref):
    out_ref[...] = (h_ref[...]
                    + jnp.dot(oa_ref[...], wa_ref[...], preferred_element_type=F32)
                    + jnp.dot(ob_ref[...], wb_ref[...], preferred_element_type=F32))


def _out_proj(oa, ob, h, w, *, tm, name):
    m = h.shape[0]
    return pl.pallas_call(
        _out_proj_kernel,
        grid=(m // tm,),
        in_specs=[
            pl.BlockSpec((tm, D_HALF), lambda i: (i, 0)),
            pl.BlockSpec((tm, D_HALF), lambda i: (i, 0)),
            pl.BlockSpec((tm, D_MODEL), lambda i: (i, 0)),
            pl.BlockSpec((D_HALF, D_MODEL), lambda i: (0, 0)),
            pl.BlockSpec((D_HALF, D_MODEL), lambda i: (1, 0)),
        ],
        out_specs=pl.BlockSpec((tm, D_MODEL), lambda i: (i, 0)),
        out_shape=jax.ShapeDtypeStruct((m, D_MODEL), F32),
        compiler_params=_params("parallel"),
        name=name,
    )(oa, ob, h, w, w)


def _tri(c, seq_len):
    r = lax.broadcasted_iota(jnp.int32, (c, c), 0)
    s = lax.broadcasted_iota(jnp.int32, (c, c), 1)
    return jnp.where((r >= s) & ((r ^ s) < seq_len), 1.0, 0.0).astype(F32)


def _off_mask(c, blk):
    nb = c // blk
    rows = nb * (nb - 1) // 2 * blk
    r = lax.broadcasted_iota(jnp.int32, (c, rows), 0) >> (blk.bit_length() - 1)
    s = lax.broadcasted_iota(jnp.int32, (c, rows), 1)
    seg = jnp.zeros((c, rows), jnp.int32)
    for i in range(1, nb):
        seg = seg + jnp.where(s >= blk * i * (i - 1) // 2, 1, 0)
    return jnp.where(r == seg, 1.0, 0.0).astype(F32)


def _diag(q, k, v, bc, blk):
    c = q.shape[0]
    row = lax.broadcasted_iota(jnp.int32, (blk, 1), 0)
    outs = []
    for i in range(c // blk):
        sl = slice(i * blk, (i + 1) * blk)
        qi, ki, vi, bi = q[sl], k[sl], v[sl], bc[sl]
        acc = jnp.zeros((blk, v.shape[1]), F32)
        for s in range(blk):
            e = jnp.exp(jnp.minimum(bi - bi[s:s + 1, :], 0.0))
            col = jnp.sum(qi * ki[s:s + 1, :] * e, axis=-1, keepdims=True)
            col = jnp.where(row >= s, col, 0.0)
            acc = acc + col * vi[s:s + 1, :]
        outs.append(acc)
    return outs[0] if len(outs) == 1 else jnp.concatenate(outs, axis=0)


def _offdiag(q, k, v, bc, blk, mask):
    c = q.shape[0]
    nb = c // blk
    qs, ks, vs = [jnp.zeros((blk, DK), F32)], [], []
    for i in range(1, nb):
        lo = i * blk
        ref = bc[lo:lo + 1, :]
        qs.append(q[lo:lo + blk] * jnp.exp(bc[lo:lo + blk] - ref))
        ks.append(k[:lo] * jnp.exp(ref - bc[:lo]))
        vs.append(v[:lo])
    qt = jnp.concatenate(qs, axis=0).astype(BF16)
    kt = jnp.concatenate(ks, axis=0).astype(BF16)
    vt = jnp.concatenate(vs, axis=0).astype(BF16)
    a = lax.dot_general(qt, kt, (((1,), (1,)), ((), ())), preferred_element_type=F32)
    return jnp.dot((a * mask).astype(BF16), vt, preferred_element_type=F32)


def _state_update(k, v, g, bc, blast, s_prev):
    c, dv = v.shape
    kdec = (k * jnp.exp(blast - bc)).astype(BF16)
    upd = lax.dot_general(kdec, v.astype(BF16), (((0,), (0,)), ((), ())),
                          preferred_element_type=F32)
    if s_prev is None:
        return upd
    tot = lax.dot_general(g, jnp.ones((c, dv), F32), (((0,), (0,)), ((), ())),
                          precision=HIGHEST, preferred_element_type=F32)
    return jnp.exp(tot) * s_prev + upd


def _hgrn_inputs(qa, fa, ia, lb):
    fg = lb + (1.0 - lb) * jax.nn.sigmoid(fa)
    return _silu(qa) * (DK ** -0.5), 1.0 - fg, ia, jnp.log(fg)


def _gla_inputs(qb, kb, vb, al, wa, ba):
    alpha = jnp.dot(al.astype(BF16), wa, preferred_element_type=F32) + ba
    logsig = jnp.minimum(alpha, 0.0) - jnp.log1p(jnp.exp(-jnp.abs(alpha)))
    return qb * (DK ** -0.5), kb, vb, logsig / GATE_TEMP


def _head_out(o, gn, gate):
    return (_rms(o, gn) * _silu(gate)).astype(BF16)


def _lower_bound(lbl_ref):
    x = lbl_ref[...]
    e = jnp.exp(x - jnp.max(x, axis=0, keepdims=True))
    return e[0:1, :] / jnp.sum(e, axis=0, keepdims=True)


def _prompt_scan_kernel(*refs, gla, seq, meta):
    if gla:
        (q_ref, k_ref, v_ref, gate_ref, al_ref, mk_ref, mv_ref, mal_ref, wa_ref, ba_ref, gn_ref,
         o_ref, sout_ref, s_ref) = refs
    else:
        (q_ref, f_ref, v_ref, gate_ref, mf_ref, mv_ref, lbl_ref, gn_ref,
         o_ref, sout_ref, s_ref) = refs
        lb = _lower_bound(lbl_ref)

    if gla:
        _, mk, mv, mg = _gla_inputs(mk_ref[...], mk_ref[...], mv_ref[...], mal_ref[...],
                                    wa_ref[...], ba_ref[...])
    else:
        _, mk, mv, mg = _hgrn_inputs(mf_ref[...], mf_ref[...], mv_ref[...], lb)
    mbc = jnp.dot(_tri(meta, meta), mg, precision=HIGHEST, preferred_element_type=F32)
    s_ref[...] = _state_update(mk, mv, mg, mbc, mbc[meta - 1:meta, :], None)

    tri = _tri(CHUNK, CHUNK)
    mask = _off_mask(CHUNK, DIAG_BLK)

    def body(ci, carry):
        rows = pl.ds(pl.multiple_of(ci * CHUNK, CHUNK), CHUNK)
        if gla:
            q, k, v, g = _gla_inputs(q_ref[rows, :], k_ref[rows, :], v_ref[rows, :],
                                     al_ref[rows, :], wa_ref[...], ba_ref[...])
        else:
            q, k, v, g = _hgrn_inputs(q_ref[rows, :], f_ref[rows, :], v_ref[rows, :], lb)
        bc = jnp.dot(tri, g, precision=HIGHEST, preferred_element_type=F32)
        s_prev = s_ref[...]
        o = jnp.dot((q * jnp.exp(bc)).astype(BF16), s_prev.astype(BF16),
                    preferred_element_type=F32)
        o = o + _offdiag(q, k, v, bc, DIAG_BLK, mask) + _diag(q, k, v, bc, DIAG_BLK)
        s_ref[...] = _state_update(k, v, g, bc, bc[CHUNK - 1:CHUNK, :], s_prev)
        o_ref[rows, :] = _head_out(o, gn_ref[...], gate_ref[rows, :])
        return carry

    lax.fori_loop(0, seq // CHUNK, body, 0)
    sout_ref[...] = s_ref[...]


def _sample_scan_kernel(*refs, gla, steps):
    if gla:
        (q_ref, k_ref, v_ref, gate_ref, al_ref, wa_ref, ba_ref, gn_ref, s0_ref,
         o_ref, sout_ref) = refs
        q, k, v, g = _gla_inputs(q_ref[...], k_ref[...], v_ref[...], al_ref[...],
                                 wa_ref[...], ba_ref[...])
    else:
        (q_ref, f_ref, v_ref, gate_ref, lbl_ref, gn_ref, s0_ref, o_ref, sout_ref) = refs
        q, k, v, g = _hgrn_inputs(q_ref[...], f_ref[...], v_ref[...], _lower_bound(lbl_ref))

    rows = SAMPLE_SEQS * steps
    bc = jnp.dot(_tri(rows, steps), g, precision=HIGHEST, preferred_element_type=F32)
    qd = (q * jnp.exp(bc)).astype(BF16)
    inter = []
    for b in range(SAMPLE_SEQS):
        sl = slice(b * steps, (b + 1) * steps)
        s_prev = s0_ref[b]
        inter.append(jnp.dot(qd[sl], s_prev.astype(BF16), preferred_element_type=F32))
        last = (b + 1) * steps - 1
        sout_ref[b] = _state_update(k[sl], v[sl], g[sl], bc[sl], bc[last:last + 1, :], s_prev)
    o = jnp.concatenate(inter, axis=0) + _diag(q, k, v, bc, steps)
    o_ref[...] = _head_out(o, gn_ref[...], gate_ref[...])


def _col(width, block):
    return block * LANES // width


def _prompt_scan(p, pm, *, gla, batch, seq, small, name):
    heads, dv = (H_B, DV_B) if gla else (H_A, DV_A)
    tok = lambda width, c0, per_head: pl.BlockSpec(
        (seq, width), lambda b, h: (b, _col(width, c0) + h * per_head))
    met = lambda width, c0, per_head: pl.BlockSpec(
        (N_META, width), lambda b, h: (0, _col(width, c0) + h * per_head))
    vec = lambda width: pl.BlockSpec((1, width), lambda b, h: (0, h))
    if gla:
        wa, ba, gn = small
        in_specs = [tok(DK, COL_QB, 1), tok(DK, COL_KB, 1), tok(dv, COL_VB, 1), tok(dv, COL_RB, 1),
                    tok(LANES, COL_AL, 0),
                    met(DK, COL_KB, 1), met(dv, COL_VB, 1), met(LANES, COL_AL, 0),
                    pl.BlockSpec((LANES, DK), lambda b, h: (0, h)), vec(DK), vec(dv)]
        args = (p, p, p, p, p, pm, pm, pm, wa, ba, gn)
    else:
        lbl, gn = small
        in_specs = [tok(DK, COL_QA, 1), tok(DK, COL_FA, 1), tok(dv, COL_IA, 1), tok(dv, COL_GA, 1),
                    met(DK, COL_FA, 1), met(dv, COL_IA, 1),
                    pl.BlockSpec((2, DK), lambda b, h: (0, h)), vec(dv)]
        args = (p, p, p, p, pm, pm, lbl, gn)
    return pl.pallas_call(
        functools.partial(_prompt_scan_kernel, gla=gla, seq=seq, meta=N_META),
        grid=(batch, heads),
        in_specs=in_specs,
        out_specs=[pl.BlockSpec((seq, dv), lambda b, h: (b, h)),
                   pl.BlockSpec((None, None, DK, dv), lambda b, h: (b, h, 0, 0))],
        out_shape=[jax.ShapeDtypeStruct((batch * seq, heads * dv), BF16),
                   jax.ShapeDtypeStruct((batch, heads, DK, dv), F32)],
        scratch_shapes=[pltpu.VMEM((DK, dv), F32)],
        compiler_params=_params("parallel", "parallel"),
        name=name,
    )(*args)


def _sample_scan(p, s0, *, gla, batch, steps, small, name):
    heads, dv = (H_B, DV_B) if gla else (H_A, DV_A)
    rows = SAMPLE_SEQS * steps
    tok = lambda width, c0, per_head: pl.BlockSpec(
        (rows, width), lambda b, h: (b, _col(width, c0) + h * per_head))
    vec = lambda width: pl.BlockSpec((1, width), lambda b, h: (0, h))
    state = pl.BlockSpec((SAMPLE_SEQS, None, DK, dv), lambda b, h: (b, h, 0, 0))
    if gla:
        wa, ba, gn = small
        in_specs = [tok(DK, COL_QB, 1), tok(DK, COL_KB, 1), tok(dv, COL_VB, 1), tok(dv, COL_RB, 1),
                    tok(LANES, COL_AL, 0),
                    pl.BlockSpec((LANES, DK), lambda b, h: (0, h)), vec(DK), vec(dv), state]
        args = (p, p, p, p, p, wa, ba, gn, s0)
    else:
        lbl, gn = small
        in_specs = [tok(DK, COL_QA, 1), tok(DK, COL_FA, 1), tok(dv, COL_IA, 1), tok(dv, COL_GA, 1),
                    pl.BlockSpec((2, DK), lambda b, h: (0, h)), vec(dv), state]
        args = (p, p, p, p, lbl, gn, s0)
    return pl.pallas_call(
        functools.partial(_sample_scan_kernel, gla=gla, steps=steps),
        grid=(batch // SAMPLE_SEQS, heads),
        in_specs=in_specs,
        out_specs=[pl.BlockSpec((rows, dv), lambda b, h: (b, h)), state],
        out_shape=[jax.ShapeDtypeStruct((batch * steps, heads * dv), BF16),
                   jax.ShapeDtypeStruct((batch, heads, DK, dv), F32)],
        compiler_params=_params("parallel", "parallel"),
        name=name,
    )(*args)


def kernel(x_prompt, x_sample, state_hgrn, state_gla, meta_tokens, lb_logits, ffn1_norm, w_ffn1_in,
           w_ffn1_out, mix_norm, w_in, w_alpha_up, b_alpha, gnorm_a, gnorm_b, w_out, ffn2_norm,
           w_ffn2_in, w_ffn2_out, final_norm):
    batch, seq, _ = x_prompt.shape
    dec_batch, steps, _ = x_sample.shape

    w1i, w1o = w_ffn1_in[0].astype(BF16), w_ffn1_out[0].astype(BF16)
    w2i, w2o = w_ffn2_in[0].astype(BF16), w_ffn2_out[0].astype(BF16)
    wip = jnp.pad(w_in[0], ((0, 0), (0, NP_PAD - D_IN_PROJ))).astype(BF16)
    wop = w_out[0].astype(BF16)
    wa = jnp.pad(w_alpha_up[0], ((0, LANES - ALPHA_RANK), (0, 0))).astype(BF16)
    n1, nm, n2, nf = ffn1_norm, mix_norm, ffn2_norm, final_norm[None]
    small_a = (lb_logits, gnorm_a)
    small_b = (wa, b_alpha, gnorm_b)

    def front(x, tm, tag):
        h1 = _ffn(x, n1, w1i, w1o, nf, tm=tm, apply_final_norm=False, name=f"ffn1_{tag}")
        return h1, _in_proj(h1, nm, wip, tm=tm, name=f"inproj_{tag}")

    def back(oa, ob, h1, tm, tag):
        h2 = _out_proj(oa, ob, h1, wop, tm=tm, name=f"outproj_{tag}")
        return _ffn(h2, n2, w2i, w2o, nf, tm=tm, apply_final_norm=True, name=f"ffn2_{tag}")

    _, pm = front(meta_tokens, N_META, "meta")

    h1p, pp = front(x_prompt.reshape(batch * seq, D_MODEL), 512, "prompt")
    oa_p, sa_p = _prompt_scan(pp, pm, gla=False, batch=batch, seq=seq, small=small_a,
                              name="scan_hgrn_prompt")
    ob_p, sb_p = _prompt_scan(pp, pm, gla=True, batch=batch, seq=seq, small=small_b,
                              name="scan_gla_prompt")
    y_p = back(oa_p, ob_p, h1p, 512, "prompt")

    h1s, ps = front(x_sample.reshape(dec_batch * steps, D_MODEL), 512, "sample")
    oa_s, sa_s = _sample_scan(ps, state_hgrn[0], gla=False, batch=dec_batch, steps=steps,
                              small=small_a, name="scan_hgrn_sample")
    ob_s, sb_s = _sample_scan(ps, state_gla[0], gla=True, batch=dec_batch, steps=steps,
                              small=small_b, name="scan_gla_sample")
    y_s = back(oa_s, ob_s, h1s, 512, "sample")

    return (y_p.reshape(batch, seq, D_MODEL), y_s.reshape(dec_batch, steps, D_MODEL),
            sa_p[None], sb_p[None], sa_s[None], sb_s[None])
```

```python
import functools

import jax
import jax.numpy as jnp
from jax import lax
from jax.experimental import pallas as pl
from jax.experimental.pallas import tpu as pltpu

F32 = jnp.float32
BF16 = jnp.bfloat16
HIGHEST = lax.Precision.HIGHEST

D_MODEL = 2048
N_META = 16
DK = 128
H_A, DV_A = 8, 128
H_B, DV_B = 4, 256
D_HALF = H_A * DV_A
ALPHA_RANK = 16
GATE_TEMP = 16.0
D_FF = 5632
CHUNK = 64
EPS = 1e-6
D_IN_PROJ = 4 * D_HALF + 2 * H_B * DK + 2 * D_HALF + ALPHA_RANK

LANES = 128
VMEM_LIMIT_BYTES = 56 * 1024 * 1024

COL_QA, COL_FA, COL_IA, COL_GA = 0, 8, 16, 24
COL_QB, COL_KB, COL_VB, COL_RB, COL_AL = 32, 36, 40, 48, 56
NP_PAD = 7680

TF = 512
TN_PROJ = 512
DIAG_BLK = 16
SAMPLE_SEQS = 8
GATE_ROWS = 256
CHUNKS_PER_TRIP = 4

FAST_MIN_LOG_GATE = -60.0 / (DIAG_BLK - 1)
FAST_MAX_KEY = 1e10


def _rms(x, w):
    return x * lax.rsqrt(jnp.mean(x * x, axis=-1, keepdims=True) + EPS) * w


def _silu(x):
    return x * jax.nn.sigmoid(x)


def _params(*sem):
    return pltpu.CompilerParams(dimension_semantics=sem, vmem_limit_bytes=VMEM_LIMIT_BYTES)


def _ffn_kernel(h_ref, nw_ref, wg_ref, wu_ref, wo_ref, fnw_ref, out_ref, u_ref, acc_ref, *,
                apply_final_norm):
    j = pl.program_id(1)

    @pl.when(j == 0)
    def _init():
        u_ref[...] = _rms(h_ref[...], nw_ref[...]).astype(BF16)
        acc_ref[...] = jnp.zeros_like(acc_ref)

    u = u_ref[...]
    g = jnp.dot(u, wg_ref[...], preferred_element_type=F32)
    up = jnp.dot(u, wu_ref[...], preferred_element_type=F32)
    a = (_silu(g) * up).astype(BF16)
    acc_ref[...] += jnp.dot(a, wo_ref[...], preferred_element_type=F32)

    @pl.when(j == pl.num_programs(1) - 1)
    def _finish():
        y = h_ref[...] + 0.5 * acc_ref[...]
        if apply_final_norm:
            y = _rms(y, fnw_ref[...])
        out_ref[...] = y


def _ffn(h, norm_w, w_in, w_out, final_norm_w, *, tm, apply_final_norm, name):
    m = h.shape[0]
    nf = D_FF // TF
    return pl.pallas_call(
        functools.partial(_ffn_kernel, apply_final_norm=apply_final_norm),
        grid=(m // tm, nf),
        in_specs=[
            pl.BlockSpec((tm, D_MODEL), lambda i, j: (i, 0)),
            pl.BlockSpec((1, D_MODEL), lambda i, j: (0, 0)),
            pl.BlockSpec((D_MODEL, TF), lambda i, j: (0, j)),
            pl.BlockSpec((D_MODEL, TF), lambda i, j: (0, j + nf)),
            pl.BlockSpec((TF, D_MODEL), lambda i, j: (j, 0)),
            pl.BlockSpec((1, D_MODEL), lambda i, j: (0, 0)),
        ],
        out_specs=pl.BlockSpec((tm, D_MODEL), lambda i, j: (i, 0)),
        out_shape=jax.ShapeDtypeStruct((m, D_MODEL), F32),
        scratch_shapes=[pltpu.VMEM((tm, D_MODEL), BF16), pltpu.VMEM((tm, D_MODEL), F32)],
        compiler_params=_params("parallel", "arbitrary"),
        name=name,
    )(h, norm_w, w_in, w_in, w_out, final_norm_w)


def _proj_kernel(h_ref, nw_ref, w_ref, out_ref, u_ref):
    @pl.when(pl.program_id(1) == 0)
    def _init():
        u_ref[...] = _rms(h_ref[...], nw_ref[...]).astype(BF16)

    out_ref[...] = jnp.dot(u_ref[...], w_ref[...], preferred_element_type=F32)


def _in_proj(h, norm_w, w, *, tm, name):
    m = h.shape[0]
    return pl.pallas_call(
        _proj_kernel,
        grid=(m // tm, NP_PAD // TN_PROJ),
        in_specs=[
            pl.BlockSpec((tm, D_MODEL), lambda i, j: (i, 0)),
            pl.BlockSpec((1, D_MODEL), lambda i, j: (0, 0)),
            pl.BlockSpec((D_MODEL, TN_PROJ), lambda i, j: (0, j)),
        ],
        out_specs=pl.BlockSpec((tm, TN_PROJ), lambda i, j: (i, j)),
        out_shape=jax.ShapeDtypeStruct((m, NP_PAD), F32),
        scratch_shapes=[pltpu.VMEM((tm, D_MODEL), BF16)],
        compiler_params=_params("parallel", "arbitrary"),
        name=name,
    )(h, norm_w, w)


def _out_proj_kernel(oa_ref, ob_ref, h_ref, wa_ref, wb_ref, out_ref):
    out_ref[...] = (h_ref[...]
                    + jnp.dot(oa_ref[...], wa_ref[...], preferred_element_type=F32)
                    + jnp.dot(ob_ref[...], wb_ref[...], preferred_element_type=F32))


def _out_proj(oa, ob, h, w, *, tm, name):
    m = h.shape[0]
    return pl.pallas_call(
        _out_proj_kernel,
        grid=(m // tm,),
        in_specs=[
            pl.BlockSpec((tm, D_HALF), lambda i: (i, 0)),
            pl.BlockSpec((tm, D_HALF), lambda i: (i, 0)),
            pl.BlockSpec((tm, D_MODEL), lambda i: (i, 0)),
            pl.BlockSpec((D_HALF, D_MODEL), lambda i: (0, 0)),
            pl.BlockSpec((D_HALF, D_MODEL), lambda i: (1, 0)),
        ],
        out_specs=pl.BlockSpec((tm, D_MODEL), lambda i: (i, 0)),
        out_shape=jax.ShapeDtypeStruct((m, D_MODEL), F32),
        compiler_params=_params("parallel"),
        name=name,
    )(oa, ob, h, w, w)


def _tri(c, seq_len):
    r = lax.broadcasted_iota(jnp.int32, (c, c), 0)
    s = lax.broadcasted_iota(jnp.int32, (c, c), 1)
    return jnp.where((r >= s) & ((r ^ s) < seq_len), 1.0, 0.0).astype(F32)


def _off_mask(c, blk):
    nb = c // blk
    rows = nb * (nb - 1) // 2 * blk
    r = lax.broadcasted_iota(jnp.int32, (c, rows), 0) >> (blk.bit_length() - 1)
    s = lax.broadcasted_iota(jnp.int32, (c, rows), 1)
    seg = jnp.zeros((c, rows), jnp.int32)
    for i in range(1, nb):
        seg = seg + jnp.where(s >= blk * i * (i - 1) // 2, 1, 0)
    return jnp.where(r == seg, 1.0, 0.0).astype(F32)


def _diag(q, k, v, bc, blk):
    c = q.shape[0]
    row = lax.broadcasted_iota(jnp.int32, (blk, 1), 0)
    outs = []
    for i in range(c // blk):
        sl = slice(i * blk, (i + 1) * blk)
        qi, ki, vi, bi = q[sl], k[sl], v[sl], bc[sl]
        acc = jnp.zeros((blk, v.shape[1]), F32)
        for s in range(blk):
            e = jnp.exp(jnp.minimum(bi - bi[s:s + 1, :], 0.0))
            col = jnp.sum(qi * ki[s:s + 1, :] * e, axis=-1, keepdims=True)
            col = jnp.where(row >= s, col, 0.0)
            acc = acc + col * vi[s:s + 1, :]
        outs.append(acc)
    return outs[0] if len(outs) == 1 else jnp.concatenate(outs, axis=0)


def _offdiag(q, k, v, bc, blk, mask):
    c = q.shape[0]
    nb = c // blk
    qs, ks, vs = [jnp.zeros((blk, DK), F32)], [], []
    for i in range(1, nb):
        lo = i * blk
        ref = bc[lo:lo + 1, :]
        qs.append(q[lo:lo + blk] * jnp.exp(bc[lo:lo + blk] - ref))
        ks.append(k[:lo] * jnp.exp(ref - bc[:lo]))
        vs.append(v[:lo])
    qt = jnp.concatenate(qs, axis=0).astype(BF16)
    kt = jnp.concatenate(ks, axis=0).astype(BF16)
    vt = jnp.concatenate(vs, axis=0).astype(BF16)
    a = lax.dot_general(qt, kt, (((1,), (1,)), ((), ())), preferred_element_type=F32)
    return jnp.dot((a * mask).astype(BF16), vt, preferred_element_type=F32)


def _state_update(k, v, g, bc, blast, s_prev):
    c, dv = v.shape
    kdec = (k * jnp.exp(blast - bc)).astype(BF16)
    upd = lax.dot_general(kdec, v.astype(BF16), (((0,), (0,)), ((), ())),
                          preferred_element_type=F32)
    if s_prev is None:
        return upd
    tot = lax.dot_general(g, jnp.ones((c, dv), F32), (((0,), (0,)), ((), ())),
                          precision=HIGHEST, preferred_element_type=F32)
    return jnp.exp(tot) * s_prev + upd


def _hgrn_inputs(qa, fa, ia, lb):
    fg = lb + (1.0 - lb) * jax.nn.sigmoid(fa)
    return _silu(qa) * (DK ** -0.5), 1.0 - fg, ia, jnp.log(fg)


def _gla_inputs(qb, kb, vb, al, wa, ba):
    alpha = jnp.dot(al.astype(BF16), wa, preferred_element_type=F32) + ba
    logsig = jnp.minimum(alpha, 0.0) - jnp.log1p(jnp.exp(-jnp.abs(alpha)))
    return qb * (DK ** -0.5), kb, vb, logsig / GATE_TEMP


def _head_out(o, gn, gate):
    return (_rms(o, gn) * _silu(gate)).astype(BF16)


def _lower_bound(lbl_ref):
    x = lbl_ref[...]
    e = jnp.exp(x - jnp.max(x, axis=0, keepdims=True))
    return e[0:1, :] / jnp.sum(e, axis=0, keepdims=True)


def _stack_mask(c, blk):
    nb = c // blk
    rows = blk * nb * (nb + 1) // 2
    t = lax.broadcasted_iota(jnp.int32, (c, rows), 0)
    r = lax.broadcasted_iota(jnp.int32, (c, rows), 1)
    seg = jnp.zeros((c, rows), jnp.int32)
    off = jnp.zeros((c, rows), jnp.int32)
    for i in range(1, nb):
        start = blk * i * (i + 1) // 2
        seg = seg + jnp.where(r >= start, 1, 0)
        off = jnp.where(r >= start, start, off)
    keep = ((t >> (blk.bit_length() - 1)) == seg) & (r - off <= t)
    return jnp.where(keep, 1.0, 0.0).astype(F32)


def _state_update_t(k, vb, bc, st):
    c = k.shape[0]
    blast = bc[c - 1:c, :]
    kdec = (k * jnp.exp(blast - bc)).astype(BF16)
    upd = lax.dot_general(vb, kdec, (((0,), (0,)), ((), ())), preferred_element_type=F32)
    if st is None:
        return upd
    return st * jnp.exp(blast) + upd


def _chunk_fast(q, k, v, g, st, tri, mask):
    c, blk = CHUNK, DIAG_BLK
    bc = jnp.dot(tri, g, precision=HIGHEST, preferred_element_type=F32)
    vb = v.astype(BF16)
    qs, ks, vs = [], [], []
    for i in range(c // blk):
        lo, hi = i * blk, (i + 1) * blk
        ref = bc[lo:lo + 1, :]
        qs.append(q[lo:hi] * jnp.exp(bc[lo:hi] - ref))
        ks.append(k[:hi] * jnp.exp(ref - bc[:hi]))
        vs.append(vb[:hi])
    qt = jnp.concatenate(qs, axis=0).astype(BF16)
    kt = jnp.concatenate(ks, axis=0).astype(BF16)
    vt = jnp.concatenate(vs, axis=0)
    a = lax.dot_general(qt, kt, (((1,), (1,)), ((), ())), preferred_element_type=F32)
    o = jnp.dot((a * mask).astype(BF16), vt, preferred_element_type=F32)
    o = o + lax.dot_general((q * jnp.exp(bc)).astype(BF16), st.astype(BF16),
                            (((1,), (1,)), ((), ())), preferred_element_type=F32)
    return o, _state_update_t(k, vb, bc, st)


def _chunk_exact(q, k, v, g, st, tri, mask):
    bc = jnp.dot(tri, g, precision=HIGHEST, preferred_element_type=F32)
    o = lax.dot_general((q * jnp.exp(bc)).astype(BF16), st.astype(BF16),
                        (((1,), (1,)), ((), ())), preferred_element_type=F32)
    o = o + _offdiag(q, k, v, bc, DIAG_BLK, mask) + _diag(q, k, v, bc, DIAG_BLK)
    return o, _state_update_t(k, v.astype(BF16), bc, st)


def _prompt_scan_kernel(*refs, gla, seq, meta):
    if gla:
        (q_ref, k_ref, v_ref, gate_ref, al_ref, mk_ref, mv_ref, mal_ref, wa_ref, ba_ref, gn_ref,
         o_ref, sout_ref, st_ref, qs_ref, ks_ref, gs_ref) = refs
    else:
        (q_ref, f_ref, v_ref, gate_ref, mf_ref, mv_ref, lbl_ref, gn_ref,
         o_ref, sout_ref, st_ref, qs_ref, ks_ref, gs_ref) = refs
        lb = _lower_bound(lbl_ref)

    if gla:
        _, mk, mv, mg = _gla_inputs(mk_ref[...], mk_ref[...], mv_ref[...], mal_ref[...],
                                    wa_ref[...], ba_ref[...])
    else:
        _, mk, mv, mg = _hgrn_inputs(mf_ref[...], mf_ref[...], mv_ref[...], lb)
    mbc = jnp.dot(_tri(meta, meta), mg, precision=HIGHEST, preferred_element_type=F32)
    st_ref[...] = _state_update_t(mk, mv.astype(BF16), mbc, None)

    def gates(ri, carry):
        gmin, kmax = carry
        rows = pl.ds(pl.multiple_of(ri * GATE_ROWS, GATE_ROWS), GATE_ROWS)
        if gla:
            q, k, _, g = _gla_inputs(q_ref[rows, :], k_ref[rows, :], None, al_ref[rows, :],
                                     wa_ref[...], ba_ref[...])
        else:
            q, k, _, g = _hgrn_inputs(q_ref[rows, :], f_ref[rows, :], None, lb)
        qs_ref[rows, :] = q
        ks_ref[rows, :] = k
        gs_ref[rows, :] = g
        return (jnp.minimum(gmin, jnp.min(g, axis=0, keepdims=True)),
                jnp.maximum(kmax, jnp.max(jnp.abs(k), axis=0, keepdims=True)))

    gmin, kmax = lax.fori_loop(0, seq // GATE_ROWS, gates,
                               (jnp.zeros((1, DK), F32), jnp.zeros((1, DK), F32)))
    in_range = (jnp.min(gmin) >= FAST_MIN_LOG_GATE) & (jnp.max(kmax) <= FAST_MAX_KEY)

    tri = _tri(CHUNK, CHUNK)

    def run(step, mask):
        def body(ci, carry):
            st = st_ref[...]
            for u in range(CHUNKS_PER_TRIP):
                start = pl.multiple_of((ci * CHUNKS_PER_TRIP + u) * CHUNK, CHUNK)
                rows = pl.ds(start, CHUNK)
                o, st = step(qs_ref[rows, :], ks_ref[rows, :], v_ref[rows, :], gs_ref[rows, :],
                             st, tri, mask)
                o_ref[rows, :] = _head_out(o, gn_ref[...], gate_ref[rows, :])
            st_ref[...] = st
            return carry

        lax.fori_loop(0, seq // (CHUNK * CHUNKS_PER_TRIP), body, 0)

    @pl.when(in_range)
    def _fast():
        run(_chunk_fast, _stack_mask(CHUNK, DIAG_BLK))

    @pl.when(jnp.logical_not(in_range))
    def _exact():
        run(_chunk_exact, _off_mask(CHUNK, DIAG_BLK))

    sout_ref[...] = st_ref[...].T


def _sample_scan_kernel(*refs, gla, steps):
    if gla:
        (q_ref, k_ref, v_ref, gate_ref, al_ref, wa_ref, ba_ref, gn_ref, s0_ref,
         o_ref, sout_ref) = refs
        q, k, v, g = _gla_inputs(q_ref[...], k_ref[...], v_ref[...], al_ref[...],
                                 wa_ref[...], ba_ref[...])
    else:
        (q_ref, f_ref, v_ref, gate_ref, lbl_ref, gn_ref, s0_ref, o_ref, sout_ref) = refs
        q, k, v, g = _hgrn_inputs(q_ref[...], f_ref[...], v_ref[...], _lower_bound(lbl_ref))

    rows = SAMPLE_SEQS * steps
    bc = jnp.dot(_tri(rows, steps), g, precision=HIGHEST, preferred_element_type=F32)
    qd = (q * jnp.exp(bc)).astype(BF16)
    inter = []
    for b in range(SAMPLE_SEQS):
        sl = slice(b * steps, (b + 1) * steps)
        s_prev = s0_ref[b]
        inter.append(jnp.dot(qd[sl], s_prev.astype(BF16), preferred_element_type=F32))
        last = (b + 1) * steps - 1
        sout_ref[b] = _state_update(k[sl], v[sl], g[sl], bc[sl], bc[last:last + 1, :], s_prev)
    o = jnp.concatenate(inter, axis=0) + _diag(q, k, v, bc, steps)
    o_ref[...] = _head_out(o, gn_ref[...], gate_ref[...])


def _col(width, block):
    return block * LANES // width


def _prompt_scan(p, pm, *, gla, batch, seq, small, name):
    heads, dv = (H_B, DV_B) if gla else (H_A, DV_A)
    tok = lambda width, c0, per_head: pl.BlockSpec(
        (seq, width), lambda b, h: (b, _col(width, c0) + h * per_head))
    met = lambda width, c0, per_head: pl.BlockSpec(
        (N_META, width), lambda b, h: (0, _col(width, c0) + h * per_head))
    vec = lambda width: pl.BlockSpec((1, width), lambda b, h: (0, h))
    if gla:
        wa, ba, gn = small
        in_specs = [tok(DK, COL_QB, 1), tok(DK, COL_KB, 1), tok(dv, COL_VB, 1), tok(dv, COL_RB, 1),
                    tok(LANES, COL_AL, 0),
                    met(DK, COL_KB, 1), met(dv, COL_VB, 1), met(LANES, COL_AL, 0),
                    pl.BlockSpec((LANES, DK), lambda b, h: (0, h)), vec(DK), vec(dv)]
        args = (p, p, p, p, p, pm, pm, pm, wa, ba, gn)
    else:
        lbl, gn = small
        in_specs = [tok(DK, COL_QA, 1), tok(DK, COL_FA, 1), tok(dv, COL_IA, 1), tok(dv, COL_GA, 1),
                    met(DK, COL_FA, 1), met(dv, COL_IA, 1),
                    pl.BlockSpec((2, DK), lambda b, h: (0, h)), vec(dv)]
        args = (p, p, p, p, pm, pm, lbl, gn)
    return pl.pallas_call(
        functools.partial(_prompt_scan_kernel, gla=gla, seq=seq, meta=N_META),
        grid=(batch, heads),
        in_specs=in_specs,
        out_specs=[pl.BlockSpec((seq, dv), lambda b, h: (b, h)),
                   pl.BlockSpec((None, None, DK, dv), lambda b, h: (b, h, 0, 0))],
        out_shape=[jax.ShapeDtypeStruct((batch * seq, heads * dv), BF16),
                   jax.ShapeDtypeStruct((batch, heads, DK, dv), F32)],
        scratch_shapes=[pltpu.VMEM((dv, DK), F32)] + [pltpu.VMEM((seq, DK), F32)] * 3,
        compiler_params=_params("parallel", "parallel"),
        name=name,
    )(*args)


def _sample_scan(p, s0, *, gla, batch, steps, small, name):
    heads, dv = (H_B, DV_B) if gla else (H_A, DV_A)
    rows = SAMPLE_SEQS * steps
    tok = lambda width, c0, per_head: pl.BlockSpec(
        (rows, width), lambda b, h: (b, _col(width, c0) + h * per_head))
    vec = lambda width: pl.BlockSpec((1, width), lambda b, h: (0, h))
    state = pl.BlockSpec((SAMPLE_SEQS, None, DK, dv), lambda b, h: (b, h, 0, 0))
    if gla:
        wa, ba, gn = small
        in_specs = [tok(DK, COL_QB, 1), tok(DK, COL_KB, 1), tok(dv, COL_VB, 1), tok(dv, COL_RB, 1),
                    tok(LANES, COL_AL, 0),
                    pl.BlockSpec((LANES, DK), lambda b, h: (0, h)), vec(DK), vec(dv), state]
        args = (p, p, p, p, p, wa, ba, gn, s0)
    else:
        lbl, gn = small
        in_specs = [tok(DK, COL_QA, 1), tok(DK, COL_FA, 1), tok(dv, COL_IA, 1), tok(dv, COL_GA, 1),
                    pl.BlockSpec((2, DK), lambda b, h: (0, h)), vec(dv), state]
        args = (p, p, p, p, lbl, gn, s0)
    return pl.pallas_call(
        functools.partial(_sample_scan_kernel, gla=gla, steps=steps),
        grid=(batch // SAMPLE_SEQS, heads),
        in_specs=in_specs,
        out_specs=[pl.BlockSpec((rows, dv), lambda b, h: (b, h)), state],
        out_shape=[jax.ShapeDtypeStruct((batch * steps, heads * dv), BF16),
                   jax.ShapeDtypeStruct((batch, heads, DK, dv), F32)],
        compiler_params=_params("parallel", "parallel"),
        name=name,
    )(*args)


def kernel(x_prompt, x_sample, state_hgrn, state_gla, meta_tokens, lb_logits, ffn1_norm, w_ffn1_in,
           w_ffn1_out, mix_norm, w_in, w_alpha_up, b_alpha, gnorm_a, gnorm_b, w_out, ffn2_norm,
           w_ffn2_in, w_ffn2_out, final_norm):
    batch, seq, _ = x_prompt.shape
    dec_batch, steps, _ = x_sample.shape

    w1i, w1o = w_ffn1_in[0].astype(BF16), w_ffn1_out[0].astype(BF16)
    w2i, w2o = w_ffn2_in[0].astype(BF16), w_ffn2_out[0].astype(BF16)
    wip = jnp.pad(w_in[0], ((0, 0), (0, NP_PAD - D_IN_PROJ))).astype(BF16)
    wop = w_out[0].astype(BF16)
    wa = jnp.pad(w_alpha_up[0], ((0, LANES - ALPHA_RANK), (0, 0))).astype(BF16)
    n1, nm, n2, nf = ffn1_norm, mix_norm, ffn2_norm, final_norm[None]
    small_a = (lb_logits, gnorm_a)
    small_b = (wa, b_alpha, gnorm_b)

    def front(x, tm, tag):
        h1 = _ffn(x, n1, w1i, w1o, nf, tm=tm, apply_final_norm=False, name=f"ffn1_{tag}")
        return h1, _in_proj(h1, nm, wip, tm=tm, name=f"inproj_{tag}")

    def back(oa, ob, h1, tm, tag):
        h2 = _out_proj(oa, ob, h1, wop, tm=tm, name=f"outproj_{tag}")
        return _ffn(h2, n2, w2i, w2o, nf, tm=tm, apply_final_norm=True, name=f"ffn2_{tag}")

    _, pm = front(meta_tokens, N_META, "meta")

    h1p, pp = front(x_prompt.reshape(batch * seq, D_MODEL), 512, "prompt")
    oa_p, sa_p = _prompt_scan(pp, pm, gla=False, batch=batch, seq=seq, small=small_a,
                              name="scan_hgrn_prompt")
    ob_p, sb_p = _prompt_scan(pp, pm, gla=True, batch=batch, seq=seq, small=small_b,
                              name="scan_gla_prompt")
    y_p = back(oa_p, ob_p, h1p, 512, "prompt")

    h1s, ps = front(x_sample.reshape(dec_batch * steps, D_MODEL), 512, "sample")
    oa_s, sa_s = _sample_scan(ps, state_hgrn[0], gla=False, batch=dec_batch, steps=steps,
                              small=small_a, name="scan_hgrn_sample")
    ob_s, sb_s = _sample_scan(ps, state_gla[0], gla=True, batch=dec_batch, steps=steps,
                              small=small_b, name="scan_gla_sample")
    y_s = back(oa_s, ob_s, h1s, 512, "sample")

    return (y_p.reshape(batch, seq, D_MODEL), y_s.reshape(dec_batch, steps, D_MODEL),
            sa_p[None], sb_p[None], sa_s[None], sb_s[None])
```

```python
import functools

import jax
import jax.numpy as jnp
from jax import lax
from jax.experimental import pallas as pl
from jax.experimental.pallas import tpu as pltpu

F32 = jnp.float32
BF16 = jnp.bfloat16
HIGHEST = lax.Precision.HIGHEST

D_MODEL = 2048
N_META = 16
DK = 128
H_A, DV_A = 8, 128
H_B, DV_B = 4, 256
D_HALF = H_A * DV_A
ALPHA_RANK = 16
GATE_TEMP = 16.0
D_FF = 5632
CHUNK = 64
EPS = 1e-6
D_IN_PROJ = 4 * D_HALF + 2 * H_B * DK + 2 * D_HALF + ALPHA_RANK

LANES = 128
SUBLANES = 8
VMEM_LIMIT_BYTES = 56 * 1024 * 1024

COL_QA, COL_FA, COL_IA, COL_GA = 0, 8, 16, 24
COL_QB, COL_KB, COL_VB, COL_RB, COL_AL = 32, 36, 40, 48, 56
NP_PAD = 7680

TF = 512
TN_PROJ = 512
DIAG_BLK = 16
SAMPLE_SEQS = 8
CHUNKS_PER_TRIP = 4
CHUNKS_PER_MXU_TRIP = 8

FAST_MIN_LOG_GATE = -60.0 / (DIAG_BLK - 1)
FAST_MAX_KEY = 1e10


def _rms(x, w):
    return x * lax.rsqrt(jnp.mean(x * x, axis=-1, keepdims=True) + EPS) * w


def _silu(x):
    return x * jax.nn.sigmoid(x)


def _params(*sem):
    return pltpu.CompilerParams(dimension_semantics=sem, vmem_limit_bytes=VMEM_LIMIT_BYTES)


def _ffn_kernel(h_ref, nw_ref, wg_ref, wu_ref, wo_ref, fnw_ref, out_ref, u_ref, acc_ref, *,
                apply_final_norm):
    j = pl.program_id(1)

    @pl.when(j == 0)
    def _init():
        u_ref[...] = _rms(h_ref[...], nw_ref[...]).astype(BF16)
        acc_ref[...] = jnp.zeros_like(acc_ref)

    u = u_ref[...]
    g = jnp.dot(u, wg_ref[...], preferred_element_type=F32)
    up = jnp.dot(u, wu_ref[...], preferred_element_type=F32)
    a = (_silu(g) * up).astype(BF16)
    acc_ref[...] += jnp.dot(a, wo_ref[...], preferred_element_type=F32)

    @pl.when(j == pl.num_programs(1) - 1)
    def _finish():
        y = h_ref[...] + 0.5 * acc_ref[...]
        if apply_final_norm:
            y = _rms(y, fnw_ref[...])
        out_ref[...] = y


def _ffn(h, norm_w, w_in, w_out, final_norm_w, *, tm, apply_final_norm, name):
    m = h.shape[0]
    nf = D_FF // TF
    return pl.pallas_call(
        functools.partial(_ffn_kernel, apply_final_norm=apply_final_norm),
        grid=(m // tm, nf),
        in_specs=[
            pl.BlockSpec((tm, D_MODEL), lambda i, j: (i, 0)),
            pl.BlockSpec((1, D_MODEL), lambda i, j: (0, 0)),
            pl.BlockSpec((D_MODEL, TF), lambda i, j: (0, j)),
            pl.BlockSpec((D_MODEL, TF), lambda i, j: (0, j + nf)),
            pl.BlockSpec((TF, D_MODEL), lambda i, j: (j, 0)),
            pl.BlockSpec((1, D_MODEL), lambda i, j: (0, 0)),
        ],
        out_specs=pl.BlockSpec((tm, D_MODEL), lambda i, j: (i, 0)),
        out_shape=jax.ShapeDtypeStruct((m, D_MODEL), F32),
        scratch_shapes=[pltpu.VMEM((tm, D_MODEL), BF16), pltpu.VMEM((tm, D_MODEL), F32)],
        compiler_params=_params("parallel", "arbitrary"),
        name=name,
    )(h, norm_w, w_in, w_in, w_out, final_norm_w)


def _proj_kernel(h_ref, nw_ref, w_ref, out_ref, u_ref):
    @pl.when(pl.program_id(1) == 0)
    def _init():
        u_ref[...] = _rms(h_ref[...], nw_ref[...]).astype(BF16)

    out_ref[...] = jnp.dot(u_ref[...], w_ref[...], preferred_element_type=F32)


def _in_proj(h, norm_w, w, *, tm, name):
    m = h.shape[0]
    return pl.pallas_call(
        _proj_kernel,
        grid=(m // tm, NP_PAD // TN_PROJ),
        in_specs=[
            pl.BlockSpec((tm, D_MODEL), lambda i, j: (i, 0)),
            pl.BlockSpec((1, D_MODEL), lambda i, j: (0, 0)),
            pl.BlockSpec((D_MODEL, TN_PROJ), lambda i, j: (0, j)),
        ],
        out_specs=pl.BlockSpec((tm, TN_PROJ), lambda i, j: (i, j)),
        out_shape=jax.ShapeDtypeStruct((m, NP_PAD), F32),
        scratch_shapes=[pltpu.VMEM((tm, D_MODEL), BF16)],
        compiler_params=_params("parallel", "arbitrary"),
        name=name,
    )(h, norm_w, w)


def _out_proj_kernel(oa_ref, ob_ref, h_ref, wa_ref, wb_ref, out_ref):
    out_ref[...] = (h_ref[...]
                    + jnp.dot(oa_ref[...], wa_ref[...], preferred_element_type=F32)
                    + jnp.dot(ob_ref[...], wb_ref[...], preferred_element_type=F32))


def _out_proj(oa, ob, h, w, *, tm, name):
    m = h.shape[0]
    return pl.pallas_call(
        _out_proj_kernel,
        grid=(m // tm,),
        in_specs=[
            pl.BlockSpec((tm, D_HALF), lambda i: (i, 0)),
            pl.BlockSpec((tm, D_HALF), lambda i: (i, 0)),
            pl.BlockSpec((tm, D_MODEL), lambda i: (i, 0)),
            pl.BlockSpec((D_HALF, D_MODEL), lambda i: (0, 0)),
            pl.BlockSpec((D_HALF, D_MODEL), lambda i: (1, 0)),
        ],
        out_specs=pl.BlockSpec((tm, D_MODEL), lambda i: (i, 0)),
        out_shape=jax.ShapeDtypeStruct((m, D_MODEL), F32),
        compiler_params=_params("parallel"),
        name=name,
    )(oa, ob, h, w, w)


def _tri(c, seq_len):
    r = lax.broadcasted_iota(jnp.int32, (c, c), 0)
    s = lax.broadcasted_iota(jnp.int32, (c, c), 1)
    return jnp.where((r >= s) & ((r ^ s) < seq_len), 1.0, 0.0).astype(F32)


def _off_mask(c, blk):
    nb = c // blk
    rows = nb * (nb - 1) // 2 * blk
    r = lax.broadcasted_iota(jnp.int32, (c, rows), 0) >> (blk.bit_length() - 1)
    s = lax.broadcasted_iota(jnp.int32, (c, rows), 1)
    seg = jnp.zeros((c, rows), jnp.int32)
    for i in range(1, nb):
        seg = seg + jnp.where(s >= blk * i * (i - 1) // 2, 1, 0)
    return jnp.where(r == seg, 1.0, 0.0).astype(F32)


def _diag(q, k, v, bc, blk):
    c = q.shape[0]
    row = lax.broadcasted_iota(jnp.int32, (blk, 1), 0)
    outs = []
    for i in range(c // blk):
        sl = slice(i * blk, (i + 1) * blk)
        qi, ki, vi, bi = q[sl], k[sl], v[sl], bc[sl]
        acc = jnp.zeros((blk, v.shape[1]), F32)
        for s in range(blk):
            e = jnp.exp(jnp.minimum(bi - bi[s:s + 1, :], 0.0))
            col = jnp.sum(qi * ki[s:s + 1, :] * e, axis=-1, keepdims=True)
            col = jnp.where(row >= s, col, 0.0)
            acc = acc + col * vi[s:s + 1, :]
        outs.append(acc)
    return outs[0] if len(outs) == 1 else jnp.concatenate(outs, axis=0)


def _offdiag(q, k, v, bc, blk, mask):
    c = q.shape[0]
    nb = c // blk
    qs, ks, vs = [jnp.zeros((blk, DK), F32)], [], []
    for i in range(1, nb):
        lo = i * blk
        ref = bc[lo:lo + 1, :]
        qs.append(q[lo:lo + blk] * jnp.exp(bc[lo:lo + blk] - ref))
        ks.append(k[:lo] * jnp.exp(ref - bc[:lo]))
        vs.append(v[:lo])
    qt = jnp.concatenate(qs, axis=0).astype(BF16)
    kt = jnp.concatenate(ks, axis=0).astype(BF16)
    vt = jnp.concatenate(vs, axis=0).astype(BF16)
    a = lax.dot_general(qt, kt, (((1,), (1,)), ((), ())), preferred_element_type=F32)
    return jnp.dot((a * mask).astype(BF16), vt, preferred_element_type=F32)


def _state_update(k, v, g, bc, blast, s_prev):
    c, dv = v.shape
    kdec = (k * jnp.exp(blast - bc)).astype(BF16)
    upd = lax.dot_general(kdec, v.astype(BF16), (((0,), (0,)), ((), ())),
                          preferred_element_type=F32)
    if s_prev is None:
        return upd
    tot = lax.dot_general(g, jnp.ones((c, dv), F32), (((0,), (0,)), ((), ())),
                          precision=HIGHEST, preferred_element_type=F32)
    return jnp.exp(tot) * s_prev + upd


def _hgrn_inputs(qa, fa, ia, lb):
    fg = lb + (1.0 - lb) * jax.nn.sigmoid(fa)
    return _silu(qa) * (DK ** -0.5), 1.0 - fg, ia, jnp.log(fg)


def _gla_inputs(qb, kb, vb, al, wa, ba):
    alpha = jnp.dot(al.astype(BF16), wa, preferred_element_type=F32) + ba
    logsig = jnp.minimum(alpha, 0.0) - jnp.log1p(jnp.exp(-jnp.abs(alpha)))
    return qb * (DK ** -0.5), kb, vb, logsig / GATE_TEMP


def _head_out(o, gn, gate):
    return (_rms(o, gn) * _silu(gate)).astype(BF16)


def _lower_bound(lbl_ref):
    x = lbl_ref[...]
    e = jnp.exp(x - jnp.max(x, axis=0, keepdims=True))
    return e[0:1, :] / jnp.sum(e, axis=0, keepdims=True)


def _stack_mask(c, blk):
    nb = c // blk
    rows = blk * nb * (nb + 1) // 2
    t = lax.broadcasted_iota(jnp.int32, (c, rows), 0)
    r = lax.broadcasted_iota(jnp.int32, (c, rows), 1)
    seg = jnp.zeros((c, rows), jnp.int32)
    off = jnp.zeros((c, rows), jnp.int32)
    for i in range(1, nb):
        start = blk * i * (i + 1) // 2
        seg = seg + jnp.where(r >= start, 1, 0)
        off = jnp.where(r >= start, start, off)
    keep = ((t >> (blk.bit_length() - 1)) == seg) & (r - off <= t)
    return jnp.where(keep, 1.0, 0.0).astype(F32)


def _state_update_t(k, vb, bc, st):
    c = k.shape[0]
    blast = bc[c - 1:c, :]
    kdec = (k * jnp.exp(blast - bc)).astype(BF16)
    upd = lax.dot_general(vb, kdec, (((0,), (0,)), ((), ())), preferred_element_type=F32)
    if st is None:
        return upd
    return st * jnp.exp(blast) + upd


def _cumsum_chunk(g):
    row = lax.broadcasted_iota(jnp.int32, (SUBLANES, DK), 0)
    out, carry = [], None
    for j in range(CHUNK // SUBLANES):
        p = g[j * SUBLANES:(j + 1) * SUBLANES]
        for s in (1, 2, 4):
            p = p + jnp.where(row >= s, pltpu.roll(p, s, axis=0), 0.0)
        if carry is not None:
            p = p + carry
        carry = p[SUBLANES - 1:SUBLANES, :]
        out.append(p)
    return jnp.concatenate(out, axis=0)


def _rescaled_operands(q, k, bc):
    c, blk = CHUNK, DIAG_BLK
    qs, ks = [], []
    for i in range(c // blk):
        lo, hi = i * blk, (i + 1) * blk
        ref = bc[lo:lo + 1, :]
        qs.append(q[lo:hi] * jnp.exp(bc[lo:hi] - ref))
        ks.append(k[:hi] * jnp.exp(ref - bc[:hi]))
    return ((q * jnp.exp(bc)).astype(BF16), jnp.concatenate(qs, axis=0).astype(BF16),
            jnp.concatenate(ks, axis=0).astype(BF16),
            (k * jnp.exp(bc[c - 1:c, :] - bc)).astype(BF16))


def _chunk_exact(q, k, v, bc, st, mask):
    o = lax.dot_general((q * jnp.exp(bc)).astype(BF16), st.astype(BF16),
                        (((1,), (1,)), ((), ())), preferred_element_type=F32)
    o = o + _offdiag(q, k, v, bc, DIAG_BLK, mask) + _diag(q, k, v, bc, DIAG_BLK)
    return o, _state_update_t(k, v.astype(BF16), bc, st)


def _prompt_scan_kernel(*refs, gla, seq, meta):
    n_in = 11 if gla else 8
    if gla:
        q_ref, k_ref, v_ref, gate_ref, al_ref, mk_ref, mv_ref, mal_ref, wa_ref, ba_ref, gn_ref = (
            refs[:n_in])
    else:
        q_ref, f_ref, v_ref, gate_ref, mf_ref, mv_ref, lbl_ref, gn_ref = refs[:n_in]
        lb = _lower_bound(lbl_ref)
    o_ref, sout_ref = refs[n_in:n_in + 2]
    st_ref, qs_ref, ks_ref, bc_ref, qd_ref, qt_ref, kt_ref, kdec_ref, eb_ref = refs[n_in + 2:]
    trips = seq // (CHUNK * CHUNKS_PER_TRIP)
    stack = kt_ref.shape[0] // (seq // CHUNK)
    nt = (((1,), (1,)), ((), ()))
    tn = (((0,), (0,)), ((), ()))

    def chunk_rows(ci, u, n=CHUNK):
        return pl.ds(pl.multiple_of((ci * CHUNKS_PER_TRIP + u) * n, n), n)

    if gla:
        _, mk, mv, mg = _gla_inputs(mk_ref[...], mk_ref[...], mv_ref[...], mal_ref[...],
                                    wa_ref[...], ba_ref[...])
    else:
        _, mk, mv, mg = _hgrn_inputs(mf_ref[...], mf_ref[...], mv_ref[...], lb)
    mbc = jnp.dot(_tri(meta, meta), mg, precision=HIGHEST, preferred_element_type=F32)
    st_ref[...] = _state_update_t(mk, mv.astype(BF16), mbc, None)

    def pass_a(ci, carry):
        gmin, kmax = carry
        trip_rows = CHUNK * CHUNKS_PER_TRIP
        rows = pl.ds(pl.multiple_of(ci * trip_rows, trip_rows), trip_rows)
        if gla:
            q, k, _, g = _gla_inputs(q_ref[rows, :], k_ref[rows, :], None, al_ref[rows, :],
                                     wa_ref[...], ba_ref[...])
        else:
            q, k, _, g = _hgrn_inputs(q_ref[rows, :], f_ref[rows, :], None, lb)
        qs_ref[rows, :] = q
        ks_ref[rows, :] = k
        for u in range(CHUNKS_PER_TRIP):
            sl = slice(u * CHUNK, (u + 1) * CHUNK)
            bc = _cumsum_chunk(g[sl])
            qd, qt, kt, kdec = _rescaled_operands(q[sl], k[sl], bc)
            bc_ref[chunk_rows(ci, u), :] = bc
            qd_ref[chunk_rows(ci, u), :] = qd
            qt_ref[chunk_rows(ci, u), :] = qt
            kt_ref[chunk_rows(ci, u, stack), :] = kt
            kdec_ref[chunk_rows(ci, u), :] = kdec
            eb_ref[ci * CHUNKS_PER_TRIP + u] = jnp.broadcast_to(
                jnp.exp(bc[CHUNK - 1:CHUNK, :]), (SUBLANES, DK))
        return (jnp.minimum(gmin, jnp.min(g, axis=0, keepdims=True)),
                jnp.maximum(kmax, jnp.max(jnp.abs(k), axis=0, keepdims=True)))

    gmin, kmax = lax.fori_loop(0, trips, pass_a,
                               (jnp.zeros((1, DK), F32), jnp.zeros((1, DK), F32)))
    in_range = (jnp.min(gmin) >= FAST_MIN_LOG_GATE) & (jnp.max(kmax) <= FAST_MAX_KEY)

    @pl.when(in_range)
    def _rescaled():
        mask = _stack_mask(CHUNK, DIAG_BLK)
        units = range(CHUNKS_PER_MXU_TRIP)

        def rows_of(c, n=CHUNK):
            return pl.ds(pl.multiple_of(c * n, n), n)

        def pass_bc(ci, carry):
            cs = [ci * CHUNKS_PER_MXU_TRIP + u for u in units]
            vb = [v_ref[rows_of(c), :].astype(BF16) for c in cs]
            a = [lax.dot_general(qt_ref[rows_of(c), :], kt_ref[rows_of(c, stack), :], nt,
                                 preferred_element_type=F32) for c in cs]
            am = [(x * mask).astype(BF16) for x in a]
            vt = [jnp.concatenate([x[:(i + 1) * DIAG_BLK] for i in range(CHUNK // DIAG_BLK)], axis=0)
                  for x in vb]
            oi = [jnp.dot(am[u], vt[u], preferred_element_type=F32) for u in units]
            inc = [lax.dot_general(vb[u], kdec_ref[rows_of(cs[u]), :], tn,
                                   preferred_element_type=F32) for u in units]
            sts = [st_ref[...]]
            for u in units:
                sts.append(sts[-1] * eb_ref[cs[u]][0:1, :] + inc[u])
            o = [oi[u] + lax.dot_general(qd_ref[rows_of(cs[u]), :], sts[u].astype(BF16), nt,
                                         preferred_element_type=F32) for u in units]
            for u in units:
                o_ref[rows_of(cs[u]), :] = _head_out(o[u], gn_ref[...], gate_ref[rows_of(cs[u]), :])
            st_ref[...] = sts[-1]
            return carry

        lax.fori_loop(0, seq // (CHUNK * CHUNKS_PER_MXU_TRIP), pass_bc, 0)

    @pl.when(jnp.logical_not(in_range))
    def _exact():
        mask = _off_mask(CHUNK, DIAG_BLK)

        def body(ci, carry):
            st = st_ref[...]
            for u in range(CHUNKS_PER_TRIP):
                rows = chunk_rows(ci, u)
                o, st = _chunk_exact(qs_ref[rows, :], ks_ref[rows, :], v_ref[rows, :],
                                     bc_ref[rows, :], st, mask)
                o_ref[rows, :] = _head_out(o, gn_ref[...], gate_ref[rows, :])
            st_ref[...] = st
            return carry

        lax.fori_loop(0, trips, body, 0)

    sout_ref[...] = st_ref[...].T


def _sample_scan_kernel(*refs, gla, steps):
    if gla:
        (q_ref, k_ref, v_ref, gate_ref, al_ref, wa_ref, ba_ref, gn_ref, s0_ref,
         o_ref, sout_ref) = refs
        q, k, v, g = _gla_inputs(q_ref[...], k_ref[...], v_ref[...], al_ref[...],
                                 wa_ref[...], ba_ref[...])
    else:
        (q_ref, f_ref, v_ref, gate_ref, lbl_ref, gn_ref, s0_ref, o_ref, sout_ref) = refs
        q, k, v, g = _hgrn_inputs(q_ref[...], f_ref[...], v_ref[...], _lower_bound(lbl_ref))

    rows = SAMPLE_SEQS * steps
    bc = jnp.dot(_tri(rows, steps), g, precision=HIGHEST, preferred_element_type=F32)
    qd = (q * jnp.exp(bc)).astype(BF16)
    inter = []
    for b in range(SAMPLE_SEQS):
        sl = slice(b * steps, (b + 1) * steps)
        s_prev = s0_ref[b]
        inter.append(jnp.dot(qd[sl], s_prev.astype(BF16), preferred_element_type=F32))
        last = (b + 1) * steps - 1
        sout_ref[b] = _state_update(k[sl], v[sl], g[sl], bc[sl], bc[last:last + 1, :], s_prev)
    o = jnp.concatenate(inter, axis=0) + _diag(q, k, v, bc, steps)
    o_ref[...] = _head_out(o, gn_ref[...], gate_ref[...])


def _col(width, block):
    return block * LANES // width


def _prompt_scan(p, pm, *, gla, batch, seq, small, name):
    heads, dv = (H_B, DV_B) if gla else (H_A, DV_A)
    n_chunks = seq // CHUNK
    blocks = CHUNK // DIAG_BLK
    stack = DIAG_BLK * blocks * (blocks + 1) // 2
    tok = lambda width, c0, per_head: pl.BlockSpec(
        (seq, width), lambda b, h: (b, _col(width, c0) + h * per_head))
    met = lambda width, c0, per_head: pl.BlockSpec(
        (N_META, width), lambda b, h: (0, _col(width, c0) + h * per_head))
    vec = lambda width: pl.BlockSpec((1, width), lambda b, h: (0, h))
    if gla:
        wa, ba, gn = small
        in_specs = [tok(DK, COL_QB, 1), tok(DK, COL_KB, 1), tok(dv, COL_VB, 1), tok(dv, COL_RB, 1),
                    tok(LANES, COL_AL, 0),
                    met(DK, COL_KB, 1), met(dv, COL_VB, 1), met(LANES, COL_AL, 0),
                    pl.BlockSpec((LANES, DK), lambda b, h: (0, h)), vec(DK), vec(dv)]
        args = (p, p, p, p, p, pm, pm, pm, wa, ba, gn)
    else:
        lbl, gn = small
        in_specs = [tok(DK, COL_QA, 1), tok(DK, COL_FA, 1), tok(dv, COL_IA, 1), tok(dv, COL_GA, 1),
                    met(DK, COL_FA, 1), met(dv, COL_IA, 1),
                    pl.BlockSpec((2, DK), lambda b, h: (0, h)), vec(dv)]
        args = (p, p, p, p, pm, pm, lbl, gn)
    return pl.pallas_call(
        functools.partial(_prompt_scan_kernel, gla=gla, seq=seq, meta=N_META),
        grid=(batch, heads),
        in_specs=in_specs,
        out_specs=[pl.BlockSpec((seq, dv), lambda b, h: (b, h)),
                   pl.BlockSpec((None, None, DK, dv), lambda b, h: (b, h, 0, 0))],
        out_shape=[jax.ShapeDtypeStruct((batch * seq, heads * dv), BF16),
                   jax.ShapeDtypeStruct((batch, heads, DK, dv), F32)],
        scratch_shapes=[
            pltpu.VMEM((dv, DK), F32),
            pltpu.VMEM((seq, DK), F32),
            pltpu.VMEM((seq, DK), F32),
            pltpu.VMEM((seq, DK), F32),
            pltpu.VMEM((seq, DK), BF16),
            pltpu.VMEM((seq, DK), BF16),
            pltpu.VMEM((n_chunks * stack, DK), BF16),
            pltpu.VMEM((seq, DK), BF16),
            pltpu.VMEM((n_chunks, SUBLANES, DK), F32),
        ],
        compiler_params=_params("parallel", "parallel"),
        name=name,
    )(*args)


def _sample_scan(p, s0, *, gla, batch, steps, small, name):
    heads, dv = (H_B, DV_B) if gla else (H_A, DV_A)
    rows = SAMPLE_SEQS * steps
    tok = lambda width, c0, per_head: pl.BlockSpec(
        (rows, width), lambda b, h: (b, _col(width, c0) + h * per_head))
    vec = lambda width: pl.BlockSpec((1, width), lambda b, h: (0, h))
    state = pl.BlockSpec((SAMPLE_SEQS, None, DK, dv), lambda b, h: (b, h, 0, 0))
    if gla:
        wa, ba, gn = small
        in_specs = [tok(DK, COL_QB, 1), tok(DK, COL_KB, 1), tok(dv, COL_VB, 1), tok(dv, COL_RB, 1),
                    tok(LANES, COL_AL, 0),
                    pl.BlockSpec((LANES, DK), lambda b, h: (0, h)), vec(DK), vec(dv), state]
        args = (p, p, p, p, p, wa, ba, gn, s0)
    else:
        lbl, gn = small
        in_specs = [tok(DK, COL_QA, 1), tok(DK, COL_FA, 1), tok(dv, COL_IA, 1), tok(dv, COL_GA, 1),
                    pl.BlockSpec((2, DK), lambda b, h: (0, h)), vec(dv), state]
        args = (p, p, p, p, lbl, gn, s0)
    return pl.pallas_call(
        functools.partial(_sample_scan_kernel, gla=gla, steps=steps),
        grid=(batch // SAMPLE_SEQS, heads),
        in_specs=in_specs,
        out_specs=[pl.BlockSpec((rows, dv), lambda b, h: (b, h)), state],
        out_shape=[jax.ShapeDtypeStruct((batch * steps, heads * dv), BF16),
                   jax.ShapeDtypeStruct((batch, heads, DK, dv), F32)],
        compiler_params=_params("parallel", "parallel"),
        name=name,
    )(*args)


def kernel(x_prompt, x_sample, state_hgrn, state_gla, meta_tokens, lb_logits, ffn1_norm, w_ffn1_in,
           w_ffn1_out, mix_norm, w_in, w_alpha_up, b_alpha, gnorm_a, gnorm_b, w_out, ffn2_norm,
           w_ffn2_in, w_ffn2_out, final_norm):
    batch, seq, _ = x_prompt.shape
    dec_batch, steps, _ = x_sample.shape

    w1i, w1o = w_ffn1_in[0].astype(BF16), w_ffn1_out[0].astype(BF16)
    w2i, w2o = w_ffn2_in[0].astype(BF16), w_ffn2_out[0].astype(BF16)
    wip = jnp.pad(w_in[0], ((0, 0), (0, NP_PAD - D_IN_PROJ))).astype(BF16)
    wop = w_out[0].astype(BF16)
    wa = jnp.pad(w_alpha_up[0], ((0, LANES - ALPHA_RANK), (0, 0))).astype(BF16)
    n1, nm, n2, nf = ffn1_norm, mix_norm, ffn2_norm, final_norm[None]
    small_a = (lb_logits, gnorm_a)
    small_b = (wa, b_alpha, gnorm_b)

    def front(x, tm, tag):
        h1 = _ffn(x, n1, w1i, w1o, nf, tm=tm, apply_final_norm=False, name=f"ffn1_{tag}")
        return h1, _in_proj(h1, nm, wip, tm=tm, name=f"inproj_{tag}")

    def back(oa, ob, h1, tm, tag):
        h2 = _out_proj(oa, ob, h1, wop, tm=tm, name=f"outproj_{tag}")
        return _ffn(h2, n2, w2i, w2o, nf, tm=tm, apply_final_norm=True, name=f"ffn2_{tag}")

    _, pm = front(meta_tokens, N_META, "meta")

    h1p, pp = front(x_prompt.reshape(batch * seq, D_MODEL), 512, "prompt")
    oa_p, sa_p = _prompt_scan(pp, pm, gla=False, batch=batch, seq=seq, small=small_a,
                              name="scan_hgrn_prompt")
    ob_p, sb_p = _prompt_scan(pp, pm, gla=True, batch=batch, seq=seq, small=small_b,
                              name="scan_gla_prompt")
    y_p = back(oa_p, ob_p, h1p, 512, "prompt")

    h1s, ps = front(x_sample.reshape(dec_batch * steps, D_MODEL), 512, "sample")
    oa_s, sa_s = _sample_scan(ps, state_hgrn[0], gla=False, batch=dec_batch, steps=steps,
                              small=small_a, name="scan_hgrn_sample")
    ob_s, sb_s = _sample_scan(ps, state_gla[0], gla=True, batch=dec_batch, steps=steps,
                              small=small_b, name="scan_gla_sample")
    y_s = back(oa_s, ob_s, h1s, 512, "sample")

    return (y_p.reshape(batch, seq, D_MODEL), y_s.reshape(dec_batch, steps, D_MODEL),
            sa_p[None], sb_p[None], sa_s[None], sb_s[None])
```

```python
import functools
import itertools

import jax
import jax.numpy as jnp
from jax import lax
from jax.experimental import pallas as pl
from jax.experimental.pallas import tpu as pltpu

F32 = jnp.float32
BF16 = jnp.bfloat16
HIGHEST = lax.Precision.HIGHEST

D_MODEL = 2048
N_META = 16
DK = 128
H_A, DV_A = 8, 128
H_B, DV_B = 4, 256
D_HALF = H_A * DV_A
ALPHA_RANK = 16
GATE_TEMP = 16.0
D_FF = 5632
CHUNK = 64
EPS = 1e-6
IN_PROJ_WIDTHS = (D_HALF, D_HALF, D_HALF, D_HALF, H_B * DK, H_B * DK, D_HALF, D_HALF, ALPHA_RANK)

LANES = 128
SUBLANES = 8
VMEM_LIMIT_BYTES = 56 * 1024 * 1024

COL_QA, COL_IA, COL_GA, COL_QB, COL_KB, COL_VB, COL_RB = 0, 8, 16, 24, 28, 32, 40
P16_COLS = 3 * D_HALF + 2 * H_B * DK + 2 * D_HALF

TF = 512
TN_PROJ = 512
TM_PROJ = 1024
N16_TILES = P16_COLS // TN_PROJ
DIAG_BLK = 16
SAMPLE_SEQS = 8
CHUNKS_PER_TRIP = 4
CHUNKS_PER_MXU_TRIP = 8

FAST_MIN_LOG_GATE = -60.0 / (DIAG_BLK - 1)
FAST_MAX_KEY = 1e10


def _rms(x, w):
    return x * lax.rsqrt(jnp.mean(x * x, axis=-1, keepdims=True) + EPS) * w


def _silu(x):
    return x * jax.nn.sigmoid(x)


def _params(*sem):
    return pltpu.CompilerParams(dimension_semantics=sem, vmem_limit_bytes=VMEM_LIMIT_BYTES)


def _ffn_kernel(h_ref, nw_ref, wg_ref, wu_ref, wo_ref, fnw_ref, out_ref, u_ref, acc_ref, *,
                apply_final_norm):
    j = pl.program_id(1)

    @pl.when(j == 0)
    def _init():
        u_ref[...] = _rms(h_ref[...], nw_ref[...]).astype(BF16)
        acc_ref[...] = jnp.zeros_like(acc_ref)

    u = u_ref[...]
    g = jnp.dot(u, wg_ref[...], preferred_element_type=F32)
    up = jnp.dot(u, wu_ref[...], preferred_element_type=F32)
    a = (_silu(g) * up).astype(BF16)
    acc_ref[...] += jnp.dot(a, wo_ref[...], preferred_element_type=F32)

    @pl.when(j == pl.num_programs(1) - 1)
    def _finish():
        y = h_ref[...] + 0.5 * acc_ref[...]
        if apply_final_norm:
            y = _rms(y, fnw_ref[...])
        out_ref[...] = y


def _ffn(h, norm_w, w_in, w_out, final_norm_w, *, tm, apply_final_norm, name):
    m = h.shape[0]
    nf = D_FF // TF
    return pl.pallas_call(
        functools.partial(_ffn_kernel, apply_final_norm=apply_final_norm),
        grid=(m // tm, nf),
        in_specs=[
            pl.BlockSpec((tm, D_MODEL), lambda i, j: (i, 0)),
            pl.BlockSpec((1, D_MODEL), lambda i, j: (0, 0)),
            pl.BlockSpec((D_MODEL, TF), lambda i, j: (0, j)),
            pl.BlockSpec((D_MODEL, TF), lambda i, j: (0, j + nf)),
            pl.BlockSpec((TF, D_MODEL), lambda i, j: (j, 0)),
            pl.BlockSpec((1, D_MODEL), lambda i, j: (0, 0)),
        ],
        out_specs=pl.BlockSpec((tm, D_MODEL), lambda i, j: (i, 0)),
        out_shape=jax.ShapeDtypeStruct((m, D_MODEL), F32),
        scratch_shapes=[pltpu.VMEM((tm, D_MODEL), BF16), pltpu.VMEM((tm, D_MODEL), F32)],
        compiler_params=_params("parallel", "arbitrary"),
        name=name,
    )(h, norm_w, w_in, w_in, w_out, final_norm_w)


def _proj_kernel(h_ref, nw_ref, w_ref, wal_ref, p16_ref, pf_ref, pa_ref, u_ref):
    j = pl.program_id(1)

    @pl.when(j == 0)
    def _init():
        u = _rms(h_ref[...], nw_ref[...]).astype(BF16)
        u_ref[...] = u
        pa_ref[...] = jnp.dot(u, wal_ref[...], preferred_element_type=F32)

    acc = jnp.dot(u_ref[...], w_ref[...], preferred_element_type=F32)

    @pl.when(j < N16_TILES)
    def _narrow():
        p16_ref[...] = acc.astype(BF16)

    @pl.when(j >= N16_TILES)
    def _wide():
        pf_ref[...] = acc


def _in_proj(h, norm_w, w, w_al, *, tm, name):
    m = h.shape[0]
    return pl.pallas_call(
        _proj_kernel,
        grid=(m // tm, (P16_COLS + D_HALF) // TN_PROJ),
        in_specs=[
            pl.BlockSpec((tm, D_MODEL), lambda i, j: (i, 0)),
            pl.BlockSpec((1, D_MODEL), lambda i, j: (0, 0)),
            pl.BlockSpec((D_MODEL, TN_PROJ), lambda i, j: (0, j)),
            pl.BlockSpec((D_MODEL, LANES), lambda i, j: (0, 0)),
        ],
        out_specs=[
            pl.BlockSpec((tm, TN_PROJ), lambda i, j: (i, jnp.minimum(j, N16_TILES - 1))),
            pl.BlockSpec((tm, TN_PROJ), lambda i, j: (i, jnp.maximum(j - N16_TILES, 0))),
            pl.BlockSpec((tm, LANES), lambda i, j: (i, 0)),
        ],
        out_shape=[jax.ShapeDtypeStruct((m, P16_COLS), BF16),
                   jax.ShapeDtypeStruct((m, D_HALF), F32),
                   jax.ShapeDtypeStruct((m, LANES), F32)],
        scratch_shapes=[pltpu.VMEM((tm, D_MODEL), BF16)],
        compiler_params=_params("parallel", "arbitrary"),
        name=name,
    )(h, norm_w, w, w_al)


def _out_proj_kernel(oa_ref, ob_ref, h_ref, wa_ref, wb_ref, out_ref):
    out_ref[...] = (h_ref[...]
                    + jnp.dot(oa_ref[...], wa_ref[...], preferred_element_type=F32)
                    + jnp.dot(ob_ref[...], wb_ref[...], preferred_element_type=F32))


def _out_proj(oa, ob, h, w, *, tm, name):
    m = h.shape[0]
    return pl.pallas_call(
        _out_proj_kernel,
        grid=(m // tm,),
        in_specs=[
            pl.BlockSpec((tm, D_HALF), lambda i: (i, 0)),
            pl.BlockSpec((tm, D_HALF), lambda i: (i, 0)),
            pl.BlockSpec((tm, D_MODEL), lambda i: (i, 0)),
            pl.BlockSpec((D_HALF, D_MODEL), lambda i: (0, 0)),
            pl.BlockSpec((D_HALF, D_MODEL), lambda i: (1, 0)),
        ],
        out_specs=pl.BlockSpec((tm, D_MODEL), lambda i: (i, 0)),
        out_shape=jax.ShapeDtypeStruct((m, D_MODEL), F32),
        compiler_params=_params("parallel"),
        name=name,
    )(oa, ob, h, w, w)


def _tri(c, seq_len):
    r = lax.broadcasted_iota(jnp.int32, (c, c), 0)
    s = lax.broadcasted_iota(jnp.int32, (c, c), 1)
    return jnp.where((r >= s) & ((r ^ s) < seq_len), 1.0, 0.0).astype(F32)


def _off_mask(c, blk):
    nb = c // blk
    rows = nb * (nb - 1) // 2 * blk
    r = lax.broadcasted_iota(jnp.int32, (c, rows), 0) >> (blk.bit_length() - 1)
    s = lax.broadcasted_iota(jnp.int32, (c, rows), 1)
    seg = jnp.zeros((c, rows), jnp.int32)
    for i in range(1, nb):
        seg = seg + jnp.where(s >= blk * i * (i - 1) // 2, 1, 0)
    return jnp.where(r == seg, 1.0, 0.0).astype(F32)


def _diag(q, k, v, bc, blk):
    c = q.shape[0]
    row = lax.broadcasted_iota(jnp.int32, (blk, 1), 0)
    outs = []
    for i in range(c // blk):
        sl = slice(i * blk, (i + 1) * blk)
        qi, ki, vi, bi = q[sl], k[sl], v[sl].astype(F32), bc[sl]
        acc = jnp.zeros((blk, v.shape[1]), F32)
        for s in range(blk):
            e = jnp.exp(jnp.minimum(bi - bi[s:s + 1, :], 0.0))
            col = jnp.sum(qi * ki[s:s + 1, :] * e, axis=-1, keepdims=True)
            col = jnp.where(row >= s, col, 0.0)
            acc = acc + col * vi[s:s + 1, :]
        outs.append(acc)
    return outs[0] if len(outs) == 1 else jnp.concatenate(outs, axis=0)


def _offdiag(q, k, v, bc, blk, mask):
    c = q.shape[0]
    nb = c // blk
    qs, ks, vs = [jnp.zeros((blk, DK), F32)], [], []
    for i in range(1, nb):
        lo = i * blk
        ref = bc[lo:lo + 1, :]
        qs.append(q[lo:lo + blk] * jnp.exp(bc[lo:lo + blk] - ref))
        ks.append(k[:lo] * jnp.exp(ref - bc[:lo]))
        vs.append(v[:lo])
    qt = jnp.concatenate(qs, axis=0).astype(BF16)
    kt = jnp.concatenate(ks, axis=0).astype(BF16)
    vt = jnp.concatenate(vs, axis=0).astype(BF16)
    a = lax.dot_general(qt, kt, (((1,), (1,)), ((), ())), preferred_element_type=F32)
    return jnp.dot((a * mask).astype(BF16), vt, preferred_element_type=F32)


def _state_update(k, v, g, bc, blast, s_prev):
    c, dv = v.shape
    kdec = (k * jnp.exp(blast - bc)).astype(BF16)
    upd = lax.dot_general(kdec, v.astype(BF16), (((0,), (0,)), ((), ())),
                          preferred_element_type=F32)
    if s_prev is None:
        return upd
    tot = lax.dot_general(g, jnp.ones((c, dv), F32), (((0,), (0,)), ((), ())),
                          precision=HIGHEST, preferred_element_type=F32)
    return jnp.exp(tot) * s_prev + upd


def _hgrn_inputs(qa, fa, ia, lb):
    fg = lb + (1.0 - lb) * jax.nn.sigmoid(fa)
    return _silu(qa.astype(F32)) * (DK ** -0.5), 1.0 - fg, ia, jnp.log(fg)


def _gla_inputs(qb, kb, vb, al, wa, ba):
    alpha = jnp.dot(al.astype(BF16), wa, preferred_element_type=F32) + ba
    logsig = jnp.minimum(alpha, 0.0) - jnp.log1p(jnp.exp(-jnp.abs(alpha)))
    return qb.astype(F32) * (DK ** -0.5), kb.astype(F32), vb, logsig / GATE_TEMP


def _head_out(o, gn, gate):
    return (_rms(o, gn) * _silu(gate.astype(F32))).astype(BF16)


def _lower_bound(lbl_ref):
    x = lbl_ref[...]
    e = jnp.exp(x - jnp.max(x, axis=0, keepdims=True))
    return e[0:1, :] / jnp.sum(e, axis=0, keepdims=True)


def _stack_mask(c, blk):
    nb = c // blk
    rows = blk * nb * (nb + 1) // 2
    t = lax.broadcasted_iota(jnp.int32, (c, rows), 0)
    r = lax.broadcasted_iota(jnp.int32, (c, rows), 1)
    seg = jnp.zeros((c, rows), jnp.int32)
    off = jnp.zeros((c, rows), jnp.int32)
    for i in range(1, nb):
        start = blk * i * (i + 1) // 2
        seg = seg + jnp.where(r >= start, 1, 0)
        off = jnp.where(r >= start, start, off)
    keep = ((t >> (blk.bit_length() - 1)) == seg) & (r - off <= t)
    return jnp.where(keep, 1.0, 0.0).astype(F32)


def _state_update_t(k, vb, bc, st):
    c = k.shape[0]
    blast = bc[c - 1:c, :]
    kdec = (k * jnp.exp(blast - bc)).astype(BF16)
    upd = lax.dot_general(vb, kdec, (((0,), (0,)), ((), ())), preferred_element_type=F32)
    if st is None:
        return upd
    return st * jnp.exp(blast) + upd


def _cumsum_chunk(g):
    row = lax.broadcasted_iota(jnp.int32, (SUBLANES, DK), 0)
    out, carry = [], None
    for j in range(CHUNK // SUBLANES):
        p = g[j * SUBLANES:(j + 1) * SUBLANES]
        for s in (1, 2, 4):
            p = p + jnp.where(row >= s, pltpu.roll(p, s, axis=0), 0.0)
        if carry is not None:
            p = p + carry
        carry = p[SUBLANES - 1:SUBLANES, :]
        out.append(p)
    return jnp.concatenate(out, axis=0)


def _rescaled_operands(q, k, bc):
    c, blk = CHUNK, DIAG_BLK
    qs, ks = [], []
    for i in range(c // blk):
        lo, hi = i * blk, (i + 1) * blk
        ref = bc[lo:lo + 1, :]
        qs.append(q[lo:hi] * jnp.exp(bc[lo:hi] - ref))
        ks.append(k[:hi] * jnp.exp(ref - bc[:hi]))
    return ((q * jnp.exp(bc)).astype(BF16), jnp.concatenate(qs, axis=0).astype(BF16),
            jnp.concatenate(ks, axis=0).astype(BF16),
            (k * jnp.exp(bc[c - 1:c, :] - bc)).astype(BF16))


def _chunk_exact(q, k, v, bc, st, mask):
    o = lax.dot_general((q * jnp.exp(bc)).astype(BF16), st.astype(BF16),
                        (((1,), (1,)), ((), ())), preferred_element_type=F32)
    o = o + _offdiag(q, k, v, bc, DIAG_BLK, mask) + _diag(q, k, v, bc, DIAG_BLK)
    return o, _state_update_t(k, v.astype(BF16), bc, st)


def _prompt_scan_kernel(*refs, gla, seq, meta):
    n_in = 11 if gla else 8
    if gla:
        q_ref, k_ref, v_ref, gate_ref, al_ref, mk_ref, mv_ref, mal_ref, wa_ref, ba_ref, gn_ref = (
            refs[:n_in])
    else:
        q_ref, f_ref, v_ref, gate_ref, mf_ref, mv_ref, lbl_ref, gn_ref = refs[:n_in]
        lb = _lower_bound(lbl_ref)
    o_ref, sout_ref = refs[n_in:n_in + 2]
    st_ref, qs_ref, ks_ref, bc_ref, qd_ref, qt_ref, kt_ref, kdec_ref, eb_ref = refs[n_in + 2:]
    trips = seq // (CHUNK * CHUNKS_PER_TRIP)
    stack = kt_ref.shape[0] // (seq // CHUNK)
    nt = (((1,), (1,)), ((), ()))
    tn = (((0,), (0,)), ((), ()))

    def chunk_rows(ci, u, n=CHUNK):
        return pl.ds(pl.multiple_of((ci * CHUNKS_PER_TRIP + u) * n, n), n)

    if gla:
        _, mk, mv, mg = _gla_inputs(mk_ref[...], mk_ref[...], mv_ref[...], mal_ref[...],
                                    wa_ref[...], ba_ref[...])
    else:
        _, mk, mv, mg = _hgrn_inputs(mf_ref[...], mf_ref[...], mv_ref[...], lb)
    mbc = jnp.dot(_tri(meta, meta), mg, precision=HIGHEST, preferred_element_type=F32)
    st_ref[...] = _state_update_t(mk, mv.astype(BF16), mbc, None)

    def pass_a(ci, carry):
        gmin, kmax = carry
        trip_rows = CHUNK * CHUNKS_PER_TRIP
        rows = pl.ds(pl.multiple_of(ci * trip_rows, trip_rows), trip_rows)
        if gla:
            q, k, _, g = _gla_inputs(q_ref[rows, :], k_ref[rows, :], None, al_ref[rows, :],
                                     wa_ref[...], ba_ref[...])
        else:
            q, k, _, g = _hgrn_inputs(q_ref[rows, :], f_ref[rows, :], None, lb)
        qs_ref[rows, :] = q
        ks_ref[rows, :] = k
        for u in range(CHUNKS_PER_TRIP):
            sl = slice(u * CHUNK, (u + 1) * CHUNK)
            bc = _cumsum_chunk(g[sl])
            qd, qt, kt, kdec = _rescaled_operands(q[sl], k[sl], bc)
            bc_ref[chunk_rows(ci, u), :] = bc
            qd_ref[chunk_rows(ci, u), :] = qd
            qt_ref[chunk_rows(ci, u), :] = qt
            kt_ref[chunk_rows(ci, u, stack), :] = kt
            kdec_ref[chunk_rows(ci, u), :] = kdec
            eb_ref[ci * CHUNKS_PER_TRIP + u] = jnp.broadcast_to(
                jnp.exp(bc[CHUNK - 1:CHUNK, :]), (SUBLANES, DK))
        return (jnp.minimum(gmin, jnp.min(g, axis=0, keepdims=True)),
                jnp.maximum(kmax, jnp.max(jnp.abs(k), axis=0, keepdims=True)))

    gmin, kmax = lax.fori_loop(0, trips, pass_a,
                               (jnp.zeros((1, DK), F32), jnp.zeros((1, DK), F32)))
    in_range = (jnp.min(gmin) >= FAST_MIN_LOG_GATE) & (jnp.max(kmax) <= FAST_MAX_KEY)

    @pl.when(in_range)
    def _rescaled():
        mask = _stack_mask(CHUNK, DIAG_BLK)
        units = range(CHUNKS_PER_MXU_TRIP)

        def rows_of(c, n=CHUNK):
            return pl.ds(pl.multiple_of(c * n, n), n)

        def pass_bc(ci, carry):
            cs = [ci * CHUNKS_PER_MXU_TRIP + u for u in units]
            vb = [v_ref[rows_of(c), :].astype(BF16) for c in cs]
            a = [lax.dot_general(qt_ref[rows_of(c), :], kt_ref[rows_of(c, stack), :], nt,
                                 preferred_element_type=F32) for c in cs]
            am = [(x * mask).astype(BF16) for x in a]
            vt = [jnp.concatenate([x[:(i + 1) * DIAG_BLK] for i in range(CHUNK // DIAG_BLK)], axis=0)
                  for x in vb]
            oi = [jnp.dot(am[u], vt[u], preferred_element_type=F32) for u in units]
            inc = [lax.dot_general(vb[u], kdec_ref[rows_of(cs[u]), :], tn,
                                   preferred_element_type=F32) for u in units]
            sts = [st_ref[...]]
            for u in units:
                sts.append(sts[-1] * eb_ref[cs[u]][0:1, :] + inc[u])
            o = [oi[u] + lax.dot_general(qd_ref[rows_of(cs[u]), :], sts[u].astype(BF16), nt,
                                         preferred_element_type=F32) for u in units]
            for u in units:
                o_ref[rows_of(cs[u]), :] = _head_out(o[u], gn_ref[...], gate_ref[rows_of(cs[u]), :])
            st_ref[...] = sts[-1]
            return carry

        lax.fori_loop(0, seq // (CHUNK * CHUNKS_PER_MXU_TRIP), pass_bc, 0)

    @pl.when(jnp.logical_not(in_range))
    def _exact():
        mask = _off_mask(CHUNK, DIAG_BLK)

        def body(ci, carry):
            st = st_ref[...]
            for u in range(CHUNKS_PER_TRIP):
                rows = chunk_rows(ci, u)
                o, st = _chunk_exact(qs_ref[rows, :], ks_ref[rows, :], v_ref[rows, :],
                                     bc_ref[rows, :], st, mask)
                o_ref[rows, :] = _head_out(o, gn_ref[...], gate_ref[rows, :])
            st_ref[...] = st
            return carry

        lax.fori_loop(0, trips, body, 0)

    sout_ref[...] = st_ref[...].T


def _sample_scan_kernel(*refs, gla, steps):
    if gla:
        (q_ref, k_ref, v_ref, gate_ref, al_ref, wa_ref, ba_ref, gn_ref, s0_ref,
         o_ref, sout_ref) = refs
        q, k, v, g = _gla_inputs(q_ref[...], k_ref[...], v_ref[...], al_ref[...],
                                 wa_ref[...], ba_ref[...])
    else:
        (q_ref, f_ref, v_ref, gate_ref, lbl_ref, gn_ref, s0_ref, o_ref, sout_ref) = refs
        q, k, v, g = _hgrn_inputs(q_ref[...], f_ref[...], v_ref[...], _lower_bound(lbl_ref))
    v = v.astype(F32)

    rows = SAMPLE_SEQS * steps
    bc = jnp.dot(_tri(rows, steps), g, precision=HIGHEST, preferred_element_type=F32)
    qd = (q * jnp.exp(bc)).astype(BF16)
    inter = []
    for b in range(SAMPLE_SEQS):
        sl = slice(b * steps, (b + 1) * steps)
        s_prev = s0_ref[b]
        inter.append(jnp.dot(qd[sl], s_prev.astype(BF16), preferred_element_type=F32))
        last = (b + 1) * steps - 1
        sout_ref[b] = _state_update(k[sl], v[sl], g[sl], bc[sl], bc[last:last + 1, :], s_prev)
    o = jnp.concatenate(inter, axis=0) + _diag(q, k, v, bc, steps)
    o_ref[...] = _head_out(o, gn_ref[...], gate_ref[...])


def _col(width, block):
    return block * LANES // width


def _prompt_scan(p, pm, *, gla, batch, seq, small, name):
    heads, dv = (H_B, DV_B) if gla else (H_A, DV_A)
    n_chunks = seq // CHUNK
    blocks = CHUNK // DIAG_BLK
    stack = DIAG_BLK * blocks * (blocks + 1) // 2
    p16, pf, pa = p
    pm16, pmf, pma = pm
    tok = lambda width, c0, per_head: pl.BlockSpec(
        (seq, width), lambda b, h: (b, _col(width, c0) + h * per_head))
    met = lambda width, c0, per_head: pl.BlockSpec(
        (N_META, width), lambda b, h: (0, _col(width, c0) + h * per_head))
    vec = lambda width: pl.BlockSpec((1, width), lambda b, h: (0, h))
    if gla:
        wa, ba, gn = small
        in_specs = [tok(DK, COL_QB, 1), tok(DK, COL_KB, 1), tok(dv, COL_VB, 1), tok(dv, COL_RB, 1),
                    tok(LANES, 0, 0),
                    met(DK, COL_KB, 1), met(dv, COL_VB, 1), met(LANES, 0, 0),
                    pl.BlockSpec((LANES, DK), lambda b, h: (0, h)), vec(DK), vec(dv)]
        args = (p16, p16, p16, p16, pa, pm16, pm16, pma, wa, ba, gn)
    else:
        lbl, gn = small
        in_specs = [tok(DK, COL_QA, 1), tok(DK, 0, 1), tok(dv, COL_IA, 1), tok(dv, COL_GA, 1),
                    met(DK, 0, 1), met(dv, COL_IA, 1),
                    pl.BlockSpec((2, DK), lambda b, h: (0, h)), vec(dv)]
        args = (p16, pf, p16, p16, pmf, pm16, lbl, gn)
    return pl.pallas_call(
        functools.partial(_prompt_scan_kernel, gla=gla, seq=seq, meta=N_META),
        grid=(batch, heads),
        in_specs=in_specs,
        out_specs=[pl.BlockSpec((seq, dv), lambda b, h: (b, h)),
                   pl.BlockSpec((None, None, DK, dv), lambda b, h: (b, h, 0, 0))],
        out_shape=[jax.ShapeDtypeStruct((batch * seq, heads * dv), BF16),
                   jax.ShapeDtypeStruct((batch, heads, DK, dv), F32)],
        scratch_shapes=[
            pltpu.VMEM((dv, DK), F32),
            pltpu.VMEM((seq, DK), F32),
            pltpu.VMEM((seq, DK), F32),
            pltpu.VMEM((seq, DK), F32),
            pltpu.VMEM((seq, DK), BF16),
            pltpu.VMEM((seq, DK), BF16),
            pltpu.VMEM((n_chunks * stack, DK), BF16),
            pltpu.VMEM((seq, DK), BF16),
            pltpu.VMEM((n_chunks, SUBLANES, DK), F32),
        ],
        compiler_params=_params("parallel", "parallel"),
        name=name,
    )(*args)


def _sample_scan(p, s0, *, gla, batch, steps, small, name):
    heads, dv = (H_B, DV_B) if gla else (H_A, DV_A)
    rows = SAMPLE_SEQS * steps
    p16, pf, pa = p
    tok = lambda width, c0, per_head: pl.BlockSpec(
        (rows, width), lambda b, h: (b, _col(width, c0) + h * per_head))
    vec = lambda width: pl.BlockSpec((1, width), lambda b, h: (0, h))
    state = pl.BlockSpec((SAMPLE_SEQS, None, DK, dv), lambda b, h: (b, h, 0, 0))
    if gla:
        wa, ba, gn = small
        in_specs = [tok(DK, COL_QB, 1), tok(DK, COL_KB, 1), tok(dv, COL_VB, 1), tok(dv, COL_RB, 1),
                    tok(LANES, 0, 0),
                    pl.BlockSpec((LANES, DK), lambda b, h: (0, h)), vec(DK), vec(dv), state]
        args = (p16, p16, p16, p16, pa, wa, ba, gn, s0)
    else:
        lbl, gn = small
        in_specs = [tok(DK, COL_QA, 1), tok(DK, 0, 1), tok(dv, COL_IA, 1), tok(dv, COL_GA, 1),
                    pl.BlockSpec((2, DK), lambda b, h: (0, h)), vec(dv), state]
        args = (p16, pf, p16, p16, lbl, gn, s0)
    return pl.pallas_call(
        functools.partial(_sample_scan_kernel, gla=gla, steps=steps),
        grid=(batch // SAMPLE_SEQS, heads),
        in_specs=in_specs,
        out_specs=[pl.BlockSpec((rows, dv), lambda b, h: (b, h)), state],
        out_shape=[jax.ShapeDtypeStruct((batch * steps, heads * dv), BF16),
                   jax.ShapeDtypeStruct((batch, heads, DK, dv), F32)],
        compiler_params=_params("parallel", "parallel"),
        name=name,
    )(*args)


def kernel(x_prompt, x_sample, state_hgrn, state_gla, meta_tokens, lb_logits, ffn1_norm, w_ffn1_in,
           w_ffn1_out, mix_norm, w_in, w_alpha_up, b_alpha, gnorm_a, gnorm_b, w_out, ffn2_norm,
           w_ffn2_in, w_ffn2_out, final_norm):
    batch, seq, _ = x_prompt.shape
    dec_batch, steps, _ = x_sample.shape

    w1i, w1o = w_ffn1_in[0].astype(BF16), w_ffn1_out[0].astype(BF16)
    w2i, w2o = w_ffn2_in[0].astype(BF16), w_ffn2_out[0].astype(BF16)
    q_a, f_a, i_a, g_a, q_b, k_b, v_b, r_b, a_low = jnp.split(
        w_in[0], list(itertools.accumulate(IN_PROJ_WIDTHS))[:-1], axis=1)
    wip = jnp.concatenate([q_a, i_a, g_a, q_b, k_b, v_b, r_b, f_a], axis=1).astype(BF16)
    wal = jnp.pad(a_low, ((0, 0), (0, LANES - ALPHA_RANK))).astype(BF16)
    wop = w_out[0].astype(BF16)
    wa = jnp.pad(w_alpha_up[0], ((0, LANES - ALPHA_RANK), (0, 0))).astype(BF16)
    n1, nm, n2, nf = ffn1_norm, mix_norm, ffn2_norm, final_norm[None]
    small_a = (lb_logits, gnorm_a)
    small_b = (wa, b_alpha, gnorm_b)

    def front(x, tm, tag):
        h1 = _ffn(x, n1, w1i, w1o, nf, tm=tm, apply_final_norm=False, name=f"ffn1_{tag}")
        return h1, _in_proj(h1, nm, wip, wal, tm=min(h1.shape[0], TM_PROJ), name=f"inproj_{tag}")

    def back(oa, ob, h1, tm, tag):
        h2 = _out_proj(oa, ob, h1, wop, tm=tm, name=f"outproj_{tag}")
        return _ffn(h2, n2, w2i, w2o, nf, tm=tm, apply_final_norm=True, name=f"ffn2_{tag}")

    _, pm = front(meta_tokens, N_META, "meta")

    h1p, pp = front(x_prompt.reshape(batch * seq, D_MODEL), 512, "prompt")
    oa_p, sa_p = _prompt_scan(pp, pm, gla=False, batch=batch, seq=seq, small=small_a,
                              name="scan_hgrn_prompt")
    ob_p, sb_p = _prompt_scan(pp, pm, gla=True, batch=batch, seq=seq, small=small_b,
                              name="scan_gla_prompt")
    y_p = back(oa_p, ob_p, h1p, 512, "prompt")

    h1s, ps = front(x_sample.reshape(dec_batch * steps, D_MODEL), 512, "sample")
    oa_s, sa_s = _sample_scan(ps, state_hgrn[0], gla=False, batch=dec_batch, steps=steps,
                              small=small_a, name="scan_hgrn_sample")
    ob_s, sb_s = _sample_scan(ps, state_gla[0], gla=True, batch=dec_batch, steps=steps,
                              small=small_b, name="scan_gla_sample")
    y_s = back(oa_s, ob_s, h1s, 512, "sample")

    return (y_p.reshape(batch, seq, D_MODEL), y_s.reshape(dec_batch, steps, D_MODEL),
            sa_p[None], sb_p[None], sa_s[None], sb_s[None])
```

```python
import functools

import jax
import jax.numpy as jnp
from jax import lax
from jax.experimental import pallas as pl
from jax.experimental.pallas import tpu as pltpu

F32 = jnp.float32
BF16 = jnp.bfloat16
HIGHEST = lax.Precision.HIGHEST

D_MODEL = 2048
N_META = 16
DK = 128
H_A, DV_A = 8, 128
H_B, DV_B = 4, 256
D_HALF = H_A * DV_A
ALPHA_RANK = 16
GATE_TEMP = 16.0
D_FF = 5632
CHUNK = 64
EPS = 1e-6
IN_PROJ_WIDTHS = (D_HALF, D_HALF, D_HALF, D_HALF, H_B * DK, H_B * DK, D_HALF, D_HALF, ALPHA_RANK)

LANES = 128
SUBLANES = 8
VMEM_LIMIT_BYTES = 56 * 1024 * 1024

COL_QA, COL_IA, COL_GA, COL_QB, COL_KB, COL_VB, COL_RB = 0, 8, 16, 24, 28, 32, 40
P16_COLS = 3 * D_HALF + 2 * H_B * DK + 2 * D_HALF

TF = 512
TN_PROJ = 512
TM_PROJ = 1024
N16_TILES = P16_COLS // TN_PROJ
DIAG_BLK = 16
SAMPLE_SEQS = 8
CHUNKS_PER_TRIP = 4
CHUNKS_PER_MXU_TRIP = 8

FAST_MIN_LOG_GATE = -60.0 / (DIAG_BLK - 1)
FAST_MAX_KEY = 1e10


def _rms(x, w):
    return x * lax.rsqrt(jnp.mean(x * x, axis=-1, keepdims=True) + EPS) * w


def _silu(x):
    return x * jax.nn.sigmoid(x)


def _params(*sem):
    return pltpu.CompilerParams(dimension_semantics=sem, vmem_limit_bytes=VMEM_LIMIT_BYTES)


def _ffn_kernel(h_ref, nw_ref, wg_ref, wu_ref, wo_ref, fnw_ref, out_ref, u_ref, acc_ref, *,
                apply_final_norm):
    j = pl.program_id(1)

    @pl.when(j == 0)
    def _init():
        u_ref[...] = _rms(h_ref[...], nw_ref[...]).astype(BF16)
        acc_ref[...] = jnp.zeros_like(acc_ref)

    u = u_ref[...]
    g = jnp.dot(u, wg_ref[...], preferred_element_type=F32)
    up = jnp.dot(u, wu_ref[...], preferred_element_type=F32)
    a = (_silu(g) * up).astype(BF16)
    acc_ref[...] += jnp.dot(a, wo_ref[...], preferred_element_type=F32)

    @pl.when(j == pl.num_programs(1) - 1)
    def _finish():
        y = h_ref[...] + 0.5 * acc_ref[...]
        if apply_final_norm:
            y = _rms(y, fnw_ref[...])
        out_ref[...] = y


def _ffn(h, norm_w, w_in, w_out, final_norm_w, *, tm, apply_final_norm, name):
    m = h.shape[0]
    nf = D_FF // TF
    return pl.pallas_call(
        functools.partial(_ffn_kernel, apply_final_norm=apply_final_norm),
        grid=(m // tm, nf),
        in_specs=[
            pl.BlockSpec((tm, D_MODEL), lambda i, j: (i, 0)),
            pl.BlockSpec((1, D_MODEL), lambda i, j: (0, 0)),
            pl.BlockSpec((D_MODEL, TF), lambda i, j: (0, j)),
            pl.BlockSpec((D_MODEL, TF), lambda i, j: (0, j + nf)),
            pl.BlockSpec((TF, D_MODEL), lambda i, j: (j, 0)),
            pl.BlockSpec((1, D_MODEL), lambda i, j: (0, 0)),
        ],
        out_specs=pl.BlockSpec((tm, D_MODEL), lambda i, j: (i, 0)),
        out_shape=jax.ShapeDtypeStruct((m, D_MODEL), F32),
        scratch_shapes=[pltpu.VMEM((tm, D_MODEL), BF16), pltpu.VMEM((tm, D_MODEL), F32)],
        compiler_params=_params("parallel", "arbitrary"),
        name=name,
    )(h, norm_w, w_in, w_in, w_out, final_norm_w)


def _proj_kernel(h_ref, nw_ref, w_ref, wal_ref, p16_ref, pf_ref, pa_ref, u_ref):
    j = pl.program_id(1)

    @pl.when(j == 0)
    def _init():
        u = _rms(h_ref[...], nw_ref[...]).astype(BF16)
        u_ref[...] = u
        lane = lax.broadcasted_iota(jnp.int32, wal_ref.shape, 1)
        wal = jnp.where(lane < ALPHA_RANK, wal_ref[...], 0.0).astype(BF16)
        pa_ref[...] = jnp.dot(u, wal, preferred_element_type=F32)

    def project():
        return jnp.dot(u_ref[...], w_ref[...].astype(BF16), preferred_element_type=F32)

    @pl.when(j < N16_TILES)
    def _narrow():
        p16_ref[...] = project().astype(BF16)

    @pl.when(j >= N16_TILES)
    def _wide():
        pf_ref[...] = project()


def _w_in_tile(j):
    fa_lo, fa_hi = D_HALF // TN_PROJ, 2 * D_HALF // TN_PROJ
    moved = fa_hi - fa_lo
    return jnp.where(j < fa_lo, j, jnp.where(j < N16_TILES, j + moved, j - N16_TILES + fa_lo))


def _in_proj(h, norm_w, w, *, tm, name):
    m = h.shape[0]
    return pl.pallas_call(
        _proj_kernel,
        grid=(m // tm, (P16_COLS + D_HALF) // TN_PROJ),
        in_specs=[
            pl.BlockSpec((tm, D_MODEL), lambda i, j: (i, 0)),
            pl.BlockSpec((1, D_MODEL), lambda i, j: (0, 0)),
            pl.BlockSpec((D_MODEL, TN_PROJ), lambda i, j: (0, _w_in_tile(j))),
            pl.BlockSpec((D_MODEL, LANES), lambda i, j: (0, (P16_COLS + D_HALF) // LANES)),
        ],
        out_specs=[
            pl.BlockSpec((tm, TN_PROJ), lambda i, j: (i, jnp.minimum(j, N16_TILES - 1))),
            pl.BlockSpec((tm, TN_PROJ), lambda i, j: (i, jnp.maximum(j - N16_TILES, 0))),
            pl.BlockSpec((tm, LANES), lambda i, j: (i, 0)),
        ],
        out_shape=[jax.ShapeDtypeStruct((m, P16_COLS), BF16),
                   jax.ShapeDtypeStruct((m, D_HALF), F32),
                   jax.ShapeDtypeStruct((m, LANES), F32)],
        scratch_shapes=[pltpu.VMEM((tm, D_MODEL), BF16)],
        compiler_params=_params("parallel", "arbitrary"),
        name=name,
    )(h, norm_w, w, w)


def _out_proj_kernel(oa_ref, ob_ref, h_ref, wa_ref, wb_ref, out_ref):
    out_ref[...] = (h_ref[...]
                    + jnp.dot(oa_ref[...], wa_ref[...], preferred_element_type=F32)
                    + jnp.dot(ob_ref[...], wb_ref[...], preferred_element_type=F32))


def _out_proj(oa, ob, h, w, *, tm, name):
    m = h.shape[0]
    return pl.pallas_call(
        _out_proj_kernel,
        grid=(m // tm,),
        in_specs=[
            pl.BlockSpec((tm, D_HALF), lambda i: (i, 0)),
            pl.BlockSpec((tm, D_HALF), lambda i: (i, 0)),
            pl.BlockSpec((tm, D_MODEL), lambda i: (i, 0)),
            pl.BlockSpec((D_HALF, D_MODEL), lambda i: (0, 0)),
            pl.BlockSpec((D_HALF, D_MODEL), lambda i: (1, 0)),
        ],
        out_specs=pl.BlockSpec((tm, D_MODEL), lambda i: (i, 0)),
        out_shape=jax.ShapeDtypeStruct((m, D_MODEL), F32),
        compiler_params=_params("parallel"),
        name=name,
    )(oa, ob, h, w, w)


def _tri(c, seq_len):
    r = lax.broadcasted_iota(jnp.int32, (c, c), 0)
    s = lax.broadcasted_iota(jnp.int32, (c, c), 1)
    return jnp.where((r >= s) & ((r ^ s) < seq_len), 1.0, 0.0).astype(F32)


def _off_mask(c, blk):
    nb = c // blk
    rows = nb * (nb - 1) // 2 * blk
    r = lax.broadcasted_iota(jnp.int32, (c, rows), 0) >> (blk.bit_length() - 1)
    s = lax.broadcasted_iota(jnp.int32, (c, rows), 1)
    seg = jnp.zeros((c, rows), jnp.int32)
    for i in range(1, nb):
        seg = seg + jnp.where(s >= blk * i * (i - 1) // 2, 1, 0)
    return jnp.where(r == seg, 1.0, 0.0).astype(F32)


def _diag(q, k, v, bc, blk):
    c = q.shape[0]
    row = lax.broadcasted_iota(jnp.int32, (blk, 1), 0)
    outs = []
    for i in range(c // blk):
        sl = slice(i * blk, (i + 1) * blk)
        qi, ki, vi, bi = q[sl], k[sl], v[sl].astype(F32), bc[sl]
        acc = jnp.zeros((blk, v.shape[1]), F32)
        for s in range(blk):
            e = jnp.exp(jnp.minimum(bi - bi[s:s + 1, :], 0.0))
            col = jnp.sum(qi * ki[s:s + 1, :] * e, axis=-1, keepdims=True)
            col = jnp.where(row >= s, col, 0.0)
            acc = acc + col * vi[s:s + 1, :]
        outs.append(acc)
    return outs[0] if len(outs) == 1 else jnp.concatenate(outs, axis=0)


def _offdiag(q, k, v, bc, blk, mask):
    c = q.shape[0]
    nb = c // blk
    qs, ks, vs = [jnp.zeros((blk, DK), F32)], [], []
    for i in range(1, nb):
        lo = i * blk
        ref = bc[lo:lo + 1, :]
        qs.append(q[lo:lo + blk] * jnp.exp(bc[lo:lo + blk] - ref))
        ks.append(k[:lo] * jnp.exp(ref - bc[:lo]))
        vs.append(v[:lo])
    qt = jnp.concatenate(qs, axis=0).astype(BF16)
    kt = jnp.concatenate(ks, axis=0).astype(BF16)
    vt = jnp.concatenate(vs, axis=0).astype(BF16)
    a = lax.dot_general(qt, kt, (((1,), (1,)), ((), ())), preferred_element_type=F32)
    return jnp.dot((a * mask).astype(BF16), vt, preferred_element_type=F32)


def _state_update(k, v, g, bc, blast, s_prev):
    c, dv = v.shape
    kdec = (k * jnp.exp(blast - bc)).astype(BF16)
    upd = lax.dot_general(kdec, v.astype(BF16), (((0,), (0,)), ((), ())),
                          preferred_element_type=F32)
    if s_prev is None:
        return upd
    tot = lax.dot_general(g, jnp.ones((c, dv), F32), (((0,), (0,)), ((), ())),
                          precision=HIGHEST, preferred_element_type=F32)
    return jnp.exp(tot) * s_prev + upd


def _hgrn_inputs(qa, fa, ia, lb):
    fg = lb + (1.0 - lb) * jax.nn.sigmoid(fa)
    return _silu(qa.astype(F32)) * (DK ** -0.5), 1.0 - fg, ia, jnp.log(fg)


def _gla_inputs(qb, kb, vb, al, wa, ba):
    alpha = jnp.dot(al.astype(BF16), wa, preferred_element_type=F32) + ba
    logsig = jnp.minimum(alpha, 0.0) - jnp.log1p(jnp.exp(-jnp.abs(alpha)))
    return qb.astype(F32) * (DK ** -0.5), kb.astype(F32), vb, logsig / GATE_TEMP


def _head_out(o, gn, gate):
    return (_rms(o, gn) * _silu(gate.astype(F32))).astype(BF16)


def _lower_bound(lbl_ref):
    x = lbl_ref[...]
    e = jnp.exp(x - jnp.max(x, axis=0, keepdims=True))
    return e[0:1, :] / jnp.sum(e, axis=0, keepdims=True)


def _stack_mask(c, blk):
    nb = c // blk
    rows = blk * nb * (nb + 1) // 2
    t = lax.broadcasted_iota(jnp.int32, (c, rows), 0)
    r = lax.broadcasted_iota(jnp.int32, (c, rows), 1)
    seg = jnp.zeros((c, rows), jnp.int32)
    off = jnp.zeros((c, rows), jnp.int32)
    for i in range(1, nb):
        start = blk * i * (i + 1) // 2
        seg = seg + jnp.where(r >= start, 1, 0)
        off = jnp.where(r >= start, start, off)
    keep = ((t >> (blk.bit_length() - 1)) == seg) & (r - off <= t)
    return jnp.where(keep, 1.0, 0.0).astype(F32)


def _state_update_t(k, vb, bc, st):
    c = k.shape[0]
    blast = bc[c - 1:c, :]
    kdec = (k * jnp.exp(blast - bc)).astype(BF16)
    upd = lax.dot_general(vb, kdec, (((0,), (0,)), ((), ())), preferred_element_type=F32)
    if st is None:
        return upd
    return st * jnp.exp(blast) + upd


def _cumsum_chunk(g):
    row = lax.broadcasted_iota(jnp.int32, (SUBLANES, DK), 0)
    out, carry = [], None
    for j in range(CHUNK // SUBLANES):
        p = g[j * SUBLANES:(j + 1) * SUBLANES]
        for s in (1, 2, 4):
            p = p + jnp.where(row >= s, pltpu.roll(p, s, axis=0), 0.0)
        if carry is not None:
            p = p + carry
        carry = p[SUBLANES - 1:SUBLANES, :]
        out.append(p)
    return jnp.concatenate(out, axis=0)


def _rescaled_operands(q, k, bc):
    c, blk = CHUNK, DIAG_BLK
    qs, ks = [], []
    for i in range(c // blk):
        lo, hi = i * blk, (i + 1) * blk
        ref = bc[lo:lo + 1, :]
        qs.append(q[lo:hi] * jnp.exp(bc[lo:hi] - ref))
        ks.append(k[:hi] * jnp.exp(ref - bc[:hi]))
    return ((q * jnp.exp(bc)).astype(BF16), jnp.concatenate(qs, axis=0).astype(BF16),
            jnp.concatenate(ks, axis=0).astype(BF16),
            (k * jnp.exp(bc[c - 1:c, :] - bc)).astype(BF16))


def _chunk_exact(q, k, v, bc, st, mask):
    o = lax.dot_general((q * jnp.exp(bc)).astype(BF16), st.astype(BF16),
                        (((1,), (1,)), ((), ())), preferred_element_type=F32)
    o = o + _offdiag(q, k, v, bc, DIAG_BLK, mask) + _diag(q, k, v, bc, DIAG_BLK)
    return o, _state_update_t(k, v.astype(BF16), bc, st)


def _prompt_scan_kernel(*refs, gla, seq, meta):
    n_in = 11 if gla else 8
    if gla:
        q_ref, k_ref, v_ref, gate_ref, al_ref, mk_ref, mv_ref, mal_ref, wa_ref, ba_ref, gn_ref = (
            refs[:n_in])
    else:
        q_ref, f_ref, v_ref, gate_ref, mf_ref, mv_ref, lbl_ref, gn_ref = refs[:n_in]
        lb = _lower_bound(lbl_ref)
    o_ref, sout_ref = refs[n_in:n_in + 2]
    st_ref, qs_ref, ks_ref, bc_ref, qd_ref, qt_ref, kt_ref, kdec_ref, eb_ref = refs[n_in + 2:]
    trips = seq // (CHUNK * CHUNKS_PER_TRIP)
    stack = kt_ref.shape[0] // (seq // CHUNK)
    nt = (((1,), (1,)), ((), ()))
    tn = (((0,), (0,)), ((), ()))

    def chunk_rows(ci, u, n=CHUNK):
        return pl.ds(pl.multiple_of((ci * CHUNKS_PER_TRIP + u) * n, n), n)

    if gla:
        _, mk, mv, mg = _gla_inputs(mk_ref[...], mk_ref[...], mv_ref[...], mal_ref[...],
                                    wa_ref[...], ba_ref[...])
    else:
        _, mk, mv, mg = _hgrn_inputs(mf_ref[...], mf_ref[...], mv_ref[...], lb)
    mbc = jnp.dot(_tri(meta, meta), mg, precision=HIGHEST, preferred_element_type=F32)
    st_ref[...] = _state_update_t(mk, mv.astype(BF16), mbc, None)

    def pass_a(ci, carry):
        gmin, kmax = carry
        trip_rows = CHUNK * CHUNKS_PER_TRIP
        rows = pl.ds(pl.multiple_of(ci * trip_rows, trip_rows), trip_rows)
        if gla:
            q, k, _, g = _gla_inputs(q_ref[rows, :], k_ref[rows, :], None, al_ref[rows, :],
                                     wa_ref[...], ba_ref[...])
        else:
            q, k, _, g = _hgrn_inputs(q_ref[rows, :], f_ref[rows, :], None, lb)
        qs_ref[rows, :] = q
        ks_ref[rows, :] = k
        for u in range(CHUNKS_PER_TRIP):
            sl = slice(u * CHUNK, (u + 1) * CHUNK)
            bc = _cumsum_chunk(g[sl])
            qd, qt, kt, kdec = _rescaled_operands(q[sl], k[sl], bc)
            bc_ref[chunk_rows(ci, u), :] = bc
            qd_ref[chunk_rows(ci, u), :] = qd
            qt_ref[chunk_rows(ci, u), :] = qt
            kt_ref[chunk_rows(ci, u, stack), :] = kt
            kdec_ref[chunk_rows(ci, u), :] = kdec
            eb_ref[ci * CHUNKS_PER_TRIP + u] = jnp.broadcast_to(
                jnp.exp(bc[CHUNK - 1:CHUNK, :]), (SUBLANES, DK))
        return (jnp.minimum(gmin, jnp.min(g, axis=0, keepdims=True)),
                jnp.maximum(kmax, jnp.max(jnp.abs(k), axis=0, keepdims=True)))

    gmin, kmax = lax.fori_loop(0, trips, pass_a,
                               (jnp.zeros((1, DK), F32), jnp.zeros((1, DK), F32)))
    in_range = (jnp.min(gmin) >= FAST_MIN_LOG_GATE) & (jnp.max(kmax) <= FAST_MAX_KEY)

    @pl.when(in_range)
    def _rescaled():
        mask = _stack_mask(CHUNK, DIAG_BLK)
        units = range(CHUNKS_PER_MXU_TRIP)

        def rows_of(c, n=CHUNK):
            return pl.ds(pl.multiple_of(c * n, n), n)

        def pass_bc(ci, carry):
            cs = [ci * CHUNKS_PER_MXU_TRIP + u for u in units]
            vb = [v_ref[rows_of(c), :].astype(BF16) for c in cs]
            a = [lax.dot_general(qt_ref[rows_of(c), :], kt_ref[rows_of(c, stack), :], nt,
                                 preferred_element_type=F32) for c in cs]
            am = [(x * mask).astype(BF16) for x in a]
            vt = [jnp.concatenate([x[:(i + 1) * DIAG_BLK] for i in range(CHUNK // DIAG_BLK)], axis=0)
                  for x in vb]
            oi = [jnp.dot(am[u], vt[u], preferred_element_type=F32) for u in units]
            inc = [lax.dot_general(vb[u], kdec_ref[rows_of(cs[u]), :], tn,
                                   preferred_element_type=F32) for u in units]
            sts = [st_ref[...]]
            for u in units:
                sts.append(sts[-1] * eb_ref[cs[u]][0:1, :] + inc[u])
            o = [oi[u] + lax.dot_general(qd_ref[rows_of(cs[u]), :], sts[u].astype(BF16), nt,
                                         preferred_element_type=F32) for u in units]
            for u in units:
                o_ref[rows_of(cs[u]), :] = _head_out(o[u], gn_ref[...], gate_ref[rows_of(cs[u]), :])
            st_ref[...] = sts[-1]
            return carry

        lax.fori_loop(0, seq // (CHUNK * CHUNKS_PER_MXU_TRIP), pass_bc, 0)

    @pl.when(jnp.logical_not(in_range))
    def _exact():
        mask = _off_mask(CHUNK, DIAG_BLK)

        def body(ci, carry):
            st = st_ref[...]
            for u in range(CHUNKS_PER_TRIP):
                rows = chunk_rows(ci, u)
                o, st = _chunk_exact(qs_ref[rows, :], ks_ref[rows, :], v_ref[rows, :],
                                     bc_ref[rows, :], st, mask)
                o_ref[rows, :] = _head_out(o, gn_ref[...], gate_ref[rows, :])
            st_ref[...] = st
            return carry

        lax.fori_loop(0, trips, body, 0)

    sout_ref[...] = st_ref[...].T


def _sample_scan_kernel(*refs, gla, steps):
    if gla:
        (q_ref, k_ref, v_ref, gate_ref, al_ref, wa_ref, ba_ref, gn_ref, s0_ref,
         o_ref, sout_ref) = refs
        q, k, v, g = _gla_inputs(q_ref[...], k_ref[...], v_ref[...], al_ref[...],
                                 wa_ref[...], ba_ref[...])
    else:
        (q_ref, f_ref, v_ref, gate_ref, lbl_ref, gn_ref, s0_ref, o_ref, sout_ref) = refs
        q, k, v, g = _hgrn_inputs(q_ref[...], f_ref[...], v_ref[...], _lower_bound(lbl_ref))
    v = v.astype(F32)

    rows = SAMPLE_SEQS * steps
    bc = jnp.dot(_tri(rows, steps), g, precision=HIGHEST, preferred_element_type=F32)
    qd = (q * jnp.exp(bc)).astype(BF16)
    inter = []
    for b in range(SAMPLE_SEQS):
        sl = slice(b * steps, (b + 1) * steps)
        s_prev = s0_ref[b]
        inter.append(jnp.dot(qd[sl], s_prev.astype(BF16), preferred_element_type=F32))
        last = (b + 1) * steps - 1
        sout_ref[b] = _state_update(k[sl], v[sl], g[sl], bc[sl], bc[last:last + 1, :], s_prev)
    o = jnp.concatenate(inter, axis=0) + _diag(q, k, v, bc, steps)
    o_ref[...] = _head_out(o, gn_ref[...], gate_ref[...])


def _col(width, block):
    return block * LANES // width


def _prompt_scan(p, pm, *, gla, batch, seq, small, name):
    heads, dv = (H_B, DV_B) if gla else (H_A, DV_A)
    n_chunks = seq // CHUNK
    blocks = CHUNK // DIAG_BLK
    stack = DIAG_BLK * blocks * (blocks + 1) // 2
    p16, pf, pa = p
    pm16, pmf, pma = pm
    tok = lambda width, c0, per_head: pl.BlockSpec(
        (seq, width), lambda b, h: (b, _col(width, c0) + h * per_head))
    met = lambda width, c0, per_head: pl.BlockSpec(
        (N_META, width), lambda b, h: (0, _col(width, c0) + h * per_head))
    vec = lambda width: pl.BlockSpec((1, width), lambda b, h: (0, h))
    if gla:
        wa, ba, gn = small
        in_specs = [tok(DK, COL_QB, 1), tok(DK, COL_KB, 1), tok(dv, COL_VB, 1), tok(dv, COL_RB, 1),
                    tok(LANES, 0, 0),
                    met(DK, COL_KB, 1), met(dv, COL_VB, 1), met(LANES, 0, 0),
                    pl.BlockSpec((LANES, DK), lambda b, h: (0, h)), vec(DK), vec(dv)]
        args = (p16, p16, p16, p16, pa, pm16, pm16, pma, wa, ba, gn)
    else:
        lbl, gn = small
        in_specs = [tok(DK, COL_QA, 1), tok(DK, 0, 1), tok(dv, COL_IA, 1), tok(dv, COL_GA, 1),
                    met(DK, 0, 1), met(dv, COL_IA, 1),
                    pl.BlockSpec((2, DK), lambda b, h: (0, h)), vec(dv)]
        args = (p16, pf, p16, p16, pmf, pm16, lbl, gn)
    return pl.pallas_call(
        functools.partial(_prompt_scan_kernel, gla=gla, seq=seq, meta=N_META),
        grid=(batch, heads),
        in_specs=in_specs,
        out_specs=[pl.BlockSpec((seq, dv), lambda b, h: (b, h)),
                   pl.BlockSpec((None, None, DK, dv), lambda b, h: (b, h, 0, 0))],
        out_shape=[jax.ShapeDtypeStruct((batch * seq, heads * dv), BF16),
                   jax.ShapeDtypeStruct((batch, heads, DK, dv), F32)],
        scratch_shapes=[
            pltpu.VMEM((dv, DK), F32),
            pltpu.VMEM((seq, DK), F32),
            pltpu.VMEM((seq, DK), F32),
            pltpu.VMEM((seq, DK), F32),
            pltpu.VMEM((seq, DK), BF16),
            pltpu.VMEM((seq, DK), BF16),
            pltpu.VMEM((n_chunks * stack, DK), BF16),
            pltpu.VMEM((seq, DK), BF16),
            pltpu.VMEM((n_chunks, SUBLANES, DK), F32),
        ],
        compiler_params=_params("parallel", "parallel"),
        name=name,
    )(*args)


def _sample_scan(p, s0, *, gla, batch, steps, small, name):
    heads, dv = (H_B, DV_B) if gla else (H_A, DV_A)
    rows = SAMPLE_SEQS * steps
    p16, pf, pa = p
    tok = lambda width, c0, per_head: pl.BlockSpec(
        (rows, width), lambda b, h: (b, _col(width, c0) + h * per_head))
    vec = lambda width: pl.BlockSpec((1, width), lambda b, h: (0, h))
    state = pl.BlockSpec((SAMPLE_SEQS, None, DK, dv), lambda b, h: (b, h, 0, 0))
    if gla:
        wa, ba, gn = small
        in_specs = [tok(DK, COL_QB, 1), tok(DK, COL_KB, 1), tok(dv, COL_VB, 1), tok(dv, COL_RB, 1),
                    tok(LANES, 0, 0),
                    pl.BlockSpec((LANES, DK), lambda b, h: (0, h)), vec(DK), vec(dv), state]
        args = (p16, p16, p16, p16, pa, wa, ba, gn, s0)
    else:
        lbl, gn = small
        in_specs = [tok(DK, COL_QA, 1), tok(DK, 0, 1), tok(dv, COL_IA, 1), tok(dv, COL_GA, 1),
                    pl.BlockSpec((2, DK), lambda b, h: (0, h)), vec(dv), state]
        args = (p16, pf, p16, p16, lbl, gn, s0)
    return pl.pallas_call(
        functools.partial(_sample_scan_kernel, gla=gla, steps=steps),
        grid=(batch // SAMPLE_SEQS, heads),
        in_specs=in_specs,
        out_specs=[pl.BlockSpec((rows, dv), lambda b, h: (b, h)), state],
        out_shape=[jax.ShapeDtypeStruct((batch * steps, heads * dv), BF16),
                   jax.ShapeDtypeStruct((batch, heads, DK, dv), F32)],
        compiler_params=_params("parallel", "parallel"),
        name=name,
    )(*args)


def kernel(x_prompt, x_sample, state_hgrn, state_gla, meta_tokens, lb_logits, ffn1_norm, w_ffn1_in,
           w_ffn1_out, mix_norm, w_in, w_alpha_up, b_alpha, gnorm_a, gnorm_b, w_out, ffn2_norm,
           w_ffn2_in, w_ffn2_out, final_norm):
    batch, seq, _ = x_prompt.shape
    dec_batch, steps, _ = x_sample.shape

    w1i, w1o = w_ffn1_in[0].astype(BF16), w_ffn1_out[0].astype(BF16)
    w2i, w2o = w_ffn2_in[0].astype(BF16), w_ffn2_out[0].astype(BF16)
    wop = w_out[0].astype(BF16)
    wa = jnp.pad(w_alpha_up[0], ((0, LANES - ALPHA_RANK), (0, 0))).astype(BF16)
    n1, nm, n2, nf = ffn1_norm, mix_norm, ffn2_norm, final_norm[None]
    small_a = (lb_logits, gnorm_a)
    small_b = (wa, b_alpha, gnorm_b)

    def front(x, tm, tag):
        h1 = _ffn(x, n1, w1i, w1o, nf, tm=tm, apply_final_norm=False, name=f"ffn1_{tag}")
        return h1, _in_proj(h1, nm, w_in[0], tm=min(h1.shape[0], TM_PROJ), name=f"inproj_{tag}")

    def back(oa, ob, h1, tm, tag):
        h2 = _out_proj(oa, ob, h1, wop, tm=tm, name=f"outproj_{tag}")
        return _ffn(h2, n2, w2i, w2o, nf, tm=tm, apply_final_norm=True, name=f"ffn2_{tag}")

    _, pm = front(meta_tokens, N_META, "meta")

    h1p, pp = front(x_prompt.reshape(batch * seq, D_MODEL), 512, "prompt")
    oa_p, sa_p = _prompt_scan(pp, pm, gla=False, batch=batch, seq=seq, small=small_a,
                              name="scan_hgrn_prompt")
    ob_p, sb_p = _prompt_scan(pp, pm, gla=True, batch=batch, seq=seq, small=small_b,
                              name="scan_gla_prompt")
    y_p = back(oa_p, ob_p, h1p, 512, "prompt")

    h1s, ps = front(x_sample.reshape(dec_batch * steps, D_MODEL), 512, "sample")
    oa_s, sa_s = _sample_scan(ps, state_hgrn[0], gla=False, batch=dec_batch, steps=steps,
                              small=small_a, name="scan_hgrn_sample")
    ob_s, sb_s = _sample_scan(ps, state_gla[0], gla=True, batch=dec_batch, steps=steps,
                              small=small_b, name="scan_gla_sample")
    y_s = back(oa_s, ob_s, h1s, 512, "sample")

    return (y_p.reshape(batch, seq, D_MODEL), y_s.reshape(dec_batch, steps, D_MODEL),
            sa_p[None], sb_p[None], sa_s[None], sb_s[None])
```

```python
import functools

import jax
import jax.numpy as jnp
from jax import lax
from jax.experimental import pallas as pl
from jax.experimental.pallas import tpu as pltpu

F32 = jnp.float32
BF16 = jnp.bfloat16
HIGHEST = lax.Precision.HIGHEST

D_MODEL = 2048
N_META = 16
DK = 128
H_A, DV_A = 8, 128
H_B, DV_B = 4, 256
D_HALF = H_A * DV_A
ALPHA_RANK = 16
GATE_TEMP = 16.0
D_FF = 5632
CHUNK = 64
EPS = 1e-6
IN_PROJ_WIDTHS = (D_HALF, D_HALF, D_HALF, D_HALF, H_B * DK, H_B * DK, D_HALF, D_HALF, ALPHA_RANK)

LANES = 128
SUBLANES = 8
VMEM_LIMIT_BYTES = 56 * 1024 * 1024

COL_QA, COL_IA, COL_GA, COL_QB, COL_KB, COL_VB, COL_RB = 0, 8, 16, 24, 28, 32, 40
P16_COLS = 3 * D_HALF + 2 * H_B * DK + 2 * D_HALF

TF = 512
TN_PROJ = 512
TM_PROJ = 1024
N16_TILES = P16_COLS // TN_PROJ
DIAG_BLK = 16
SAMPLE_SEQS = 8
CHUNKS_PER_TRIP = 4
CHUNKS_PER_MXU_TRIP = 8

FAST_MIN_LOG_GATE = -60.0 / (DIAG_BLK - 1)
FAST_MAX_KEY = 1e10


def _rms(x, w):
    return x * lax.rsqrt(jnp.mean(x * x, axis=-1, keepdims=True) + EPS) * w


def _silu(x):
    return x * jax.nn.sigmoid(x)


def _params(*sem):
    return pltpu.CompilerParams(dimension_semantics=sem, vmem_limit_bytes=VMEM_LIMIT_BYTES)


def _ffn_kernel(h_ref, nw_ref, wg_ref, wu_ref, wo_ref, fnw_ref, out_ref, u_ref, acc_ref, *,
                apply_final_norm):
    j = pl.program_id(1)

    @pl.when(j == 0)
    def _init():
        u_ref[...] = _rms(h_ref[...], nw_ref[...]).astype(BF16)
        acc_ref[...] = jnp.zeros_like(acc_ref)

    u = u_ref[...]
    g = jnp.dot(u, wg_ref[...], preferred_element_type=F32)
    up = jnp.dot(u, wu_ref[...], preferred_element_type=F32)
    a = (_silu(g) * up).astype(BF16)
    acc_ref[...] += jnp.dot(a, wo_ref[...], preferred_element_type=F32)

    @pl.when(j == pl.num_programs(1) - 1)
    def _finish():
        y = h_ref[...] + 0.5 * acc_ref[...]
        if apply_final_norm:
            y = _rms(y, fnw_ref[...])
        out_ref[...] = y


def _ffn(h, norm_w, w_in, w_out, final_norm_w, *, tm, apply_final_norm, name):
    m = h.shape[0]
    nf = D_FF // TF
    return pl.pallas_call(
        functools.partial(_ffn_kernel, apply_final_norm=apply_final_norm),
        grid=(m // tm, nf),
        in_specs=[
            pl.BlockSpec((tm, D_MODEL), lambda i, j: (i, 0)),
            pl.BlockSpec((1, D_MODEL), lambda i, j: (0, 0)),
            pl.BlockSpec((D_MODEL, TF), lambda i, j: (0, j)),
            pl.BlockSpec((D_MODEL, TF), lambda i, j: (0, j + nf)),
            pl.BlockSpec((TF, D_MODEL), lambda i, j: (j, 0)),
            pl.BlockSpec((1, D_MODEL), lambda i, j: (0, 0)),
        ],
        out_specs=pl.BlockSpec((tm, D_MODEL), lambda i, j: (i, 0)),
        out_shape=jax.ShapeDtypeStruct((m, D_MODEL), F32),
        scratch_shapes=[pltpu.VMEM((tm, D_MODEL), BF16), pltpu.VMEM((tm, D_MODEL), F32)],
        compiler_params=_params("parallel", "arbitrary"),
        name=name,
    )(h, norm_w, w_in, w_in, w_out, final_norm_w)


def _proj_kernel(h_ref, nw_ref, w_ref, wal_ref, p16_ref, pf_ref, pa_ref, u_ref):
    j = pl.program_id(1)

    @pl.when(j == 0)
    def _init():
        u = _rms(h_ref[...], nw_ref[...]).astype(BF16)
        u_ref[...] = u
        wal = jnp.concatenate([wal_ref[...].astype(BF16),
                               jnp.zeros((LANES - ALPHA_RANK, D_MODEL), BF16)], axis=0)
        pa_ref[...] = lax.dot_general(u, wal, (((1,), (1,)), ((), ())),
                                      preferred_element_type=F32)

    def project():
        return lax.dot_general(u_ref[...], w_ref[...].astype(BF16), (((1,), (1,)), ((), ())),
                               preferred_element_type=F32)

    @pl.when(j < N16_TILES)
    def _narrow():
        p16_ref[...] = project().astype(BF16)

    @pl.when(j >= N16_TILES)
    def _wide():
        pf_ref[...] = project()


def _w_in_tile(j):
    fa_lo, fa_hi = D_HALF // TN_PROJ, 2 * D_HALF // TN_PROJ
    moved = fa_hi - fa_lo
    return jnp.where(j < fa_lo, j, jnp.where(j < N16_TILES, j + moved, j - N16_TILES + fa_lo))


def _in_proj(h, norm_w, w, *, tm, name):
    m = h.shape[0]
    return pl.pallas_call(
        _proj_kernel,
        grid=(m // tm, (P16_COLS + D_HALF) // TN_PROJ),
        in_specs=[
            pl.BlockSpec((tm, D_MODEL), lambda i, j: (i, 0)),
            pl.BlockSpec((1, D_MODEL), lambda i, j: (0, 0)),
            pl.BlockSpec((TN_PROJ, D_MODEL), lambda i, j: (_w_in_tile(j), 0)),
            pl.BlockSpec((ALPHA_RANK, D_MODEL), lambda i, j: ((P16_COLS + D_HALF) // ALPHA_RANK, 0)),
        ],
        out_specs=[
            pl.BlockSpec((tm, TN_PROJ), lambda i, j: (i, jnp.minimum(j, N16_TILES - 1))),
            pl.BlockSpec((tm, TN_PROJ), lambda i, j: (i, jnp.maximum(j - N16_TILES, 0))),
            pl.BlockSpec((tm, LANES), lambda i, j: (i, 0)),
        ],
        out_shape=[jax.ShapeDtypeStruct((m, P16_COLS), BF16),
                   jax.ShapeDtypeStruct((m, D_HALF), F32),
                   jax.ShapeDtypeStruct((m, LANES), F32)],
        scratch_shapes=[pltpu.VMEM((tm, D_MODEL), BF16)],
        compiler_params=_params("parallel", "arbitrary"),
        name=name,
    )(h, norm_w, w, w)


def _out_proj_kernel(oa_ref, ob_ref, h_ref, wa_ref, wb_ref, out_ref):
    out_ref[...] = (h_ref[...]
                    + jnp.dot(oa_ref[...], wa_ref[...], preferred_element_type=F32)
                    + jnp.dot(ob_ref[...], wb_ref[...], preferred_element_type=F32))


def _out_proj(oa, ob, h, w, *, tm, name):
    m = h.shape[0]
    return pl.pallas_call(
        _out_proj_kernel,
        grid=(m // tm,),
        in_specs=[
            pl.BlockSpec((tm, D_HALF), lambda i: (i, 0)),
            pl.BlockSpec((tm, D_HALF), lambda i: (i, 0)),
            pl.BlockSpec((tm, D_MODEL), lambda i: (i, 0)),
            pl.BlockSpec((D_HALF, D_MODEL), lambda i: (0, 0)),
            pl.BlockSpec((D_HALF, D_MODEL), lambda i: (1, 0)),
        ],
        out_specs=pl.BlockSpec((tm, D_MODEL), lambda i: (i, 0)),
        out_shape=jax.ShapeDtypeStruct((m, D_MODEL), F32),
        compiler_params=_params("parallel"),
        name=name,
    )(oa, ob, h, w, w)


def _tri(c, seq_len):
    r = lax.broadcasted_iota(jnp.int32, (c, c), 0)
    s = lax.broadcasted_iota(jnp.int32, (c, c), 1)
    return jnp.where((r >= s) & ((r ^ s) < seq_len), 1.0, 0.0).astype(F32)


def _off_mask(c, blk):
    nb = c // blk
    rows = nb * (nb - 1) // 2 * blk
    r = lax.broadcasted_iota(jnp.int32, (c, rows), 0) >> (blk.bit_length() - 1)
    s = lax.broadcasted_iota(jnp.int32, (c, rows), 1)
    seg = jnp.zeros((c, rows), jnp.int32)
    for i in range(1, nb):
        seg = seg + jnp.where(s >= blk * i * (i - 1) // 2, 1, 0)
    return jnp.where(r == seg, 1.0, 0.0).astype(F32)


def _diag(q, k, v, bc, blk):
    c = q.shape[0]
    row = lax.broadcasted_iota(jnp.int32, (blk, 1), 0)
    outs = []
    for i in range(c // blk):
        sl = slice(i * blk, (i + 1) * blk)
        qi, ki, vi, bi = q[sl], k[sl], v[sl].astype(F32), bc[sl]
        acc = jnp.zeros((blk, v.shape[1]), F32)
        for s in range(blk):
            e = jnp.exp(jnp.minimum(bi - bi[s:s + 1, :], 0.0))
            col = jnp.sum(qi * ki[s:s + 1, :] * e, axis=-1, keepdims=True)
            col = jnp.where(row >= s, col, 0.0)
            acc = acc + col * vi[s:s + 1, :]
        outs.append(acc)
    return outs[0] if len(outs) == 1 else jnp.concatenate(outs, axis=0)


def _offdiag(q, k, v, bc, blk, mask):
    c = q.shape[0]
    nb = c // blk
    qs, ks, vs = [jnp.zeros((blk, DK), F32)], [], []
    for i in range(1, nb):
        lo = i * blk
        ref = bc[lo:lo + 1, :]
        qs.append(q[lo:lo + blk] * jnp.exp(bc[lo:lo + blk] - ref))
        ks.append(k[:lo] * jnp.exp(ref - bc[:lo]))
        vs.append(v[:lo])
    qt = jnp.concatenate(qs, axis=0).astype(BF16)
    kt = jnp.concatenate(ks, axis=0).astype(BF16)
    vt = jnp.concatenate(vs, axis=0).astype(BF16)
    a = lax.dot_general(qt, kt, (((1,), (1,)), ((), ())), preferred_element_type=F32)
    return jnp.dot((a * mask).astype(BF16), vt, preferred_element_type=F32)


def _state_update(k, v, g, bc, blast, s_prev):
    c, dv = v.shape
    kdec = (k * jnp.exp(blast - bc)).astype(BF16)
    upd = lax.dot_general(kdec, v.astype(BF16), (((0,), (0,)), ((), ())),
                          preferred_element_type=F32)
    if s_prev is None:
        return upd
    tot = lax.dot_general(g, jnp.ones((c, dv), F32), (((0,), (0,)), ((), ())),
                          precision=HIGHEST, preferred_element_type=F32)
    return jnp.exp(tot) * s_prev + upd


def _hgrn_inputs(qa, fa, ia, lb):
    fg = lb + (1.0 - lb) * jax.nn.sigmoid(fa)
    return _silu(qa.astype(F32)) * (DK ** -0.5), 1.0 - fg, ia, jnp.log(fg)


def _gla_inputs(qb, kb, vb, al, wa, ba):
    alpha = jnp.dot(al.astype(BF16), wa, preferred_element_type=F32) + ba
    logsig = jnp.minimum(alpha, 0.0) - jnp.log1p(jnp.exp(-jnp.abs(alpha)))
    return qb.astype(F32) * (DK ** -0.5), kb.astype(F32), vb, logsig / GATE_TEMP


def _head_out(o, gn, gate):
    return (_rms(o, gn) * _silu(gate.astype(F32))).astype(BF16)


def _lower_bound(lbl_ref):
    x = lbl_ref[...]
    e = jnp.exp(x - jnp.max(x, axis=0, keepdims=True))
    return e[0:1, :] / jnp.sum(e, axis=0, keepdims=True)


def _stack_mask(c, blk):
    nb = c // blk
    rows = blk * nb * (nb + 1) // 2
    t = lax.broadcasted_iota(jnp.int32, (c, rows), 0)
    r = lax.broadcasted_iota(jnp.int32, (c, rows), 1)
    seg = jnp.zeros((c, rows), jnp.int32)
    off = jnp.zeros((c, rows), jnp.int32)
    for i in range(1, nb):
        start = blk * i * (i + 1) // 2
        seg = seg + jnp.where(r >= start, 1, 0)
        off = jnp.where(r >= start, start, off)
    keep = ((t >> (blk.bit_length() - 1)) == seg) & (r - off <= t)
    return jnp.where(keep, 1.0, 0.0).astype(F32)


def _state_update_t(k, vb, bc, st):
    c = k.shape[0]
    blast = bc[c - 1:c, :]
    kdec = (k * jnp.exp(blast - bc)).astype(BF16)
    upd = lax.dot_general(vb, kdec, (((0,), (0,)), ((), ())), preferred_element_type=F32)
    if st is None:
        return upd
    return st * jnp.exp(blast) + upd


def _cumsum_chunk(g):
    row = lax.broadcasted_iota(jnp.int32, (SUBLANES, DK), 0)
    out, carry = [], None
    for j in range(CHUNK // SUBLANES):
        p = g[j * SUBLANES:(j + 1) * SUBLANES]
        for s in (1, 2, 4):
            p = p + jnp.where(row >= s, pltpu.roll(p, s, axis=0), 0.0)
        if carry is not None:
            p = p + carry
        carry = p[SUBLANES - 1:SUBLANES, :]
        out.append(p)
    return jnp.concatenate(out, axis=0)


def _rescaled_operands(q, k, bc):
    c, blk = CHUNK, DIAG_BLK
    qs, ks = [], []
    for i in range(c // blk):
        lo, hi = i * blk, (i + 1) * blk
        ref = bc[lo:lo + 1, :]
        qs.append(q[lo:hi] * jnp.exp(bc[lo:hi] - ref))
        ks.append(k[:hi] * jnp.exp(ref - bc[:hi]))
    return ((q * jnp.exp(bc)).astype(BF16), jnp.concatenate(qs, axis=0).astype(BF16),
            jnp.concatenate(ks, axis=0).astype(BF16),
            (k * jnp.exp(bc[c - 1:c, :] - bc)).astype(BF16))


def _chunk_exact(q, k, v, bc, st, mask):
    o = lax.dot_general((q * jnp.exp(bc)).astype(BF16), st.astype(BF16),
                        (((1,), (1,)), ((), ())), preferred_element_type=F32)
    o = o + _offdiag(q, k, v, bc, DIAG_BLK, mask) + _diag(q, k, v, bc, DIAG_BLK)
    return o, _state_update_t(k, v.astype(BF16), bc, st)


def _prompt_scan_kernel(*refs, gla, seq, meta):
    n_in = 11 if gla else 8
    if gla:
        q_ref, k_ref, v_ref, gate_ref, al_ref, mk_ref, mv_ref, mal_ref, wa_ref, ba_ref, gn_ref = (
            refs[:n_in])
    else:
        q_ref, f_ref, v_ref, gate_ref, mf_ref, mv_ref, lbl_ref, gn_ref = refs[:n_in]
        lb = _lower_bound(lbl_ref)
    o_ref, sout_ref = refs[n_in:n_in + 2]
    st_ref, qs_ref, ks_ref, bc_ref, qd_ref, qt_ref, kt_ref, kdec_ref, eb_ref = refs[n_in + 2:]
    trips = seq // (CHUNK * CHUNKS_PER_TRIP)
    stack = kt_ref.shape[0] // (seq // CHUNK)
    nt = (((1,), (1,)), ((), ()))
    tn = (((0,), (0,)), ((), ()))

    def chunk_rows(ci, u, n=CHUNK):
        return pl.ds(pl.multiple_of((ci * CHUNKS_PER_TRIP + u) * n, n), n)

    if gla:
        _, mk, mv, mg = _gla_inputs(mk_ref[...], mk_ref[...], mv_ref[...], mal_ref[...],
                                    wa_ref[...], ba_ref[...])
    else:
        _, mk, mv, mg = _hgrn_inputs(mf_ref[...], mf_ref[...], mv_ref[...], lb)
    mbc = jnp.dot(_tri(meta, meta), mg, precision=HIGHEST, preferred_element_type=F32)
    st_ref[...] = _state_update_t(mk, mv.astype(BF16), mbc, None)

    def pass_a(ci, carry):
        gmin, kmax = carry
        trip_rows = CHUNK * CHUNKS_PER_TRIP
        rows = pl.ds(pl.multiple_of(ci * trip_rows, trip_rows), trip_rows)
        if gla:
            q, k, _, g = _gla_inputs(q_ref[rows, :], k_ref[rows, :], None, al_ref[rows, :],
                                     wa_ref[...], ba_ref[...])
        else:
            q, k, _, g = _hgrn_inputs(q_ref[rows, :], f_ref[rows, :], None, lb)
        qs_ref[rows, :] = q
        ks_ref[rows, :] = k
        for u in range(CHUNKS_PER_TRIP):
            sl = slice(u * CHUNK, (u + 1) * CHUNK)
            bc = _cumsum_chunk(g[sl])
            qd, qt, kt, kdec = _rescaled_operands(q[sl], k[sl], bc)
            bc_ref[chunk_rows(ci, u), :] = bc
            qd_ref[chunk_rows(ci, u), :] = qd
            qt_ref[chunk_rows(ci, u), :] = qt
            kt_ref[chunk_rows(ci, u, stack), :] = kt
            kdec_ref[chunk_rows(ci, u), :] = kdec
            eb_ref[ci * CHUNKS_PER_TRIP + u] = jnp.broadcast_to(
                jnp.exp(bc[CHUNK - 1:CHUNK, :]), (SUBLANES, DK))
        return (jnp.minimum(gmin, jnp.min(g, axis=0, keepdims=True)),
                jnp.maximum(kmax, jnp.max(jnp.abs(k), axis=0, keepdims=True)))

    gmin, kmax = lax.fori_loop(0, trips, pass_a,
                               (jnp.zeros((1, DK), F32), jnp.zeros((1, DK), F32)))
    in_range = (jnp.min(gmin) >= FAST_MIN_LOG_GATE) & (jnp.max(kmax) <= FAST_MAX_KEY)

    @pl.when(in_range)
    def _rescaled():
        mask = _stack_mask(CHUNK, DIAG_BLK)
        units = range(CHUNKS_PER_MXU_TRIP)

        def rows_of(c, n=CHUNK):
            return pl.ds(pl.multiple_of(c * n, n), n)

        def pass_bc(ci, carry):
            cs = [ci * CHUNKS_PER_MXU_TRIP + u for u in units]
            vb = [v_ref[rows_of(c), :].astype(BF16) for c in cs]
            a = [lax.dot_general(qt_ref[rows_of(c), :], kt_ref[rows_of(c, stack), :], nt,
                                 preferred_element_type=F32) for c in cs]
            am = [(x * mask).astype(BF16) for x in a]
            vt = [jnp.concatenate([x[:(i + 1) * DIAG_BLK] for i in range(CHUNK // DIAG_BLK)], axis=0)
                  for x in vb]
            oi = [jnp.dot(am[u], vt[u], preferred_element_type=F32) for u in units]
            inc = [lax.dot_general(vb[u], kdec_ref[rows_of(cs[u]), :], tn,
                                   preferred_element_type=F32) for u in units]
            sts = [st_ref[...]]
            for u in units:
                sts.append(sts[-1] * eb_ref[cs[u]][0:1, :] + inc[u])
            o = [oi[u] + lax.dot_general(qd_ref[rows_of(cs[u]), :], sts[u].astype(BF16), nt,
                                         preferred_element_type=F32) for u in units]
            for u in units:
                o_ref[rows_of(cs[u]), :] = _head_out(o[u], gn_ref[...], gate_ref[rows_of(cs[u]), :])
            st_ref[...] = sts[-1]
            return carry

        lax.fori_loop(0, seq // (CHUNK * CHUNKS_PER_MXU_TRIP), pass_bc, 0)

    @pl.when(jnp.logical_not(in_range))
    def _exact():
        mask = _off_mask(CHUNK, DIAG_BLK)

        def body(ci, carry):
            st = st_ref[...]
            for u in range(CHUNKS_PER_TRIP):
                rows = chunk_rows(ci, u)
                o, st = _chunk_exact(qs_ref[rows, :], ks_ref[rows, :], v_ref[rows, :],
                                     bc_ref[rows, :], st, mask)
                o_ref[rows, :] = _head_out(o, gn_ref[...], gate_ref[rows, :])
            st_ref[...] = st
            return carry

        lax.fori_loop(0, trips, body, 0)

    sout_ref[...] = st_ref[...].T


def _sample_scan_kernel(*refs, gla, steps):
    if gla:
        (q_ref, k_ref, v_ref, gate_ref, al_ref, wa_ref, ba_ref, gn_ref, s0_ref,
         o_ref, sout_ref) = refs
        q, k, v, g = _gla_inputs(q_ref[...], k_ref[...], v_ref[...], al_ref[...],
                                 wa_ref[...], ba_ref[...])
    else:
        (q_ref, f_ref, v_ref, gate_ref, lbl_ref, gn_ref, s0_ref, o_ref, sout_ref) = refs
        q, k, v, g = _hgrn_inputs(q_ref[...], f_ref[...], v_ref[...], _lower_bound(lbl_ref))
    v = v.astype(F32)

    rows = SAMPLE_SEQS * steps
    bc = jnp.dot(_tri(rows, steps), g, precision=HIGHEST, preferred_element_type=F32)
    qd = (q * jnp.exp(bc)).astype(BF16)
    inter = []
    for b in range(SAMPLE_SEQS):
        sl = slice(b * steps, (b + 1) * steps)
        s_prev = s0_ref[b]
        inter.append(jnp.dot(qd[sl], s_prev.astype(BF16), preferred_element_type=F32))
        last = (b + 1) * steps - 1
        sout_ref[b] = _state_update(k[sl], v[sl], g[sl], bc[sl], bc[last:last + 1, :], s_prev)
    o = jnp.concatenate(inter, axis=0) + _diag(q, k, v, bc, steps)
    o_ref[...] = _head_out(o, gn_ref[...], gate_ref[...])


def _col(width, block):
    return block * LANES // width


def _prompt_scan(p, pm, *, gla, batch, seq, small, name):
    heads, dv = (H_B, DV_B) if gla else (H_A, DV_A)
    n_chunks = seq // CHUNK
    blocks = CHUNK // DIAG_BLK
    stack = DIAG_BLK * blocks * (blocks + 1) // 2
    p16, pf, pa = p
    pm16, pmf, pma = pm
    tok = lambda width, c0, per_head: pl.BlockSpec(
        (seq, width), lambda b, h: (b, _col(width, c0) + h * per_head))
    met = lambda width, c0, per_head: pl.BlockSpec(
        (N_META, width), lambda b, h: (0, _col(width, c0) + h * per_head))
    vec = lambda width: pl.BlockSpec((1, width), lambda b, h: (0, h))
    if gla:
        wa, ba, gn = small
        in_specs = [tok(DK, COL_QB, 1), tok(DK, COL_KB, 1), tok(dv, COL_VB, 1), tok(dv, COL_RB, 1),
                    tok(LANES, 0, 0),
                    met(DK, COL_KB, 1), met(dv, COL_VB, 1), met(LANES, 0, 0),
                    pl.BlockSpec((LANES, DK), lambda b, h: (0, h)), vec(DK), vec(dv)]
        args = (p16, p16, p16, p16, pa, pm16, pm16, pma, wa, ba, gn)
    else:
        lbl, gn = small
        in_specs = [tok(DK, COL_QA, 1), tok(DK, 0, 1), tok(dv, COL_IA, 1), tok(dv, COL_GA, 1),
                    met(DK, 0, 1), met(dv, COL_IA, 1),
                    pl.BlockSpec((2, DK), lambda b, h: (0, h)), vec(dv)]
        args = (p16, pf, p16, p16, pmf, pm16, lbl, gn)
    return pl.pallas_call(
        functools.partial(_prompt_scan_kernel, gla=gla, seq=seq, meta=N_META),
        grid=(batch, heads),
        in_specs=in_specs,
        out_specs=[pl.BlockSpec((seq, dv), lambda b, h: (b, h)),
                   pl.BlockSpec((None, None, DK, dv), lambda b, h: (b, h, 0, 0))],
        out_shape=[jax.ShapeDtypeStruct((batch * seq, heads * dv), BF16),
                   jax.ShapeDtypeStruct((batch, heads, DK, dv), F32)],
        scratch_shapes=[
            pltpu.VMEM((dv, DK), F32),
            pltpu.VMEM((seq, DK), F32),
            pltpu.VMEM((seq, DK), F32),
            pltpu.VMEM((seq, DK), F32),
            pltpu.VMEM((seq, DK), BF16),
            pltpu.VMEM((seq, DK), BF16),
            pltpu.VMEM((n_chunks * stack, DK), BF16),
            pltpu.VMEM((seq, DK), BF16),
            pltpu.VMEM((n_chunks, SUBLANES, DK), F32),
        ],
        compiler_params=_params("parallel", "parallel"),
        name=name,
    )(*args)


def _sample_scan(p, s0, *, gla, batch, steps, small, name):
    heads, dv = (H_B, DV_B) if gla else (H_A, DV_A)
    rows = SAMPLE_SEQS * steps
    p16, pf, pa = p
    tok = lambda width, c0, per_head: pl.BlockSpec(
        (rows, width), lambda b, h: (b, _col(width, c0) + h * per_head))
    vec = lambda width: pl.BlockSpec((1, width), lambda b, h: (0, h))
    state = pl.BlockSpec((SAMPLE_SEQS, None, DK, dv), lambda b, h: (b, h, 0, 0))
    if gla:
        wa, ba, gn = small
        in_specs = [tok(DK, COL_QB, 1), tok(DK, COL_KB, 1), tok(dv, COL_VB, 1), tok(dv, COL_RB, 1),
                    tok(LANES, 0, 0),
                    pl.BlockSpec((LANES, DK), lambda b, h: (0, h)), vec(DK), vec(dv), state]
        args = (p16, p16, p16, p16, pa, wa, ba, gn, s0)
    else:
        lbl, gn = small
        in_specs = [tok(DK, COL_QA, 1), tok(DK, 0, 1), tok(dv, COL_IA, 1), tok(dv, COL_GA, 1),
                    pl.BlockSpec((2, DK), lambda b, h: (0, h)), vec(dv), state]
        args = (p16, pf, p16, p16, lbl, gn, s0)
    return pl.pallas_call(
        functools.partial(_sample_scan_kernel, gla=gla, steps=steps),
        grid=(batch // SAMPLE_SEQS, heads),
        in_specs=in_specs,
        out_specs=[pl.BlockSpec((rows, dv), lambda b, h: (b, h)), state],
        out_shape=[jax.ShapeDtypeStruct((batch * steps, heads * dv), BF16),
                   jax.ShapeDtypeStruct((batch, heads, DK, dv), F32)],
        compiler_params=_params("parallel", "parallel"),
        name=name,
    )(*args)


def kernel(x_prompt, x_sample, state_hgrn, state_gla, meta_tokens, lb_logits, ffn1_norm, w_ffn1_in,
           w_ffn1_out, mix_norm, w_in, w_alpha_up, b_alpha, gnorm_a, gnorm_b, w_out, ffn2_norm,
           w_ffn2_in, w_ffn2_out, final_norm):
    batch, seq, _ = x_prompt.shape
    dec_batch, steps, _ = x_sample.shape

    w1i, w1o = w_ffn1_in[0].astype(BF16), w_ffn1_out[0].astype(BF16)
    w2i, w2o = w_ffn2_in[0].astype(BF16), w_ffn2_out[0].astype(BF16)
    w_in_t = w_in[0].T
    wop = w_out[0].astype(BF16)
    wa = jnp.pad(w_alpha_up[0], ((0, LANES - ALPHA_RANK), (0, 0))).astype(BF16)
    n1, nm, n2, nf = ffn1_norm, mix_norm, ffn2_norm, final_norm[None]
    small_a = (lb_logits, gnorm_a)
    small_b = (wa, b_alpha, gnorm_b)

    def front(x, tm, tag):
        h1 = _ffn(x, n1, w1i, w1o, nf, tm=tm, apply_final_norm=False, name=f"ffn1_{tag}")
        return h1, _in_proj(h1, nm, w_in_t, tm=min(h1.shape[0], TM_PROJ), name=f"inproj_{tag}")

    def back(oa, ob, h1, tm, tag):
        h2 = _out_proj(oa, ob, h1, wop, tm=tm, name=f"outproj_{tag}")
        return _ffn(h2, n2, w2i, w2o, nf, tm=tm, apply_final_norm=True, name=f"ffn2_{tag}")

    _, pm = front(meta_tokens, N_META, "meta")

    h1p, pp = front(x_prompt.reshape(batch * seq, D_MODEL), 512, "prompt")
    oa_p, sa_p = _prompt_scan(pp, pm, gla=False, batch=batch, seq=seq, small=small_a,
                              name="scan_hgrn_prompt")
    ob_p, sb_p = _prompt_scan(pp, pm, gla=True, batch=batch, seq=seq, small=small_b,
                              name="scan_gla_prompt")
    y_p = back(oa_p, ob_p, h1p, 512, "prompt")

    h1s, ps = front(x_sample.reshape(dec_batch * steps, D_MODEL), 512, "sample")
    oa_s, sa_s = _sample_scan(ps, state_hgrn[0], gla=False, batch=dec_batch, steps=steps,
                              small=small_a, name="scan_hgrn_sample")
    ob_s, sb_s = _sample_scan(ps, state_gla[0], gla=True, batch=dec_batch, steps=steps,
                              small=small_b, name="scan_gla_sample")
    y_s = back(oa_s, ob_s, h1s, 512, "sample")

    return (y_p.reshape(batch, seq, D_MODEL), y_s.reshape(dec_batch, steps, D_MODEL),
            sa_p[None], sb_p[None], sa_s[None], sb_s[None])
```

```python
import functools

import jax
import jax.numpy as jnp
from jax import lax
from jax.experimental import pallas as pl
from jax.experimental.pallas import tpu as pltpu

F32 = jnp.float32
BF16 = jnp.bfloat16
HIGHEST = lax.Precision.HIGHEST

D_MODEL = 2048
N_META = 16
DK = 128
H_A, DV_A = 8, 128
H_B, DV_B = 4, 256
D_HALF = H_A * DV_A
ALPHA_RANK = 16
GATE_TEMP = 16.0
D_FF = 5632
CHUNK = 64
EPS = 1e-6
IN_PROJ_WIDTHS = (D_HALF, D_HALF, D_HALF, D_HALF, H_B * DK, H_B * DK, D_HALF, D_HALF, ALPHA_RANK)

LANES = 128
SUBLANES = 8
VMEM_LIMIT_BYTES = 56 * 1024 * 1024

COL_QA, COL_IA, COL_GA, COL_QB, COL_KB, COL_VB, COL_RB = 0, 8, 16, 24, 28, 32, 40
P16_COLS = 3 * D_HALF + 2 * H_B * DK + 2 * D_HALF

TF = 512
TN_PROJ = 512
TM_PROJ = 1024
N16_TILES = P16_COLS // TN_PROJ
DIAG_BLK = 16
SAMPLE_SEQS = 8
CHUNKS_PER_TRIP = 4
CHUNKS_PER_MXU_TRIP = 8

FAST_MIN_LOG_GATE = -60.0 / (DIAG_BLK - 1)
FAST_MAX_KEY = 1e10


def _rms(x, w):
    return x * lax.rsqrt(jnp.mean(x * x, axis=-1, keepdims=True) + EPS) * w


def _silu(x):
    return x * jax.nn.sigmoid(x)


def _params(*sem):
    return pltpu.CompilerParams(dimension_semantics=sem, vmem_limit_bytes=VMEM_LIMIT_BYTES)


def _ffn_kernel(h_ref, nw_ref, wg_ref, wu_ref, wo_ref, fnw_ref, out_ref, u_ref, acc_ref, *,
                apply_final_norm):
    j = pl.program_id(1)

    @pl.when(j == 0)
    def _init():
        u_ref[...] = _rms(h_ref[...], nw_ref[...]).astype(BF16)
        acc_ref[...] = jnp.zeros_like(acc_ref)

    u = u_ref[...]
    g = jnp.dot(u, wg_ref[...], preferred_element_type=F32)
    up = jnp.dot(u, wu_ref[...], preferred_element_type=F32)
    a = (_silu(g) * up).astype(BF16)
    acc_ref[...] += jnp.dot(a, wo_ref[...], preferred_element_type=F32)

    @pl.when(j == pl.num_programs(1) - 1)
    def _finish():
        y = h_ref[...] + 0.5 * acc_ref[...]
        if apply_final_norm:
            y = _rms(y, fnw_ref[...])
        out_ref[...] = y


def _ffn(h, norm_w, w_in, w_out, final_norm_w, *, tm, apply_final_norm, name):
    m = h.shape[0]
    nf = D_FF // TF
    return pl.pallas_call(
        functools.partial(_ffn_kernel, apply_final_norm=apply_final_norm),
        grid=(m // tm, nf),
        in_specs=[
            pl.BlockSpec((tm, D_MODEL), lambda i, j: (i, 0)),
            pl.BlockSpec((1, D_MODEL), lambda i, j: (0, 0)),
            pl.BlockSpec((D_MODEL, TF), lambda i, j: (0, j)),
            pl.BlockSpec((D_MODEL, TF), lambda i, j: (0, j + nf)),
            pl.BlockSpec((TF, D_MODEL), lambda i, j: (j, 0)),
            pl.BlockSpec((1, D_MODEL), lambda i, j: (0, 0)),
        ],
        out_specs=pl.BlockSpec((tm, D_MODEL), lambda i, j: (i, 0)),
        out_shape=jax.ShapeDtypeStruct((m, D_MODEL), F32),
        scratch_shapes=[pltpu.VMEM((tm, D_MODEL), BF16), pltpu.VMEM((tm, D_MODEL), F32)],
        compiler_params=_params("parallel", "arbitrary"),
        name=name,
    )(h, norm_w, w_in, w_in, w_out, final_norm_w)


def _proj_kernel(h_ref, nw_ref, w_ref, wal_ref, p16_ref, pf_ref, pa_ref, u_ref):
    j = pl.program_id(1)

    @pl.when(j == 0)
    def _init():
        u = _rms(h_ref[...], nw_ref[...]).astype(BF16)
        u_ref[...] = u
        wal = jnp.concatenate([wal_ref[...].astype(BF16),
                               jnp.zeros((LANES - ALPHA_RANK, D_MODEL), BF16)], axis=0)
        pa_ref[...] = lax.dot_general(u, wal, (((1,), (1,)), ((), ())),
                                      preferred_element_type=F32)

    def project():
        return lax.dot_general(u_ref[...], w_ref[...].astype(BF16), (((1,), (1,)), ((), ())),
                               preferred_element_type=F32)

    @pl.when(j < N16_TILES)
    def _narrow():
        p16_ref[...] = project().astype(BF16)

    @pl.when(j >= N16_TILES)
    def _wide():
        pf_ref[...] = project()


def _w_in_tile(j):
    fa_lo, fa_hi = D_HALF // TN_PROJ, 2 * D_HALF // TN_PROJ
    moved = fa_hi - fa_lo
    return jnp.where(j < fa_lo, j, jnp.where(j < N16_TILES, j + moved, j - N16_TILES + fa_lo))


def _in_proj(h, norm_w, w, *, tm, name):
    m = h.shape[0]
    return pl.pallas_call(
        _proj_kernel,
        grid=(m // tm, (P16_COLS + D_HALF) // TN_PROJ),
        in_specs=[
            pl.BlockSpec((tm, D_MODEL), lambda i, j: (i, 0)),
            pl.BlockSpec((1, D_MODEL), lambda i, j: (0, 0)),
            pl.BlockSpec((TN_PROJ, D_MODEL), lambda i, j: (_w_in_tile(j), 0)),
            pl.BlockSpec((ALPHA_RANK, D_MODEL), lambda i, j: ((P16_COLS + D_HALF) // ALPHA_RANK, 0)),
        ],
        out_specs=[
            pl.BlockSpec((None, tm, TN_PROJ), lambda i, j: (jnp.minimum(j, N16_TILES - 1), i, 0)),
            pl.BlockSpec((None, tm, TN_PROJ), lambda i, j: (jnp.maximum(j - N16_TILES, 0), i, 0)),
            pl.BlockSpec((None, tm, LANES), lambda i, j: (0, i, 0)),
        ],
        out_shape=[jax.ShapeDtypeStruct((N16_TILES, m, TN_PROJ), BF16),
                   jax.ShapeDtypeStruct((D_HALF // TN_PROJ, m, TN_PROJ), F32),
                   jax.ShapeDtypeStruct((1, m, LANES), F32)],
        scratch_shapes=[pltpu.VMEM((tm, D_MODEL), BF16)],
        compiler_params=_params("parallel", "arbitrary"),
        name=name,
    )(h, norm_w, w, w)


def _out_proj_kernel(oa_ref, ob_ref, h_ref, wa_ref, wb_ref, out_ref):
    out_ref[...] = (h_ref[...]
                    + jnp.dot(oa_ref[...], wa_ref[...], preferred_element_type=F32)
                    + jnp.dot(ob_ref[...], wb_ref[...], preferred_element_type=F32))


def _out_proj(oa, ob, h, w, *, tm, name):
    m = h.shape[0]
    return pl.pallas_call(
        _out_proj_kernel,
        grid=(m // tm,),
        in_specs=[
            pl.BlockSpec((tm, D_HALF), lambda i: (i, 0)),
            pl.BlockSpec((tm, D_HALF), lambda i: (i, 0)),
            pl.BlockSpec((tm, D_MODEL), lambda i: (i, 0)),
            pl.BlockSpec((D_HALF, D_MODEL), lambda i: (0, 0)),
            pl.BlockSpec((D_HALF, D_MODEL), lambda i: (1, 0)),
        ],
        out_specs=pl.BlockSpec((tm, D_MODEL), lambda i: (i, 0)),
        out_shape=jax.ShapeDtypeStruct((m, D_MODEL), F32),
        compiler_params=_params("parallel"),
        name=name,
    )(oa, ob, h, w, w)


def _tri(c, seq_len):
    r = lax.broadcasted_iota(jnp.int32, (c, c), 0)
    s = lax.broadcasted_iota(jnp.int32, (c, c), 1)
    return jnp.where((r >= s) & ((r ^ s) < seq_len), 1.0, 0.0).astype(F32)


def _off_mask(c, blk):
    nb = c // blk
    rows = nb * (nb - 1) // 2 * blk
    r = lax.broadcasted_iota(jnp.int32, (c, rows), 0) >> (blk.bit_length() - 1)
    s = lax.broadcasted_iota(jnp.int32, (c, rows), 1)
    seg = jnp.zeros((c, rows), jnp.int32)
    for i in range(1, nb):
        seg = seg + jnp.where(s >= blk * i * (i - 1) // 2, 1, 0)
    return jnp.where(r == seg, 1.0, 0.0).astype(F32)


def _diag(q, k, v, bc, blk):
    c = q.shape[0]
    row = lax.broadcasted_iota(jnp.int32, (blk, 1), 0)
    outs = []
    for i in range(c // blk):
        sl = slice(i * blk, (i + 1) * blk)
        qi, ki, vi, bi = q[sl], k[sl], v[sl].astype(F32), bc[sl]
        acc = jnp.zeros((blk, v.shape[1]), F32)
        for s in range(blk):
            e = jnp.exp(jnp.minimum(bi - bi[s:s + 1, :], 0.0))
            col = jnp.sum(qi * ki[s:s + 1, :] * e, axis=-1, keepdims=True)
            col = jnp.where(row >= s, col, 0.0)
            acc = acc + col * vi[s:s + 1, :]
        outs.append(acc)
    return outs[0] if len(outs) == 1 else jnp.concatenate(outs, axis=0)


def _offdiag(q, k, v, bc, blk, mask):
    c = q.shape[0]
    nb = c // blk
    qs, ks, vs = [jnp.zeros((blk, DK), F32)], [], []
    for i in range(1, nb):
        lo = i * blk
        ref = bc[lo:lo + 1, :]
        qs.append(q[lo:lo + blk] * jnp.exp(bc[lo:lo + blk] - ref))
        ks.append(k[:lo] * jnp.exp(ref - bc[:lo]))
        vs.append(v[:lo])
    qt = jnp.concatenate(qs, axis=0).astype(BF16)
    kt = jnp.concatenate(ks, axis=0).astype(BF16)
    vt = jnp.concatenate(vs, axis=0).astype(BF16)
    a = lax.dot_general(qt, kt, (((1,), (1,)), ((), ())), preferred_element_type=F32)
    return jnp.dot((a * mask).astype(BF16), vt, preferred_element_type=F32)


def _state_update(k, v, g, bc, blast, s_prev):
    c, dv = v.shape
    kdec = (k * jnp.exp(blast - bc)).astype(BF16)
    upd = lax.dot_general(kdec, v.astype(BF16), (((0,), (0,)), ((), ())),
                          preferred_element_type=F32)
    if s_prev is None:
        return upd
    tot = lax.dot_general(g, jnp.ones((c, dv), F32), (((0,), (0,)), ((), ())),
                          precision=HIGHEST, preferred_element_type=F32)
    return jnp.exp(tot) * s_prev + upd


def _hgrn_inputs(qa, fa, ia, lb):
    fg = lb + (1.0 - lb) * jax.nn.sigmoid(fa)
    return _silu(qa.astype(F32)) * (DK ** -0.5), 1.0 - fg, ia, jnp.log(fg)


def _gla_inputs(qb, kb, vb, al, wa, ba):
    alpha = jnp.dot(al.astype(BF16), wa, preferred_element_type=F32) + ba
    logsig = jnp.minimum(alpha, 0.0) - jnp.log1p(jnp.exp(-jnp.abs(alpha)))
    return qb.astype(F32) * (DK ** -0.5), kb.astype(F32), vb, logsig / GATE_TEMP


def _head_out(o, gn, gate):
    return (_rms(o, gn) * _silu(gate.astype(F32))).astype(BF16)


def _lower_bound(lbl_ref):
    x = lbl_ref[...]
    e = jnp.exp(x - jnp.max(x, axis=0, keepdims=True))
    return e[0:1, :] / jnp.sum(e, axis=0, keepdims=True)


def _stack_mask(c, blk):
    nb = c // blk
    rows = blk * nb * (nb + 1) // 2
    t = lax.broadcasted_iota(jnp.int32, (c, rows), 0)
    r = lax.broadcasted_iota(jnp.int32, (c, rows), 1)
    seg = jnp.zeros((c, rows), jnp.int32)
    off = jnp.zeros((c, rows), jnp.int32)
    for i in range(1, nb):
        start = blk * i * (i + 1) // 2
        seg = seg + jnp.where(r >= start, 1, 0)
        off = jnp.where(r >= start, start, off)
    keep = ((t >> (blk.bit_length() - 1)) == seg) & (r - off <= t)
    return jnp.where(keep, 1.0, 0.0).astype(F32)


def _state_update_t(k, vb, bc, st):
    c = k.shape[0]
    blast = bc[c - 1:c, :]
    kdec = (k * jnp.exp(blast - bc)).astype(BF16)
    upd = lax.dot_general(vb, kdec, (((0,), (0,)), ((), ())), preferred_element_type=F32)
    if st is None:
        return upd
    return st * jnp.exp(blast) + upd


def _cumsum_chunk(g):
    row = lax.broadcasted_iota(jnp.int32, (SUBLANES, DK), 0)
    out, carry = [], None
    for j in range(CHUNK // SUBLANES):
        p = g[j * SUBLANES:(j + 1) * SUBLANES]
        for s in (1, 2, 4):
            p = p + jnp.where(row >= s, pltpu.roll(p, s, axis=0), 0.0)
        if carry is not None:
            p = p + carry
        carry = p[SUBLANES - 1:SUBLANES, :]
        out.append(p)
    return jnp.concatenate(out, axis=0)


def _rescaled_operands(q, k, bc):
    c, blk = CHUNK, DIAG_BLK
    qs, ks = [], []
    for i in range(c // blk):
        lo, hi = i * blk, (i + 1) * blk
        ref = bc[lo:lo + 1, :]
        qs.append(q[lo:hi] * jnp.exp(bc[lo:hi] - ref))
        ks.append(k[:hi] * jnp.exp(ref - bc[:hi]))
    return ((q * jnp.exp(bc)).astype(BF16), jnp.concatenate(qs, axis=0).astype(BF16),
            jnp.concatenate(ks, axis=0).astype(BF16),
            (k * jnp.exp(bc[c - 1:c, :] - bc)).astype(BF16))


def _chunk_exact(q, k, v, bc, st, mask):
    o = lax.dot_general((q * jnp.exp(bc)).astype(BF16), st.astype(BF16),
                        (((1,), (1,)), ((), ())), preferred_element_type=F32)
    o = o + _offdiag(q, k, v, bc, DIAG_BLK, mask) + _diag(q, k, v, bc, DIAG_BLK)
    return o, _state_update_t(k, v.astype(BF16), bc, st)


def _prompt_scan_kernel(*refs, gla, seq, meta):
    n_in = 11 if gla else 8
    if gla:
        q_ref, k_ref, v_ref, gate_ref, al_ref, mk_ref, mv_ref, mal_ref, wa_ref, ba_ref, gn_ref = (
            refs[:n_in])
    else:
        q_ref, f_ref, v_ref, gate_ref, mf_ref, mv_ref, lbl_ref, gn_ref = refs[:n_in]
        lb = _lower_bound(lbl_ref)
    o_ref, sout_ref = refs[n_in:n_in + 2]
    st_ref, qs_ref, ks_ref, bc_ref, qd_ref, qt_ref, kt_ref, kdec_ref, eb_ref = refs[n_in + 2:]
    trips = seq // (CHUNK * CHUNKS_PER_TRIP)
    stack = kt_ref.shape[0] // (seq // CHUNK)
    nt = (((1,), (1,)), ((), ()))
    tn = (((0,), (0,)), ((), ()))

    def chunk_rows(ci, u, n=CHUNK):
        return pl.ds(pl.multiple_of((ci * CHUNKS_PER_TRIP + u) * n, n), n)

    if gla:
        _, mk, mv, mg = _gla_inputs(mk_ref[...], mk_ref[...], mv_ref[...], mal_ref[...],
                                    wa_ref[...], ba_ref[...])
    else:
        _, mk, mv, mg = _hgrn_inputs(mf_ref[...], mf_ref[...], mv_ref[...], lb)
    mbc = jnp.dot(_tri(meta, meta), mg, precision=HIGHEST, preferred_element_type=F32)
    st_ref[...] = _state_update_t(mk, mv.astype(BF16), mbc, None)

    def pass_a(ci, carry):
        gmin, kmax = carry
        trip_rows = CHUNK * CHUNKS_PER_TRIP
        rows = pl.ds(pl.multiple_of(ci * trip_rows, trip_rows), trip_rows)
        if gla:
            q, k, _, g = _gla_inputs(q_ref[rows, :], k_ref[rows, :], None, al_ref[rows, :],
                                     wa_ref[...], ba_ref[...])
        else:
            q, k, _, g = _hgrn_inputs(q_ref[rows, :], f_ref[rows, :], None, lb)
        qs_ref[rows, :] = q
        ks_ref[rows, :] = k
        for u in range(CHUNKS_PER_TRIP):
            sl = slice(u * CHUNK, (u + 1) * CHUNK)
            bc = _cumsum_chunk(g[sl])
            qd, qt, kt, kdec = _rescaled_operands(q[sl], k[sl], bc)
            bc_ref[chunk_rows(ci, u), :] = bc
            qd_ref[chunk_rows(ci, u), :] = qd
            qt_ref[chunk_rows(ci, u), :] = qt
            kt_ref[chunk_rows(ci, u, stack), :] = kt
            kdec_ref[chunk_rows(ci, u), :] = kdec
            eb_ref[ci * CHUNKS_PER_TRIP + u] = jnp.broadcast_to(
                jnp.exp(bc[CHUNK - 1:CHUNK, :]), (SUBLANES, DK))
        return (jnp.minimum(gmin, jnp.min(g, axis=0, keepdims=True)),
                jnp.maximum(kmax, jnp.max(jnp.abs(k), axis=0, keepdims=True)))

    gmin, kmax = lax.fori_loop(0, trips, pass_a,
                               (jnp.zeros((1, DK), F32), jnp.zeros((1, DK), F32)))
    in_range = (jnp.min(gmin) >= FAST_MIN_LOG_GATE) & (jnp.max(kmax) <= FAST_MAX_KEY)

    @pl.when(in_range)
    def _rescaled():
        mask = _stack_mask(CHUNK, DIAG_BLK)
        units = range(CHUNKS_PER_MXU_TRIP)

        def rows_of(c, n=CHUNK):
            return pl.ds(pl.multiple_of(c * n, n), n)

        def pass_bc(ci, carry):
            cs = [ci * CHUNKS_PER_MXU_TRIP + u for u in units]
            vb = [v_ref[rows_of(c), :].astype(BF16) for c in cs]
            a = [lax.dot_general(qt_ref[rows_of(c), :], kt_ref[rows_of(c, stack), :], nt,
                                 preferred_element_type=F32) for c in cs]
            am = [(x * mask).astype(BF16) for x in a]
            vt = [jnp.concatenate([x[:(i + 1) * DIAG_BLK] for i in range(CHUNK // DIAG_BLK)], axis=0)
                  for x in vb]
            oi = [jnp.dot(am[u], vt[u], preferred_element_type=F32) for u in units]
            inc = [lax.dot_general(vb[u], kdec_ref[rows_of(cs[u]), :], tn,
                                   preferred_element_type=F32) for u in units]
            sts = [st_ref[...]]
            for u in units:
                sts.append(sts[-1] * eb_ref[cs[u]][0:1, :] + inc[u])
            o = [oi[u] + lax.dot_general(qd_ref[rows_of(cs[u]), :], sts[u].astype(BF16), nt,
                                         preferred_element_type=F32) for u in units]
            for u in units:
                o_ref[rows_of(cs[u]), :] = _head_out(o[u], gn_ref[...], gate_ref[rows_of(cs[u]), :])
            st_ref[...] = sts[-1]
            return carry

        lax.fori_loop(0, seq // (CHUNK * CHUNKS_PER_MXU_TRIP), pass_bc, 0)

    @pl.when(jnp.logical_not(in_range))
    def _exact():
        mask = _off_mask(CHUNK, DIAG_BLK)

        def body(ci, carry):
            st = st_ref[...]
            for u in range(CHUNKS_PER_TRIP):
                rows = chunk_rows(ci, u)
                o, st = _chunk_exact(qs_ref[rows, :], ks_ref[rows, :], v_ref[rows, :],
                                     bc_ref[rows, :], st, mask)
                o_ref[rows, :] = _head_out(o, gn_ref[...], gate_ref[rows, :])
            st_ref[...] = st
            return carry

        lax.fori_loop(0, trips, body, 0)

    sout_ref[...] = st_ref[...].T


def _sample_scan_kernel(*refs, gla, steps):
    if gla:
        (q_ref, k_ref, v_ref, gate_ref, al_ref, wa_ref, ba_ref, gn_ref, s0_ref,
         o_ref, sout_ref) = refs
        q, k, v, g = _gla_inputs(q_ref[...], k_ref[...], v_ref[...], al_ref[...],
                                 wa_ref[...], ba_ref[...])
    else:
        (q_ref, f_ref, v_ref, gate_ref, lbl_ref, gn_ref, s0_ref, o_ref, sout_ref) = refs
        q, k, v, g = _hgrn_inputs(q_ref[...], f_ref[...], v_ref[...], _lower_bound(lbl_ref))
    v = v.astype(F32)

    rows = SAMPLE_SEQS * steps
    bc = jnp.dot(_tri(rows, steps), g, precision=HIGHEST, preferred_element_type=F32)
    qd = (q * jnp.exp(bc)).astype(BF16)
    inter = []
    for b in range(SAMPLE_SEQS):
        sl = slice(b * steps, (b + 1) * steps)
        s_prev = s0_ref[b]
        inter.append(jnp.dot(qd[sl], s_prev.astype(BF16), preferred_element_type=F32))
        last = (b + 1) * steps - 1
        sout_ref[b] = _state_update(k[sl], v[sl], g[sl], bc[sl], bc[last:last + 1, :], s_prev)
    o = jnp.concatenate(inter, axis=0) + _diag(q, k, v, bc, steps)
    o_ref[...] = _head_out(o, gn_ref[...], gate_ref[...])


def _proj_spec(rows, width, c0, per_head, row_block, tile_width=TN_PROJ):
    per_tile = tile_width // width

    def index(b, h):
        g = c0 * LANES // width + h * per_head
        return (g // per_tile, row_block(b), g % per_tile)

    return pl.BlockSpec((None, rows, width), index)


def _prompt_scan(p, pm, *, gla, batch, seq, small, name):
    heads, dv = (H_B, DV_B) if gla else (H_A, DV_A)
    n_chunks = seq // CHUNK
    blocks = CHUNK // DIAG_BLK
    stack = DIAG_BLK * blocks * (blocks + 1) // 2
    p16, pf, pa = p
    pm16, pmf, pma = pm
    tok = functools.partial(_proj_spec, seq, row_block=lambda b: b)
    met = functools.partial(_proj_spec, N_META, row_block=lambda b: 0)
    vec = lambda width: pl.BlockSpec((1, width), lambda b, h: (0, h))
    if gla:
        wa, ba, gn = small
        in_specs = [tok(DK, COL_QB, 1), tok(DK, COL_KB, 1), tok(dv, COL_VB, 1), tok(dv, COL_RB, 1),
                    tok(LANES, 0, 0, tile_width=LANES),
                    met(DK, COL_KB, 1), met(dv, COL_VB, 1), met(LANES, 0, 0, tile_width=LANES),
                    pl.BlockSpec((LANES, DK), lambda b, h: (0, h)), vec(DK), vec(dv)]
        args = (p16, p16, p16, p16, pa, pm16, pm16, pma, wa, ba, gn)
    else:
        lbl, gn = small
        in_specs = [tok(DK, COL_QA, 1), tok(DK, 0, 1), tok(dv, COL_IA, 1), tok(dv, COL_GA, 1),
                    met(DK, 0, 1), met(dv, COL_IA, 1),
                    pl.BlockSpec((2, DK), lambda b, h: (0, h)), vec(dv)]
        args = (p16, pf, p16, p16, pmf, pm16, lbl, gn)
    return pl.pallas_call(
        functools.partial(_prompt_scan_kernel, gla=gla, seq=seq, meta=N_META),
        grid=(batch, heads),
        in_specs=in_specs,
        out_specs=[pl.BlockSpec((seq, dv), lambda b, h: (b, h)),
                   pl.BlockSpec((None, None, DK, dv), lambda b, h: (b, h, 0, 0))],
        out_shape=[jax.ShapeDtypeStruct((batch * seq, heads * dv), BF16),
                   jax.ShapeDtypeStruct((batch, heads, DK, dv), F32)],
        scratch_shapes=[
            pltpu.VMEM((dv, DK), F32),
            pltpu.VMEM((seq, DK), F32),
            pltpu.VMEM((seq, DK), F32),
            pltpu.VMEM((seq, DK), F32),
            pltpu.VMEM((seq, DK), BF16),
            pltpu.VMEM((seq, DK), BF16),
            pltpu.VMEM((n_chunks * stack, DK), BF16),
            pltpu.VMEM((seq, DK), BF16),
            pltpu.VMEM((n_chunks, SUBLANES, DK), F32),
        ],
        compiler_params=_params("parallel", "parallel"),
        name=name,
    )(*args)


def _sample_scan(p, s0, *, gla, batch, steps, small, name):
    heads, dv = (H_B, DV_B) if gla else (H_A, DV_A)
    rows = SAMPLE_SEQS * steps
    p16, pf, pa = p
    tok = functools.partial(_proj_spec, rows, row_block=lambda b: b)
    vec = lambda width: pl.BlockSpec((1, width), lambda b, h: (0, h))
    state = pl.BlockSpec((SAMPLE_SEQS, None, DK, dv), lambda b, h: (b, h, 0, 0))
    if gla:
        wa, ba, gn = small
        in_specs = [tok(DK, COL_QB, 1), tok(DK, COL_KB, 1), tok(dv, COL_VB, 1), tok(dv, COL_RB, 1),
                    tok(LANES, 0, 0, tile_width=LANES),
                    pl.BlockSpec((LANES, DK), lambda b, h: (0, h)), vec(DK), vec(dv), state]
        args = (p16, p16, p16, p16, pa, wa, ba, gn, s0)
    else:
        lbl, gn = small
        in_specs = [tok(DK, COL_QA, 1), tok(DK, 0, 1), tok(dv, COL_IA, 1), tok(dv, COL_GA, 1),
                    pl.BlockSpec((2, DK), lambda b, h: (0, h)), vec(dv), state]
        args = (p16, pf, p16, p16, lbl, gn, s0)
    return pl.pallas_call(
        functools.partial(_sample_scan_kernel, gla=gla, steps=steps),
        grid=(batch // SAMPLE_SEQS, heads),
        in_specs=in_specs,
        out_specs=[pl.BlockSpec((rows, dv), lambda b, h: (b, h)), state],
        out_shape=[jax.ShapeDtypeStruct((batch * steps, heads * dv), BF16),
                   jax.ShapeDtypeStruct((batch, heads, DK, dv), F32)],
        compiler_params=_params("parallel", "parallel"),
        name=name,
    )(*args)


def kernel(x_prompt, x_sample, state_hgrn, state_gla, meta_tokens, lb_logits, ffn1_norm, w_ffn1_in,
           w_ffn1_out, mix_norm, w_in, w_alpha_up, b_alpha, gnorm_a, gnorm_b, w_out, ffn2_norm,
           w_ffn2_in, w_ffn2_out, final_norm):
    batch, seq, _ = x_prompt.shape
    dec_batch, steps, _ = x_sample.shape

    w1i, w1o = w_ffn1_in[0].astype(BF16), w_ffn1_out[0].astype(BF16)
    w2i, w2o = w_ffn2_in[0].astype(BF16), w_ffn2_out[0].astype(BF16)
    w_in_t = w_in[0].T
    wop = w_out[0].astype(BF16)
    wa = jnp.pad(w_alpha_up[0], ((0, LANES - ALPHA_RANK), (0, 0))).astype(BF16)
    n1, nm, n2, nf = ffn1_norm, mix_norm, ffn2_norm, final_norm[None]
    small_a = (lb_logits, gnorm_a)
    small_b = (wa, b_alpha, gnorm_b)

    def front(x, tm, tag):
        h1 = _ffn(x, n1, w1i, w1o, nf, tm=tm, apply_final_norm=False, name=f"ffn1_{tag}")
        return h1, _in_proj(h1, nm, w_in_t, tm=min(h1.shape[0], TM_PROJ), name=f"inproj_{tag}")

    def back(oa, ob, h1, tm, tag):
        h2 = _out_proj(oa, ob, h1, wop, tm=tm, name=f"outproj_{tag}")
        return _ffn(h2, n2, w2i, w2o, nf, tm=tm, apply_final_norm=True, name=f"ffn2_{tag}")

    _, pm = front(meta_tokens, N_META, "meta")

    h1p, pp = front(x_prompt.reshape(batch * seq, D_MODEL), 512, "prompt")
    oa_p, sa_p = _prompt_scan(pp, pm, gla=False, batch=batch, seq=seq, small=small_a,
                              name="scan_hgrn_prompt")
    ob_p, sb_p = _prompt_scan(pp, pm, gla=True, batch=batch, seq=seq, small=small_b,
                              name="scan_gla_prompt")
    y_p = back(oa_p, ob_p, h1p, 512, "prompt")

    h1s, ps = front(x_sample.reshape(dec_batch * steps, D_MODEL), 512, "sample")
    oa_s, sa_s = _sample_scan(ps, state_hgrn[0], gla=False, batch=dec_batch, steps=steps,
                              small=small_a, name="scan_hgrn_sample")
    ob_s, sb_s = _sample_scan(ps, state_gla[0], gla=True, batch=dec_batch, steps=steps,
                              small=small_b, name="scan_gla_sample")
    y_s = back(oa_s, ob_s, h1s, 512, "sample")

    return (y_p.reshape(batch, seq, D_MODEL), y_s.reshape(dec_batch, steps, D_MODEL),
            sa_p[None], sb_p[None], sa_s[None], sb_s[None])
```

```python
import functools

import jax
import jax.numpy as jnp
from jax import lax
from jax.experimental import pallas as pl
from jax.experimental.pallas import tpu as pltpu

F32 = jnp.float32
BF16 = jnp.bfloat16
HIGHEST = lax.Precision.HIGHEST

D_MODEL = 2048
N_META = 16
DK = 128
H_A, DV_A = 8, 128
H_B, DV_B = 4, 256
D_HALF = H_A * DV_A
ALPHA_RANK = 16
GATE_TEMP = 16.0
D_FF = 5632
CHUNK = 64
EPS = 1e-6
IN_PROJ_WIDTHS = (D_HALF, D_HALF, D_HALF, D_HALF, H_B * DK, H_B * DK, D_HALF, D_HALF, ALPHA_RANK)

LANES = 128
SUBLANES = 8
VMEM_LIMIT_BYTES = 56 * 1024 * 1024

COL_QA, COL_IA, COL_GA, COL_QB, COL_KB, COL_VB, COL_RB = 0, 8, 16, 24, 28, 32, 40
P16_COLS = 3 * D_HALF + 2 * H_B * DK + 2 * D_HALF

TF = 512
TN_PROJ = 512
TM_PROJ = 1024
N16_TILES = P16_COLS // TN_PROJ
DIAG_BLK = 16
SAMPLE_SEQS = 32
CHUNKS_PER_TRIP = 4
CHUNKS_PER_MXU_TRIP = 8

FAST_MIN_LOG_GATE = -60.0 / (DIAG_BLK - 1)
FAST_MAX_KEY = 1e10


def _rms(x, w):
    return x * lax.rsqrt(jnp.mean(x * x, axis=-1, keepdims=True) + EPS) * w


def _silu(x):
    return x * jax.nn.sigmoid(x)


def _params(*sem):
    return pltpu.CompilerParams(dimension_semantics=sem, vmem_limit_bytes=VMEM_LIMIT_BYTES)


def _ffn_kernel(h_ref, nw_ref, wg_ref, wu_ref, wo_ref, fnw_ref, out_ref, u_ref, acc_ref, *,
                apply_final_norm):
    j = pl.program_id(1)

    @pl.when(j == 0)
    def _init():
        u_ref[...] = _rms(h_ref[...], nw_ref[...]).astype(BF16)
        acc_ref[...] = jnp.zeros_like(acc_ref)

    u = u_ref[...]
    g = jnp.dot(u, wg_ref[...], preferred_element_type=F32)
    up = jnp.dot(u, wu_ref[...], preferred_element_type=F32)
    a = (_silu(g) * up).astype(BF16)
    acc_ref[...] += jnp.dot(a, wo_ref[...], preferred_element_type=F32)

    @pl.when(j == pl.num_programs(1) - 1)
    def _finish():
        y = h_ref[...] + 0.5 * acc_ref[...]
        if apply_final_norm:
            y = _rms(y, fnw_ref[...])
        out_ref[...] = y


def _ffn(h, norm_w, w_in, w_out, final_norm_w, *, tm, apply_final_norm, name):
    m = h.shape[0]
    nf = D_FF // TF
    return pl.pallas_call(
        functools.partial(_ffn_kernel, apply_final_norm=apply_final_norm),
        grid=(m // tm, nf),
        in_specs=[
            pl.BlockSpec((tm, D_MODEL), lambda i, j: (i, 0)),
            pl.BlockSpec((1, D_MODEL), lambda i, j: (0, 0)),
            pl.BlockSpec((D_MODEL, TF), lambda i, j: (0, j)),
            pl.BlockSpec((D_MODEL, TF), lambda i, j: (0, j + nf)),
            pl.BlockSpec((TF, D_MODEL), lambda i, j: (j, 0)),
            pl.BlockSpec((1, D_MODEL), lambda i, j: (0, 0)),
        ],
        out_specs=pl.BlockSpec((tm, D_MODEL), lambda i, j: (i, 0)),
        out_shape=jax.ShapeDtypeStruct((m, D_MODEL), F32),
        scratch_shapes=[pltpu.VMEM((tm, D_MODEL), BF16), pltpu.VMEM((tm, D_MODEL), F32)],
        compiler_params=_params("parallel", "arbitrary"),
        name=name,
    )(h, norm_w, w_in, w_in, w_out, final_norm_w)


def _proj_kernel(h_ref, nw_ref, w_ref, wal_ref, p16_ref, pf_ref, pa_ref, u_ref):
    j = pl.program_id(1)

    @pl.when(j == 0)
    def _init():
        u = _rms(h_ref[...], nw_ref[...]).astype(BF16)
        u_ref[...] = u
        wal = jnp.concatenate([wal_ref[...].astype(BF16),
                               jnp.zeros((LANES - ALPHA_RANK, D_MODEL), BF16)], axis=0)
        pa_ref[...] = lax.dot_general(u, wal, (((1,), (1,)), ((), ())),
                                      preferred_element_type=F32)

    def project():
        return lax.dot_general(u_ref[...], w_ref[...].astype(BF16), (((1,), (1,)), ((), ())),
                               preferred_element_type=F32)

    @pl.when(j < N16_TILES)
    def _narrow():
        p16_ref[...] = project().astype(BF16)

    @pl.when(j >= N16_TILES)
    def _wide():
        pf_ref[...] = project()


def _w_in_tile(j):
    fa_lo, fa_hi = D_HALF // TN_PROJ, 2 * D_HALF // TN_PROJ
    moved = fa_hi - fa_lo
    return jnp.where(j < fa_lo, j, jnp.where(j < N16_TILES, j + moved, j - N16_TILES + fa_lo))


def _in_proj(h, norm_w, w, *, tm, name):
    m = h.shape[0]
    return pl.pallas_call(
        _proj_kernel,
        grid=(m // tm, (P16_COLS + D_HALF) // TN_PROJ),
        in_specs=[
            pl.BlockSpec((tm, D_MODEL), lambda i, j: (i, 0)),
            pl.BlockSpec((1, D_MODEL), lambda i, j: (0, 0)),
            pl.BlockSpec((TN_PROJ, D_MODEL), lambda i, j: (_w_in_tile(j), 0)),
            pl.BlockSpec((ALPHA_RANK, D_MODEL), lambda i, j: ((P16_COLS + D_HALF) // ALPHA_RANK, 0)),
        ],
        out_specs=[
            pl.BlockSpec((None, tm, TN_PROJ), lambda i, j: (jnp.minimum(j, N16_TILES - 1), i, 0)),
            pl.BlockSpec((None, tm, TN_PROJ), lambda i, j: (jnp.maximum(j - N16_TILES, 0), i, 0)),
            pl.BlockSpec((None, tm, LANES), lambda i, j: (0, i, 0)),
        ],
        out_shape=[jax.ShapeDtypeStruct((N16_TILES, m, TN_PROJ), BF16),
                   jax.ShapeDtypeStruct((D_HALF // TN_PROJ, m, TN_PROJ), F32),
                   jax.ShapeDtypeStruct((1, m, LANES), F32)],
        scratch_shapes=[pltpu.VMEM((tm, D_MODEL), BF16)],
        compiler_params=_params("parallel", "arbitrary"),
        name=name,
    )(h, norm_w, w, w)


def _out_proj_kernel(oa_ref, ob_ref, h_ref, wa_ref, wb_ref, out_ref):
    out_ref[...] = (h_ref[...]
                    + jnp.dot(oa_ref[...], wa_ref[...], preferred_element_type=F32)
                    + jnp.dot(ob_ref[...], wb_ref[...], preferred_element_type=F32))


def _out_proj(oa, ob, h, w, *, tm, name):
    m = h.shape[0]
    return pl.pallas_call(
        _out_proj_kernel,
        grid=(m // tm,),
        in_specs=[
            pl.BlockSpec((tm, D_HALF), lambda i: (i, 0)),
            pl.BlockSpec((tm, D_HALF), lambda i: (i, 0)),
            pl.BlockSpec((tm, D_MODEL), lambda i: (i, 0)),
            pl.BlockSpec((D_HALF, D_MODEL), lambda i: (0, 0)),
            pl.BlockSpec((D_HALF, D_MODEL), lambda i: (1, 0)),
        ],
        out_specs=pl.BlockSpec((tm, D_MODEL), lambda i: (i, 0)),
        out_shape=jax.ShapeDtypeStruct((m, D_MODEL), F32),
        compiler_params=_params("parallel"),
        name=name,
    )(oa, ob, h, w, w)


def _tri(c, seq_len):
    r = lax.broadcasted_iota(jnp.int32, (c, c), 0)
    s = lax.broadcasted_iota(jnp.int32, (c, c), 1)
    return jnp.where((r >= s) & ((r ^ s) < seq_len), 1.0, 0.0).astype(F32)


def _off_mask(c, blk):
    nb = c // blk
    rows = nb * (nb - 1) // 2 * blk
    r = lax.broadcasted_iota(jnp.int32, (c, rows), 0) >> (blk.bit_length() - 1)
    s = lax.broadcasted_iota(jnp.int32, (c, rows), 1)
    seg = jnp.zeros((c, rows), jnp.int32)
    for i in range(1, nb):
        seg = seg + jnp.where(s >= blk * i * (i - 1) // 2, 1, 0)
    return jnp.where(r == seg, 1.0, 0.0).astype(F32)


def _diag(q, k, v, bc, blk):
    c = q.shape[0]
    row = lax.broadcasted_iota(jnp.int32, (blk, 1), 0)
    outs = []
    for i in range(c // blk):
        sl = slice(i * blk, (i + 1) * blk)
        qi, ki, vi, bi = q[sl], k[sl], v[sl].astype(F32), bc[sl]
        acc = jnp.zeros((blk, v.shape[1]), F32)
        for s in range(blk):
            e = jnp.exp(jnp.minimum(bi - bi[s:s + 1, :], 0.0))
            col = jnp.sum(qi * ki[s:s + 1, :] * e, axis=-1, keepdims=True)
            col = jnp.where(row >= s, col, 0.0)
            acc = acc + col * vi[s:s + 1, :]
        outs.append(acc)
    return outs[0] if len(outs) == 1 else jnp.concatenate(outs, axis=0)


def _offdiag(q, k, v, bc, blk, mask):
    c = q.shape[0]
    nb = c // blk
    qs, ks, vs = [jnp.zeros((blk, DK), F32)], [], []
    for i in range(1, nb):
        lo = i * blk
        ref = bc[lo:lo + 1, :]
        qs.append(q[lo:lo + blk] * jnp.exp(bc[lo:lo + blk] - ref))
        ks.append(k[:lo] * jnp.exp(ref - bc[:lo]))
        vs.append(v[:lo])
    qt = jnp.concatenate(qs, axis=0).astype(BF16)
    kt = jnp.concatenate(ks, axis=0).astype(BF16)
    vt = jnp.concatenate(vs, axis=0).astype(BF16)
    a = lax.dot_general(qt, kt, (((1,), (1,)), ((), ())), preferred_element_type=F32)
    return jnp.dot((a * mask).astype(BF16), vt, preferred_element_type=F32)


def _hgrn_inputs(qa, fa, ia, lb):
    fg = lb + (1.0 - lb) * jax.nn.sigmoid(fa)
    return _silu(qa.astype(F32)) * (DK ** -0.5), 1.0 - fg, ia, jnp.log(fg)


def _gla_inputs(qb, kb, vb, al, wa, ba):
    alpha = jnp.dot(al.astype(BF16), wa, preferred_element_type=F32) + ba
    logsig = jnp.minimum(alpha, 0.0) - jnp.log1p(jnp.exp(-jnp.abs(alpha)))
    return qb.astype(F32) * (DK ** -0.5), kb.astype(F32), vb, logsig / GATE_TEMP


def _head_out(o, gn, gate):
    return (_rms(o, gn) * _silu(gate.astype(F32))).astype(BF16)


def _lower_bound(lbl_ref):
    x = lbl_ref[...]
    e = jnp.exp(x - jnp.max(x, axis=0, keepdims=True))
    return e[0:1, :] / jnp.sum(e, axis=0, keepdims=True)


def _stack_mask(c, blk):
    nb = c // blk
    rows = blk * nb * (nb + 1) // 2
    t = lax.broadcasted_iota(jnp.int32, (c, rows), 0)
    r = lax.broadcasted_iota(jnp.int32, (c, rows), 1)
    seg = jnp.zeros((c, rows), jnp.int32)
    off = jnp.zeros((c, rows), jnp.int32)
    for i in range(1, nb):
        start = blk * i * (i + 1) // 2
        seg = seg + jnp.where(r >= start, 1, 0)
        off = jnp.where(r >= start, start, off)
    keep = ((t >> (blk.bit_length() - 1)) == seg) & (r - off <= t)
    return jnp.where(keep, 1.0, 0.0).astype(F32)


def _state_update_t(k, vb, bc, st):
    c = k.shape[0]
    blast = bc[c - 1:c, :]
    kdec = (k * jnp.exp(blast - bc)).astype(BF16)
    upd = lax.dot_general(vb, kdec, (((0,), (0,)), ((), ())), preferred_element_type=F32)
    if st is None:
        return upd
    return st * jnp.exp(blast) + upd


def _scan_sublanes(p):
    row = lax.broadcasted_iota(jnp.int32, (SUBLANES, DK), 0)
    for s in (1, 2, 4):
        p = p + jnp.where(row >= s, pltpu.roll(p, s, axis=0), 0.0)
    return p


def _cumsum_chunk(g):
    out, carry = [], None
    for j in range(CHUNK // SUBLANES):
        p = _scan_sublanes(g[j * SUBLANES:(j + 1) * SUBLANES])
        if carry is not None:
            p = p + carry
        carry = p[SUBLANES - 1:SUBLANES, :]
        out.append(p)
    return jnp.concatenate(out, axis=0)


def _rescaled_operands(q, k, bc):
    c, blk = CHUNK, DIAG_BLK
    qs, ks = [], []
    for i in range(c // blk):
        lo, hi = i * blk, (i + 1) * blk
        ref = bc[lo:lo + 1, :]
        qs.append(q[lo:hi] * jnp.exp(bc[lo:hi] - ref))
        ks.append(k[:hi] * jnp.exp(ref - bc[:hi]))
    return ((q * jnp.exp(bc)).astype(BF16), jnp.concatenate(qs, axis=0).astype(BF16),
            jnp.concatenate(ks, axis=0).astype(BF16),
            (k * jnp.exp(bc[c - 1:c, :] - bc)).astype(BF16))


def _chunk_exact(q, k, v, bc, st, mask):
    o = lax.dot_general((q * jnp.exp(bc)).astype(BF16), st.astype(BF16),
                        (((1,), (1,)), ((), ())), preferred_element_type=F32)
    o = o + _offdiag(q, k, v, bc, DIAG_BLK, mask) + _diag(q, k, v, bc, DIAG_BLK)
    return o, _state_update_t(k, v.astype(BF16), bc, st)


def _prompt_scan_kernel(*refs, gla, seq, meta):
    n_in = 11 if gla else 8
    if gla:
        q_ref, k_ref, v_ref, gate_ref, al_ref, mk_ref, mv_ref, mal_ref, wa_ref, ba_ref, gn_ref = (
            refs[:n_in])
    else:
        q_ref, f_ref, v_ref, gate_ref, mf_ref, mv_ref, lbl_ref, gn_ref = refs[:n_in]
        lb = _lower_bound(lbl_ref)
    o_ref, sout_ref = refs[n_in:n_in + 2]
    st_ref, qs_ref, ks_ref, bc_ref, qd_ref, qt_ref, kt_ref, kdec_ref, eb_ref = refs[n_in + 2:]
    trips = seq // (CHUNK * CHUNKS_PER_TRIP)
    stack = kt_ref.shape[0] // (seq // CHUNK)
    nt = (((1,), (1,)), ((), ()))
    tn = (((0,), (0,)), ((), ()))

    def chunk_rows(ci, u, n=CHUNK):
        return pl.ds(pl.multiple_of((ci * CHUNKS_PER_TRIP + u) * n, n), n)

    if gla:
        _, mk, mv, mg = _gla_inputs(mk_ref[...], mk_ref[...], mv_ref[...], mal_ref[...],
                                    wa_ref[...], ba_ref[...])
    else:
        _, mk, mv, mg = _hgrn_inputs(mf_ref[...], mf_ref[...], mv_ref[...], lb)
    mbc = jnp.dot(_tri(meta, meta), mg, precision=HIGHEST, preferred_element_type=F32)
    st_ref[...] = _state_update_t(mk, mv.astype(BF16), mbc, None)

    def pass_a(ci, carry):
        gmin, kmax = carry
        trip_rows = CHUNK * CHUNKS_PER_TRIP
        rows = pl.ds(pl.multiple_of(ci * trip_rows, trip_rows), trip_rows)
        if gla:
            q, k, _, g = _gla_inputs(q_ref[rows, :], k_ref[rows, :], None, al_ref[rows, :],
                                     wa_ref[...], ba_ref[...])
        else:
            q, k, _, g = _hgrn_inputs(q_ref[rows, :], f_ref[rows, :], None, lb)
        qs_ref[rows, :] = q
        ks_ref[rows, :] = k
        for u in range(CHUNKS_PER_TRIP):
            sl = slice(u * CHUNK, (u + 1) * CHUNK)
            bc = _cumsum_chunk(g[sl])
            qd, qt, kt, kdec = _rescaled_operands(q[sl], k[sl], bc)
            bc_ref[chunk_rows(ci, u), :] = bc
            qd_ref[chunk_rows(ci, u), :] = qd
            qt_ref[chunk_rows(ci, u), :] = qt
            kt_ref[chunk_rows(ci, u, stack), :] = kt
            kdec_ref[chunk_rows(ci, u), :] = kdec
            eb_ref[ci * CHUNKS_PER_TRIP + u] = jnp.broadcast_to(
                jnp.exp(bc[CHUNK - 1:CHUNK, :]), (SUBLANES, DK))
        return (jnp.minimum(gmin, jnp.min(g, axis=0, keepdims=True)),
                jnp.maximum(kmax, jnp.max(jnp.abs(k), axis=0, keepdims=True)))

    gmin, kmax = lax.fori_loop(0, trips, pass_a,
                               (jnp.zeros((1, DK), F32), jnp.zeros((1, DK), F32)))
    in_range = (jnp.min(gmin) >= FAST_MIN_LOG_GATE) & (jnp.max(kmax) <= FAST_MAX_KEY)

    @pl.when(in_range)
    def _rescaled():
        mask = _stack_mask(CHUNK, DIAG_BLK)
        units = range(CHUNKS_PER_MXU_TRIP)

        def rows_of(c, n=CHUNK):
            return pl.ds(pl.multiple_of(c * n, n), n)

        def pass_bc(ci, carry):
            cs = [ci * CHUNKS_PER_MXU_TRIP + u for u in units]
            vb = [v_ref[rows_of(c), :].astype(BF16) for c in cs]
            a = [lax.dot_general(qt_ref[rows_of(c), :], kt_ref[rows_of(c, stack), :], nt,
                                 preferred_element_type=F32) for c in cs]
            am = [(x * mask).astype(BF16) for x in a]
            vt = [jnp.concatenate([x[:(i + 1) * DIAG_BLK] for i in range(CHUNK // DIAG_BLK)], axis=0)
                  for x in vb]
            oi = [jnp.dot(am[u], vt[u], preferred_element_type=F32) for u in units]
            inc = [lax.dot_general(vb[u], kdec_ref[rows_of(cs[u]), :], tn,
                                   preferred_element_type=F32) for u in units]
            sts = [st_ref[...]]
            for u in units:
                sts.append(sts[-1] * eb_ref[cs[u]][0:1, :] + inc[u])
            o = [oi[u] + lax.dot_general(qd_ref[rows_of(cs[u]), :], sts[u].astype(BF16), nt,
                                         preferred_element_type=F32) for u in units]
            for u in units:
                o_ref[rows_of(cs[u]), :] = _head_out(o[u], gn_ref[...], gate_ref[rows_of(cs[u]), :])
            st_ref[...] = sts[-1]
            return carry

        lax.fori_loop(0, seq // (CHUNK * CHUNKS_PER_MXU_TRIP), pass_bc, 0)

    @pl.when(jnp.logical_not(in_range))
    def _exact():
        mask = _off_mask(CHUNK, DIAG_BLK)

        def body(ci, carry):
            st = st_ref[...]
            for u in range(CHUNKS_PER_TRIP):
                rows = chunk_rows(ci, u)
                o, st = _chunk_exact(qs_ref[rows, :], ks_ref[rows, :], v_ref[rows, :],
                                     bc_ref[rows, :], st, mask)
                o_ref[rows, :] = _head_out(o, gn_ref[...], gate_ref[rows, :])
            st_ref[...] = st
            return carry

        lax.fori_loop(0, trips, body, 0)

    sout_ref[...] = st_ref[...].T


def _sample_scan_kernel(*refs, gla, steps):
    if gla:
        (q_ref, k_ref, v_ref, gate_ref, al_ref, wa_ref, ba_ref, gn_ref, s0_ref,
         o_ref, sout_ref) = refs
        q, k, v, g = _gla_inputs(q_ref[...], k_ref[...], v_ref[...], al_ref[...],
                                 wa_ref[...], ba_ref[...])
    else:
        (q_ref, f_ref, v_ref, gate_ref, lbl_ref, gn_ref, s0_ref, o_ref, sout_ref) = refs
        q, k, v, g = _hgrn_inputs(q_ref[...], f_ref[...], v_ref[...], _lower_bound(lbl_ref))
    v32 = v.astype(F32)
    rows = SAMPLE_SEQS * steps
    groups = [slice(b * steps, (b + 1) * steps) for b in range(SAMPLE_SEQS)]
    tn = (((0,), (0,)), ((), ()))

    bc = jnp.concatenate([_scan_sublanes(g[sl]) for sl in groups], axis=0)

    def repeated(r):
        return jnp.concatenate([jnp.broadcast_to(bc[sl][r:r + 1, :], (steps, DK)) for sl in groups],
                               axis=0)

    qd = (q * jnp.exp(bc)).astype(BF16)
    kdec = (k * jnp.exp(repeated(steps - 1) - bc)).astype(BF16)
    r = lax.broadcasted_iota(jnp.int32, (rows, SAMPLE_SEQS), 0)
    c = lax.broadcasted_iota(jnp.int32, (rows, SAMPLE_SEQS), 1)
    selector = jnp.where((r >> (steps.bit_length() - 1)) == c, 1.0, 0.0).astype(F32)
    decay = jnp.exp(lax.dot_general(g, selector, tn, precision=HIGHEST,
                                    preferred_element_type=F32))

    inter = [jnp.dot(qd[sl], s0_ref[b].astype(BF16), preferred_element_type=F32)
             for b, sl in enumerate(groups)]
    upd = [lax.dot_general(kdec[sl], v32[sl].astype(BF16), tn, preferred_element_type=F32)
           for sl in groups]
    for b in range(SAMPLE_SEQS):
        sout_ref[b] = decay[:, b:b + 1] * s0_ref[b] + upd[b]
    o = jnp.concatenate(inter, axis=0) + _diag(q, k, v32, bc, steps)
    o_ref[...] = _head_out(o, gn_ref[...], gate_ref[...])


def _proj_spec(rows, width, c0, per_head, row_block, tile_width=TN_PROJ):
    per_tile = tile_width // width

    def index(b, h):
        g = c0 * LANES // width + h * per_head
        return (g // per_tile, row_block(b), g % per_tile)

    return pl.BlockSpec((None, rows, width), index)


def _prompt_scan(p, pm, *, gla, batch, seq, small, name):
    heads, dv = (H_B, DV_B) if gla else (H_A, DV_A)
    n_chunks = seq // CHUNK
    blocks = CHUNK // DIAG_BLK
    stack = DIAG_BLK * blocks * (blocks + 1) // 2
    p16, pf, pa = p
    pm16, pmf, pma = pm
    tok = functools.partial(_proj_spec, seq, row_block=lambda b: b)
    met = functools.partial(_proj_spec, N_META, row_block=lambda b: 0)
    vec = lambda width: pl.BlockSpec((1, width), lambda b, h: (0, h))
    if gla:
        wa, ba, gn = small
        in_specs = [tok(DK, COL_QB, 1), tok(DK, COL_KB, 1), tok(dv, COL_VB, 1), tok(dv, COL_RB, 1),
                    tok(LANES, 0, 0, tile_width=LANES),
                    met(DK, COL_KB, 1), met(dv, COL_VB, 1), met(LANES, 0, 0, tile_width=LANES),
                    pl.BlockSpec((LANES, DK), lambda b, h: (0, h)), vec(DK), vec(dv)]
        args = (p16, p16, p16, p16, pa, pm16, pm16, pma, wa, ba, gn)
    else:
        lbl, gn = small
        in_specs = [tok(DK, COL_QA, 1), tok(DK, 0, 1), tok(dv, COL_IA, 1), tok(dv, COL_GA, 1),
                    met(DK, 0, 1), met(dv, COL_IA, 1),
                    pl.BlockSpec((2, DK), lambda b, h: (0, h)), vec(dv)]
        args = (p16, pf, p16, p16, pmf, pm16, lbl, gn)
    return pl.pallas_call(
        functools.partial(_prompt_scan_kernel, gla=gla, seq=seq, meta=N_META),
        grid=(batch, heads),
        in_specs=in_specs,
        out_specs=[pl.BlockSpec((seq, dv), lambda b, h: (b, h)),
                   pl.BlockSpec((None, None, DK, dv), lambda b, h: (b, h, 0, 0))],
        out_shape=[jax.ShapeDtypeStruct((batch * seq, heads * dv), BF16),
                   jax.ShapeDtypeStruct((batch, heads, DK, dv), F32)],
        scratch_shapes=[
            pltpu.VMEM((dv, DK), F32),
            pltpu.VMEM((seq, DK), F32),
            pltpu.VMEM((seq, DK), F32),
            pltpu.VMEM((seq, DK), F32),
            pltpu.VMEM((seq, DK), BF16),
            pltpu.VMEM((seq, DK), BF16),
            pltpu.VMEM((n_chunks * stack, DK), BF16),
            pltpu.VMEM((seq, DK), BF16),
            pltpu.VMEM((n_chunks, SUBLANES, DK), F32),
        ],
        compiler_params=_params("parallel", "parallel"),
        name=name,
    )(*args)


def _sample_scan(p, s0, *, gla, batch, steps, small, name):
    assert steps == SUBLANES, "the sample recurrence keeps one sequence per sublane group"
    heads, dv = (H_B, DV_B) if gla else (H_A, DV_A)
    rows = SAMPLE_SEQS * steps
    p16, pf, pa = p
    tok = functools.partial(_proj_spec, rows, row_block=lambda b: b)
    vec = lambda width: pl.BlockSpec((1, width), lambda b, h: (0, h))
    state = pl.BlockSpec((SAMPLE_SEQS, None, DK, dv), lambda b, h: (b, h, 0, 0))
    if gla:
        wa, ba, gn = small
        in_specs = [tok(DK, COL_QB, 1), tok(DK, COL_KB, 1), tok(dv, COL_VB, 1), tok(dv, COL_RB, 1),
                    tok(LANES, 0, 0, tile_width=LANES),
                    pl.BlockSpec((LANES, DK), lambda b, h: (0, h)), vec(DK), vec(dv), state]
        args = (p16, p16, p16, p16, pa, wa, ba, gn, s0)
    else:
        lbl, gn = small
        in_specs = [tok(DK, COL_QA, 1), tok(DK, 0, 1), tok(dv, COL_IA, 1), tok(dv, COL_GA, 1),
                    pl.BlockSpec((2, DK), lambda b, h: (0, h)), vec(dv), state]
        args = (p16, pf, p16, p16, lbl, gn, s0)
    return pl.pallas_call(
        functools.partial(_sample_scan_kernel, gla=gla, steps=steps),
        grid=(batch // SAMPLE_SEQS, heads),
        in_specs=in_specs,
        out_specs=[pl.BlockSpec((rows, dv), lambda b, h: (b, h)), state],
        out_shape=[jax.ShapeDtypeStruct((batch * steps, heads * dv), BF16),
                   jax.ShapeDtypeStruct((batch, heads, DK, dv), F32)],
        compiler_params=_params("parallel", "parallel"),
        name=name,
    )(*args)


def kernel(x_prompt, x_sample, state_hgrn, state_gla, meta_tokens, lb_logits, ffn1_norm, w_ffn1_in,
           w_ffn1_out, mix_norm, w_in, w_alpha_up, b_alpha, gnorm_a, gnorm_b, w_out, ffn2_norm,
           w_ffn2_in, w_ffn2_out, final_norm):
    batch, seq, _ = x_prompt.shape
    dec_batch, steps, _ = x_sample.shape

    w1i, w1o = w_ffn1_in[0].astype(BF16), w_ffn1_out[0].astype(BF16)
    w2i, w2o = w_ffn2_in[0].astype(BF16), w_ffn2_out[0].astype(BF16)
    w_in_t = w_in[0].T
    wop = w_out[0].astype(BF16)
    wa = jnp.pad(w_alpha_up[0], ((0, LANES - ALPHA_RANK), (0, 0))).astype(BF16)
    n1, nm, n2, nf = ffn1_norm, mix_norm, ffn2_norm, final_norm[None]
    small_a = (lb_logits, gnorm_a)
    small_b = (wa, b_alpha, gnorm_b)

    def front(x, tm, tag):
        h1 = _ffn(x, n1, w1i, w1o, nf, tm=tm, apply_final_norm=False, name=f"ffn1_{tag}")
        return h1, _in_proj(h1, nm, w_in_t, tm=min(h1.shape[0], TM_PROJ), name=f"inproj_{tag}")

    def back(oa, ob, h1, tm, tag):
        h2 = _out_proj(oa, ob, h1, wop, tm=tm, name=f"outproj_{tag}")
        return _ffn(h2, n2, w2i, w2o, nf, tm=tm, apply_final_norm=True, name=f"ffn2_{tag}")

    _, pm = front(meta_tokens, N_META, "meta")

    h1p, pp = front(x_prompt.reshape(batch * seq, D_MODEL), 512, "prompt")
    oa_p, sa_p = _prompt_scan(pp, pm, gla=False, batch=batch, seq=seq, small=small_a,
                              name="scan_hgrn_prompt")
    ob_p, sb_p = _prompt_scan(pp, pm, gla=True, batch=batch, seq=seq, small=small_b,
                              name="scan_gla_prompt")
    y_p = back(oa_p, ob_p, h1p, 512, "prompt")

    h1s, ps = front(x_sample.reshape(dec_batch * steps, D_MODEL), 512, "sample")
    oa_s, sa_s = _sample_scan(ps, state_hgrn[0], gla=False, batch=dec_batch, steps=steps,
                              small=small_a, name="scan_hgrn_sample")
    ob_s, sb_s = _sample_scan(ps, state_gla[0], gla=True, batch=dec_batch, steps=steps,
                              small=small_b, name="scan_gla_sample")
    y_s = back(oa_s, ob_s, h1s, 512, "sample")

    return (y_p.reshape(batch, seq, D_MODEL), y_s.reshape(dec_batch, steps, D_MODEL),
            sa_p[None], sb_p[None], sa_s[None], sb_s[None])
```

```python
import functools

import jax
import jax.numpy as jnp
from jax import lax
from jax.experimental import pallas as pl
from jax.experimental.pallas import tpu as pltpu

F32 = jnp.float32
BF16 = jnp.bfloat16
HIGHEST = lax.Precision.HIGHEST

D_MODEL = 2048
N_META = 16
DK = 128
H_A, DV_A = 8, 128
H_B, DV_B = 4, 256
D_HALF = H_A * DV_A
ALPHA_RANK = 16
GATE_TEMP = 16.0
D_FF = 5632
CHUNK = 64
EPS = 1e-6
IN_PROJ_WIDTHS = (D_HALF, D_HALF, D_HALF, D_HALF, H_B * DK, H_B * DK, D_HALF, D_HALF, ALPHA_RANK)

LANES = 128
SUBLANES = 8
VMEM_LIMIT_BYTES = 56 * 1024 * 1024

COL_QA, COL_IA, COL_GA, COL_QB, COL_KB, COL_VB, COL_RB = 0, 8, 16, 24, 28, 32, 40
P16_COLS = 3 * D_HALF + 2 * H_B * DK + 2 * D_HALF

TM = 512
TF = 512
TF_CAST = 256
TN_PROJ = 512
TM_PROJ = 1024
N16_TILES = P16_COLS // TN_PROJ
DIAG_BLK = 16
SAMPLE_SEQS = 32
CHUNKS_PER_TRIP = 4
CHUNKS_PER_MXU_TRIP = 8

FAST_MIN_LOG_GATE = -60.0 / (DIAG_BLK - 1)
FAST_MAX_KEY = 1e10


def _rms(x, w):
    return x * lax.rsqrt(jnp.mean(x * x, axis=-1, keepdims=True) + EPS) * w


def _silu(x):
    return x * jax.nn.sigmoid(x)


def _params(*sem):
    return pltpu.CompilerParams(dimension_semantics=sem, vmem_limit_bytes=VMEM_LIMIT_BYTES)


def _ffn_kernel(h_ref, nw_ref, wg_ref, wu_ref, wo_ref, fnw_ref, out_ref, *rest,
                apply_final_norm, emit_weights):
    u_ref = rest[-1]
    j = pl.program_id(1)

    @pl.when(j == 0)
    def _init():
        u_ref[...] = _rms(h_ref[...], nw_ref[...]).astype(BF16)
        out_ref[...] = jnp.zeros_like(out_ref)

    wg, wu, wo = (r[...].astype(BF16) for r in (wg_ref, wu_ref, wo_ref))
    if emit_weights:
        for dst, w in zip(rest[:3], (wg, wu, wo)):
            dst[...] = w
    u = u_ref[...]
    g = jnp.dot(u, wg, preferred_element_type=F32)
    up = jnp.dot(u, wu, preferred_element_type=F32)
    a = (_silu(g) * up).astype(BF16)
    out_ref[...] += jnp.dot(a, wo, preferred_element_type=F32)

    @pl.when(j == pl.num_programs(1) - 1)
    def _finish():
        y = h_ref[...] + 0.5 * out_ref[...]
        if apply_final_norm:
            y = _rms(y, fnw_ref[...])
        out_ref[...] = y


def _ffn(h, norm_w, weights, final_norm_w, *, tm, tf, apply_final_norm, emit_weights, name):
    m = h.shape[0]
    nf = D_FF // tf
    rows = pl.BlockSpec((tm, D_MODEL), lambda i, j: (i, 0),
                        pipeline_mode=pl.Buffered(1) if m == tm else None)
    vec = pl.BlockSpec((1, D_MODEL), lambda i, j: (0, 0))
    col_tile = pl.BlockSpec((D_MODEL, tf), lambda i, j: (0, j))
    row_tile = pl.BlockSpec((tf, D_MODEL), lambda i, j: (j, 0))
    out_specs = [rows]
    out_shape = [jax.ShapeDtypeStruct((m, D_MODEL), F32)]
    if emit_weights:
        assert m == tm, "weight casts are written once, by a single row tile"
        up_tile = pl.BlockSpec((D_MODEL, tf), lambda i, j: (0, j + nf))
        out_specs += [col_tile, col_tile, row_tile]
        out_shape += [jax.ShapeDtypeStruct((D_MODEL, D_FF), BF16)] * 2
        out_shape += [jax.ShapeDtypeStruct((D_FF, D_MODEL), BF16)]
    else:
        up_tile = col_tile
    res = pl.pallas_call(
        functools.partial(_ffn_kernel, apply_final_norm=apply_final_norm,
                          emit_weights=emit_weights),
        grid=(m // tm, nf),
        in_specs=[rows, vec, col_tile, up_tile, row_tile, vec],
        out_specs=out_specs,
        out_shape=out_shape,
        scratch_shapes=[pltpu.VMEM((tm, D_MODEL), BF16)],
        compiler_params=_params("parallel", "arbitrary"),
        name=name,
    )(h, norm_w, *weights, final_norm_w)
    return res if emit_weights else res[0]


def _proj_kernel(h_ref, nw_ref, w_ref, wal_ref, p16_ref, pf_ref, pa_ref, u_ref):
    j = pl.program_id(1)

    @pl.when(j == 0)
    def _init():
        u = _rms(h_ref[...], nw_ref[...]).astype(BF16)
        u_ref[...] = u
        wal = jnp.concatenate([wal_ref[...].astype(BF16),
                               jnp.zeros((LANES - ALPHA_RANK, D_MODEL), BF16)], axis=0)
        pa_ref[...] = lax.dot_general(u, wal, (((1,), (1,)), ((), ())),
                                      preferred_element_type=F32)

    def project():
        return lax.dot_general(u_ref[...], w_ref[...].astype(BF16), (((1,), (1,)), ((), ())),
                               preferred_element_type=F32)

    @pl.when(j < N16_TILES)
    def _narrow():
        p16_ref[...] = project().astype(BF16)

    @pl.when(j >= N16_TILES)
    def _wide():
        pf_ref[...] = project()


def _w_in_tile(j):
    fa_lo, fa_hi = D_HALF // TN_PROJ, 2 * D_HALF // TN_PROJ
    moved = fa_hi - fa_lo
    return jnp.where(j < fa_lo, j, jnp.where(j < N16_TILES, j + moved, j - N16_TILES + fa_lo))


def _in_proj(h, norm_w, w, *, tm, name):
    m = h.shape[0]
    return pl.pallas_call(
        _proj_kernel,
        grid=(m // tm, (P16_COLS + D_HALF) // TN_PROJ),
        in_specs=[
            pl.BlockSpec((tm, D_MODEL), lambda i, j: (i, 0)),
            pl.BlockSpec((1, D_MODEL), lambda i, j: (0, 0)),
            pl.BlockSpec((TN_PROJ, D_MODEL), lambda i, j: (_w_in_tile(j), 0)),
            pl.BlockSpec((ALPHA_RANK, D_MODEL), lambda i, j: ((P16_COLS + D_HALF) // ALPHA_RANK, 0)),
        ],
        out_specs=[
            pl.BlockSpec((None, tm, TN_PROJ), lambda i, j: (jnp.minimum(j, N16_TILES - 1), i, 0)),
            pl.BlockSpec((None, tm, TN_PROJ), lambda i, j: (jnp.maximum(j - N16_TILES, 0), i, 0)),
            pl.BlockSpec((None, tm, LANES), lambda i, j: (0, i, 0)),
        ],
        out_shape=[jax.ShapeDtypeStruct((N16_TILES, m, TN_PROJ), BF16),
                   jax.ShapeDtypeStruct((D_HALF // TN_PROJ, m, TN_PROJ), F32),
                   jax.ShapeDtypeStruct((1, m, LANES), F32)],
        scratch_shapes=[pltpu.VMEM((tm, D_MODEL), BF16)],
        compiler_params=_params("parallel", "arbitrary"),
        name=name,
    )(h, norm_w, w, w)


def _out_proj_kernel(oa_ref, ob_ref, h_ref, wa_ref, wb_ref, out_ref):
    out_ref[...] = (h_ref[...]
                    + jnp.dot(oa_ref[...], wa_ref[...], preferred_element_type=F32)
                    + jnp.dot(ob_ref[...], wb_ref[...], preferred_element_type=F32))


def _out_proj(oa, ob, h, w, *, tm, name):
    m = oa.shape[0]
    return pl.pallas_call(
        _out_proj_kernel,
        grid=(m // tm,),
        in_specs=[
            pl.BlockSpec((tm, D_HALF), lambda i: (i, 0)),
            pl.BlockSpec((tm, D_HALF), lambda i: (i, 0)),
            pl.BlockSpec((tm, D_MODEL), lambda i: (i, 0)),
            pl.BlockSpec((D_HALF, D_MODEL), lambda i: (0, 0)),
            pl.BlockSpec((D_HALF, D_MODEL), lambda i: (1, 0)),
        ],
        out_specs=pl.BlockSpec((tm, D_MODEL), lambda i: (i, 0)),
        out_shape=jax.ShapeDtypeStruct((m, D_MODEL), F32),
        compiler_params=_params("parallel"),
        name=name,
    )(oa, ob, h, w, w)


def _tri(c, seq_len):
    r = lax.broadcasted_iota(jnp.int32, (c, c), 0)
    s = lax.broadcasted_iota(jnp.int32, (c, c), 1)
    return jnp.where((r >= s) & ((r ^ s) < seq_len), 1.0, 0.0).astype(F32)


def _off_mask(c, blk):
    nb = c // blk
    rows = nb * (nb - 1) // 2 * blk
    r = lax.broadcasted_iota(jnp.int32, (c, rows), 0) >> (blk.bit_length() - 1)
    s = lax.broadcasted_iota(jnp.int32, (c, rows), 1)
    seg = jnp.zeros((c, rows), jnp.int32)
    for i in range(1, nb):
        seg = seg + jnp.where(s >= blk * i * (i - 1) // 2, 1, 0)
    return jnp.where(r == seg, 1.0, 0.0).astype(F32)


def _diag(q, k, v, bc, blk):
    c = q.shape[0]
    row = lax.broadcasted_iota(jnp.int32, (blk, 1), 0)
    outs = []
    for i in range(c // blk):
        sl = slice(i * blk, (i + 1) * blk)
        qi, ki, vi, bi = q[sl], k[sl], v[sl].astype(F32), bc[sl]
        acc = jnp.zeros((blk, v.shape[1]), F32)
        for s in range(blk):
            e = jnp.exp(jnp.minimum(bi - bi[s:s + 1, :], 0.0))
            col = jnp.sum(qi * ki[s:s + 1, :] * e, axis=-1, keepdims=True)
            col = jnp.where(row >= s, col, 0.0)
            acc = acc + col * vi[s:s + 1, :]
        outs.append(acc)
    return outs[0] if len(outs) == 1 else jnp.concatenate(outs, axis=0)


def _offdiag(q, k, v, bc, blk, mask):
    c = q.shape[0]
    nb = c // blk
    qs, ks, vs = [jnp.zeros((blk, DK), F32)], [], []
    for i in range(1, nb):
        lo = i * blk
        ref = bc[lo:lo + 1, :]
        qs.append(q[lo:lo + blk] * jnp.exp(bc[lo:lo + blk] - ref))
        ks.append(k[:lo] * jnp.exp(ref - bc[:lo]))
        vs.append(v[:lo])
    qt = jnp.concatenate(qs, axis=0).astype(BF16)
    kt = jnp.concatenate(ks, axis=0).astype(BF16)
    vt = jnp.concatenate(vs, axis=0).astype(BF16)
    a = lax.dot_general(qt, kt, (((1,), (1,)), ((), ())), preferred_element_type=F32)
    return jnp.dot((a * mask).astype(BF16), vt, preferred_element_type=F32)


def _hgrn_inputs(qa, fa, ia, lb):
    fg = lb + (1.0 - lb) * jax.nn.sigmoid(fa)
    return _silu(qa.astype(F32)) * (DK ** -0.5), 1.0 - fg, ia, jnp.log(fg)


def _gla_inputs(qb, kb, vb, al, wa, ba):
    alpha = jnp.dot(al.astype(BF16), wa, preferred_element_type=F32) + ba
    logsig = jnp.minimum(alpha, 0.0) - jnp.log1p(jnp.exp(-jnp.abs(alpha)))
    return qb.astype(F32) * (DK ** -0.5), kb.astype(F32), vb, logsig / GATE_TEMP


def _head_out(o, gn, gate):
    return (_rms(o, gn) * _silu(gate.astype(F32))).astype(BF16)


def _lower_bound(lbl_ref):
    x = lbl_ref[...]
    e = jnp.exp(x - jnp.max(x, axis=0, keepdims=True))
    return e[0:1, :] / jnp.sum(e, axis=0, keepdims=True)


def _stack_mask(c, blk):
    nb = c // blk
    rows = blk * nb * (nb + 1) // 2
    t = lax.broadcasted_iota(jnp.int32, (c, rows), 0)
    r = lax.broadcasted_iota(jnp.int32, (c, rows), 1)
    seg = jnp.zeros((c, rows), jnp.int32)
    off = jnp.zeros((c, rows), jnp.int32)
    for i in range(1, nb):
        start = blk * i * (i + 1) // 2
        seg = seg + jnp.where(r >= start, 1, 0)
        off = jnp.where(r >= start, start, off)
    keep = ((t >> (blk.bit_length() - 1)) == seg) & (r - off <= t)
    return jnp.where(keep, 1.0, 0.0).astype(F32)


def _state_update_t(k, vb, bc, st):
    c = k.shape[0]
    blast = bc[c - 1:c, :]
    kdec = (k * jnp.exp(blast - bc)).astype(BF16)
    upd = lax.dot_general(vb, kdec, (((0,), (0,)), ((), ())), preferred_element_type=F32)
    if st is None:
        return upd
    return st * jnp.exp(blast) + upd


def _scan_sublanes(p):
    row = lax.broadcasted_iota(jnp.int32, (SUBLANES, DK), 0)
    for s in (1, 2, 4):
        p = p + jnp.where(row >= s, pltpu.roll(p, s, axis=0), 0.0)
    return p


def _cumsum_chunk(g):
    out, carry = [], None
    for j in range(CHUNK // SUBLANES):
        p = _scan_sublanes(g[j * SUBLANES:(j + 1) * SUBLANES])
        if carry is not None:
            p = p + carry
        carry = p[SUBLANES - 1:SUBLANES, :]
        out.append(p)
    return jnp.concatenate(out, axis=0)


def _rescaled_operands(q, k, bc):
    c, blk = CHUNK, DIAG_BLK
    qs, ks = [], []
    for i in range(c // blk):
        lo, hi = i * blk, (i + 1) * blk
        ref = bc[lo:lo + 1, :]
        qs.append(q[lo:hi] * jnp.exp(bc[lo:hi] - ref))
        ks.append(k[:hi] * jnp.exp(ref - bc[:hi]))
    return ((q * jnp.exp(bc)).astype(BF16), jnp.concatenate(qs, axis=0).astype(BF16),
            jnp.concatenate(ks, axis=0).astype(BF16),
            (k * jnp.exp(bc[c - 1:c, :] - bc)).astype(BF16))


def _chunk_exact(q, k, v, bc, st, mask):
    o = lax.dot_general((q * jnp.exp(bc)).astype(BF16), st.astype(BF16),
                        (((1,), (1,)), ((), ())), preferred_element_type=F32)
    o = o + _offdiag(q, k, v, bc, DIAG_BLK, mask) + _diag(q, k, v, bc, DIAG_BLK)
    return o, _state_update_t(k, v.astype(BF16), bc, st)


def _prompt_scan_kernel(*refs, gla, seq, meta):
    n_in = 11 if gla else 8
    if gla:
        q_ref, k_ref, v_ref, gate_ref, al_ref, mk_ref, mv_ref, mal_ref, wa_ref, ba_ref, gn_ref = (
            refs[:n_in])
    else:
        q_ref, f_ref, v_ref, gate_ref, mf_ref, mv_ref, lbl_ref, gn_ref = refs[:n_in]
        lb = _lower_bound(lbl_ref)
    o_ref, sout_ref = refs[n_in:n_in + 2]
    st_ref, qs_ref, ks_ref, bc_ref, qd_ref, qt_ref, kt_ref, kdec_ref, eb_ref = refs[n_in + 2:]
    trips = seq // (CHUNK * CHUNKS_PER_TRIP)
    stack = kt_ref.shape[0] // (seq // CHUNK)
    nt = (((1,), (1,)), ((), ()))
    tn = (((0,), (0,)), ((), ()))

    def chunk_rows(ci, u, n=CHUNK):
        return pl.ds(pl.multiple_of((ci * CHUNKS_PER_TRIP + u) * n, n), n)

    if gla:
        _, mk, mv, mg = _gla_inputs(mk_ref[...], mk_ref[...], mv_ref[...], mal_ref[...],
                                    wa_ref[...], ba_ref[...])
    else:
        _, mk, mv, mg = _hgrn_inputs(mf_ref[...], mf_ref[...], mv_ref[...], lb)
    mbc = jnp.dot(_tri(meta, meta), mg, precision=HIGHEST, preferred_element_type=F32)
    st_ref[...] = _state_update_t(mk, mv.astype(BF16), mbc, None)

    def pass_a(ci, carry):
        gmin, kmax = carry
        trip_rows = CHUNK * CHUNKS_PER_TRIP
        rows = pl.ds(pl.multiple_of(ci * trip_rows, trip_rows), trip_rows)
        if gla:
            q, k, _, g = _gla_inputs(q_ref[rows, :], k_ref[rows, :], None, al_ref[rows, :],
                                     wa_ref[...], ba_ref[...])
        else:
            q, k, _, g = _hgrn_inputs(q_ref[rows, :], f_ref[rows, :], None, lb)
        qs_ref[rows, :] = q
        ks_ref[rows, :] = k
        for u in range(CHUNKS_PER_TRIP):
            sl = slice(u * CHUNK, (u + 1) * CHUNK)
            bc = _cumsum_chunk(g[sl])
            qd, qt, kt, kdec = _rescaled_operands(q[sl], k[sl], bc)
            bc_ref[chunk_rows(ci, u), :] = bc
            qd_ref[chunk_rows(ci, u), :] = qd
            qt_ref[chunk_rows(ci, u), :] = qt
            kt_ref[chunk_rows(ci, u, stack), :] = kt
            kdec_ref[chunk_rows(ci, u), :] = kdec
            eb_ref[ci * CHUNKS_PER_TRIP + u] = jnp.broadcast_to(
                jnp.exp(bc[CHUNK - 1:CHUNK, :]), (SUBLANES, DK))
        return (jnp.minimum(gmin, jnp.min(g, axis=0, keepdims=True)),
                jnp.maximum(kmax, jnp.max(jnp.abs(k), axis=0, keepdims=True)))

    gmin, kmax = lax.fori_loop(0, trips, pass_a,
                               (jnp.zeros((1, DK), F32), jnp.zeros((1, DK), F32)))
    in_range = (jnp.min(gmin) >= FAST_MIN_LOG_GATE) & (jnp.max(kmax) <= FAST_MAX_KEY)

    @pl.when(in_range)
    def _rescaled():
        mask = _stack_mask(CHUNK, DIAG_BLK)
        units = range(CHUNKS_PER_MXU_TRIP)

        def rows_of(c, n=CHUNK):
            return pl.ds(pl.multiple_of(c * n, n), n)

        def pass_bc(ci, carry):
            cs = [ci * CHUNKS_PER_MXU_TRIP + u for u in units]
            vb = [v_ref[rows_of(c), :].astype(BF16) for c in cs]
            a = [lax.dot_general(qt_ref[rows_of(c), :], kt_ref[rows_of(c, stack), :], nt,
                                 preferred_element_type=F32) for c in cs]
            am = [(x * mask).astype(BF16) for x in a]
            vt = [jnp.concatenate([x[:(i + 1) * DIAG_BLK] for i in range(CHUNK // DIAG_BLK)], axis=0)
                  for x in vb]
            oi = [jnp.dot(am[u], vt[u], preferred_element_type=F32) for u in units]
            inc = [lax.dot_general(vb[u], kdec_ref[rows_of(cs[u]), :], tn,
                                   preferred_element_type=F32) for u in units]
            sts = [st_ref[...]]
            for u in units:
                sts.append(sts[-1] * eb_ref[cs[u]][0:1, :] + inc[u])
            o = [oi[u] + lax.dot_general(qd_ref[rows_of(cs[u]), :], sts[u].astype(BF16), nt,
                                         preferred_element_type=F32) for u in units]
            for u in units:
                o_ref[rows_of(cs[u]), :] = _head_out(o[u], gn_ref[...], gate_ref[rows_of(cs[u]), :])
            st_ref[...] = sts[-1]
            return carry

        lax.fori_loop(0, seq // (CHUNK * CHUNKS_PER_MXU_TRIP), pass_bc, 0)

    @pl.when(jnp.logical_not(in_range))
    def _exact():
        mask = _off_mask(CHUNK, DIAG_BLK)

        def body(ci, carry):
            st = st_ref[...]
            for u in range(CHUNKS_PER_TRIP):
                rows = chunk_rows(ci, u)
                o, st = _chunk_exact(qs_ref[rows, :], ks_ref[rows, :], v_ref[rows, :],
                                     bc_ref[rows, :], st, mask)
                o_ref[rows, :] = _head_out(o, gn_ref[...], gate_ref[rows, :])
            st_ref[...] = st
            return carry

        lax.fori_loop(0, trips, body, 0)

    sout_ref[...] = st_ref[...].T


def _sample_scan_kernel(*refs, gla, steps):
    if gla:
        (q_ref, k_ref, v_ref, gate_ref, al_ref, wa_ref, ba_ref, gn_ref, s0_ref,
         o_ref, sout_ref) = refs
        q, k, v, g = _gla_inputs(q_ref[...], k_ref[...], v_ref[...], al_ref[...],
                                 wa_ref[...], ba_ref[...])
    else:
        (q_ref, f_ref, v_ref, gate_ref, lbl_ref, gn_ref, s0_ref, o_ref, sout_ref) = refs
        q, k, v, g = _hgrn_inputs(q_ref[...], f_ref[...], v_ref[...], _lower_bound(lbl_ref))
    v32 = v.astype(F32)
    rows = SAMPLE_SEQS * steps
    groups = [slice(b * steps, (b + 1) * steps) for b in range(SAMPLE_SEQS)]
    tn = (((0,), (0,)), ((), ()))

    bc = jnp.concatenate([_scan_sublanes(g[sl]) for sl in groups], axis=0)

    def repeated(r):
        return jnp.concatenate([jnp.broadcast_to(bc[sl][r:r + 1, :], (steps, DK)) for sl in groups],
                               axis=0)

    qd = (q * jnp.exp(bc)).astype(BF16)
    kdec = (k * jnp.exp(repeated(steps - 1) - bc)).astype(BF16)
    r = lax.broadcasted_iota(jnp.int32, (rows, SAMPLE_SEQS), 0)
    c = lax.broadcasted_iota(jnp.int32, (rows, SAMPLE_SEQS), 1)
    selector = jnp.where((r >> (steps.bit_length() - 1)) == c, 1.0, 0.0).astype(F32)
    decay = jnp.exp(lax.dot_general(g, selector, tn, precision=HIGHEST,
                                    preferred_element_type=F32))

    inter = [jnp.dot(qd[sl], s0_ref[b].astype(BF16), preferred_element_type=F32)
             for b, sl in enumerate(groups)]
    upd = [lax.dot_general(kdec[sl], v32[sl].astype(BF16), tn, preferred_element_type=F32)
           for sl in groups]
    for b in range(SAMPLE_SEQS):
        sout_ref[b] = decay[:, b:b + 1] * s0_ref[b] + upd[b]
    o = jnp.concatenate(inter, axis=0) + _diag(q, k, v32, bc, steps)
    o_ref[...] = _head_out(o, gn_ref[...], gate_ref[...])


def _proj_spec(rows, width, c0, per_head, row_block, tile_width=TN_PROJ):
    per_tile = tile_width // width

    def index(b, h):
        g = c0 * LANES // width + h * per_head
        return (g // per_tile, row_block(b), g % per_tile)

    return pl.BlockSpec((None, rows, width), index)


def _prompt_scan(p, pm, *, gla, batch, seq, meta_block, small, name):
    heads, dv = (H_B, DV_B) if gla else (H_A, DV_A)
    n_chunks = seq // CHUNK
    blocks = CHUNK // DIAG_BLK
    stack = DIAG_BLK * blocks * (blocks + 1) // 2
    p16, pf, pa = p
    pm16, pmf, pma = pm
    tok = functools.partial(_proj_spec, seq, row_block=lambda b: b)
    met = functools.partial(_proj_spec, N_META, row_block=lambda b: meta_block)
    vec = lambda width: pl.BlockSpec((1, width), lambda b, h: (0, h))
    if gla:
        wa, ba, gn = small
        in_specs = [tok(DK, COL_QB, 1), tok(DK, COL_KB, 1), tok(dv, COL_VB, 1), tok(dv, COL_RB, 1),
                    tok(LANES, 0, 0, tile_width=LANES),
                    met(DK, COL_KB, 1), met(dv, COL_VB, 1), met(LANES, 0, 0, tile_width=LANES),
                    pl.BlockSpec((LANES, DK), lambda b, h: (0, h)), vec(DK), vec(dv)]
        args = (p16, p16, p16, p16, pa, pm16, pm16, pma, wa, ba, gn)
    else:
        lbl, gn = small
        in_specs = [tok(DK, COL_QA, 1), tok(DK, 0, 1), tok(dv, COL_IA, 1), tok(dv, COL_GA, 1),
                    met(DK, 0, 1), met(dv, COL_IA, 1),
                    pl.BlockSpec((2, DK), lambda b, h: (0, h)), vec(dv)]
        args = (p16, pf, p16, p16, pmf, pm16, lbl, gn)
    return pl.pallas_call(
        functools.partial(_prompt_scan_kernel, gla=gla, seq=seq, meta=N_META),
        grid=(batch, heads),
        in_specs=in_specs,
        out_specs=[pl.BlockSpec((seq, dv), lambda b, h: (b, h)),
                   pl.BlockSpec((None, None, DK, dv), lambda b, h: (b, h, 0, 0))],
        out_shape=[jax.ShapeDtypeStruct((batch * seq, heads * dv), BF16),
                   jax.ShapeDtypeStruct((batch, heads, DK, dv), F32)],
        scratch_shapes=[
            pltpu.VMEM((dv, DK), F32),
            pltpu.VMEM((seq, DK), F32),
            pltpu.VMEM((seq, DK), F32),
            pltpu.VMEM((seq, DK), F32),
            pltpu.VMEM((seq, DK), BF16),
            pltpu.VMEM((seq, DK), BF16),
            pltpu.VMEM((n_chunks * stack, DK), BF16),
            pltpu.VMEM((seq, DK), BF16),
            pltpu.VMEM((n_chunks, SUBLANES, DK), F32),
        ],
        compiler_params=_params("parallel", "parallel"),
        name=name,
    )(*args)


def _sample_scan(p, s0, *, gla, batch, steps, small, name):
    assert steps == SUBLANES, "the sample recurrence keeps one sequence per sublane group"
    heads, dv = (H_B, DV_B) if gla else (H_A, DV_A)
    rows = SAMPLE_SEQS * steps
    p16, pf, pa = p
    tok = functools.partial(_proj_spec, rows, row_block=lambda b: b)
    vec = lambda width: pl.BlockSpec((1, width), lambda b, h: (0, h))
    state = pl.BlockSpec((SAMPLE_SEQS, None, DK, dv), lambda b, h: (b, h, 0, 0))
    if gla:
        wa, ba, gn = small
        in_specs = [tok(DK, COL_QB, 1), tok(DK, COL_KB, 1), tok(dv, COL_VB, 1), tok(dv, COL_RB, 1),
                    tok(LANES, 0, 0, tile_width=LANES),
                    pl.BlockSpec((LANES, DK), lambda b, h: (0, h)), vec(DK), vec(dv), state]
        args = (p16, p16, p16, p16, pa, wa, ba, gn, s0)
    else:
        lbl, gn = small
        in_specs = [tok(DK, COL_QA, 1), tok(DK, 0, 1), tok(dv, COL_IA, 1), tok(dv, COL_GA, 1),
                    pl.BlockSpec((2, DK), lambda b, h: (0, h)), vec(dv), state]
        args = (p16, pf, p16, p16, lbl, gn, s0)
    return pl.pallas_call(
        functools.partial(_sample_scan_kernel, gla=gla, steps=steps),
        grid=(batch // SAMPLE_SEQS, heads),
        in_specs=in_specs,
        out_specs=[pl.BlockSpec((rows, dv), lambda b, h: (b, h)), state],
        out_shape=[jax.ShapeDtypeStruct((batch * steps, heads * dv), BF16),
                   jax.ShapeDtypeStruct((batch, heads, DK, dv), F32)],
        compiler_params=_params("parallel", "parallel"),
        name=name,
    )(*args)


def kernel(x_prompt, x_sample, state_hgrn, state_gla, meta_tokens, lb_logits, ffn1_norm, w_ffn1_in,
           w_ffn1_out, mix_norm, w_in, w_alpha_up, b_alpha, gnorm_a, gnorm_b, w_out, ffn2_norm,
           w_ffn2_in, w_ffn2_out, final_norm):
    batch, seq, _ = x_prompt.shape
    dec_batch, steps, _ = x_sample.shape

    w_in_t = w_in[0].T
    wop = w_out[0].astype(BF16)
    wa = jnp.pad(w_alpha_up[0], ((0, LANES - ALPHA_RANK), (0, 0))).astype(BF16)
    n1, nm, n2, nf = ffn1_norm, mix_norm, ffn2_norm, final_norm[None]
    small_a = (lb_logits, gnorm_a)
    small_b = (wa, b_alpha, gnorm_b)
    n_sample = dec_batch * steps

    xs = jnp.concatenate([x_sample.reshape(n_sample, D_MODEL), meta_tokens], axis=0)
    h1s, *w1 = _ffn(xs, n1, (w_ffn1_in[0], w_ffn1_in[0], w_ffn1_out[0]), nf, tm=xs.shape[0],
                    tf=TF_CAST, apply_final_norm=False, emit_weights=True, name="ffn1_sample")
    ps = _in_proj(h1s, nm, w_in_t, tm=xs.shape[0], name="inproj_sample")
    oa_s, sa_s = _sample_scan(ps, state_hgrn[0], gla=False, batch=dec_batch, steps=steps,
                              small=small_a, name="scan_hgrn_sample")
    ob_s, sb_s = _sample_scan(ps, state_gla[0], gla=True, batch=dec_batch, steps=steps,
                              small=small_b, name="scan_gla_sample")
    h2s = _out_proj(oa_s, ob_s, h1s, wop, tm=TM, name="outproj_sample")
    y_s, *w2 = _ffn(h2s, n2, (w_ffn2_in[0], w_ffn2_in[0], w_ffn2_out[0]), nf, tm=n_sample,
                    tf=TF_CAST, apply_final_norm=True, emit_weights=True, name="ffn2_sample")

    h1p = _ffn(x_prompt.reshape(batch * seq, D_MODEL), n1, w1, nf, tm=TM, tf=TF,
               apply_final_norm=False, emit_weights=False, name="ffn1_prompt")
    pp = _in_proj(h1p, nm, w_in_t, tm=TM_PROJ, name="inproj_prompt")
    meta_block = n_sample // N_META
    oa_p, sa_p = _prompt_scan(pp, ps, gla=False, batch=batch, seq=seq, meta_block=meta_block,
                              small=small_a, name="scan_hgrn_prompt")
    ob_p, sb_p = _prompt_scan(pp, ps, gla=True, batch=batch, seq=seq, meta_block=meta_block,
                              small=small_b, name="scan_gla_prompt")
    h2p = _out_proj(oa_p, ob_p, h1p, wop, tm=TM, name="outproj_prompt")
    y_p = _ffn(h2p, n2, w2, nf, tm=TM, tf=TF, apply_final_norm=True, emit_weights=False,
               name="ffn2_prompt")

    return (y_p.reshape(batch, seq, D_MODEL), y_s.reshape(dec_batch, steps, D_MODEL),
            sa_p[None], sb_p[None], sa_s[None], sb_s[None])
```

```python
import functools

import jax
import jax.numpy as jnp
from jax import lax
from jax.experimental import pallas as pl
from jax.experimental.pallas import tpu as pltpu

F32 = jnp.float32
BF16 = jnp.bfloat16
HIGHEST = lax.Precision.HIGHEST

D_MODEL = 2048
N_META = 16
DK = 128
H_A, DV_A = 8, 128
H_B, DV_B = 4, 256
D_HALF = H_A * DV_A
ALPHA_RANK = 16
GATE_TEMP = 16.0
D_FF = 5632
CHUNK = 64
EPS = 1e-6
IN_PROJ_WIDTHS = (D_HALF, D_HALF, D_HALF, D_HALF, H_B * DK, H_B * DK, D_HALF, D_HALF, ALPHA_RANK)

LANES = 128
SUBLANES = 8
VMEM_LIMIT_BYTES = 56 * 1024 * 1024

COL_QA, COL_IA, COL_GA, COL_QB, COL_KB, COL_VB, COL_RB = 0, 8, 16, 24, 28, 32, 40
P16_COLS = 3 * D_HALF + 2 * H_B * DK + 2 * D_HALF

TM = 512
TF = 512
TF_CAST = 256
TN_PROJ = 512
TM_PROJ = 1024
N16_TILES = P16_COLS // TN_PROJ
DIAG_BLK = 16
SAMPLE_SEQS = 32
CHUNKS_PER_TRIP = 4
CHUNKS_PER_MXU_TRIP = 8

FAST_MIN_LOG_GATE = -60.0 / (DIAG_BLK - 1)
FAST_MAX_KEY = 1e10


def _rms(x, w):
    return x * lax.rsqrt(jnp.mean(x * x, axis=-1, keepdims=True) + EPS) * w


def _silu(x):
    return x * jax.nn.sigmoid(x)


def _params(*sem):
    return pltpu.CompilerParams(dimension_semantics=sem, vmem_limit_bytes=VMEM_LIMIT_BYTES)


def _ffn_kernel(h_ref, nw_ref, wg_ref, wu_ref, wo_ref, fnw_ref, out_ref, *rest,
                apply_final_norm, emit_weights):
    u_ref = rest[-1]
    j = pl.program_id(1)

    @pl.when(j == 0)
    def _init():
        u_ref[...] = _rms(h_ref[...], nw_ref[...]).astype(BF16)
        out_ref[...] = jnp.zeros_like(out_ref)

    wg, wu, wo = (r[...].astype(BF16) for r in (wg_ref, wu_ref, wo_ref))
    if emit_weights:
        for dst, w in zip(rest[:3], (wg, wu, wo)):
            dst[...] = w
    u = u_ref[...]
    g = jnp.dot(u, wg, preferred_element_type=F32)
    up = jnp.dot(u, wu, preferred_element_type=F32)
    a = (_silu(g) * up).astype(BF16)
    out_ref[...] += jnp.dot(a, wo, preferred_element_type=F32)

    @pl.when(j == pl.num_programs(1) - 1)
    def _finish():
        y = h_ref[...] + 0.5 * out_ref[...]
        if apply_final_norm:
            y = _rms(y, fnw_ref[...])
        out_ref[...] = y


def _ffn(h, norm_w, weights, final_norm_w, *, tm, tf, apply_final_norm, emit_weights, name):
    m = h.shape[0]
    nf = D_FF // tf
    rows = pl.BlockSpec((tm, D_MODEL), lambda i, j: (i, 0),
                        pipeline_mode=pl.Buffered(1) if m == tm else None)
    vec = pl.BlockSpec((1, D_MODEL), lambda i, j: (0, 0))
    col_tile = pl.BlockSpec((D_MODEL, tf), lambda i, j: (0, j))
    row_tile = pl.BlockSpec((tf, D_MODEL), lambda i, j: (j, 0))
    out_specs = [rows]
    out_shape = [jax.ShapeDtypeStruct((m, D_MODEL), F32)]
    if emit_weights:
        assert m == tm, "weight casts are written once, by a single row tile"
        up_tile = pl.BlockSpec((D_MODEL, tf), lambda i, j: (0, j + nf))
        out_specs += [col_tile, col_tile, row_tile]
        out_shape += [jax.ShapeDtypeStruct((D_MODEL, D_FF), BF16)] * 2
        out_shape += [jax.ShapeDtypeStruct((D_FF, D_MODEL), BF16)]
    else:
        up_tile = col_tile
    res = pl.pallas_call(
        functools.partial(_ffn_kernel, apply_final_norm=apply_final_norm,
                          emit_weights=emit_weights),
        grid=(m // tm, nf),
        in_specs=[rows, vec, col_tile, up_tile, row_tile, vec],
        out_specs=out_specs,
        out_shape=out_shape,
        scratch_shapes=[pltpu.VMEM((tm, D_MODEL), BF16)],
        compiler_params=_params("parallel", "arbitrary"),
        name=name,
    )(h, norm_w, *weights, final_norm_w)
    return res if emit_weights else res[0]


def _proj_kernel(h_ref, nw_ref, w_ref, wal_ref, p16_ref, pf_ref, pa_ref, u_ref):
    j = pl.program_id(1)

    @pl.when(j == 0)
    def _init():
        u = _rms(h_ref[...], nw_ref[...]).astype(BF16)
        u_ref[...] = u
        wal = jnp.concatenate([wal_ref[...].astype(BF16),
                               jnp.zeros((LANES - ALPHA_RANK, D_MODEL), BF16)], axis=0)
        pa_ref[...] = lax.dot_general(u, wal, (((1,), (1,)), ((), ())),
                                      preferred_element_type=F32)

    def project():
        return lax.dot_general(u_ref[...], w_ref[...].astype(BF16), (((1,), (1,)), ((), ())),
                               preferred_element_type=F32)

    @pl.when(j < N16_TILES)
    def _narrow():
        p16_ref[...] = project().astype(BF16)

    @pl.when(j >= N16_TILES)
    def _wide():
        pf_ref[...] = project()


def _w_in_tile(j):
    fa_lo, fa_hi = D_HALF // TN_PROJ, 2 * D_HALF // TN_PROJ
    moved = fa_hi - fa_lo
    return jnp.where(j < fa_lo, j, jnp.where(j < N16_TILES, j + moved, j - N16_TILES + fa_lo))


def _in_proj(h, norm_w, w, *, tm, name):
    m = h.shape[0]
    return pl.pallas_call(
        _proj_kernel,
        grid=(m // tm, (P16_COLS + D_HALF) // TN_PROJ),
        in_specs=[
            pl.BlockSpec((tm, D_MODEL), lambda i, j: (i, 0)),
            pl.BlockSpec((1, D_MODEL), lambda i, j: (0, 0)),
            pl.BlockSpec((TN_PROJ, D_MODEL), lambda i, j: (_w_in_tile(j), 0)),
            pl.BlockSpec((ALPHA_RANK, D_MODEL), lambda i, j: ((P16_COLS + D_HALF) // ALPHA_RANK, 0)),
        ],
        out_specs=[
            pl.BlockSpec((None, tm, TN_PROJ), lambda i, j: (jnp.minimum(j, N16_TILES - 1), i, 0)),
            pl.BlockSpec((None, tm, TN_PROJ), lambda i, j: (jnp.maximum(j - N16_TILES, 0), i, 0)),
            pl.BlockSpec((None, tm, LANES), lambda i, j: (0, i, 0)),
        ],
        out_shape=[jax.ShapeDtypeStruct((N16_TILES, m, TN_PROJ), BF16),
                   jax.ShapeDtypeStruct((D_HALF // TN_PROJ, m, TN_PROJ), F32),
                   jax.ShapeDtypeStruct((1, m, LANES), F32)],
        scratch_shapes=[pltpu.VMEM((tm, D_MODEL), BF16)],
        compiler_params=_params("parallel", "arbitrary"),
        name=name,
    )(h, norm_w, w, w)


def _out_proj_kernel(oa_ref, ob_ref, h_ref, wa_ref, wb_ref, out_ref):
    out_ref[...] = (h_ref[...]
                    + jnp.dot(oa_ref[...], wa_ref[...], preferred_element_type=F32)
                    + jnp.dot(ob_ref[...], wb_ref[...], preferred_element_type=F32))


def _out_proj(oa, ob, h, w, *, tm, name):
    m = oa.shape[0]
    return pl.pallas_call(
        _out_proj_kernel,
        grid=(m // tm,),
        in_specs=[
            pl.BlockSpec((tm, D_HALF), lambda i: (i, 0)),
            pl.BlockSpec((tm, D_HALF), lambda i: (i, 0)),
            pl.BlockSpec((tm, D_MODEL), lambda i: (i, 0)),
            pl.BlockSpec((D_HALF, D_MODEL), lambda i: (0, 0)),
            pl.BlockSpec((D_HALF, D_MODEL), lambda i: (1, 0)),
        ],
        out_specs=pl.BlockSpec((tm, D_MODEL), lambda i: (i, 0)),
        out_shape=jax.ShapeDtypeStruct((m, D_MODEL), F32),
        compiler_params=_params("parallel"),
        name=name,
    )(oa, ob, h, w, w)


def _tri(c, seq_len):
    r = lax.broadcasted_iota(jnp.int32, (c, c), 0)
    s = lax.broadcasted_iota(jnp.int32, (c, c), 1)
    return jnp.where((r >= s) & ((r ^ s) < seq_len), 1.0, 0.0).astype(F32)


def _off_mask(c, blk):
    nb = c // blk
    rows = nb * (nb - 1) // 2 * blk
    r = lax.broadcasted_iota(jnp.int32, (c, rows), 0) >> (blk.bit_length() - 1)
    s = lax.broadcasted_iota(jnp.int32, (c, rows), 1)
    seg = jnp.zeros((c, rows), jnp.int32)
    for i in range(1, nb):
        seg = seg + jnp.where(s >= blk * i * (i - 1) // 2, 1, 0)
    return jnp.where(r == seg, 1.0, 0.0).astype(F32)


def _diag(q, k, v, bc, blk):
    c = q.shape[0]
    row = lax.broadcasted_iota(jnp.int32, (blk, 1), 0)
    outs = []
    for i in range(c // blk):
        sl = slice(i * blk, (i + 1) * blk)
        qi, ki, vi, bi = q[sl], k[sl], v[sl].astype(F32), bc[sl]
        acc = jnp.zeros((blk, v.shape[1]), F32)
        for s in range(blk):
            e = jnp.exp(jnp.minimum(bi - bi[s:s + 1, :], 0.0))
            col = jnp.sum(qi * ki[s:s + 1, :] * e, axis=-1, keepdims=True)
            col = jnp.where(row >= s, col, 0.0)
            acc = acc + col * vi[s:s + 1, :]
        outs.append(acc)
    return outs[0] if len(outs) == 1 else jnp.concatenate(outs, axis=0)


def _offdiag(q, k, v, bc, blk, mask):
    c = q.shape[0]
    nb = c // blk
    qs, ks, vs = [jnp.zeros((blk, DK), F32)], [], []
    for i in range(1, nb):
        lo = i * blk
        ref = bc[lo:lo + 1, :]
        qs.append(q[lo:lo + blk] * jnp.exp(bc[lo:lo + blk] - ref))
        ks.append(k[:lo] * jnp.exp(ref - bc[:lo]))
        vs.append(v[:lo])
    qt = jnp.concatenate(qs, axis=0).astype(BF16)
    kt = jnp.concatenate(ks, axis=0).astype(BF16)
    vt = jnp.concatenate(vs, axis=0).astype(BF16)
    a = lax.dot_general(qt, kt, (((1,), (1,)), ((), ())), preferred_element_type=F32)
    return jnp.dot((a * mask).astype(BF16), vt, preferred_element_type=F32)


def _hgrn_inputs(qa, fa, ia, lb):
    fg = lb + (1.0 - lb) * jax.nn.sigmoid(fa)
    return _silu(qa.astype(F32)) * (DK ** -0.5), 1.0 - fg, ia, jnp.log(fg)


def _gla_inputs(qb, kb, vb, al, wa, ba):
    alpha = jnp.dot(al.astype(BF16), wa, preferred_element_type=F32) + ba
    logsig = jnp.minimum(alpha, 0.0) - jnp.log1p(jnp.exp(-jnp.abs(alpha)))
    return qb.astype(F32) * (DK ** -0.5), kb.astype(F32), vb, logsig / GATE_TEMP


def _head_out(o, gn, gate):
    return (_rms(o, gn) * _silu(gate.astype(F32))).astype(BF16)


def _lower_bound(lbl_ref):
    x = lbl_ref[...]
    e = jnp.exp(x - jnp.max(x, axis=0, keepdims=True))
    return e[0:1, :] / jnp.sum(e, axis=0, keepdims=True)


def _stack_mask(c, blk):
    nb = c // blk
    rows = blk * nb * (nb + 1) // 2
    t = lax.broadcasted_iota(jnp.int32, (c, rows), 0)
    r = lax.broadcasted_iota(jnp.int32, (c, rows), 1)
    seg = jnp.zeros((c, rows), jnp.int32)
    off = jnp.zeros((c, rows), jnp.int32)
    for i in range(1, nb):
        start = blk * i * (i + 1) // 2
        seg = seg + jnp.where(r >= start, 1, 0)
        off = jnp.where(r >= start, start, off)
    keep = ((t >> (blk.bit_length() - 1)) == seg) & (r - off <= t)
    return jnp.where(keep, 1.0, 0.0).astype(F32)


def _state_update_t(k, vb, bc, st):
    c = k.shape[0]
    blast = bc[c - 1:c, :]
    kdec = (k * jnp.exp(blast - bc)).astype(BF16)
    upd = lax.dot_general(vb, kdec, (((0,), (0,)), ((), ())), preferred_element_type=F32)
    if st is None:
        return upd
    return st * jnp.exp(blast) + upd


def _scan_sublanes(p):
    row = lax.broadcasted_iota(jnp.int32, (SUBLANES, DK), 0)
    for s in (1, 2, 4):
        p = p + jnp.where(row >= s, pltpu.roll(p, s, axis=0), 0.0)
    return p


def _cumsum_chunk(g):
    out, carry = [], None
    for j in range(CHUNK // SUBLANES):
        p = _scan_sublanes(g[j * SUBLANES:(j + 1) * SUBLANES])
        if carry is not None:
            p = p + carry
        carry = p[SUBLANES - 1:SUBLANES, :]
        out.append(p)
    return jnp.concatenate(out, axis=0)


def _rescaled_operands(q, k, bc):
    c, blk = CHUNK, DIAG_BLK
    qs, ks = [], []
    for i in range(c // blk):
        lo, hi = i * blk, (i + 1) * blk
        ref = bc[lo:lo + 1, :]
        qs.append(q[lo:hi] * jnp.exp(bc[lo:hi] - ref))
        ks.append(k[:hi] * jnp.exp(ref - bc[:hi]))
    return ((q * jnp.exp(bc)).astype(BF16), jnp.concatenate(qs, axis=0).astype(BF16),
            jnp.concatenate(ks, axis=0).astype(BF16),
            (k * jnp.exp(bc[c - 1:c, :] - bc)).astype(BF16))


def _chunk_exact(q, k, v, bc, st, mask):
    o = lax.dot_general((q * jnp.exp(bc)).astype(BF16), st.astype(BF16),
                        (((1,), (1,)), ((), ())), preferred_element_type=F32)
    o = o + _offdiag(q, k, v, bc, DIAG_BLK, mask) + _diag(q, k, v, bc, DIAG_BLK)
    return o, _state_update_t(k, v.astype(BF16), bc, st)


def _prompt_scan_kernel(*refs, gla, seq, meta):
    n_in = 11 if gla else 8
    if gla:
        q_ref, k_ref, v_ref, gate_ref, al_ref, mk_ref, mv_ref, mal_ref, wa_ref, ba_ref, gn_ref = (
            refs[:n_in])
    else:
        q_ref, f_ref, v_ref, gate_ref, mf_ref, mv_ref, lbl_ref, gn_ref = refs[:n_in]
        lb = _lower_bound(lbl_ref)
    o_ref, sout_ref = refs[n_in:n_in + 2]
    st_ref, qs_ref, ks_ref, gs_ref, qd_ref, qt_ref, kt_ref, kdec_ref, eb_ref = refs[n_in + 2:]
    trips = seq // (CHUNK * CHUNKS_PER_TRIP)
    stack = kt_ref.shape[0] // (seq // CHUNK)
    nt = (((1,), (1,)), ((), ()))
    tn = (((0,), (0,)), ((), ()))

    def chunk_rows(ci, u, n=CHUNK):
        return pl.ds(pl.multiple_of((ci * CHUNKS_PER_TRIP + u) * n, n), n)

    if gla:
        _, mk, mv, mg = _gla_inputs(mk_ref[...], mk_ref[...], mv_ref[...], mal_ref[...],
                                    wa_ref[...], ba_ref[...])
    else:
        _, mk, mv, mg = _hgrn_inputs(mf_ref[...], mf_ref[...], mv_ref[...], lb)
    mbc = jnp.dot(_tri(meta, meta), mg, precision=HIGHEST, preferred_element_type=F32)
    st_ref[...] = _state_update_t(mk, mv.astype(BF16), mbc, None)

    def gate_pass(ci, carry):
        gmin, kmax = carry
        trip_rows = CHUNK * CHUNKS_PER_TRIP
        rows = pl.ds(pl.multiple_of(ci * trip_rows, trip_rows), trip_rows)
        if gla:
            q, k, _, g = _gla_inputs(q_ref[rows, :], k_ref[rows, :], None, al_ref[rows, :],
                                     wa_ref[...], ba_ref[...])
        else:
            q, k, _, g = _hgrn_inputs(q_ref[rows, :], f_ref[rows, :], None, lb)
        qs_ref[rows, :] = q
        ks_ref[rows, :] = k
        gs_ref[rows, :] = g
        return (jnp.minimum(gmin, jnp.min(g, axis=0, keepdims=True)),
                jnp.maximum(kmax, jnp.max(jnp.abs(k), axis=0, keepdims=True)))

    gmin, kmax = lax.fori_loop(0, trips, gate_pass,
                               (jnp.zeros((1, DK), F32), jnp.zeros((1, DK), F32)))
    in_range = (jnp.min(gmin) >= FAST_MIN_LOG_GATE) & (jnp.max(kmax) <= FAST_MAX_KEY)

    def rows_of(c, n=CHUNK):
        return pl.ds(pl.multiple_of(c * n, n), n)

    @pl.when(in_range)
    def _rescaled():
        mask = _stack_mask(CHUNK, DIAG_BLK)
        units = range(CHUNKS_PER_MXU_TRIP)
        mxu_trips = seq // (CHUNK * CHUNKS_PER_MXU_TRIP)

        def prepare(ti):
            for u in units:
                c = ti * CHUNKS_PER_MXU_TRIP + u
                bc = _cumsum_chunk(gs_ref[rows_of(c), :])
                qd, qt, kt, kdec = _rescaled_operands(qs_ref[rows_of(c), :], ks_ref[rows_of(c), :],
                                                      bc)
                qd_ref[rows_of(c), :] = qd
                qt_ref[rows_of(c), :] = qt
                kt_ref[rows_of(c, stack), :] = kt
                kdec_ref[rows_of(c), :] = kdec
                eb_ref[c] = jnp.broadcast_to(jnp.exp(bc[CHUNK - 1:CHUNK, :]), (SUBLANES, DK))

        def contract(ti):
            cs = [ti * CHUNKS_PER_MXU_TRIP + u for u in units]
            vb = [v_ref[rows_of(c), :].astype(BF16) for c in cs]
            a = [lax.dot_general(qt_ref[rows_of(c), :], kt_ref[rows_of(c, stack), :], nt,
                                 preferred_element_type=F32) for c in cs]
            am = [(x * mask).astype(BF16) for x in a]
            vt = [jnp.concatenate([x[:(i + 1) * DIAG_BLK] for i in range(CHUNK // DIAG_BLK)], axis=0)
                  for x in vb]
            oi = [jnp.dot(am[u], vt[u], preferred_element_type=F32) for u in units]
            inc = [lax.dot_general(vb[u], kdec_ref[rows_of(cs[u]), :], tn,
                                   preferred_element_type=F32) for u in units]
            sts = [st_ref[...]]
            for u in units:
                sts.append(sts[-1] * eb_ref[cs[u]][0:1, :] + inc[u])
            o = [oi[u] + lax.dot_general(qd_ref[rows_of(cs[u]), :], sts[u].astype(BF16), nt,
                                         preferred_element_type=F32) for u in units]
            for u in units:
                o_ref[rows_of(cs[u]), :] = _head_out(o[u], gn_ref[...], gate_ref[rows_of(cs[u]), :])
            st_ref[...] = sts[-1]

        prepare(0)

        def body(ti, carry):
            contract(ti)
            prepare(ti + 1)
            return carry

        lax.fori_loop(0, mxu_trips - 1, body, 0)
        contract(mxu_trips - 1)

    @pl.when(jnp.logical_not(in_range))
    def _exact():
        mask = _off_mask(CHUNK, DIAG_BLK)

        def body(ci, carry):
            st = st_ref[...]
            for u in range(CHUNKS_PER_TRIP):
                rows = chunk_rows(ci, u)
                o, st = _chunk_exact(qs_ref[rows, :], ks_ref[rows, :], v_ref[rows, :],
                                     _cumsum_chunk(gs_ref[rows, :]), st, mask)
                o_ref[rows, :] = _head_out(o, gn_ref[...], gate_ref[rows, :])
            st_ref[...] = st
            return carry

        lax.fori_loop(0, trips, body, 0)

    sout_ref[...] = st_ref[...].T


def _sample_scan_kernel(*refs, gla, steps):
    if gla:
        (q_ref, k_ref, v_ref, gate_ref, al_ref, wa_ref, ba_ref, gn_ref, s0_ref,
         o_ref, sout_ref) = refs
        q, k, v, g = _gla_inputs(q_ref[...], k_ref[...], v_ref[...], al_ref[...],
                                 wa_ref[...], ba_ref[...])
    else:
        (q_ref, f_ref, v_ref, gate_ref, lbl_ref, gn_ref, s0_ref, o_ref, sout_ref) = refs
        q, k, v, g = _hgrn_inputs(q_ref[...], f_ref[...], v_ref[...], _lower_bound(lbl_ref))
    v32 = v.astype(F32)
    rows = SAMPLE_SEQS * steps
    groups = [slice(b * steps, (b + 1) * steps) for b in range(SAMPLE_SEQS)]
    tn = (((0,), (0,)), ((), ()))

    bc = jnp.concatenate([_scan_sublanes(g[sl]) for sl in groups], axis=0)

    def repeated(r):
        return jnp.concatenate([jnp.broadcast_to(bc[sl][r:r + 1, :], (steps, DK)) for sl in groups],
                               axis=0)

    qd = (q * jnp.exp(bc)).astype(BF16)
    kdec = (k * jnp.exp(repeated(steps - 1) - bc)).astype(BF16)
    r = lax.broadcasted_iota(jnp.int32, (rows, SAMPLE_SEQS), 0)
    c = lax.broadcasted_iota(jnp.int32, (rows, SAMPLE_SEQS), 1)
    selector = jnp.where((r >> (steps.bit_length() - 1)) == c, 1.0, 0.0).astype(F32)
    decay = jnp.exp(lax.dot_general(g, selector, tn, precision=HIGHEST,
                                    preferred_element_type=F32))

    inter = [jnp.dot(qd[sl], s0_ref[b].astype(BF16), preferred_element_type=F32)
             for b, sl in enumerate(groups)]
    upd = [lax.dot_general(kdec[sl], v32[sl].astype(BF16), tn, preferred_element_type=F32)
           for sl in groups]
    for b in range(SAMPLE_SEQS):
        sout_ref[b] = decay[:, b:b + 1] * s0_ref[b] + upd[b]
    o = jnp.concatenate(inter, axis=0) + _diag(q, k, v32, bc, steps)
    o_ref[...] = _head_out(o, gn_ref[...], gate_ref[...])


def _proj_spec(rows, width, c0, per_head, row_block, tile_width=TN_PROJ):
    per_tile = tile_width // width

    def index(b, h):
        g = c0 * LANES // width + h * per_head
        return (g // per_tile, row_block(b), g % per_tile)

    return pl.BlockSpec((None, rows, width), index)


def _prompt_scan(p, pm, *, gla, batch, seq, meta_block, small, name):
    heads, dv = (H_B, DV_B) if gla else (H_A, DV_A)
    n_chunks = seq // CHUNK
    blocks = CHUNK // DIAG_BLK
    stack = DIAG_BLK * blocks * (blocks + 1) // 2
    p16, pf, pa = p
    pm16, pmf, pma = pm
    tok = functools.partial(_proj_spec, seq, row_block=lambda b: b)
    met = functools.partial(_proj_spec, N_META, row_block=lambda b: meta_block)
    vec = lambda width: pl.BlockSpec((1, width), lambda b, h: (0, h))
    if gla:
        wa, ba, gn = small
        in_specs = [tok(DK, COL_QB, 1), tok(DK, COL_KB, 1), tok(dv, COL_VB, 1), tok(dv, COL_RB, 1),
                    tok(LANES, 0, 0, tile_width=LANES),
                    met(DK, COL_KB, 1), met(dv, COL_VB, 1), met(LANES, 0, 0, tile_width=LANES),
                    pl.BlockSpec((LANES, DK), lambda b, h: (0, h)), vec(DK), vec(dv)]
        args = (p16, p16, p16, p16, pa, pm16, pm16, pma, wa, ba, gn)
    else:
        lbl, gn = small
        in_specs = [tok(DK, COL_QA, 1), tok(DK, 0, 1), tok(dv, COL_IA, 1), tok(dv, COL_GA, 1),
                    met(DK, 0, 1), met(dv, COL_IA, 1),
                    pl.BlockSpec((2, DK), lambda b, h: (0, h)), vec(dv)]
        args = (p16, pf, p16, p16, pmf, pm16, lbl, gn)
    return pl.pallas_call(
        functools.partial(_prompt_scan_kernel, gla=gla, seq=seq, meta=N_META),
        grid=(batch, heads),
        in_specs=in_specs,
        out_specs=[pl.BlockSpec((seq, dv), lambda b, h: (b, h)),
                   pl.BlockSpec((None, None, DK, dv), lambda b, h: (b, h, 0, 0))],
        out_shape=[jax.ShapeDtypeStruct((batch * seq, heads * dv), BF16),
                   jax.ShapeDtypeStruct((batch, heads, DK, dv), F32)],
        scratch_shapes=[
            pltpu.VMEM((dv, DK), F32),
            pltpu.VMEM((seq, DK), F32),
            pltpu.VMEM((seq, DK), F32),
            pltpu.VMEM((seq, DK), F32),
            pltpu.VMEM((seq, DK), BF16),
            pltpu.VMEM((seq, DK), BF16),
            pltpu.VMEM((n_chunks * stack, DK), BF16),
            pltpu.VMEM((seq, DK), BF16),
            pltpu.VMEM((n_chunks, SUBLANES, DK), F32),
        ],
        compiler_params=_params("parallel", "parallel"),
        name=name,
    )(*args)


def _sample_scan(p, s0, *, gla, batch, steps, small, name):
    assert steps == SUBLANES, "the sample recurrence keeps one sequence per sublane group"
    heads, dv = (H_B, DV_B) if gla else (H_A, DV_A)
    rows = SAMPLE_SEQS * steps
    p16, pf, pa = p
    tok = functools.partial(_proj_spec, rows, row_block=lambda b: b)
    vec = lambda width: pl.BlockSpec((1, width), lambda b, h: (0, h))
    state = pl.BlockSpec((SAMPLE_SEQS, None, DK, dv), lambda b, h: (b, h, 0, 0))
    if gla:
        wa, ba, gn = small
        in_specs = [tok(DK, COL_QB, 1), tok(DK, COL_KB, 1), tok(dv, COL_VB, 1), tok(dv, COL_RB, 1),
                    tok(LANES, 0, 0, tile_width=LANES),
                    pl.BlockSpec((LANES, DK), lambda b, h: (0, h)), vec(DK), vec(dv), state]
        args = (p16, p16, p16, p16, pa, wa, ba, gn, s0)
    else:
        lbl, gn = small
        in_specs = [tok(DK, COL_QA, 1), tok(DK, 0, 1), tok(dv, COL_IA, 1), tok(dv, COL_GA, 1),
                    pl.BlockSpec((2, DK), lambda b, h: (0, h)), vec(dv), state]
        args = (p16, pf, p16, p16, lbl, gn, s0)
    return pl.pallas_call(
        functools.partial(_sample_scan_kernel, gla=gla, steps=steps),
        grid=(batch // SAMPLE_SEQS, heads),
        in_specs=in_specs,
        out_specs=[pl.BlockSpec((rows, dv), lambda b, h: (b, h)), state],
        out_shape=[jax.ShapeDtypeStruct((batch * steps, heads * dv), BF16),
                   jax.ShapeDtypeStruct((batch, heads, DK, dv), F32)],
        compiler_params=_params("parallel", "parallel"),
        name=name,
    )(*args)


def kernel(x_prompt, x_sample, state_hgrn, state_gla, meta_tokens, lb_logits, ffn1_norm, w_ffn1_in,
           w_ffn1_out, mix_norm, w_in, w_alpha_up, b_alpha, gnorm_a, gnorm_b, w_out, ffn2_norm,
           w_ffn2_in, w_ffn2_out, final_norm):
    batch, seq, _ = x_prompt.shape
    dec_batch, steps, _ = x_sample.shape

    w_in_t = w_in[0].T
    wop = w_out[0].astype(BF16)
    wa = jnp.pad(w_alpha_up[0], ((0, LANES - ALPHA_RANK), (0, 0))).astype(BF16)
    n1, nm, n2, nf = ffn1_norm, mix_norm, ffn2_norm, final_norm[None]
    small_a = (lb_logits, gnorm_a)
    small_b = (wa, b_alpha, gnorm_b)
    n_sample = dec_batch * steps

    xs = jnp.concatenate([x_sample.reshape(n_sample, D_MODEL), meta_tokens], axis=0)
    h1s, *w1 = _ffn(xs, n1, (w_ffn1_in[0], w_ffn1_in[0], w_ffn1_out[0]), nf, tm=xs.shape[0],
                    tf=TF_CAST, apply_final_norm=False, emit_weights=True, name="ffn1_sample")
    ps = _in_proj(h1s, nm, w_in_t, tm=xs.shape[0], name="inproj_sample")
    oa_s, sa_s = _sample_scan(ps, state_hgrn[0], gla=False, batch=dec_batch, steps=steps,
                              small=small_a, name="scan_hgrn_sample")
    ob_s, sb_s = _sample_scan(ps, state_gla[0], gla=True, batch=dec_batch, steps=steps,
                              small=small_b, name="scan_gla_sample")
    h2s = _out_proj(oa_s, ob_s, h1s, wop, tm=TM, name="outproj_sample")
    y_s, *w2 = _ffn(h2s, n2, (w_ffn2_in[0], w_ffn2_in[0], w_ffn2_out[0]), nf, tm=n_sample,
                    tf=TF_CAST, apply_final_norm=True, emit_weights=True, name="ffn2_sample")

    h1p = _ffn(x_prompt.reshape(batch * seq, D_MODEL), n1, w1, nf, tm=TM, tf=TF,
               apply_final_norm=False, emit_weights=False, name="ffn1_prompt")
    pp = _in_proj(h1p, nm, w_in_t, tm=TM_PROJ, name="inproj_prompt")
    meta_block = n_sample // N_META
    oa_p, sa_p = _prompt_scan(pp, ps, gla=False, batch=batch, seq=seq, meta_block=meta_block,
                              small=small_a, name="scan_hgrn_prompt")
    ob_p, sb_p = _prompt_scan(pp, ps, gla=True, batch=batch, seq=seq, meta_block=meta_block,
                              small=small_b, name="scan_gla_prompt")
    h2p = _out_proj(oa_p, ob_p, h1p, wop, tm=TM, name="outproj_prompt")
    y_p = _ffn(h2p, n2, w2, nf, tm=TM, tf=TF, apply_final_norm=True, emit_weights=False,
               name="ffn2_prompt")

    return (y_p.reshape(batch, seq, D_MODEL), y_s.reshape(dec_batch, steps, D_MODEL),
            sa_p[None], sb_p[None], sa_s[None], sb_s[None])
```

```python
import functools

import jax
import jax.numpy as jnp
from jax import lax
from jax.experimental import pallas as pl
from jax.experimental.pallas import tpu as pltpu

F32 = jnp.float32
BF16 = jnp.bfloat16
HIGHEST = lax.Precision.HIGHEST

D_MODEL = 2048
N_META = 16
DK = 128
H_A, DV_A = 8, 128
H_B, DV_B = 4, 256
D_HALF = H_A * DV_A
ALPHA_RANK = 16
GATE_TEMP = 16.0
D_FF = 5632
CHUNK = 64
EPS = 1e-6
IN_PROJ_WIDTHS = (D_HALF, D_HALF, D_HALF, D_HALF, H_B * DK, H_B * DK, D_HALF, D_HALF, ALPHA_RANK)

LANES = 128
SUBLANES = 8
VMEM_LIMIT_BYTES = 56 * 1024 * 1024

COL_QA, COL_IA, COL_GA, COL_QB, COL_KB, COL_VB, COL_RB = 0, 8, 16, 24, 28, 32, 40
P16_COLS = 3 * D_HALF + 2 * H_B * DK + 2 * D_HALF

TM = 512
TF = 512
TF_CAST = 256
TN_PROJ = 512
TM_PROJ = 512
N16_TILES = P16_COLS // TN_PROJ
DIAG_BLK = 16
SAMPLE_SEQS = 32
CHUNKS_PER_TRIP = 4
CHUNKS_PER_MXU_TRIP = 8

FAST_MIN_LOG_GATE = -60.0 / (DIAG_BLK - 1)
FAST_MAX_KEY = 1e10


def _rms(x, w):
    return x * lax.rsqrt(jnp.mean(x * x, axis=-1, keepdims=True) + EPS) * w


def _silu(x):
    return x * jax.nn.sigmoid(x)


def _params(*sem):
    return pltpu.CompilerParams(dimension_semantics=sem, vmem_limit_bytes=VMEM_LIMIT_BYTES)


def _ffn_kernel(h_ref, nw_ref, wg_ref, wu_ref, wo_ref, fnw_ref, out_ref, *rest,
                apply_final_norm, emit_weights):
    u_ref = rest[-1]
    j = pl.program_id(1)

    @pl.when(j == 0)
    def _init():
        u_ref[...] = _rms(h_ref[...], nw_ref[...]).astype(BF16)
        out_ref[...] = jnp.zeros_like(out_ref)

    wg, wu, wo = (r[...].astype(BF16) for r in (wg_ref, wu_ref, wo_ref))
    if emit_weights:
        for dst, w in zip(rest[:3], (wg, wu, wo)):
            dst[...] = w
    u = u_ref[...]
    g = jnp.dot(u, wg, preferred_element_type=F32)
    up = jnp.dot(u, wu, preferred_element_type=F32)
    a = (_silu(g) * up).astype(BF16)
    out_ref[...] += jnp.dot(a, wo, preferred_element_type=F32)

    @pl.when(j == pl.num_programs(1) - 1)
    def _finish():
        y = h_ref[...] + 0.5 * out_ref[...]
        if apply_final_norm:
            y = _rms(y, fnw_ref[...])
        out_ref[...] = y


def _ffn(h, norm_w, weights, final_norm_w, *, tm, tf, apply_final_norm, emit_weights, name):
    m = h.shape[0]
    nf = D_FF // tf
    rows = pl.BlockSpec((tm, D_MODEL), lambda i, j: (i, 0),
                        pipeline_mode=pl.Buffered(1) if m == tm else None)
    vec = pl.BlockSpec((1, D_MODEL), lambda i, j: (0, 0))
    col_tile = pl.BlockSpec((D_MODEL, tf), lambda i, j: (0, j))
    row_tile = pl.BlockSpec((tf, D_MODEL), lambda i, j: (j, 0))
    out_specs = [rows]
    out_shape = [jax.ShapeDtypeStruct((m, D_MODEL), F32)]
    if emit_weights:
        assert m == tm, "weight casts are written once, by a single row tile"
        up_tile = pl.BlockSpec((D_MODEL, tf), lambda i, j: (0, j + nf))
        out_specs += [col_tile, col_tile, row_tile]
        out_shape += [jax.ShapeDtypeStruct((D_MODEL, D_FF), BF16)] * 2
        out_shape += [jax.ShapeDtypeStruct((D_FF, D_MODEL), BF16)]
    else:
        up_tile = col_tile
    res = pl.pallas_call(
        functools.partial(_ffn_kernel, apply_final_norm=apply_final_norm,
                          emit_weights=emit_weights),
        grid=(m // tm, nf),
        in_specs=[rows, vec, col_tile, up_tile, row_tile, vec],
        out_specs=out_specs,
        out_shape=out_shape,
        scratch_shapes=[pltpu.VMEM((tm, D_MODEL), BF16)],
        compiler_params=_params("parallel", "arbitrary"),
        name=name,
    )(h, norm_w, *weights, final_norm_w)
    return res if emit_weights else res[0]


def _proj_kernel(h_ref, nw_ref, w_ref, wal_ref, p16_ref, pf_ref, pa_ref, u_ref):
    j = pl.program_id(1)

    @pl.when(j == 0)
    def _init():
        u = _rms(h_ref[...], nw_ref[...]).astype(BF16)
        u_ref[...] = u
        wal = jnp.concatenate([wal_ref[...].astype(BF16),
                               jnp.zeros((LANES - ALPHA_RANK, D_MODEL), BF16)], axis=0)
        pa_ref[...] = lax.dot_general(u, wal, (((1,), (1,)), ((), ())),
                                      preferred_element_type=F32)

    def project():
        return lax.dot_general(u_ref[...], w_ref[...].astype(BF16), (((1,), (1,)), ((), ())),
                               preferred_element_type=F32)

    @pl.when(j < N16_TILES)
    def _narrow():
        p16_ref[...] = project().astype(BF16)

    @pl.when(j >= N16_TILES)
    def _wide():
        pf_ref[...] = project()


def _w_in_tile(j):
    fa_lo, fa_hi = D_HALF // TN_PROJ, 2 * D_HALF // TN_PROJ
    moved = fa_hi - fa_lo
    return jnp.where(j < fa_lo, j, jnp.where(j < N16_TILES, j + moved, j - N16_TILES + fa_lo))


def _in_proj(h, norm_w, w, *, tm, name):
    m = h.shape[0]
    return pl.pallas_call(
        _proj_kernel,
        grid=(m // tm, (P16_COLS + D_HALF) // TN_PROJ),
        in_specs=[
            pl.BlockSpec((tm, D_MODEL), lambda i, j: (i, 0)),
            pl.BlockSpec((1, D_MODEL), lambda i, j: (0, 0)),
            pl.BlockSpec((TN_PROJ, D_MODEL), lambda i, j: (_w_in_tile(j), 0)),
            pl.BlockSpec((ALPHA_RANK, D_MODEL), lambda i, j: ((P16_COLS + D_HALF) // ALPHA_RANK, 0)),
        ],
        out_specs=[
            pl.BlockSpec((None, tm, TN_PROJ), lambda i, j: (jnp.minimum(j, N16_TILES - 1), i, 0)),
            pl.BlockSpec((None, tm, TN_PROJ), lambda i, j: (jnp.maximum(j - N16_TILES, 0), i, 0)),
            pl.BlockSpec((None, tm, LANES), lambda i, j: (0, i, 0)),
        ],
        out_shape=[jax.ShapeDtypeStruct((N16_TILES, m, TN_PROJ), BF16),
                   jax.ShapeDtypeStruct((D_HALF // TN_PROJ, m, TN_PROJ), F32),
                   jax.ShapeDtypeStruct((1, m, LANES), F32)],
        scratch_shapes=[pltpu.VMEM((tm, D_MODEL), BF16)],
        compiler_params=_params("parallel", "arbitrary"),
        name=name,
    )(h, norm_w, w, w)


def _out_proj_kernel(oa_ref, ob_ref, h_ref, wa_ref, wb_ref, out_ref):
    out_ref[...] = (h_ref[...]
                    + jnp.dot(oa_ref[...], wa_ref[...], preferred_element_type=F32)
                    + jnp.dot(ob_ref[...], wb_ref[...], preferred_element_type=F32))


def _out_proj(oa, ob, h, w, *, tm, name):
    m = oa.shape[0]
    return pl.pallas_call(
        _out_proj_kernel,
        grid=(m // tm,),
        in_specs=[
            pl.BlockSpec((tm, D_HALF), lambda i: (i, 0)),
            pl.BlockSpec((tm, D_HALF), lambda i: (i, 0)),
            pl.BlockSpec((tm, D_MODEL), lambda i: (i, 0)),
            pl.BlockSpec((D_HALF, D_MODEL), lambda i: (0, 0)),
            pl.BlockSpec((D_HALF, D_MODEL), lambda i: (1, 0)),
        ],
        out_specs=pl.BlockSpec((tm, D_MODEL), lambda i: (i, 0)),
        out_shape=jax.ShapeDtypeStruct((m, D_MODEL), F32),
        compiler_params=_params("parallel"),
        name=name,
    )(oa, ob, h, w, w)


def _tri(c, seq_len):
    r = lax.broadcasted_iota(jnp.int32, (c, c), 0)
    s = lax.broadcasted_iota(jnp.int32, (c, c), 1)
    return jnp.where((r >= s) & ((r ^ s) < seq_len), 1.0, 0.0).astype(F32)


def _off_mask(c, blk):
    nb = c // blk
    rows = nb * (nb - 1) // 2 * blk
    r = lax.broadcasted_iota(jnp.int32, (c, rows), 0) >> (blk.bit_length() - 1)
    s = lax.broadcasted_iota(jnp.int32, (c, rows), 1)
    seg = jnp.zeros((c, rows), jnp.int32)
    for i in range(1, nb):
        seg = seg + jnp.where(s >= blk * i * (i - 1) // 2, 1, 0)
    return jnp.where(r == seg, 1.0, 0.0).astype(F32)


def _diag(q, k, v, bc, blk):
    c = q.shape[0]
    row = lax.broadcasted_iota(jnp.int32, (blk, 1), 0)
    outs = []
    for i in range(c // blk):
        sl = slice(i * blk, (i + 1) * blk)
        qi, ki, vi, bi = q[sl], k[sl], v[sl].astype(F32), bc[sl]
        acc = jnp.zeros((blk, v.shape[1]), F32)
        for s in range(blk):
            e = jnp.exp(jnp.minimum(bi - bi[s:s + 1, :], 0.0))
            col = jnp.sum(qi * ki[s:s + 1, :] * e, axis=-1, keepdims=True)
            col = jnp.where(row >= s, col, 0.0)
            acc = acc + col * vi[s:s + 1, :]
        outs.append(acc)
    return outs[0] if len(outs) == 1 else jnp.concatenate(outs, axis=0)


def _offdiag(q, k, v, bc, blk, mask):
    c = q.shape[0]
    nb = c // blk
    qs, ks, vs = [jnp.zeros((blk, DK), F32)], [], []
    for i in range(1, nb):
        lo = i * blk
        ref = bc[lo:lo + 1, :]
        qs.append(q[lo:lo + blk] * jnp.exp(bc[lo:lo + blk] - ref))
        ks.append(k[:lo] * jnp.exp(ref - bc[:lo]))
        vs.append(v[:lo])
    qt = jnp.concatenate(qs, axis=0).astype(BF16)
    kt = jnp.concatenate(ks, axis=0).astype(BF16)
    vt = jnp.concatenate(vs, axis=0).astype(BF16)
    a = lax.dot_general(qt, kt, (((1,), (1,)), ((), ())), preferred_element_type=F32)
    return jnp.dot((a * mask).astype(BF16), vt, preferred_element_type=F32)


def _hgrn_inputs(qa, fa, ia, lb):
    fg = lb + (1.0 - lb) * jax.nn.sigmoid(fa)
    return _silu(qa.astype(F32)) * (DK ** -0.5), 1.0 - fg, ia, jnp.log(fg)


def _gla_inputs(qb, kb, vb, al, wa, ba):
    alpha = jnp.dot(al.astype(BF16), wa, preferred_element_type=F32) + ba
    logsig = jnp.minimum(alpha, 0.0) - jnp.log1p(jnp.exp(-jnp.abs(alpha)))
    return qb.astype(F32) * (DK ** -0.5), kb.astype(F32), vb, logsig / GATE_TEMP


def _head_out(o, gn, gate):
    return (_rms(o, gn) * _silu(gate.astype(F32))).astype(BF16)


def _lower_bound(lbl_ref):
    x = lbl_ref[...]
    e = jnp.exp(x - jnp.max(x, axis=0, keepdims=True))
    return e[0:1, :] / jnp.sum(e, axis=0, keepdims=True)


def _stack_mask(c, blk):
    nb = c // blk
    rows = blk * nb * (nb + 1) // 2
    t = lax.broadcasted_iota(jnp.int32, (c, rows), 0)
    r = lax.broadcasted_iota(jnp.int32, (c, rows), 1)
    seg = jnp.zeros((c, rows), jnp.int32)
    off = jnp.zeros((c, rows), jnp.int32)
    for i in range(1, nb):
        start = blk * i * (i + 1) // 2
        seg = seg + jnp.where(r >= start, 1, 0)
        off = jnp.where(r >= start, start, off)
    keep = ((t >> (blk.bit_length() - 1)) == seg) & (r - off <= t)
    return jnp.where(keep, 1.0, 0.0).astype(F32)


def _state_update_t(k, vb, bc, st):
    c = k.shape[0]
    blast = bc[c - 1:c, :]
    kdec = (k * jnp.exp(blast - bc)).astype(BF16)
    upd = lax.dot_general(vb, kdec, (((0,), (0,)), ((), ())), preferred_element_type=F32)
    if st is None:
        return upd
    return st * jnp.exp(blast) + upd


def _scan_sublanes(p):
    row = lax.broadcasted_iota(jnp.int32, (SUBLANES, DK), 0)
    for s in (1, 2, 4):
        p = p + jnp.where(row >= s, pltpu.roll(p, s, axis=0), 0.0)
    return p


def _cumsum_chunk(g):
    out, carry = [], None
    for j in range(CHUNK // SUBLANES):
        p = _scan_sublanes(g[j * SUBLANES:(j + 1) * SUBLANES])
        if carry is not None:
            p = p + carry
        carry = p[SUBLANES - 1:SUBLANES, :]
        out.append(p)
    return jnp.concatenate(out, axis=0)


def _rescaled_operands(q, k, bc):
    c, blk = CHUNK, DIAG_BLK
    qs, ks = [], []
    for i in range(c // blk):
        lo, hi = i * blk, (i + 1) * blk
        ref = bc[lo:lo + 1, :]
        qs.append(q[lo:hi] * jnp.exp(bc[lo:hi] - ref))
        ks.append(k[:hi] * jnp.exp(ref - bc[:hi]))
    return ((q * jnp.exp(bc)).astype(BF16), jnp.concatenate(qs, axis=0).astype(BF16),
            jnp.concatenate(ks, axis=0).astype(BF16),
            (k * jnp.exp(bc[c - 1:c, :] - bc)).astype(BF16))


def _chunk_exact(q, k, v, bc, st, mask):
    o = lax.dot_general((q * jnp.exp(bc)).astype(BF16), st.astype(BF16),
                        (((1,), (1,)), ((), ())), preferred_element_type=F32)
    o = o + _offdiag(q, k, v, bc, DIAG_BLK, mask) + _diag(q, k, v, bc, DIAG_BLK)
    return o, _state_update_t(k, v.astype(BF16), bc, st)


def _prompt_scan_kernel(*refs, gla, seq, meta):
    n_in = 11 if gla else 8
    if gla:
        q_ref, k_ref, v_ref, gate_ref, al_ref, mk_ref, mv_ref, mal_ref, wa_ref, ba_ref, gn_ref = (
            refs[:n_in])
    else:
        q_ref, f_ref, v_ref, gate_ref, mf_ref, mv_ref, lbl_ref, gn_ref = refs[:n_in]
        lb = _lower_bound(lbl_ref)
    o_ref, sout_ref = refs[n_in:n_in + 2]
    st_ref, qs_ref, ks_ref, gs_ref, qd_ref, qt_ref, kt_ref, kdec_ref, eb_ref = refs[n_in + 2:]
    trips = seq // (CHUNK * CHUNKS_PER_TRIP)
    stack = kt_ref.shape[0] // (seq // CHUNK)
    nt = (((1,), (1,)), ((), ()))
    tn = (((0,), (0,)), ((), ()))

    def chunk_rows(ci, u, n=CHUNK):
        return pl.ds(pl.multiple_of((ci * CHUNKS_PER_TRIP + u) * n, n), n)

    if gla:
        _, mk, mv, mg = _gla_inputs(mk_ref[...], mk_ref[...], mv_ref[...], mal_ref[...],
                                    wa_ref[...], ba_ref[...])
    else:
        _, mk, mv, mg = _hgrn_inputs(mf_ref[...], mf_ref[...], mv_ref[...], lb)
    mbc = jnp.dot(_tri(meta, meta), mg, precision=HIGHEST, preferred_element_type=F32)
    st_ref[...] = _state_update_t(mk, mv.astype(BF16), mbc, None)

    def gate_pass(ci, carry):
        gmin, kmax = carry
        trip_rows = CHUNK * CHUNKS_PER_TRIP
        rows = pl.ds(pl.multiple_of(ci * trip_rows, trip_rows), trip_rows)
        if gla:
            q, k, _, g = _gla_inputs(q_ref[rows, :], k_ref[rows, :], None, al_ref[rows, :],
                                     wa_ref[...], ba_ref[...])
        else:
            q, k, _, g = _hgrn_inputs(q_ref[rows, :], f_ref[rows, :], None, lb)
        qs_ref[rows, :] = q
        ks_ref[rows, :] = k
        gs_ref[rows, :] = g
        return (jnp.minimum(gmin, jnp.min(g, axis=0, keepdims=True)),
                jnp.maximum(kmax, jnp.max(jnp.abs(k), axis=0, keepdims=True)))

    gmin, kmax = lax.fori_loop(0, trips, gate_pass,
                               (jnp.zeros((1, DK), F32), jnp.zeros((1, DK), F32)))
    in_range = (jnp.min(gmin) >= FAST_MIN_LOG_GATE) & (jnp.max(kmax) <= FAST_MAX_KEY)

    def rows_of(c, n=CHUNK):
        return pl.ds(pl.multiple_of(c * n, n), n)

    @pl.when(in_range)
    def _rescaled():
        mask = _stack_mask(CHUNK, DIAG_BLK)
        units = range(CHUNKS_PER_MXU_TRIP)
        mxu_trips = seq // (CHUNK * CHUNKS_PER_MXU_TRIP)

        def prepare(ti):
            for u in units:
                c = ti * CHUNKS_PER_MXU_TRIP + u
                bc = _cumsum_chunk(gs_ref[rows_of(c), :])
                qd, qt, kt, kdec = _rescaled_operands(qs_ref[rows_of(c), :], ks_ref[rows_of(c), :],
                                                      bc)
                qd_ref[rows_of(c), :] = qd
                qt_ref[rows_of(c), :] = qt
                kt_ref[rows_of(c, stack), :] = kt
                kdec_ref[rows_of(c), :] = kdec
                eb_ref[c] = jnp.broadcast_to(jnp.exp(bc[CHUNK - 1:CHUNK, :]), (SUBLANES, DK))

        def contract(ti):
            cs = [ti * CHUNKS_PER_MXU_TRIP + u for u in units]
            vb = [v_ref[rows_of(c), :].astype(BF16) for c in cs]
            a = [lax.dot_general(qt_ref[rows_of(c), :], kt_ref[rows_of(c, stack), :], nt,
                                 preferred_element_type=F32) for c in cs]
            am = [(x * mask).astype(BF16) for x in a]
            vt = [jnp.concatenate([x[:(i + 1) * DIAG_BLK] for i in range(CHUNK // DIAG_BLK)], axis=0)
                  for x in vb]
            oi = [jnp.dot(am[u], vt[u], preferred_element_type=F32) for u in units]
            inc = [lax.dot_general(vb[u], kdec_ref[rows_of(cs[u]), :], tn,
                                   preferred_element_type=F32) for u in units]
            sts = [st_ref[...]]
            for u in units:
                sts.append(sts[-1] * eb_ref[cs[u]][0:1, :] + inc[u])
            o = [oi[u] + lax.dot_general(qd_ref[rows_of(cs[u]), :], sts[u].astype(BF16), nt,
                                         preferred_element_type=F32) for u in units]
            for u in units:
                o_ref[rows_of(cs[u]), :] = _head_out(o[u], gn_ref[...], gate_ref[rows_of(cs[u]), :])
            st_ref[...] = sts[-1]

        prepare(0)

        def body(ti, carry):
            contract(ti)
            prepare(ti + 1)
            return carry

        lax.fori_loop(0, mxu_trips - 1, body, 0)
        contract(mxu_trips - 1)

    @pl.when(jnp.logical_not(in_range))
    def _exact():
        mask = _off_mask(CHUNK, DIAG_BLK)

        def body(ci, carry):
            st = st_ref[...]
            for u in range(CHUNKS_PER_TRIP):
                rows = chunk_rows(ci, u)
                o, st = _chunk_exact(qs_ref[rows, :], ks_ref[rows, :], v_ref[rows, :],
                                     _cumsum_chunk(gs_ref[rows, :]), st, mask)
                o_ref[rows, :] = _head_out(o, gn_ref[...], gate_ref[rows, :])
            st_ref[...] = st
            return carry

        lax.fori_loop(0, trips, body, 0)

    sout_ref[...] = st_ref[...].T


def _sample_scan_kernel(*refs, gla, steps):
    if gla:
        (q_ref, k_ref, v_ref, gate_ref, al_ref, wa_ref, ba_ref, gn_ref, s0_ref,
         o_ref, sout_ref) = refs
        q, k, v, g = _gla_inputs(q_ref[...], k_ref[...], v_ref[...], al_ref[...],
                                 wa_ref[...], ba_ref[...])
    else:
        (q_ref, f_ref, v_ref, gate_ref, lbl_ref, gn_ref, s0_ref, o_ref, sout_ref) = refs
        q, k, v, g = _hgrn_inputs(q_ref[...], f_ref[...], v_ref[...], _lower_bound(lbl_ref))
    v32 = v.astype(F32)
    rows = SAMPLE_SEQS * steps
    groups = [slice(b * steps, (b + 1) * steps) for b in range(SAMPLE_SEQS)]
    tn = (((0,), (0,)), ((), ()))

    bc = jnp.concatenate([_scan_sublanes(g[sl]) for sl in groups], axis=0)

    def repeated(r):
        return jnp.concatenate([jnp.broadcast_to(bc[sl][r:r + 1, :], (steps, DK)) for sl in groups],
                               axis=0)

    qd = (q * jnp.exp(bc)).astype(BF16)
    kdec = (k * jnp.exp(repeated(steps - 1) - bc)).astype(BF16)
    r = lax.broadcasted_iota(jnp.int32, (rows, SAMPLE_SEQS), 0)
    c = lax.broadcasted_iota(jnp.int32, (rows, SAMPLE_SEQS), 1)
    selector = jnp.where((r >> (steps.bit_length() - 1)) == c, 1.0, 0.0).astype(F32)
    decay = jnp.exp(lax.dot_general(g, selector, tn, precision=HIGHEST,
                                    preferred_element_type=F32))

    inter = [jnp.dot(qd[sl], s0_ref[b].astype(BF16), preferred_element_type=F32)
             for b, sl in enumerate(groups)]
    upd = [lax.dot_general(kdec[sl], v32[sl].astype(BF16), tn, preferred_element_type=F32)
           for sl in groups]
    for b in range(SAMPLE_SEQS):
        sout_ref[b] = decay[:, b:b + 1] * s0_ref[b] + upd[b]
    o = jnp.concatenate(inter, axis=0) + _diag(q, k, v32, bc, steps)
    o_ref[...] = _head_out(o, gn_ref[...], gate_ref[...])


def _proj_spec(rows, width, c0, per_head, row_block, tile_width=TN_PROJ):
    per_tile = tile_width // width

    def index(b, h):
        g = c0 * LANES // width + h * per_head
        return (g // per_tile, row_block(b), g % per_tile)

    return pl.BlockSpec((None, rows, width), index)


def _prompt_scan(p, pm, *, gla, batch, seq, meta_block, small, name):
    heads, dv = (H_B, DV_B) if gla else (H_A, DV_A)
    n_chunks = seq // CHUNK
    blocks = CHUNK // DIAG_BLK
    stack = DIAG_BLK * blocks * (blocks + 1) // 2
    p16, pf, pa = p
    pm16, pmf, pma = pm
    tok = functools.partial(_proj_spec, seq, row_block=lambda b: b)
    met = functools.partial(_proj_spec, N_META, row_block=lambda b: meta_block)
    vec = lambda width: pl.BlockSpec((1, width), lambda b, h: (0, h))
    if gla:
        wa, ba, gn = small
        in_specs = [tok(DK, COL_QB, 1), tok(DK, COL_KB, 1), tok(dv, COL_VB, 1), tok(dv, COL_RB, 1),
                    tok(LANES, 0, 0, tile_width=LANES),
                    met(DK, COL_KB, 1), met(dv, COL_VB, 1), met(LANES, 0, 0, tile_width=LANES),
                    pl.BlockSpec((LANES, DK), lambda b, h: (0, h)), vec(DK), vec(dv)]
        args = (p16, p16, p16, p16, pa, pm16, pm16, pma, wa, ba, gn)
    else:
        lbl, gn = small
        in_specs = [tok(DK, COL_QA, 1), tok(DK, 0, 1), tok(dv, COL_IA, 1), tok(dv, COL_GA, 1),
                    met(DK, 0, 1), met(dv, COL_IA, 1),
                    pl.BlockSpec((2, DK), lambda b, h: (0, h)), vec(dv)]
        args = (p16, pf, p16, p16, pmf, pm16, lbl, gn)
    return pl.pallas_call(
        functools.partial(_prompt_scan_kernel, gla=gla, seq=seq, meta=N_META),
        grid=(batch, heads),
        in_specs=in_specs,
        out_specs=[pl.BlockSpec((seq, dv), lambda b, h: (b, h)),
                   pl.BlockSpec((None, None, DK, dv), lambda b, h: (b, h, 0, 0))],
        out_shape=[jax.ShapeDtypeStruct((batch * seq, heads * dv), BF16),
                   jax.ShapeDtypeStruct((batch, heads, DK, dv), F32)],
        scratch_shapes=[
            pltpu.VMEM((dv, DK), F32),
            pltpu.VMEM((seq, DK), F32),
            pltpu.VMEM((seq, DK), F32),
            pltpu.VMEM((seq, DK), F32),
            pltpu.VMEM((seq, DK), BF16),
            pltpu.VMEM((seq, DK), BF16),
            pltpu.VMEM((n_chunks * stack, DK), BF16),
            pltpu.VMEM((seq, DK), BF16),
            pltpu.VMEM((n_chunks, SUBLANES, DK), F32),
        ],
        compiler_params=_params("parallel", "parallel"),
        name=name,
    )(*args)


def _sample_scan(p, s0, *, gla, batch, steps, small, name):
    assert steps == SUBLANES, "the sample recurrence keeps one sequence per sublane group"
    heads, dv = (H_B, DV_B) if gla else (H_A, DV_A)
    rows = SAMPLE_SEQS * steps
    p16, pf, pa = p
    tok = functools.partial(_proj_spec, rows, row_block=lambda b: b)
    vec = lambda width: pl.BlockSpec((1, width), lambda b, h: (0, h))
    state = pl.BlockSpec((SAMPLE_SEQS, None, DK, dv), lambda b, h: (b, h, 0, 0))
    if gla:
        wa, ba, gn = small
        in_specs = [tok(DK, COL_QB, 1), tok(DK, COL_KB, 1), tok(dv, COL_VB, 1), tok(dv, COL_RB, 1),
                    tok(LANES, 0, 0, tile_width=LANES),
                    pl.BlockSpec((LANES, DK), lambda b, h: (0, h)), vec(DK), vec(dv), state]
        args = (p16, p16, p16, p16, pa, wa, ba, gn, s0)
    else:
        lbl, gn = small
        in_specs = [tok(DK, COL_QA, 1), tok(DK, 0, 1), tok(dv, COL_IA, 1), tok(dv, COL_GA, 1),
                    pl.BlockSpec((2, DK), lambda b, h: (0, h)), vec(dv), state]
        args = (p16, pf, p16, p16, lbl, gn, s0)
    return pl.pallas_call(
        functools.partial(_sample_scan_kernel, gla=gla, steps=steps),
        grid=(batch // SAMPLE_SEQS, heads),
        in_specs=in_specs,
        out_specs=[pl.BlockSpec((rows, dv), lambda b, h: (b, h)), state],
        out_shape=[jax.ShapeDtypeStruct((batch * steps, heads * dv), BF16),
                   jax.ShapeDtypeStruct((batch, heads, DK, dv), F32)],
        compiler_params=_params("parallel", "parallel"),
        name=name,
    )(*args)


def kernel(x_prompt, x_sample, state_hgrn, state_gla, meta_tokens, lb_logits, ffn1_norm, w_ffn1_in,
           w_ffn1_out, mix_norm, w_in, w_alpha_up, b_alpha, gnorm_a, gnorm_b, w_out, ffn2_norm,
           w_ffn2_in, w_ffn2_out, final_norm):
    batch, seq, _ = x_prompt.shape
    dec_batch, steps, _ = x_sample.shape

    w_in_t = w_in[0].T
    wop = w_out[0].astype(BF16)
    wa = jnp.pad(w_alpha_up[0], ((0, LANES - ALPHA_RANK), (0, 0))).astype(BF16)
    n1, nm, n2, nf = ffn1_norm, mix_norm, ffn2_norm, final_norm[None]
    small_a = (lb_logits, gnorm_a)
    small_b = (wa, b_alpha, gnorm_b)
    n_sample = dec_batch * steps

    xs = jnp.concatenate([x_sample.reshape(n_sample, D_MODEL), meta_tokens], axis=0)
    h1s, *w1 = _ffn(xs, n1, (w_ffn1_in[0], w_ffn1_in[0], w_ffn1_out[0]), nf, tm=xs.shape[0],
                    tf=TF_CAST, apply_final_norm=False, emit_weights=True, name="ffn1_sample")
    ps = _in_proj(h1s, nm, w_in_t, tm=xs.shape[0], name="inproj_sample")
    oa_s, sa_s = _sample_scan(ps, state_hgrn[0], gla=False, batch=dec_batch, steps=steps,
                              small=small_a, name="scan_hgrn_sample")
    ob_s, sb_s = _sample_scan(ps, state_gla[0], gla=True, batch=dec_batch, steps=steps,
                              small=small_b, name="scan_gla_sample")
    h2s = _out_proj(oa_s, ob_s, h1s, wop, tm=TM, name="outproj_sample")
    y_s, *w2 = _ffn(h2s, n2, (w_ffn2_in[0], w_ffn2_in[0], w_ffn2_out[0]), nf, tm=n_sample,
                    tf=TF_CAST, apply_final_norm=True, emit_weights=True, name="ffn2_sample")

    h1p = _ffn(x_prompt.reshape(batch * seq, D_MODEL), n1, w1, nf, tm=TM, tf=TF,
               apply_final_norm=False, emit_weights=False, name="ffn1_prompt")
    pp = _in_proj(h1p, nm, w_in_t, tm=TM_PROJ, name="inproj_prompt")
    meta_block = n_sample // N_META
    oa_p, sa_p = _prompt_scan(pp, ps, gla=False, batch=batch, seq=seq, meta_block=meta_block,
                              small=small_a, name="scan_hgrn_prompt")
    ob_p, sb_p = _prompt_scan(pp, ps, gla=True, batch=batch, seq=seq, meta_block=meta_block,
                              small=small_b, name="scan_gla_prompt")
    h2p = _out_proj(oa_p, ob_p, h1p, wop, tm=TM, name="outproj_prompt")
    y_p = _ffn(h2p, n2, w2, nf, tm=TM, tf=TF, apply_final_norm=True, emit_weights=False,
               name="ffn2_prompt")

    return (y_p.reshape(batch, seq, D_MODEL), y_s.reshape(dec_batch, steps, D_MODEL),
            sa_p[None], sb_p[None], sa_s[None], sb_s[None])
```

```python
import functools

import jax
import jax.numpy as jnp
from jax import lax
from jax.experimental import pallas as pl
from jax.experimental.pallas import tpu as pltpu

F32 = jnp.float32
BF16 = jnp.bfloat16
HIGHEST = lax.Precision.HIGHEST

D_MODEL = 2048
N_META = 16
DK = 128
H_A, DV_A = 8, 128
H_B, DV_B = 4, 256
D_HALF = H_A * DV_A
ALPHA_RANK = 16
GATE_TEMP = 16.0
D_FF = 5632
CHUNK = 64
EPS = 1e-6
IN_PROJ_WIDTHS = (D_HALF, D_HALF, D_HALF, D_HALF, H_B * DK, H_B * DK, D_HALF, D_HALF, ALPHA_RANK)

LANES = 128
SUBLANES = 8
VMEM_LIMIT_BYTES = 56 * 1024 * 1024

COL_QA, COL_IA, COL_GA, COL_QB, COL_KB, COL_VB, COL_RB = 0, 8, 16, 24, 28, 32, 40
P16_COLS = 3 * D_HALF + 2 * H_B * DK + 2 * D_HALF

TM = 512
TF = 512
TF_CAST = 256
TN_PROJ = 512
TM_PROJ = 2048
N16_TILES = P16_COLS // TN_PROJ
DIAG_BLK = 16
SAMPLE_SEQS = 32
CHUNKS_PER_TRIP = 4
CHUNKS_PER_MXU_TRIP = 8

FAST_MIN_LOG_GATE = -60.0 / (DIAG_BLK - 1)
FAST_MAX_KEY = 1e10


def _rms(x, w):
    return x * lax.rsqrt(jnp.mean(x * x, axis=-1, keepdims=True) + EPS) * w


def _silu(x):
    return x * jax.nn.sigmoid(x)


def _params(*sem):
    return pltpu.CompilerParams(dimension_semantics=sem, vmem_limit_bytes=VMEM_LIMIT_BYTES)


def _ffn_kernel(h_ref, nw_ref, wg_ref, wu_ref, wo_ref, tnw_ref, out_ref, *rest,
                final_norm, emit_weights):
    u_ref = rest[-1]
    j = pl.program_id(1)

    @pl.when(j == 0)
    def _init():
        u_ref[...] = _rms(h_ref[...], nw_ref[...]).astype(BF16)
        out_ref[...] = jnp.zeros_like(out_ref)

    wg, wu, wo = (r[...].astype(BF16) for r in (wg_ref, wu_ref, wo_ref))
    if emit_weights:
        for dst, w in zip(rest[-4:-1], (wg, wu, wo)):
            dst[...] = w
    u = u_ref[...]
    g = jnp.dot(u, wg, preferred_element_type=F32)
    up = jnp.dot(u, wu, preferred_element_type=F32)
    a = (_silu(g) * up).astype(BF16)
    out_ref[...] += jnp.dot(a, wo, preferred_element_type=F32)

    @pl.when(j == pl.num_programs(1) - 1)
    def _finish():
        y = h_ref[...] + 0.5 * out_ref[...]
        normed = _rms(y, tnw_ref[...])
        if final_norm:
            out_ref[...] = normed
        else:
            out_ref[...] = y
            rest[0][...] = normed.astype(BF16)


def _ffn(h, norm_w, weights, tail_norm_w, *, tm, tf, final_norm, emit_weights, name):
    m = h.shape[0]
    nf = D_FF // tf
    rows = pl.BlockSpec((tm, D_MODEL), lambda i, j: (i, 0),
                        pipeline_mode=pl.Buffered(1) if m == tm else None)
    vec = pl.BlockSpec((1, D_MODEL), lambda i, j: (0, 0))
    col_tile = pl.BlockSpec((D_MODEL, tf), lambda i, j: (0, j))
    row_tile = pl.BlockSpec((tf, D_MODEL), lambda i, j: (j, 0))
    out_specs = [rows]
    out_shape = [jax.ShapeDtypeStruct((m, D_MODEL), F32)]
    if not final_norm:
        out_specs += [rows]
        out_shape += [jax.ShapeDtypeStruct((m, D_MODEL), BF16)]
    if emit_weights:
        assert m == tm, "weight casts are written once, by a single row tile"
        up_tile = pl.BlockSpec((D_MODEL, tf), lambda i, j: (0, j + nf))
        out_specs += [col_tile, col_tile, row_tile]
        out_shape += [jax.ShapeDtypeStruct((D_MODEL, D_FF), BF16)] * 2
        out_shape += [jax.ShapeDtypeStruct((D_FF, D_MODEL), BF16)]
    else:
        up_tile = col_tile
    return pl.pallas_call(
        functools.partial(_ffn_kernel, final_norm=final_norm, emit_weights=emit_weights),
        grid=(m // tm, nf),
        in_specs=[rows, vec, col_tile, up_tile, row_tile, vec],
        out_specs=out_specs,
        out_shape=out_shape,
        scratch_shapes=[pltpu.VMEM((tm, D_MODEL), BF16)],
        compiler_params=_params("parallel", "arbitrary"),
        name=name,
    )(h, norm_w, *weights, tail_norm_w)


def _proj_kernel(u_ref, w_ref, wal_ref, p16_ref, pf_ref, pa_ref):
    j = pl.program_id(1)
    nt = (((1,), (1,)), ((), ()))

    @pl.when(j == 0)
    def _low_rank():
        wal = jnp.concatenate([wal_ref[...].astype(BF16),
                               jnp.zeros((LANES - ALPHA_RANK, D_MODEL), BF16)], axis=0)
        pa_ref[...] = lax.dot_general(u_ref[...], wal, nt, preferred_element_type=F32)

    def project():
        return lax.dot_general(u_ref[...], w_ref[...].astype(BF16), nt,
                               preferred_element_type=F32)

    @pl.when(j < N16_TILES)
    def _narrow():
        p16_ref[...] = project().astype(BF16)

    @pl.when(j >= N16_TILES)
    def _wide():
        pf_ref[...] = project()


def _w_in_tile(j):
    fa_lo, fa_hi = D_HALF // TN_PROJ, 2 * D_HALF // TN_PROJ
    moved = fa_hi - fa_lo
    return jnp.where(j < fa_lo, j, jnp.where(j < N16_TILES, j + moved, j - N16_TILES + fa_lo))


def _in_proj(u, w, *, tm, name):
    m = u.shape[0]
    return pl.pallas_call(
        _proj_kernel,
        grid=(m // tm, (P16_COLS + D_HALF) // TN_PROJ),
        in_specs=[
            pl.BlockSpec((tm, D_MODEL), lambda i, j: (i, 0)),
            pl.BlockSpec((TN_PROJ, D_MODEL), lambda i, j: (_w_in_tile(j), 0)),
            pl.BlockSpec((ALPHA_RANK, D_MODEL), lambda i, j: ((P16_COLS + D_HALF) // ALPHA_RANK, 0)),
        ],
        out_specs=[
            pl.BlockSpec((None, tm, TN_PROJ), lambda i, j: (jnp.minimum(j, N16_TILES - 1), i, 0)),
            pl.BlockSpec((None, tm, TN_PROJ), lambda i, j: (jnp.maximum(j - N16_TILES, 0), i, 0)),
            pl.BlockSpec((None, tm, LANES), lambda i, j: (0, i, 0)),
        ],
        out_shape=[jax.ShapeDtypeStruct((N16_TILES, m, TN_PROJ), BF16),
                   jax.ShapeDtypeStruct((D_HALF // TN_PROJ, m, TN_PROJ), F32),
                   jax.ShapeDtypeStruct((1, m, LANES), F32)],
        compiler_params=_params("parallel", "arbitrary"),
        name=name,
    )(u, w, w)


def _out_proj_kernel(oa_ref, ob_ref, h_ref, wa_ref, wb_ref, out_ref):
    out_ref[...] = (h_ref[...]
                    + jnp.dot(oa_ref[...], wa_ref[...], preferred_element_type=F32)
                    + jnp.dot(ob_ref[...], wb_ref[...], preferred_element_type=F32))


def _out_proj(oa, ob, h, w, *, tm, name):
    m = oa.shape[0]
    return pl.pallas_call(
        _out_proj_kernel,
        grid=(m // tm,),
        in_specs=[
            pl.BlockSpec((tm, D_HALF), lambda i: (i, 0)),
            pl.BlockSpec((tm, D_HALF), lambda i: (i, 0)),
            pl.BlockSpec((tm, D_MODEL), lambda i: (i, 0)),
            pl.BlockSpec((D_HALF, D_MODEL), lambda i: (0, 0)),
            pl.BlockSpec((D_HALF, D_MODEL), lambda i: (1, 0)),
        ],
        out_specs=pl.BlockSpec((tm, D_MODEL), lambda i: (i, 0)),
        out_shape=jax.ShapeDtypeStruct((m, D_MODEL), F32),
        compiler_params=_params("parallel"),
        name=name,
    )(oa, ob, h, w, w)


def _tri(c, seq_len):
    r = lax.broadcasted_iota(jnp.int32, (c, c), 0)
    s = lax.broadcasted_iota(jnp.int32, (c, c), 1)
    return jnp.where((r >= s) & ((r ^ s) < seq_len), 1.0, 0.0).astype(F32)


def _off_mask(c, blk):
    nb = c // blk
    rows = nb * (nb - 1) // 2 * blk
    r = lax.broadcasted_iota(jnp.int32, (c, rows), 0) >> (blk.bit_length() - 1)
    s = lax.broadcasted_iota(jnp.int32, (c, rows), 1)
    seg = jnp.zeros((c, rows), jnp.int32)
    for i in range(1, nb):
        seg = seg + jnp.where(s >= blk * i * (i - 1) // 2, 1, 0)
    return jnp.where(r == seg, 1.0, 0.0).astype(F32)


def _diag(q, k, v, bc, blk):
    c = q.shape[0]
    row = lax.broadcasted_iota(jnp.int32, (blk, 1), 0)
    outs = []
    for i in range(c // blk):
        sl = slice(i * blk, (i + 1) * blk)
        qi, ki, vi, bi = q[sl], k[sl], v[sl].astype(F32), bc[sl]
        acc = jnp.zeros((blk, v.shape[1]), F32)
        for s in range(blk):
            e = jnp.exp(jnp.minimum(bi - bi[s:s + 1, :], 0.0))
            col = jnp.sum(qi * ki[s:s + 1, :] * e, axis=-1, keepdims=True)
            col = jnp.where(row >= s, col, 0.0)
            acc = acc + col * vi[s:s + 1, :]
        outs.append(acc)
    return outs[0] if len(outs) == 1 else jnp.concatenate(outs, axis=0)


def _offdiag(q, k, v, bc, blk, mask):
    c = q.shape[0]
    nb = c // blk
    qs, ks, vs = [jnp.zeros((blk, DK), F32)], [], []
    for i in range(1, nb):
        lo = i * blk
        ref = bc[lo:lo + 1, :]
        qs.append(q[lo:lo + blk] * jnp.exp(bc[lo:lo + blk] - ref))
        ks.append(k[:lo] * jnp.exp(ref - bc[:lo]))
        vs.append(v[:lo])
    qt = jnp.concatenate(qs, axis=0).astype(BF16)
    kt = jnp.concatenate(ks, axis=0).astype(BF16)
    vt = jnp.concatenate(vs, axis=0).astype(BF16)
    a = lax.dot_general(qt, kt, (((1,), (1,)), ((), ())), preferred_element_type=F32)
    return jnp.dot((a * mask).astype(BF16), vt, preferred_element_type=F32)


def _hgrn_inputs(qa, fa, ia, lb):
    fg = lb + (1.0 - lb) * jax.nn.sigmoid(fa)
    return _silu(qa.astype(F32)) * (DK ** -0.5), 1.0 - fg, ia, jnp.log(fg)


def _gla_inputs(qb, kb, vb, al, wa, ba):
    alpha = jnp.dot(al.astype(BF16), wa, preferred_element_type=F32) + ba
    logsig = jnp.minimum(alpha, 0.0) - jnp.log1p(jnp.exp(-jnp.abs(alpha)))
    return qb.astype(F32) * (DK ** -0.5), kb.astype(F32), vb, logsig / GATE_TEMP


def _head_out(o, gn, gate):
    return (_rms(o, gn) * _silu(gate.astype(F32))).astype(BF16)


def _lower_bound(lbl_ref):
    x = lbl_ref[...]
    e = jnp.exp(x - jnp.max(x, axis=0, keepdims=True))
    return e[0:1, :] / jnp.sum(e, axis=0, keepdims=True)


def _stack_mask(c, blk):
    nb = c // blk
    rows = blk * nb * (nb + 1) // 2
    t = lax.broadcasted_iota(jnp.int32, (c, rows), 0)
    r = lax.broadcasted_iota(jnp.int32, (c, rows), 1)
    seg = jnp.zeros((c, rows), jnp.int32)
    off = jnp.zeros((c, rows), jnp.int32)
    for i in range(1, nb):
        start = blk * i * (i + 1) // 2
        seg = seg + jnp.where(r >= start, 1, 0)
        off = jnp.where(r >= start, start, off)
    keep = ((t >> (blk.bit_length() - 1)) == seg) & (r - off <= t)
    return jnp.where(keep, 1.0, 0.0).astype(F32)


def _state_update_t(k, vb, bc, st):
    c = k.shape[0]
    blast = bc[c - 1:c, :]
    kdec = (k * jnp.exp(blast - bc)).astype(BF16)
    upd = lax.dot_general(vb, kdec, (((0,), (0,)), ((), ())), preferred_element_type=F32)
    if st is None:
        return upd
    return st * jnp.exp(blast) + upd


def _scan_sublanes(p):
    row = lax.broadcasted_iota(jnp.int32, (SUBLANES, DK), 0)
    for s in (1, 2, 4):
        p = p + jnp.where(row >= s, pltpu.roll(p, s, axis=0), 0.0)
    return p


def _cumsum_chunk(g):
    out, carry = [], None
    for j in range(CHUNK // SUBLANES):
        p = _scan_sublanes(g[j * SUBLANES:(j + 1) * SUBLANES])
        if carry is not None:
            p = p + carry
        carry = p[SUBLANES - 1:SUBLANES, :]
        out.append(p)
    return jnp.concatenate(out, axis=0)


def _rescaled_operands(q, k, bc):
    c, blk = CHUNK, DIAG_BLK
    qs, ks = [], []
    for i in range(c // blk):
        lo, hi = i * blk, (i + 1) * blk
        ref = bc[lo:lo + 1, :]
        qs.append(q[lo:hi] * jnp.exp(bc[lo:hi] - ref))
        ks.append(k[:hi] * jnp.exp(ref - bc[:hi]))
    return ((q * jnp.exp(bc)).astype(BF16), jnp.concatenate(qs, axis=0).astype(BF16),
            jnp.concatenate(ks, axis=0).astype(BF16),
            (k * jnp.exp(bc[c - 1:c, :] - bc)).astype(BF16))


def _chunk_exact(q, k, v, bc, st, mask):
    o = lax.dot_general((q * jnp.exp(bc)).astype(BF16), st.astype(BF16),
                        (((1,), (1,)), ((), ())), preferred_element_type=F32)
    o = o + _offdiag(q, k, v, bc, DIAG_BLK, mask) + _diag(q, k, v, bc, DIAG_BLK)
    return o, _state_update_t(k, v.astype(BF16), bc, st)


def _prompt_scan_kernel(*refs, gla, seq, meta):
    n_in = 11 if gla else 8
    if gla:
        q_ref, k_ref, v_ref, gate_ref, al_ref, mk_ref, mv_ref, mal_ref, wa_ref, ba_ref, gn_ref = (
            refs[:n_in])
    else:
        q_ref, f_ref, v_ref, gate_ref, mf_ref, mv_ref, lbl_ref, gn_ref = refs[:n_in]
        lb = _lower_bound(lbl_ref)
    o_ref, sout_ref = refs[n_in:n_in + 2]
    st_ref, qs_ref, ks_ref, gs_ref, qd_ref, qt_ref, kt_ref, kdec_ref, eb_ref = refs[n_in + 2:]
    trips = seq // (CHUNK * CHUNKS_PER_TRIP)
    stack = kt_ref.shape[0] // (seq // CHUNK)
    nt = (((1,), (1,)), ((), ()))
    tn = (((0,), (0,)), ((), ()))

    def chunk_rows(ci, u, n=CHUNK):
        return pl.ds(pl.multiple_of((ci * CHUNKS_PER_TRIP + u) * n, n), n)

    if gla:
        _, mk, mv, mg = _gla_inputs(mk_ref[...], mk_ref[...], mv_ref[...], mal_ref[...],
                                    wa_ref[...], ba_ref[...])
    else:
        _, mk, mv, mg = _hgrn_inputs(mf_ref[...], mf_ref[...], mv_ref[...], lb)
    mbc = jnp.dot(_tri(meta, meta), mg, precision=HIGHEST, preferred_element_type=F32)
    st_ref[...] = _state_update_t(mk, mv.astype(BF16), mbc, None)

    def gate_pass(ci, carry):
        gmin, kmax = carry
        trip_rows = CHUNK * CHUNKS_PER_TRIP
        rows = pl.ds(pl.multiple_of(ci * trip_rows, trip_rows), trip_rows)
        if gla:
            q, k, _, g = _gla_inputs(q_ref[rows, :], k_ref[rows, :], None, al_ref[rows, :],
                                     wa_ref[...], ba_ref[...])
        else:
            q, k, _, g = _hgrn_inputs(q_ref[rows, :], f_ref[rows, :], None, lb)
        qs_ref[rows, :] = q
        ks_ref[rows, :] = k
        gs_ref[rows, :] = g
        return (jnp.minimum(gmin, jnp.min(g, axis=0, keepdims=True)),
                jnp.maximum(kmax, jnp.max(jnp.abs(k), axis=0, keepdims=True)))

    gmin, kmax = lax.fori_loop(0, trips, gate_pass,
                               (jnp.zeros((1, DK), F32), jnp.zeros((1, DK), F32)))
    in_range = (jnp.min(gmin) >= FAST_MIN_LOG_GATE) & (jnp.max(kmax) <= FAST_MAX_KEY)

    def rows_of(c, n=CHUNK):
        return pl.ds(pl.multiple_of(c * n, n), n)

    @pl.when(in_range)
    def _rescaled():
        mask = _stack_mask(CHUNK, DIAG_BLK)
        units = range(CHUNKS_PER_MXU_TRIP)
        mxu_trips = seq // (CHUNK * CHUNKS_PER_MXU_TRIP)

        def prepare(ti):
            for u in units:
                c = ti * CHUNKS_PER_MXU_TRIP + u
                bc = _cumsum_chunk(gs_ref[rows_of(c), :])
                qd, qt, kt, kdec = _rescaled_operands(qs_ref[rows_of(c), :], ks_ref[rows_of(c), :],
                                                      bc)
                qd_ref[rows_of(c), :] = qd
                qt_ref[rows_of(c), :] = qt
                kt_ref[rows_of(c, stack), :] = kt
                kdec_ref[rows_of(c), :] = kdec
                eb_ref[c] = jnp.broadcast_to(jnp.exp(bc[CHUNK - 1:CHUNK, :]), (SUBLANES, DK))

        def contract(ti):
            cs = [ti * CHUNKS_PER_MXU_TRIP + u for u in units]
            vb = [v_ref[rows_of(c), :].astype(BF16) for c in cs]
            a = [lax.dot_general(qt_ref[rows_of(c), :], kt_ref[rows_of(c, stack), :], nt,
                                 preferred_element_type=F32) for c in cs]
            am = [(x * mask).astype(BF16) for x in a]
            vt = [jnp.concatenate([x[:(i + 1) * DIAG_BLK] for i in range(CHUNK // DIAG_BLK)], axis=0)
                  for x in vb]
            oi = [jnp.dot(am[u], vt[u], preferred_element_type=F32) for u in units]
            inc = [lax.dot_general(vb[u], kdec_ref[rows_of(cs[u]), :], tn,
                                   preferred_element_type=F32) for u in units]
            sts = [st_ref[...]]
            for u in units:
                sts.append(sts[-1] * eb_ref[cs[u]][0:1, :] + inc[u])
            o = [oi[u] + lax.dot_general(qd_ref[rows_of(cs[u]), :], sts[u].astype(BF16), nt,
                                         preferred_element_type=F32) for u in units]
            for u in units:
                o_ref[rows_of(cs[u]), :] = _head_out(o[u], gn_ref[...], gate_ref[rows_of(cs[u]), :])
            st_ref[...] = sts[-1]

        prepare(0)

        def body(ti, carry):
            contract(ti)
            prepare(ti + 1)
            return carry

        lax.fori_loop(0, mxu_trips - 1, body, 0)
        contract(mxu_trips - 1)

    @pl.when(jnp.logical_not(in_range))
    def _exact():
        mask = _off_mask(CHUNK, DIAG_BLK)

        def body(ci, carry):
            st = st_ref[...]
            for u in range(CHUNKS_PER_TRIP):
                rows = chunk_rows(ci, u)
                o, st = _chunk_exact(qs_ref[rows, :], ks_ref[rows, :], v_ref[rows, :],
                                     _cumsum_chunk(gs_ref[rows, :]), st, mask)
                o_ref[rows, :] = _head_out(o, gn_ref[...], gate_ref[rows, :])
            st_ref[...] = st
            return carry

        lax.fori_loop(0, trips, body, 0)

    sout_ref[...] = st_ref[...].T


def _sample_scan_kernel(*refs, gla, steps):
    if gla:
        (q_ref, k_ref, v_ref, gate_ref, al_ref, wa_ref, ba_ref, gn_ref, s0_ref,
         o_ref, sout_ref) = refs
        q, k, v, g = _gla_inputs(q_ref[...], k_ref[...], v_ref[...], al_ref[...],
                                 wa_ref[...], ba_ref[...])
    else:
        (q_ref, f_ref, v_ref, gate_ref, lbl_ref, gn_ref, s0_ref, o_ref, sout_ref) = refs
        q, k, v, g = _hgrn_inputs(q_ref[...], f_ref[...], v_ref[...], _lower_bound(lbl_ref))
    v32 = v.astype(F32)
    rows = SAMPLE_SEQS * steps
    groups = [slice(b * steps, (b + 1) * steps) for b in range(SAMPLE_SEQS)]
    tn = (((0,), (0,)), ((), ()))

    bc = jnp.concatenate([_scan_sublanes(g[sl]) for sl in groups], axis=0)

    def repeated(r):
        return jnp.concatenate([jnp.broadcast_to(bc[sl][r:r + 1, :], (steps, DK)) for sl in groups],
                               axis=0)

    qd = (q * jnp.exp(bc)).astype(BF16)
    kdec = (k * jnp.exp(repeated(steps - 1) - bc)).astype(BF16)
    r = lax.broadcasted_iota(jnp.int32, (rows, SAMPLE_SEQS), 0)
    c = lax.broadcasted_iota(jnp.int32, (rows, SAMPLE_SEQS), 1)
    selector = jnp.where((r >> (steps.bit_length() - 1)) == c, 1.0, 0.0).astype(F32)
    decay = jnp.exp(lax.dot_general(g, selector, tn, precision=HIGHEST,
                                    preferred_element_type=F32))

    inter = [jnp.dot(qd[sl], s0_ref[b].astype(BF16), preferred_element_type=F32)
             for b, sl in enumerate(groups)]
    upd = [lax.dot_general(kdec[sl], v32[sl].astype(BF16), tn, preferred_element_type=F32)
           for sl in groups]
    for b in range(SAMPLE_SEQS):
        sout_ref[b] = decay[:, b:b + 1] * s0_ref[b] + upd[b]
    o = jnp.concatenate(inter, axis=0) + _diag(q, k, v32, bc, steps)
    o_ref[...] = _head_out(o, gn_ref[...], gate_ref[...])


def _proj_spec(rows, width, c0, per_head, row_block, tile_width=TN_PROJ):
    per_tile = tile_width // width

    def index(b, h):
        g = c0 * LANES // width + h * per_head
        return (g // per_tile, row_block(b), g % per_tile)

    return pl.BlockSpec((None, rows, width), index)


def _prompt_scan(p, pm, *, gla, batch, seq, meta_block, small, name):
    heads, dv = (H_B, DV_B) if gla else (H_A, DV_A)
    n_chunks = seq // CHUNK
    blocks = CHUNK // DIAG_BLK
    stack = DIAG_BLK * blocks * (blocks + 1) // 2
    p16, pf, pa = p
    pm16, pmf, pma = pm
    tok = functools.partial(_proj_spec, seq, row_block=lambda b: b)
    met = functools.partial(_proj_spec, N_META, row_block=lambda b: meta_block)
    vec = lambda width: pl.BlockSpec((1, width), lambda b, h: (0, h))
    if gla:
        wa, ba, gn = small
        in_specs = [tok(DK, COL_QB, 1), tok(DK, COL_KB, 1), tok(dv, COL_VB, 1), tok(dv, COL_RB, 1),
                    tok(LANES, 0, 0, tile_width=LANES),
                    met(DK, COL_KB, 1), met(dv, COL_VB, 1), met(LANES, 0, 0, tile_width=LANES),
                    pl.BlockSpec((LANES, DK), lambda b, h: (0, h)), vec(DK), vec(dv)]
        args = (p16, p16, p16, p16, pa, pm16, pm16, pma, wa, ba, gn)
    else:
        lbl, gn = small
        in_specs = [tok(DK, COL_QA, 1), tok(DK, 0, 1), tok(dv, COL_IA, 1), tok(dv, COL_GA, 1),
                    met(DK, 0, 1), met(dv, COL_IA, 1),
                    pl.BlockSpec((2, DK), lambda b, h: (0, h)), vec(dv)]
        args = (p16, pf, p16, p16, pmf, pm16, lbl, gn)
    return pl.pallas_call(
        functools.partial(_prompt_scan_kernel, gla=gla, seq=seq, meta=N_META),
        grid=(batch, heads),
        in_specs=in_specs,
        out_specs=[pl.BlockSpec((seq, dv), lambda b, h: (b, h)),
                   pl.BlockSpec((None, None, DK, dv), lambda b, h: (b, h, 0, 0))],
        out_shape=[jax.ShapeDtypeStruct((batch * seq, heads * dv), BF16),
                   jax.ShapeDtypeStruct((batch, heads, DK, dv), F32)],
        scratch_shapes=[
            pltpu.VMEM((dv, DK), F32),
            pltpu.VMEM((seq, DK), F32),
            pltpu.VMEM((seq, DK), F32),
            pltpu.VMEM((seq, DK), F32),
            pltpu.VMEM((seq, DK), BF16),
            pltpu.VMEM((seq, DK), BF16),
            pltpu.VMEM((n_chunks * stack, DK), BF16),
            pltpu.VMEM((seq, DK), BF16),
            pltpu.VMEM((n_chunks, SUBLANES, DK), F32),
        ],
        compiler_params=_params("parallel", "parallel"),
        name=name,
    )(*args)


def _sample_scan(p, s0, *, gla, batch, steps, small, name):
    assert steps == SUBLANES, "the sample recurrence keeps one sequence per sublane group"
    heads, dv = (H_B, DV_B) if gla else (H_A, DV_A)
    rows = SAMPLE_SEQS * steps
    p16, pf, pa = p
    tok = functools.partial(_proj_spec, rows, row_block=lambda b: b)
    vec = lambda width: pl.BlockSpec((1, width), lambda b, h: (0, h))
    state = pl.BlockSpec((SAMPLE_SEQS, None, DK, dv), lambda b, h: (b, h, 0, 0))
    if gla:
        wa, ba, gn = small
        in_specs = [tok(DK, COL_QB, 1), tok(DK, COL_KB, 1), tok(dv, COL_VB, 1), tok(dv, COL_RB, 1),
                    tok(LANES, 0, 0, tile_width=LANES),
                    pl.BlockSpec((LANES, DK), lambda b, h: (0, h)), vec(DK), vec(dv), state]
        args = (p16, p16, p16, p16, pa, wa, ba, gn, s0)
    else:
        lbl, gn = small
        in_specs = [tok(DK, COL_QA, 1), tok(DK, 0, 1), tok(dv, COL_IA, 1), tok(dv, COL_GA, 1),
                    pl.BlockSpec((2, DK), lambda b, h: (0, h)), vec(dv), state]
        args = (p16, pf, p16, p16, lbl, gn, s0)
    return pl.pallas_call(
        functools.partial(_sample_scan_kernel, gla=gla, steps=steps),
        grid=(batch // SAMPLE_SEQS, heads),
        in_specs=in_specs,
        out_specs=[pl.BlockSpec((rows, dv), lambda b, h: (b, h)), state],
        out_shape=[jax.ShapeDtypeStruct((batch * steps, heads * dv), BF16),
                   jax.ShapeDtypeStruct((batch, heads, DK, dv), F32)],
        compiler_params=_params("parallel", "parallel"),
        name=name,
    )(*args)


def kernel(x_prompt, x_sample, state_hgrn, state_gla, meta_tokens, lb_logits, ffn1_norm, w_ffn1_in,
           w_ffn1_out, mix_norm, w_in, w_alpha_up, b_alpha, gnorm_a, gnorm_b, w_out, ffn2_norm,
           w_ffn2_in, w_ffn2_out, final_norm):
    batch, seq, _ = x_prompt.shape
    dec_batch, steps, _ = x_sample.shape

    w_in_t = w_in[0].T
    wop = w_out[0].astype(BF16)
    wa = jnp.pad(w_alpha_up[0], ((0, LANES - ALPHA_RANK), (0, 0))).astype(BF16)
    n1, nm, n2, nf = ffn1_norm, mix_norm, ffn2_norm, final_norm[None]
    small_a = (lb_logits, gnorm_a)
    small_b = (wa, b_alpha, gnorm_b)
    n_sample = dec_batch * steps

    xs = jnp.concatenate([x_sample.reshape(n_sample, D_MODEL), meta_tokens], axis=0)
    h1s, u1s, *w1 = _ffn(xs, n1, (w_ffn1_in[0], w_ffn1_in[0], w_ffn1_out[0]), nm, tm=xs.shape[0],
                         tf=TF_CAST, final_norm=False, emit_weights=True, name="ffn1_sample")
    ps = _in_proj(u1s, w_in_t, tm=xs.shape[0], name="inproj_sample")
    oa_s, sa_s = _sample_scan(ps, state_hgrn[0], gla=False, batch=dec_batch, steps=steps,
                              small=small_a, name="scan_hgrn_sample")
    ob_s, sb_s = _sample_scan(ps, state_gla[0], gla=True, batch=dec_batch, steps=steps,
                              small=small_b, name="scan_gla_sample")
    h2s = _out_proj(oa_s, ob_s, h1s, wop, tm=TM, name="outproj_sample")
    y_s, *w2 = _ffn(h2s, n2, (w_ffn2_in[0], w_ffn2_in[0], w_ffn2_out[0]), nf, tm=n_sample,
                    tf=TF_CAST, final_norm=True, emit_weights=True, name="ffn2_sample")

    h1p, u1p = _ffn(x_prompt.reshape(batch * seq, D_MODEL), n1, w1, nm, tm=TM, tf=TF,
                    final_norm=False, emit_weights=False, name="ffn1_prompt")
    pp = _in_proj(u1p, w_in_t, tm=TM_PROJ, name="inproj_prompt")
    meta_block = n_sample // N_META
    oa_p, sa_p = _prompt_scan(pp, ps, gla=False, batch=batch, seq=seq, meta_block=meta_block,
                              small=small_a, name="scan_hgrn_prompt")
    ob_p, sb_p = _prompt_scan(pp, ps, gla=True, batch=batch, seq=seq, meta_block=meta_block,
                              small=small_b, name="scan_gla_prompt")
    h2p = _out_proj(oa_p, ob_p, h1p, wop, tm=TM, name="outproj_prompt")
    (y_p,) = _ffn(h2p, n2, w2, nf, tm=TM, tf=TF, final_norm=True, emit_weights=False,
                  name="ffn2_prompt")

    return (y_p.reshape(batch, seq, D_MODEL), y_s.reshape(dec_batch, steps, D_MODEL),
            sa_p[None], sb_p[None], sa_s[None], sb_s[None])
```

```python
import functools

import jax
import jax.numpy as jnp
from jax import lax
from jax.experimental import pallas as pl
from jax.experimental.pallas import tpu as pltpu

F32 = jnp.float32
BF16 = jnp.bfloat16
HIGHEST = lax.Precision.HIGHEST

D_MODEL = 2048
N_META = 16
DK = 128
H_A, DV_A = 8, 128
H_B, DV_B = 4, 256
D_HALF = H_A * DV_A
ALPHA_RANK = 16
GATE_TEMP = 16.0
D_FF = 5632
CHUNK = 64
EPS = 1e-6
IN_PROJ_WIDTHS = (D_HALF, D_HALF, D_HALF, D_HALF, H_B * DK, H_B * DK, D_HALF, D_HALF, ALPHA_RANK)

LANES = 128
SUBLANES = 8
VMEM_LIMIT_BYTES = 56 * 1024 * 1024

COL_QA, COL_IA, COL_GA, COL_QB, COL_KB, COL_VB, COL_RB = 0, 8, 16, 24, 28, 32, 40
P16_COLS = 3 * D_HALF + 2 * H_B * DK + 2 * D_HALF

TM = 512
TF = 512
TF_CAST = 256
TN_PROJ = 512
TM_PROJ = 2048
N16_TILES = P16_COLS // TN_PROJ
DIAG_BLK = 16
SAMPLE_SEQS = 32
CHUNKS_PER_TRIP = 4
CHUNKS_PER_MXU_TRIP = 8

FAST_MIN_LOG_GATE = -60.0 / (DIAG_BLK - 1)
FAST_MAX_KEY = 1e10


def _rms(x, w):
    return x * lax.rsqrt(jnp.mean(x * x, axis=-1, keepdims=True) + EPS) * w


def _silu(x):
    return x * jax.nn.sigmoid(x)


def _params(*sem):
    return pltpu.CompilerParams(dimension_semantics=sem, vmem_limit_bytes=VMEM_LIMIT_BYTES)


def _ffn_kernel(h_ref, nw_ref, wg_ref, wu_ref, wo_ref, tnw_ref, out_ref, *rest,
                final_norm, emit_weights):
    u_ref = rest[-1]
    j = pl.program_id(1)
    last_j = pl.num_programs(1) - 1

    def step(first, last):
        if first:
            u = _rms(h_ref[...], nw_ref[...]).astype(BF16)
            u_ref[...] = u
        else:
            u = u_ref[...]
        wg, wu, wo = (r[...].astype(BF16) for r in (wg_ref, wu_ref, wo_ref))
        if emit_weights:
            for dst, w in zip(rest[-4:-1], (wg, wu, wo)):
                dst[...] = w
        g = jnp.dot(u, wg, preferred_element_type=F32)
        up = jnp.dot(u, wu, preferred_element_type=F32)
        a = (_silu(g) * up).astype(BF16)
        acc = jnp.dot(a, wo, preferred_element_type=F32)
        if not first:
            acc = out_ref[...] + acc
        if not last:
            out_ref[...] = acc
            return
        y = h_ref[...] + 0.5 * acc
        normed = _rms(y, tnw_ref[...])
        if final_norm:
            out_ref[...] = normed
        else:
            out_ref[...] = y
            rest[0][...] = normed.astype(BF16)

    pl.when(j == 0)(functools.partial(step, True, False))
    pl.when((j > 0) & (j < last_j))(functools.partial(step, False, False))
    pl.when(j == last_j)(functools.partial(step, False, True))


def _ffn(h, norm_w, weights, tail_norm_w, *, tm, tf, final_norm, emit_weights, name):
    m = h.shape[0]
    nf = D_FF // tf
    rows = pl.BlockSpec((tm, D_MODEL), lambda i, j: (i, 0),
                        pipeline_mode=pl.Buffered(1) if m == tm else None)
    vec = pl.BlockSpec((1, D_MODEL), lambda i, j: (0, 0))
    col_tile = pl.BlockSpec((D_MODEL, tf), lambda i, j: (0, j))
    row_tile = pl.BlockSpec((tf, D_MODEL), lambda i, j: (j, 0))
    out_specs = [rows]
    out_shape = [jax.ShapeDtypeStruct((m, D_MODEL), F32)]
    if not final_norm:
        out_specs += [rows]
        out_shape += [jax.ShapeDtypeStruct((m, D_MODEL), BF16)]
    if emit_weights:
        assert m == tm, "weight casts are written once, by a single row tile"
        up_tile = pl.BlockSpec((D_MODEL, tf), lambda i, j: (0, j + nf))
        out_specs += [col_tile, col_tile, row_tile]
        out_shape += [jax.ShapeDtypeStruct((D_MODEL, D_FF), BF16)] * 2
        out_shape += [jax.ShapeDtypeStruct((D_FF, D_MODEL), BF16)]
    else:
        up_tile = col_tile
    return pl.pallas_call(
        functools.partial(_ffn_kernel, final_norm=final_norm, emit_weights=emit_weights),
        grid=(m // tm, nf),
        in_specs=[rows, vec, col_tile, up_tile, row_tile, vec],
        out_specs=out_specs,
        out_shape=out_shape,
        scratch_shapes=[pltpu.VMEM((tm, D_MODEL), BF16)],
        compiler_params=_params("parallel", "arbitrary"),
        name=name,
    )(h, norm_w, *weights, tail_norm_w)


def _proj_kernel(u_ref, w_ref, wal_ref, p16_ref, pf_ref, pa_ref):
    j = pl.program_id(1)
    nt = (((1,), (1,)), ((), ()))

    @pl.when(j == 0)
    def _low_rank():
        wal = jnp.concatenate([wal_ref[...].astype(BF16),
                               jnp.zeros((LANES - ALPHA_RANK, D_MODEL), BF16)], axis=0)
        pa_ref[...] = lax.dot_general(u_ref[...], wal, nt, preferred_element_type=F32)

    def project():
        return lax.dot_general(u_ref[...], w_ref[...].astype(BF16), nt,
                               preferred_element_type=F32)

    @pl.when(j < N16_TILES)
    def _narrow():
        p16_ref[...] = project().astype(BF16)

    @pl.when(j >= N16_TILES)
    def _wide():
        pf_ref[...] = project()


def _w_in_tile(j):
    fa_lo, fa_hi = D_HALF // TN_PROJ, 2 * D_HALF // TN_PROJ
    moved = fa_hi - fa_lo
    return jnp.where(j < fa_lo, j, jnp.where(j < N16_TILES, j + moved, j - N16_TILES + fa_lo))


def _in_proj(u, w, *, tm, name):
    m = u.shape[0]
    return pl.pallas_call(
        _proj_kernel,
        grid=(m // tm, (P16_COLS + D_HALF) // TN_PROJ),
        in_specs=[
            pl.BlockSpec((tm, D_MODEL), lambda i, j: (i, 0)),
            pl.BlockSpec((TN_PROJ, D_MODEL), lambda i, j: (_w_in_tile(j), 0)),
            pl.BlockSpec((ALPHA_RANK, D_MODEL), lambda i, j: ((P16_COLS + D_HALF) // ALPHA_RANK, 0)),
        ],
        out_specs=[
            pl.BlockSpec((None, tm, TN_PROJ), lambda i, j: (jnp.minimum(j, N16_TILES - 1), i, 0)),
            pl.BlockSpec((None, tm, TN_PROJ), lambda i, j: (jnp.maximum(j - N16_TILES, 0), i, 0)),
            pl.BlockSpec((None, tm, LANES), lambda i, j: (0, i, 0)),
        ],
        out_shape=[jax.ShapeDtypeStruct((N16_TILES, m, TN_PROJ), BF16),
                   jax.ShapeDtypeStruct((D_HALF // TN_PROJ, m, TN_PROJ), F32),
                   jax.ShapeDtypeStruct((1, m, LANES), F32)],
        compiler_params=_params("parallel", "arbitrary"),
        name=name,
    )(u, w, w)


def _out_proj_kernel(oa_ref, ob_ref, h_ref, wa_ref, wb_ref, out_ref):
    out_ref[...] = (h_ref[...]
                    + jnp.dot(oa_ref[...], wa_ref[...], preferred_element_type=F32)
                    + jnp.dot(ob_ref[...], wb_ref[...], preferred_element_type=F32))


def _out_proj(oa, ob, h, w, *, tm, name):
    m = oa.shape[0]
    return pl.pallas_call(
        _out_proj_kernel,
        grid=(m // tm,),
        in_specs=[
            pl.BlockSpec((tm, D_HALF), lambda i: (i, 0)),
            pl.BlockSpec((tm, D_HALF), lambda i: (i, 0)),
            pl.BlockSpec((tm, D_MODEL), lambda i: (i, 0)),
            pl.BlockSpec((D_HALF, D_MODEL), lambda i: (0, 0)),
            pl.BlockSpec((D_HALF, D_MODEL), lambda i: (1, 0)),
        ],
        out_specs=pl.BlockSpec((tm, D_MODEL), lambda i: (i, 0)),
        out_shape=jax.ShapeDtypeStruct((m, D_MODEL), F32),
        compiler_params=_params("parallel"),
        name=name,
    )(oa, ob, h, w, w)


def _tri(c, seq_len):
    r = lax.broadcasted_iota(jnp.int32, (c, c), 0)
    s = lax.broadcasted_iota(jnp.int32, (c, c), 1)
    return jnp.where((r >= s) & ((r ^ s) < seq_len), 1.0, 0.0).astype(F32)


def _off_mask(c, blk):
    nb = c // blk
    rows = nb * (nb - 1) // 2 * blk
    r = lax.broadcasted_iota(jnp.int32, (c, rows), 0) >> (blk.bit_length() - 1)
    s = lax.broadcasted_iota(jnp.int32, (c, rows), 1)
    seg = jnp.zeros((c, rows), jnp.int32)
    for i in range(1, nb):
        seg = seg + jnp.where(s >= blk * i * (i - 1) // 2, 1, 0)
    return jnp.where(r == seg, 1.0, 0.0).astype(F32)


def _diag(q, k, v, bc, blk):
    c = q.shape[0]
    row = lax.broadcasted_iota(jnp.int32, (blk, 1), 0)
    outs = []
    for i in range(c // blk):
        sl = slice(i * blk, (i + 1) * blk)
        qi, ki, vi, bi = q[sl], k[sl], v[sl].astype(F32), bc[sl]
        acc = jnp.zeros((blk, v.shape[1]), F32)
        for s in range(blk):
            e = jnp.exp(jnp.minimum(bi - bi[s:s + 1, :], 0.0))
            col = jnp.sum(qi * ki[s:s + 1, :] * e, axis=-1, keepdims=True)
            col = jnp.where(row >= s, col, 0.0)
            acc = acc + col * vi[s:s + 1, :]
        outs.append(acc)
    return outs[0] if len(outs) == 1 else jnp.concatenate(outs, axis=0)


def _offdiag(q, k, v, bc, blk, mask):
    c = q.shape[0]
    nb = c // blk
    qs, ks, vs = [jnp.zeros((blk, DK), F32)], [], []
    for i in range(1, nb):
        lo = i * blk
        ref = bc[lo:lo + 1, :]
        qs.append(q[lo:lo + blk] * jnp.exp(bc[lo:lo + blk] - ref))
        ks.append(k[:lo] * jnp.exp(ref - bc[:lo]))
        vs.append(v[:lo])
    qt = jnp.concatenate(qs, axis=0).astype(BF16)
    kt = jnp.concatenate(ks, axis=0).astype(BF16)
    vt = jnp.concatenate(vs, axis=0).astype(BF16)
    a = lax.dot_general(qt, kt, (((1,), (1,)), ((), ())), preferred_element_type=F32)
    return jnp.dot((a * mask).astype(BF16), vt, preferred_element_type=F32)


def _hgrn_inputs(qa, fa, ia, lb):
    fg = lb + (1.0 - lb) * jax.nn.sigmoid(fa)
    return _silu(qa.astype(F32)) * (DK ** -0.5), 1.0 - fg, ia, jnp.log(fg)


def _gla_inputs(qb, kb, vb, al, wa, ba):
    alpha = jnp.dot(al.astype(BF16), wa, preferred_element_type=F32) + ba
    logsig = jnp.minimum(alpha, 0.0) - jnp.log1p(jnp.exp(-jnp.abs(alpha)))
    return qb.astype(F32) * (DK ** -0.5), kb.astype(F32), vb, logsig / GATE_TEMP


def _head_out(o, gn, gate):
    return (_rms(o, gn) * _silu(gate.astype(F32))).astype(BF16)


def _lower_bound(lbl_ref):
    x = lbl_ref[...]
    e = jnp.exp(x - jnp.max(x, axis=0, keepdims=True))
    return e[0:1, :] / jnp.sum(e, axis=0, keepdims=True)


def _stack_mask(c, blk):
    nb = c // blk
    rows = blk * nb * (nb + 1) // 2
    t = lax.broadcasted_iota(jnp.int32, (c, rows), 0)
    r = lax.broadcasted_iota(jnp.int32, (c, rows), 1)
    seg = jnp.zeros((c, rows), jnp.int32)
    off = jnp.zeros((c, rows), jnp.int32)
    for i in range(1, nb):
        start = blk * i * (i + 1) // 2
        seg = seg + jnp.where(r >= start, 1, 0)
        off = jnp.where(r >= start, start, off)
    keep = ((t >> (blk.bit_length() - 1)) == seg) & (r - off <= t)
    return jnp.where(keep, 1.0, 0.0).astype(F32)


def _state_update_t(k, vb, bc, st):
    c = k.shape[0]
    blast = bc[c - 1:c, :]
    kdec = (k * jnp.exp(blast - bc)).astype(BF16)
    upd = lax.dot_general(vb, kdec, (((0,), (0,)), ((), ())), preferred_element_type=F32)
    if st is None:
        return upd
    return st * jnp.exp(blast) + upd


def _scan_sublanes(p):
    row = lax.broadcasted_iota(jnp.int32, (SUBLANES, DK), 0)
    for s in (1, 2, 4):
        p = p + jnp.where(row >= s, pltpu.roll(p, s, axis=0), 0.0)
    return p


def _cumsum_chunk(g):
    out, carry = [], None
    for j in range(CHUNK // SUBLANES):
        p = _scan_sublanes(g[j * SUBLANES:(j + 1) * SUBLANES])
        if carry is not None:
            p = p + carry
        carry = p[SUBLANES - 1:SUBLANES, :]
        out.append(p)
    return jnp.concatenate(out, axis=0)


def _rescaled_operands(q, k, bc):
    c, blk = CHUNK, DIAG_BLK
    qs, ks = [], []
    for i in range(c // blk):
        lo, hi = i * blk, (i + 1) * blk
        ref = bc[lo:lo + 1, :]
        qs.append(q[lo:hi] * jnp.exp(bc[lo:hi] - ref))
        ks.append(k[:hi] * jnp.exp(ref - bc[:hi]))
    return ((q * jnp.exp(bc)).astype(BF16), jnp.concatenate(qs, axis=0).astype(BF16),
            jnp.concatenate(ks, axis=0).astype(BF16),
            (k * jnp.exp(bc[c - 1:c, :] - bc)).astype(BF16))


def _chunk_exact(q, k, v, bc, st, mask):
    o = lax.dot_general((q * jnp.exp(bc)).astype(BF16), st.astype(BF16),
                        (((1,), (1,)), ((), ())), preferred_element_type=F32)
    o = o + _offdiag(q, k, v, bc, DIAG_BLK, mask) + _diag(q, k, v, bc, DIAG_BLK)
    return o, _state_update_t(k, v.astype(BF16), bc, st)


def _prompt_scan_kernel(*refs, gla, seq, meta):
    n_in = 11 if gla else 8
    if gla:
        q_ref, k_ref, v_ref, gate_ref, al_ref, mk_ref, mv_ref, mal_ref, wa_ref, ba_ref, gn_ref = (
            refs[:n_in])
    else:
        q_ref, f_ref, v_ref, gate_ref, mf_ref, mv_ref, lbl_ref, gn_ref = refs[:n_in]
        lb = _lower_bound(lbl_ref)
    o_ref, sout_ref = refs[n_in:n_in + 2]
    st_ref, qs_ref, ks_ref, gs_ref, qd_ref, qt_ref, kt_ref, kdec_ref, eb_ref = refs[n_in + 2:]
    trips = seq // (CHUNK * CHUNKS_PER_TRIP)
    stack = kt_ref.shape[0] // (seq // CHUNK)
    nt = (((1,), (1,)), ((), ()))
    tn = (((0,), (0,)), ((), ()))

    def chunk_rows(ci, u, n=CHUNK):
        return pl.ds(pl.multiple_of((ci * CHUNKS_PER_TRIP + u) * n, n), n)

    if gla:
        _, mk, mv, mg = _gla_inputs(mk_ref[...], mk_ref[...], mv_ref[...], mal_ref[...],
                                    wa_ref[...], ba_ref[...])
    else:
        _, mk, mv, mg = _hgrn_inputs(mf_ref[...], mf_ref[...], mv_ref[...], lb)
    mbc = jnp.dot(_tri(meta, meta), mg, precision=HIGHEST, preferred_element_type=F32)
    st_ref[...] = _state_update_t(mk, mv.astype(BF16), mbc, None)

    def gate_pass(ci, carry):
        gmin, kmax = carry
        trip_rows = CHUNK * CHUNKS_PER_TRIP
        rows = pl.ds(pl.multiple_of(ci * trip_rows, trip_rows), trip_rows)
        if gla:
            q, k, _, g = _gla_inputs(q_ref[rows, :], k_ref[rows, :], None, al_ref[rows, :],
                                     wa_ref[...], ba_ref[...])
        else:
            q, k, _, g = _hgrn_inputs(q_ref[rows, :], f_ref[rows, :], None, lb)
        qs_ref[rows, :] = q
        ks_ref[rows, :] = k
        gs_ref[rows, :] = g
        return (jnp.minimum(gmin, jnp.min(g, axis=0, keepdims=True)),
                jnp.maximum(kmax, jnp.max(jnp.abs(k), axis=0, keepdims=True)))

    gmin, kmax = lax.fori_loop(0, trips, gate_pass,
                               (jnp.zeros((1, DK), F32), jnp.zeros((1, DK), F32)))
    in_range = (jnp.min(gmin) >= FAST_MIN_LOG_GATE) & (jnp.max(kmax) <= FAST_MAX_KEY)

    def rows_of(c, n=CHUNK):
        return pl.ds(pl.multiple_of(c * n, n), n)

    @pl.when(in_range)
    def _rescaled():
        mask = _stack_mask(CHUNK, DIAG_BLK)
        units = range(CHUNKS_PER_MXU_TRIP)
        mxu_trips = seq // (CHUNK * CHUNKS_PER_MXU_TRIP)

        def prepare(ti):
            for u in units:
                c = ti * CHUNKS_PER_MXU_TRIP + u
                bc = _cumsum_chunk(gs_ref[rows_of(c), :])
                qd, qt, kt, kdec = _rescaled_operands(qs_ref[rows_of(c), :], ks_ref[rows_of(c), :],
                                                      bc)
                qd_ref[rows_of(c), :] = qd
                qt_ref[rows_of(c), :] = qt
                kt_ref[rows_of(c, stack), :] = kt
                kdec_ref[rows_of(c), :] = kdec
                eb_ref[c] = jnp.broadcast_to(jnp.exp(bc[CHUNK - 1:CHUNK, :]), (SUBLANES, DK))

        def contract(ti):
            cs = [ti * CHUNKS_PER_MXU_TRIP + u for u in units]
            vb = [v_ref[rows_of(c), :].astype(BF16) for c in cs]
            a = [lax.dot_general(qt_ref[rows_of(c), :], kt_ref[rows_of(c, stack), :], nt,
                                 preferred_element_type=F32) for c in cs]
            am = [(x * mask).astype(BF16) for x in a]
            vt = [jnp.concatenate([x[:(i + 1) * DIAG_BLK] for i in range(CHUNK // DIAG_BLK)], axis=0)
                  for x in vb]
            oi = [jnp.dot(am[u], vt[u], preferred_element_type=F32) for u in units]
            inc = [lax.dot_general(vb[u], kdec_ref[rows_of(cs[u]), :], tn,
                                   preferred_element_type=F32) for u in units]
            sts = [st_ref[...]]
            for u in units:
                sts.append(sts[-1] * eb_ref[cs[u]][0:1, :] + inc[u])
            o = [oi[u] + lax.dot_general(qd_ref[rows_of(cs[u]), :], sts[u].astype(BF16), nt,
                                         preferred_element_type=F32) for u in units]
            for u in units:
                o_ref[rows_of(cs[u]), :] = _head_out(o[u], gn_ref[...], gate_ref[rows_of(cs[u]), :])
            st_ref[...] = sts[-1]

        prepare(0)

        def body(ti, carry):
            contract(ti)
            prepare(ti + 1)
            return carry

        lax.fori_loop(0, mxu_trips - 1, body, 0)
        contract(mxu_trips - 1)

    @pl.when(jnp.logical_not(in_range))
    def _exact():
        mask = _off_mask(CHUNK, DIAG_BLK)

        def body(ci, carry):
            st = st_ref[...]
            for u in range(CHUNKS_PER_TRIP):
                rows = chunk_rows(ci, u)
                o, st = _chunk_exact(qs_ref[rows, :], ks_ref[rows, :], v_ref[rows, :],
                                     _cumsum_chunk(gs_ref[rows, :]), st, mask)
                o_ref[rows, :] = _head_out(o, gn_ref[...], gate_ref[rows, :])
            st_ref[...] = st
            return carry

        lax.fori_loop(0, trips, body, 0)

    sout_ref[...] = st_ref[...].T


def _sample_scan_kernel(*refs, gla, steps):
    if gla:
        (q_ref, k_ref, v_ref, gate_ref, al_ref, wa_ref, ba_ref, gn_ref, s0_ref,
         o_ref, sout_ref) = refs
        q, k, v, g = _gla_inputs(q_ref[...], k_ref[...], v_ref[...], al_ref[...],
                                 wa_ref[...], ba_ref[...])
    else:
        (q_ref, f_ref, v_ref, gate_ref, lbl_ref, gn_ref, s0_ref, o_ref, sout_ref) = refs
        q, k, v, g = _hgrn_inputs(q_ref[...], f_ref[...], v_ref[...], _lower_bound(lbl_ref))
    v32 = v.astype(F32)
    rows = SAMPLE_SEQS * steps
    groups = [slice(b * steps, (b + 1) * steps) for b in range(SAMPLE_SEQS)]
    tn = (((0,), (0,)), ((), ()))

    bc = jnp.concatenate([_scan_sublanes(g[sl]) for sl in groups], axis=0)

    def repeated(r):
        return jnp.concatenate([jnp.broadcast_to(bc[sl][r:r + 1, :], (steps, DK)) for sl in groups],
                               axis=0)

    qd = (q * jnp.exp(bc)).astype(BF16)
    kdec = (k * jnp.exp(repeated(steps - 1) - bc)).astype(BF16)
    r = lax.broadcasted_iota(jnp.int32, (rows, SAMPLE_SEQS), 0)
    c = lax.broadcasted_iota(jnp.int32, (rows, SAMPLE_SEQS), 1)
    selector = jnp.where((r >> (steps.bit_length() - 1)) == c, 1.0, 0.0).astype(F32)
    decay = jnp.exp(lax.dot_general(g, selector, tn, precision=HIGHEST,
                                    preferred_element_type=F32))

    inter = [jnp.dot(qd[sl], s0_ref[b].astype(BF16), preferred_element_type=F32)
             for b, sl in enumerate(groups)]
    upd = [lax.dot_general(kdec[sl], v32[sl].astype(BF16), tn, preferred_element_type=F32)
           for sl in groups]
    for b in range(SAMPLE_SEQS):
        sout_ref[b] = decay[:, b:b + 1] * s0_ref[b] + upd[b]
    o = jnp.concatenate(inter, axis=0) + _diag(q, k, v32, bc, steps)
    o_ref[...] = _head_out(o, gn_ref[...], gate_ref[...])


def _proj_spec(rows, width, c0, per_head, row_block, tile_width=TN_PROJ):
    per_tile = tile_width // width

    def index(b, h):
        g = c0 * LANES // width + h * per_head
        return (g // per_tile, row_block(b), g % per_tile)

    return pl.BlockSpec((None, rows, width), index)


def _prompt_scan(p, pm, *, gla, batch, seq, meta_block, small, name):
    heads, dv = (H_B, DV_B) if gla else (H_A, DV_A)
    n_chunks = seq // CHUNK
    blocks = CHUNK // DIAG_BLK
    stack = DIAG_BLK * blocks * (blocks + 1) // 2
    p16, pf, pa = p
    pm16, pmf, pma = pm
    tok = functools.partial(_proj_spec, seq, row_block=lambda b: b)
    met = functools.partial(_proj_spec, N_META, row_block=lambda b: meta_block)
    vec = lambda width: pl.BlockSpec((1, width), lambda b, h: (0, h))
    if gla:
        wa, ba, gn = small
        in_specs = [tok(DK, COL_QB, 1), tok(DK, COL_KB, 1), tok(dv, COL_VB, 1), tok(dv, COL_RB, 1),
                    tok(LANES, 0, 0, tile_width=LANES),
                    met(DK, COL_KB, 1), met(dv, COL_VB, 1), met(LANES, 0, 0, tile_width=LANES),
                    pl.BlockSpec((LANES, DK), lambda b, h: (0, h)), vec(DK), vec(dv)]
        args = (p16, p16, p16, p16, pa, pm16, pm16, pma, wa, ba, gn)
    else:
        lbl, gn = small
        in_specs = [tok(DK, COL_QA, 1), tok(DK, 0, 1), tok(dv, COL_IA, 1), tok(dv, COL_GA, 1),
                    met(DK, 0, 1), met(dv, COL_IA, 1),
                    pl.BlockSpec((2, DK), lambda b, h: (0, h)), vec(dv)]
        args = (p16, pf, p16, p16, pmf, pm16, lbl, gn)
    return pl.pallas_call(
        functools.partial(_prompt_scan_kernel, gla=gla, seq=seq, meta=N_META),
        grid=(batch, heads),
        in_specs=in_specs,
        out_specs=[pl.BlockSpec((seq, dv), lambda b, h: (b, h)),
                   pl.BlockSpec((None, None, DK, dv), lambda b, h: (b, h, 0, 0))],
        out_shape=[jax.ShapeDtypeStruct((batch * seq, heads * dv), BF16),
                   jax.ShapeDtypeStruct((batch, heads, DK, dv), F32)],
        scratch_shapes=[
            pltpu.VMEM((dv, DK), F32),
            pltpu.VMEM((seq, DK), F32),
            pltpu.VMEM((seq, DK), F32),
            pltpu.VMEM((seq, DK), F32),
            pltpu.VMEM((seq, DK), BF16),
            pltpu.VMEM((seq, DK), BF16),
            pltpu.VMEM((n_chunks * stack, DK), BF16),
            pltpu.VMEM((seq, DK), BF16),
            pltpu.VMEM((n_chunks, SUBLANES, DK), F32),
        ],
        compiler_params=_params("parallel", "parallel"),
        name=name,
    )(*args)


def _sample_scan(p, s0, *, gla, batch, steps, small, name):
    assert steps == SUBLANES, "the sample recurrence keeps one sequence per sublane group"
    heads, dv = (H_B, DV_B) if gla else (H_A, DV_A)
    rows = SAMPLE_SEQS * steps
    p16, pf, pa = p
    tok = functools.partial(_proj_spec, rows, row_block=lambda b: b)
    vec = lambda width: pl.BlockSpec((1, width), lambda b, h: (0, h))
    state = pl.BlockSpec((SAMPLE_SEQS, None, DK, dv), lambda b, h: (b, h, 0, 0))
    if gla:
        wa, ba, gn = small
        in_specs = [tok(DK, COL_QB, 1), tok(DK, COL_KB, 1), tok(dv, COL_VB, 1), tok(dv, COL_RB, 1),
                    tok(LANES, 0, 0, tile_width=LANES),
                    pl.BlockSpec((LANES, DK), lambda b, h: (0, h)), vec(DK), vec(dv), state]
        args = (p16, p16, p16, p16, pa, wa, ba, gn, s0)
    else:
        lbl, gn = small
        in_specs = [tok(DK, COL_QA, 1), tok(DK, 0, 1), tok(dv, COL_IA, 1), tok(dv, COL_GA, 1),
                    pl.BlockSpec((2, DK), lambda b, h: (0, h)), vec(dv), state]
        args = (p16, pf, p16, p16, lbl, gn, s0)
    return pl.pallas_call(
        functools.partial(_sample_scan_kernel, gla=gla, steps=steps),
        grid=(batch // SAMPLE_SEQS, heads),
        in_specs=in_specs,
        out_specs=[pl.BlockSpec((rows, dv), lambda b, h: (b, h)), state],
        out_shape=[jax.ShapeDtypeStruct((batch * steps, heads * dv), BF16),
                   jax.ShapeDtypeStruct((batch, heads, DK, dv), F32)],
        compiler_params=_params("parallel", "parallel"),
        name=name,
    )(*args)


def kernel(x_prompt, x_sample, state_hgrn, state_gla, meta_tokens, lb_logits, ffn1_norm, w_ffn1_in,
           w_ffn1_out, mix_norm, w_in, w_alpha_up, b_alpha, gnorm_a, gnorm_b, w_out, ffn2_norm,
           w_ffn2_in, w_ffn2_out, final_norm):
    batch, seq, _ = x_prompt.shape
    dec_batch, steps, _ = x_sample.shape

    w_in_t = w_in[0].T
    wop = w_out[0].astype(BF16)
    wa = jnp.pad(w_alpha_up[0], ((0, LANES - ALPHA_RANK), (0, 0))).astype(BF16)
    n1, nm, n2, nf = ffn1_norm, mix_norm, ffn2_norm, final_norm[None]
    small_a = (lb_logits, gnorm_a)
    small_b = (wa, b_alpha, gnorm_b)
    n_sample = dec_batch * steps

    xs = jnp.concatenate([x_sample.reshape(n_sample, D_MODEL), meta_tokens], axis=0)
    h1s, u1s, *w1 = _ffn(xs, n1, (w_ffn1_in[0], w_ffn1_in[0], w_ffn1_out[0]), nm, tm=xs.shape[0],
                         tf=TF_CAST, final_norm=False, emit_weights=True, name="ffn1_sample")
    ps = _in_proj(u1s, w_in_t, tm=xs.shape[0], name="inproj_sample")
    oa_s, sa_s = _sample_scan(ps, state_hgrn[0], gla=False, batch=dec_batch, steps=steps,
                              small=small_a, name="scan_hgrn_sample")
    ob_s, sb_s = _sample_scan(ps, state_gla[0], gla=True, batch=dec_batch, steps=steps,
                              small=small_b, name="scan_gla_sample")
    h2s = _out_proj(oa_s, ob_s, h1s, wop, tm=TM, name="outproj_sample")
    y_s, *w2 = _ffn(h2s, n2, (w_ffn2_in[0], w_ffn2_in[0], w_ffn2_out[0]), nf, tm=n_sample,
                    tf=TF_CAST, final_norm=True, emit_weights=True, name="ffn2_sample")

    h1p, u1p = _ffn(x_prompt.reshape(batch * seq, D_MODEL), n1, w1, nm, tm=TM, tf=TF,
                    final_norm=False, emit_weights=False, name="ffn1_prompt")
    pp = _in_proj(u1p, w_in_t, tm=TM_PROJ, name="inproj_prompt")
    meta_block = n_sample // N_META
    oa_p, sa_p = _prompt_scan(pp, ps, gla=False, batch=batch, seq=seq, meta_block=meta_block,
                              small=small_a, name="scan_hgrn_prompt")
    ob_p, sb_p = _prompt_scan(pp, ps, gla=True, batch=batch, seq=seq, meta_block=meta_block,
                              small=small_b, name="scan_gla_prompt")
    h2p = _out_proj(oa_p, ob_p, h1p, wop, tm=TM, name="outproj_prompt")
    (y_p,) = _ffn(h2p, n2, w2, nf, tm=TM, tf=TF, final_norm=True, emit_weights=False,
                  name="ffn2_prompt")

    return (y_p.reshape(batch, seq, D_MODEL), y_s.reshape(dec_batch, steps, D_MODEL),
            sa_p[None], sb_p[None], sa_s[None], sb_s[None])
```

```python
import functools

import jax
import jax.numpy as jnp
from jax import lax
from jax.experimental import pallas as pl
from jax.experimental.pallas import tpu as pltpu

F32 = jnp.float32
BF16 = jnp.bfloat16
HIGHEST = lax.Precision.HIGHEST

D_MODEL = 2048
N_META = 16
DK = 128
H_A, DV_A = 8, 128
H_B, DV_B = 4, 256
D_HALF = H_A * DV_A
ALPHA_RANK = 16
GATE_TEMP = 16.0
D_FF = 5632
CHUNK = 64
EPS = 1e-6
IN_PROJ_WIDTHS = (D_HALF, D_HALF, D_HALF, D_HALF, H_B * DK, H_B * DK, D_HALF, D_HALF, ALPHA_RANK)

LANES = 128
SUBLANES = 8
VMEM_LIMIT_BYTES = 56 * 1024 * 1024

COL_QA, COL_IA, COL_GA, COL_QB, COL_KB, COL_VB, COL_RB = 0, 8, 16, 24, 28, 32, 40
P16_COLS = 3 * D_HALF + 2 * H_B * DK + 2 * D_HALF

TM = 512
TM_FFN = 1024
TF = 256
TF_CAST = 256
TN_PROJ = 512
TM_PROJ = 2048
N16_TILES = P16_COLS // TN_PROJ
DIAG_BLK = 16
SAMPLE_SEQS = 32
CHUNKS_PER_TRIP = 4
CHUNKS_PER_MXU_TRIP = 8

FAST_MIN_LOG_GATE = -60.0 / (DIAG_BLK - 1)
FAST_MAX_KEY = 1e10


def _rms(x, w):
    return x * lax.rsqrt(jnp.mean(x * x, axis=-1, keepdims=True) + EPS) * w


def _silu(x):
    return x * jax.nn.sigmoid(x)


def _params(*sem):
    return pltpu.CompilerParams(dimension_semantics=sem, vmem_limit_bytes=VMEM_LIMIT_BYTES)


def _ffn_kernel(h_ref, nw_ref, wg_ref, wu_ref, wo_ref, tnw_ref, out_ref, *rest,
                final_norm, emit_weights):
    u_ref = rest[-1]
    j = pl.program_id(1)
    last_j = pl.num_programs(1) - 1

    def step(first, last):
        if first:
            u = _rms(h_ref[...], nw_ref[...]).astype(BF16)
            u_ref[...] = u
        else:
            u = u_ref[...]
        wg, wu, wo = (r[...].astype(BF16) for r in (wg_ref, wu_ref, wo_ref))
        if emit_weights:
            for dst, w in zip(rest[-4:-1], (wg, wu, wo)):
                dst[...] = w
        g = jnp.dot(u, wg, preferred_element_type=F32)
        up = jnp.dot(u, wu, preferred_element_type=F32)
        a = (_silu(g) * up).astype(BF16)
        acc = jnp.dot(a, wo, preferred_element_type=F32)
        if not first:
            acc = out_ref[...] + acc
        if not last:
            out_ref[...] = acc
            return
        y = h_ref[...] + 0.5 * acc
        normed = _rms(y, tnw_ref[...])
        if final_norm:
            out_ref[...] = normed
        else:
            out_ref[...] = y
            rest[0][...] = normed.astype(BF16)

    pl.when(j == 0)(functools.partial(step, True, False))
    pl.when((j > 0) & (j < last_j))(functools.partial(step, False, False))
    pl.when(j == last_j)(functools.partial(step, False, True))


def _ffn(h, norm_w, weights, tail_norm_w, *, tm, tf, final_norm, emit_weights, name):
    m = h.shape[0]
    nf = D_FF // tf
    rows = pl.BlockSpec((tm, D_MODEL), lambda i, j: (i, 0),
                        pipeline_mode=pl.Buffered(1) if m == tm else None)
    vec = pl.BlockSpec((1, D_MODEL), lambda i, j: (0, 0))
    col_tile = pl.BlockSpec((D_MODEL, tf), lambda i, j: (0, j))
    row_tile = pl.BlockSpec((tf, D_MODEL), lambda i, j: (j, 0))
    out_specs = [rows]
    out_shape = [jax.ShapeDtypeStruct((m, D_MODEL), F32)]
    if not final_norm:
        out_specs += [rows]
        out_shape += [jax.ShapeDtypeStruct((m, D_MODEL), BF16)]
    if emit_weights:
        assert m == tm, "weight casts are written once, by a single row tile"
        up_tile = pl.BlockSpec((D_MODEL, tf), lambda i, j: (0, j + nf))
        out_specs += [col_tile, col_tile, row_tile]
        out_shape += [jax.ShapeDtypeStruct((D_MODEL, D_FF), BF16)] * 2
        out_shape += [jax.ShapeDtypeStruct((D_FF, D_MODEL), BF16)]
    else:
        up_tile = col_tile
    return pl.pallas_call(
        functools.partial(_ffn_kernel, final_norm=final_norm, emit_weights=emit_weights),
        grid=(m // tm, nf),
        in_specs=[rows, vec, col_tile, up_tile, row_tile, vec],
        out_specs=out_specs,
        out_shape=out_shape,
        scratch_shapes=[pltpu.VMEM((tm, D_MODEL), BF16)],
        compiler_params=_params("parallel", "arbitrary"),
        name=name,
    )(h, norm_w, *weights, tail_norm_w)


def _proj_kernel(u_ref, w_ref, wal_ref, p16_ref, pf_ref, pa_ref):
    j = pl.program_id(1)
    nt = (((1,), (1,)), ((), ()))

    @pl.when(j == 0)
    def _low_rank():
        wal = jnp.concatenate([wal_ref[...].astype(BF16),
                               jnp.zeros((LANES - ALPHA_RANK, D_MODEL), BF16)], axis=0)
        pa_ref[...] = lax.dot_general(u_ref[...], wal, nt, preferred_element_type=F32)

    def project():
        return lax.dot_general(u_ref[...], w_ref[...].astype(BF16), nt,
                               preferred_element_type=F32)

    @pl.when(j < N16_TILES)
    def _narrow():
        p16_ref[...] = project().astype(BF16)

    @pl.when(j >= N16_TILES)
    def _wide():
        pf_ref[...] = project()


def _w_in_tile(j):
    fa_lo, fa_hi = D_HALF // TN_PROJ, 2 * D_HALF // TN_PROJ
    moved = fa_hi - fa_lo
    return jnp.where(j < fa_lo, j, jnp.where(j < N16_TILES, j + moved, j - N16_TILES + fa_lo))


def _in_proj(u, w, *, tm, name):
    m = u.shape[0]
    return pl.pallas_call(
        _proj_kernel,
        grid=(m // tm, (P16_COLS + D_HALF) // TN_PROJ),
        in_specs=[
            pl.BlockSpec((tm, D_MODEL), lambda i, j: (i, 0)),
            pl.BlockSpec((TN_PROJ, D_MODEL), lambda i, j: (_w_in_tile(j), 0)),
            pl.BlockSpec((ALPHA_RANK, D_MODEL), lambda i, j: ((P16_COLS + D_HALF) // ALPHA_RANK, 0)),
        ],
        out_specs=[
            pl.BlockSpec((None, tm, TN_PROJ), lambda i, j: (jnp.minimum(j, N16_TILES - 1), i, 0)),
            pl.BlockSpec((None, tm, TN_PROJ), lambda i, j: (jnp.maximum(j - N16_TILES, 0), i, 0)),
            pl.BlockSpec((None, tm, LANES), lambda i, j: (0, i, 0)),
        ],
        out_shape=[jax.ShapeDtypeStruct((N16_TILES, m, TN_PROJ), BF16),
                   jax.ShapeDtypeStruct((D_HALF // TN_PROJ, m, TN_PROJ), F32),
                   jax.ShapeDtypeStruct((1, m, LANES), F32)],
        compiler_params=_params("parallel", "arbitrary"),
        name=name,
    )(u, w, w)


def _out_proj_kernel(oa_ref, ob_ref, h_ref, wa_ref, wb_ref, out_ref):
    out_ref[...] = (h_ref[...]
                    + jnp.dot(oa_ref[...], wa_ref[...], preferred_element_type=F32)
                    + jnp.dot(ob_ref[...], wb_ref[...], preferred_element_type=F32))


def _out_proj(oa, ob, h, w, *, tm, name):
    m = oa.shape[0]
    return pl.pallas_call(
        _out_proj_kernel,
        grid=(m // tm,),
        in_specs=[
            pl.BlockSpec((tm, D_HALF), lambda i: (i, 0)),
            pl.BlockSpec((tm, D_HALF), lambda i: (i, 0)),
            pl.BlockSpec((tm, D_MODEL), lambda i: (i, 0)),
            pl.BlockSpec((D_HALF, D_MODEL), lambda i: (0, 0)),
            pl.BlockSpec((D_HALF, D_MODEL), lambda i: (1, 0)),
        ],
        out_specs=pl.BlockSpec((tm, D_MODEL), lambda i: (i, 0)),
        out_shape=jax.ShapeDtypeStruct((m, D_MODEL), F32),
        compiler_params=_params("parallel"),
        name=name,
    )(oa, ob, h, w, w)


def _tri(c, seq_len):
    r = lax.broadcasted_iota(jnp.int32, (c, c), 0)
    s = lax.broadcasted_iota(jnp.int32, (c, c), 1)
    return jnp.where((r >= s) & ((r ^ s) < seq_len), 1.0, 0.0).astype(F32)


def _off_mask(c, blk):
    nb = c // blk
    rows = nb * (nb - 1) // 2 * blk
    r = lax.broadcasted_iota(jnp.int32, (c, rows), 0) >> (blk.bit_length() - 1)
    s = lax.broadcasted_iota(jnp.int32, (c, rows), 1)
    seg = jnp.zeros((c, rows), jnp.int32)
    for i in range(1, nb):
        seg = seg + jnp.where(s >= blk * i * (i - 1) // 2, 1, 0)
    return jnp.where(r == seg, 1.0, 0.0).astype(F32)


def _diag(q, k, v, bc, blk):
    c = q.shape[0]
    row = lax.broadcasted_iota(jnp.int32, (blk, 1), 0)
    outs = []
    for i in range(c // blk):
        sl = slice(i * blk, (i + 1) * blk)
        qi, ki, vi, bi = q[sl], k[sl], v[sl].astype(F32), bc[sl]
        acc = jnp.zeros((blk, v.shape[1]), F32)
        for s in range(blk):
            e = jnp.exp(jnp.minimum(bi - bi[s:s + 1, :], 0.0))
            col = jnp.sum(qi * ki[s:s + 1, :] * e, axis=-1, keepdims=True)
            col = jnp.where(row >= s, col, 0.0)
            acc = acc + col * vi[s:s + 1, :]
        outs.append(acc)
    return outs[0] if len(outs) == 1 else jnp.concatenate(outs, axis=0)


def _offdiag(q, k, v, bc, blk, mask):
    c = q.shape[0]
    nb = c // blk
    qs, ks, vs = [jnp.zeros((blk, DK), F32)], [], []
    for i in range(1, nb):
        lo = i * blk
        ref = bc[lo:lo + 1, :]
        qs.append(q[lo:lo + blk] * jnp.exp(bc[lo:lo + blk] - ref))
        ks.append(k[:lo] * jnp.exp(ref - bc[:lo]))
        vs.append(v[:lo])
    qt = jnp.concatenate(qs, axis=0).astype(BF16)
    kt = jnp.concatenate(ks, axis=0).astype(BF16)
    vt = jnp.concatenate(vs, axis=0).astype(BF16)
    a = lax.dot_general(qt, kt, (((1,), (1,)), ((), ())), preferred_element_type=F32)
    return jnp.dot((a * mask).astype(BF16), vt, preferred_element_type=F32)


def _hgrn_inputs(qa, fa, ia, lb):
    fg = lb + (1.0 - lb) * jax.nn.sigmoid(fa)
    return _silu(qa.astype(F32)) * (DK ** -0.5), 1.0 - fg, ia, jnp.log(fg)


def _gla_inputs(qb, kb, vb, al, wa, ba):
    alpha = jnp.dot(al.astype(BF16), wa, preferred_element_type=F32) + ba
    logsig = jnp.minimum(alpha, 0.0) - jnp.log1p(jnp.exp(-jnp.abs(alpha)))
    return qb.astype(F32) * (DK ** -0.5), kb.astype(F32), vb, logsig / GATE_TEMP


def _head_out(o, gn, gate):
    return (_rms(o, gn) * _silu(gate.astype(F32))).astype(BF16)


def _lower_bound(lbl_ref):
    x = lbl_ref[...]
    e = jnp.exp(x - jnp.max(x, axis=0, keepdims=True))
    return e[0:1, :] / jnp.sum(e, axis=0, keepdims=True)


def _stack_mask(c, blk):
    nb = c // blk
    rows = blk * nb * (nb + 1) // 2
    t = lax.broadcasted_iota(jnp.int32, (c, rows), 0)
    r = lax.broadcasted_iota(jnp.int32, (c, rows), 1)
    seg = jnp.zeros((c, rows), jnp.int32)
    off = jnp.zeros((c, rows), jnp.int32)
    for i in range(1, nb):
        start = blk * i * (i + 1) // 2
        seg = seg + jnp.where(r >= start, 1, 0)
        off = jnp.where(r >= start, start, off)
    keep = ((t >> (blk.bit_length() - 1)) == seg) & (r - off <= t)
    return jnp.where(keep, 1.0, 0.0).astype(F32)


def _state_update_t(k, vb, bc, st):
    c = k.shape[0]
    blast = bc[c - 1:c, :]
    kdec = (k * jnp.exp(blast - bc)).astype(BF16)
    upd = lax.dot_general(vb, kdec, (((0,), (0,)), ((), ())), preferred_element_type=F32)
    if st is None:
        return upd
    return st * jnp.exp(blast) + upd


def _scan_sublanes(p):
    row = lax.broadcasted_iota(jnp.int32, (SUBLANES, DK), 0)
    for s in (1, 2, 4):
        p = p + jnp.where(row >= s, pltpu.roll(p, s, axis=0), 0.0)
    return p


def _cumsum_chunk(g):
    out, carry = [], None
    for j in range(CHUNK // SUBLANES):
        p = _scan_sublanes(g[j * SUBLANES:(j + 1) * SUBLANES])
        if carry is not None:
            p = p + carry
        carry = p[SUBLANES - 1:SUBLANES, :]
        out.append(p)
    return jnp.concatenate(out, axis=0)


def _rescaled_operands(q, k, bc):
    c, blk = CHUNK, DIAG_BLK
    qs, ks = [], []
    for i in range(c // blk):
        lo, hi = i * blk, (i + 1) * blk
        ref = bc[lo:lo + 1, :]
        qs.append(q[lo:hi] * jnp.exp(bc[lo:hi] - ref))
        ks.append(k[:hi] * jnp.exp(ref - bc[:hi]))
    return ((q * jnp.exp(bc)).astype(BF16), jnp.concatenate(qs, axis=0).astype(BF16),
            jnp.concatenate(ks, axis=0).astype(BF16),
            (k * jnp.exp(bc[c - 1:c, :] - bc)).astype(BF16))


def _chunk_exact(q, k, v, bc, st, mask):
    o = lax.dot_general((q * jnp.exp(bc)).astype(BF16), st.astype(BF16),
                        (((1,), (1,)), ((), ())), preferred_element_type=F32)
    o = o + _offdiag(q, k, v, bc, DIAG_BLK, mask) + _diag(q, k, v, bc, DIAG_BLK)
    return o, _state_update_t(k, v.astype(BF16), bc, st)


def _prompt_scan_kernel(*refs, gla, seq, meta):
    n_in = 11 if gla else 8
    if gla:
        q_ref, k_ref, v_ref, gate_ref, al_ref, mk_ref, mv_ref, mal_ref, wa_ref, ba_ref, gn_ref = (
            refs[:n_in])
    else:
        q_ref, f_ref, v_ref, gate_ref, mf_ref, mv_ref, lbl_ref, gn_ref = refs[:n_in]
        lb = _lower_bound(lbl_ref)
    o_ref, sout_ref = refs[n_in:n_in + 2]
    st_ref, qs_ref, ks_ref, gs_ref, qd_ref, qt_ref, kt_ref, kdec_ref, eb_ref = refs[n_in + 2:]
    trips = seq // (CHUNK * CHUNKS_PER_TRIP)
    stack = kt_ref.shape[0] // (seq // CHUNK)
    nt = (((1,), (1,)), ((), ()))
    tn = (((0,), (0,)), ((), ()))

    def chunk_rows(ci, u, n=CHUNK):
        return pl.ds(pl.multiple_of((ci * CHUNKS_PER_TRIP + u) * n, n), n)

    if gla:
        _, mk, mv, mg = _gla_inputs(mk_ref[...], mk_ref[...], mv_ref[...], mal_ref[...],
                                    wa_ref[...], ba_ref[...])
    else:
        _, mk, mv, mg = _hgrn_inputs(mf_ref[...], mf_ref[...], mv_ref[...], lb)
    mbc = jnp.dot(_tri(meta, meta), mg, precision=HIGHEST, preferred_element_type=F32)
    st_ref[...] = _state_update_t(mk, mv.astype(BF16), mbc, None)

    def gate_pass(ci, carry):
        gmin, kmax = carry
        trip_rows = CHUNK * CHUNKS_PER_TRIP
        rows = pl.ds(pl.multiple_of(ci * trip_rows, trip_rows), trip_rows)
        if gla:
            q, k, _, g = _gla_inputs(q_ref[rows, :], k_ref[rows, :], None, al_ref[rows, :],
                                     wa_ref[...], ba_ref[...])
        else:
            q, k, _, g = _hgrn_inputs(q_ref[rows, :], f_ref[rows, :], None, lb)
        qs_ref[rows, :] = q
        ks_ref[rows, :] = k
        gs_ref[rows, :] = g
        return (jnp.minimum(gmin, jnp.min(g, axis=0, keepdims=True)),
                jnp.maximum(kmax, jnp.max(jnp.abs(k), axis=0, keepdims=True)))

    gmin, kmax = lax.fori_loop(0, trips, gate_pass,
                               (jnp.zeros((1, DK), F32), jnp.zeros((1, DK), F32)))
    in_range = (jnp.min(gmin) >= FAST_MIN_LOG_GATE) & (jnp.max(kmax) <= FAST_MAX_KEY)

    def rows_of(c, n=CHUNK):
        return pl.ds(pl.multiple_of(c * n, n), n)

    @pl.when(in_range)
    def _rescaled():
        mask = _stack_mask(CHUNK, DIAG_BLK)
        units = range(CHUNKS_PER_MXU_TRIP)
        mxu_trips = seq // (CHUNK * CHUNKS_PER_MXU_TRIP)

        def prepare(ti):
            for u in units:
                c = ti * CHUNKS_PER_MXU_TRIP + u
                bc = _cumsum_chunk(gs_ref[rows_of(c), :])
                qd, qt, kt, kdec = _rescaled_operands(qs_ref[rows_of(c), :], ks_ref[rows_of(c), :],
                                                      bc)
                qd_ref[rows_of(c), :] = qd
                qt_ref[rows_of(c), :] = qt
                kt_ref[rows_of(c, stack), :] = kt
                kdec_ref[rows_of(c), :] = kdec
                eb_ref[c] = jnp.broadcast_to(jnp.exp(bc[CHUNK - 1:CHUNK, :]), (SUBLANES, DK))

        def contract(ti):
            cs = [ti * CHUNKS_PER_MXU_TRIP + u for u in units]
            vb = [v_ref[rows_of(c), :].astype(BF16) for c in cs]
            a = [lax.dot_general(qt_ref[rows_of(c), :], kt_ref[rows_of(c, stack), :], nt,
                                 preferred_element_type=F32) for c in cs]
            am = [(x * mask).astype(BF16) for x in a]
            vt = [jnp.concatenate([x[:(i + 1) * DIAG_BLK] for i in range(CHUNK // DIAG_BLK)], axis=0)
                  for x in vb]
            oi = [jnp.dot(am[u], vt[u], preferred_element_type=F32) for u in units]
            inc = [lax.dot_general(vb[u], kdec_ref[rows_of(cs[u]), :], tn,
                                   preferred_element_type=F32) for u in units]
            sts = [st_ref[...]]
            for u in units:
                sts.append(sts[-1] * eb_ref[cs[u]][0:1, :] + inc[u])
            o = [oi[u] + lax.dot_general(qd_ref[rows_of(cs[u]), :], sts[u].astype(BF16), nt,
                                         preferred_element_type=F32) for u in units]
            for u in units:
                o_ref[rows_of(cs[u]), :] = _head_out(o[u], gn_ref[...], gate_ref[rows_of(cs[u]), :])
            st_ref[...] = sts[-1]

        prepare(0)

        def body(ti, carry):
            contract(ti)
            prepare(ti + 1)
            return carry

        lax.fori_loop(0, mxu_trips - 1, body, 0)
        contract(mxu_trips - 1)

    @pl.when(jnp.logical_not(in_range))
    def _exact():
        mask = _off_mask(CHUNK, DIAG_BLK)

        def body(ci, carry):
            st = st_ref[...]
            for u in range(CHUNKS_PER_TRIP):
                rows = chunk_rows(ci, u)
                o, st = _chunk_exact(qs_ref[rows, :], ks_ref[rows, :], v_ref[rows, :],
                                     _cumsum_chunk(gs_ref[rows, :]), st, mask)
                o_ref[rows, :] = _head_out(o, gn_ref[...], gate_ref[rows, :])
            st_ref[...] = st
            return carry

        lax.fori_loop(0, trips, body, 0)

    sout_ref[...] = st_ref[...].T


def _sample_scan_kernel(*refs, gla, steps):
    if gla:
        (q_ref, k_ref, v_ref, gate_ref, al_ref, wa_ref, ba_ref, gn_ref, s0_ref,
         o_ref, sout_ref) = refs
        q, k, v, g = _gla_inputs(q_ref[...], k_ref[...], v_ref[...], al_ref[...],
                                 wa_ref[...], ba_ref[...])
    else:
        (q_ref, f_ref, v_ref, gate_ref, lbl_ref, gn_ref, s0_ref, o_ref, sout_ref) = refs
        q, k, v, g = _hgrn_inputs(q_ref[...], f_ref[...], v_ref[...], _lower_bound(lbl_ref))
    v32 = v.astype(F32)
    rows = SAMPLE_SEQS * steps
    groups = [slice(b * steps, (b + 1) * steps) for b in range(SAMPLE_SEQS)]
    tn = (((0,), (0,)), ((), ()))

    bc = jnp.concatenate([_scan_sublanes(g[sl]) for sl in groups], axis=0)

    def repeated(r):
        return jnp.concatenate([jnp.broadcast_to(bc[sl][r:r + 1, :], (steps, DK)) for sl in groups],
                               axis=0)

    qd = (q * jnp.exp(bc)).astype(BF16)
    kdec = (k * jnp.exp(repeated(steps - 1) - bc)).astype(BF16)
    r = lax.broadcasted_iota(jnp.int32, (rows, SAMPLE_SEQS), 0)
    c = lax.broadcasted_iota(jnp.int32, (rows, SAMPLE_SEQS), 1)
    selector = jnp.where((r >> (steps.bit_length() - 1)) == c, 1.0, 0.0).astype(F32)
    decay = jnp.exp(lax.dot_general(g, selector, tn, precision=HIGHEST,
                                    preferred_element_type=F32))

    inter = [jnp.dot(qd[sl], s0_ref[b].astype(BF16), preferred_element_type=F32)
             for b, sl in enumerate(groups)]
    upd = [lax.dot_general(kdec[sl], v32[sl].astype(BF16), tn, preferred_element_type=F32)
           for sl in groups]
    for b in range(SAMPLE_SEQS):
        sout_ref[b] = decay[:, b:b + 1] * s0_ref[b] + upd[b]
    o = jnp.concatenate(inter, axis=0) + _diag(q, k, v32, bc, steps)
    o_ref[...] = _head_out(o, gn_ref[...], gate_ref[...])


def _proj_spec(rows, width, c0, per_head, row_block, tile_width=TN_PROJ):
    per_tile = tile_width // width

    def index(b, h):
        g = c0 * LANES // width + h * per_head
        return (g // per_tile, row_block(b), g % per_tile)

    return pl.BlockSpec((None, rows, width), index)


def _prompt_scan(p, pm, *, gla, batch, seq, meta_block, small, name):
    heads, dv = (H_B, DV_B) if gla else (H_A, DV_A)
    n_chunks = seq // CHUNK
    blocks = CHUNK // DIAG_BLK
    stack = DIAG_BLK * blocks * (blocks + 1) // 2
    p16, pf, pa = p
    pm16, pmf, pma = pm
    tok = functools.partial(_proj_spec, seq, row_block=lambda b: b)
    met = functools.partial(_proj_spec, N_META, row_block=lambda b: meta_block)
    vec = lambda width: pl.BlockSpec((1, width), lambda b, h: (0, h))
    if gla:
        wa, ba, gn = small
        in_specs = [tok(DK, COL_QB, 1), tok(DK, COL_KB, 1), tok(dv, COL_VB, 1), tok(dv, COL_RB, 1),
                    tok(LANES, 0, 0, tile_width=LANES),
                    met(DK, COL_KB, 1), met(dv, COL_VB, 1), met(LANES, 0, 0, tile_width=LANES),
                    pl.BlockSpec((LANES, DK), lambda b, h: (0, h)), vec(DK), vec(dv)]
        args = (p16, p16, p16, p16, pa, pm16, pm16, pma, wa, ba, gn)
    else:
        lbl, gn = small
        in_specs = [tok(DK, COL_QA, 1), tok(DK, 0, 1), tok(dv, COL_IA, 1), tok(dv, COL_GA, 1),
                    met(DK, 0, 1), met(dv, COL_IA, 1),
                    pl.BlockSpec((2, DK), lambda b, h: (0, h)), vec(dv)]
        args = (p16, pf, p16, p16, pmf, pm16, lbl, gn)
    return pl.pallas_call(
        functools.partial(_prompt_scan_kernel, gla=gla, seq=seq, meta=N_META),
        grid=(batch, heads),
        in_specs=in_specs,
        out_specs=[pl.BlockSpec((seq, dv), lambda b, h: (b, h)),
                   pl.BlockSpec((None, None, DK, dv), lambda b, h: (b, h, 0, 0))],
        out_shape=[jax.ShapeDtypeStruct((batch * seq, heads * dv), BF16),
                   jax.ShapeDtypeStruct((batch, heads, DK, dv), F32)],
        scratch_shapes=[
            pltpu.VMEM((dv, DK), F32),
            pltpu.VMEM((seq, DK), F32),
            pltpu.VMEM((seq, DK), F32),
            pltpu.VMEM((seq, DK), F32),
            pltpu.VMEM((seq, DK), BF16),
            pltpu.VMEM((seq, DK), BF16),
            pltpu.VMEM((n_chunks * stack, DK), BF16),
            pltpu.VMEM((seq, DK), BF16),
            pltpu.VMEM((n_chunks, SUBLANES, DK), F32),
        ],
        compiler_params=_params("parallel", "parallel"),
        name=name,
    )(*args)


def _sample_scan(p, s0, *, gla, batch, steps, small, name):
    assert steps == SUBLANES, "the sample recurrence keeps one sequence per sublane group"
    heads, dv = (H_B, DV_B) if gla else (H_A, DV_A)
    rows = SAMPLE_SEQS * steps
    p16, pf, pa = p
    tok = functools.partial(_proj_spec, rows, row_block=lambda b: b)
    vec = lambda width: pl.BlockSpec((1, width), lambda b, h: (0, h))
    state = pl.BlockSpec((SAMPLE_SEQS, None, DK, dv), lambda b, h: (b, h, 0, 0))
    if gla:
        wa, ba, gn = small
        in_specs = [tok(DK, COL_QB, 1), tok(DK, COL_KB, 1), tok(dv, COL_VB, 1), tok(dv, COL_RB, 1),
                    tok(LANES, 0, 0, tile_width=LANES),
                    pl.BlockSpec((LANES, DK), lambda b, h: (0, h)), vec(DK), vec(dv), state]
        args = (p16, p16, p16, p16, pa, wa, ba, gn, s0)
    else:
        lbl, gn = small
        in_specs = [tok(DK, COL_QA, 1), tok(DK, 0, 1), tok(dv, COL_IA, 1), tok(dv, COL_GA, 1),
                    pl.BlockSpec((2, DK), lambda b, h: (0, h)), vec(dv), state]
        args = (p16, pf, p16, p16, lbl, gn, s0)
    return pl.pallas_call(
        functools.partial(_sample_scan_kernel, gla=gla, steps=steps),
        grid=(batch // SAMPLE_SEQS, heads),
        in_specs=in_specs,
        out_specs=[pl.BlockSpec((rows, dv), lambda b, h: (b, h)), state],
        out_shape=[jax.ShapeDtypeStruct((batch * steps, heads * dv), BF16),
                   jax.ShapeDtypeStruct((batch, heads, DK, dv), F32)],
        compiler_params=_params("parallel", "parallel"),
        name=name,
    )(*args)


def kernel(x_prompt, x_sample, state_hgrn, state_gla, meta_tokens, lb_logits, ffn1_norm, w_ffn1_in,
           w_ffn1_out, mix_norm, w_in, w_alpha_up, b_alpha, gnorm_a, gnorm_b, w_out, ffn2_norm,
           w_ffn2_in, w_ffn2_out, final_norm):
    batch, seq, _ = x_prompt.shape
    dec_batch, steps, _ = x_sample.shape

    w_in_t = w_in[0].T
    wop = w_out[0].astype(BF16)
    wa = jnp.pad(w_alpha_up[0], ((0, LANES - ALPHA_RANK), (0, 0))).astype(BF16)
    n1, nm, n2, nf = ffn1_norm, mix_norm, ffn2_norm, final_norm[None]
    small_a = (lb_logits, gnorm_a)
    small_b = (wa, b_alpha, gnorm_b)
    n_sample = dec_batch * steps

    xs = jnp.concatenate([x_sample.reshape(n_sample, D_MODEL), meta_tokens], axis=0)
    h1s, u1s, *w1 = _ffn(xs, n1, (w_ffn1_in[0], w_ffn1_in[0], w_ffn1_out[0]), nm, tm=xs.shape[0],
                         tf=TF_CAST, final_norm=False, emit_weights=True, name="ffn1_sample")
    ps = _in_proj(u1s, w_in_t, tm=xs.shape[0], name="inproj_sample")
    oa_s, sa_s = _sample_scan(ps, state_hgrn[0], gla=False, batch=dec_batch, steps=steps,
                              small=small_a, name="scan_hgrn_sample")
    ob_s, sb_s = _sample_scan(ps, state_gla[0], gla=True, batch=dec_batch, steps=steps,
                              small=small_b, name="scan_gla_sample")
    h2s = _out_proj(oa_s, ob_s, h1s, wop, tm=TM, name="outproj_sample")
    y_s, *w2 = _ffn(h2s, n2, (w_ffn2_in[0], w_ffn2_in[0], w_ffn2_out[0]), nf, tm=n_sample,
                    tf=TF_CAST, final_norm=True, emit_weights=True, name="ffn2_sample")

    h1p, u1p = _ffn(x_prompt.reshape(batch * seq, D_MODEL), n1, w1, nm, tm=TM_FFN, tf=TF,
                    final_norm=False, emit_weights=False, name="ffn1_prompt")
    pp = _in_proj(u1p, w_in_t, tm=TM_PROJ, name="inproj_prompt")
    meta_block = n_sample // N_META
    oa_p, sa_p = _prompt_scan(pp, ps, gla=False, batch=batch, seq=seq, meta_block=meta_block,
                              small=small_a, name="scan_hgrn_prompt")
    ob_p, sb_p = _prompt_scan(pp, ps, gla=True, batch=batch, seq=seq, meta_block=meta_block,
                              small=small_b, name="scan_gla_prompt")
    h2p = _out_proj(oa_p, ob_p, h1p, wop, tm=TM, name="outproj_prompt")
    (y_p,) = _ffn(h2p, n2, w2, nf, tm=TM_FFN, tf=TF, final_norm=True, emit_weights=False,
                  name="ffn2_prompt")

    return (y_p.reshape(batch, seq, D_MODEL), y_s.reshape(dec_batch, steps, D_MODEL),
            sa_p[None], sb_p[None], sa_s[None], sb_s[None])
```

```python
import functools

import jax
import jax.numpy as jnp
from jax import lax
from jax.experimental import pallas as pl
from jax.experimental.pallas import tpu as pltpu

F32 = jnp.float32
BF16 = jnp.bfloat16
HIGHEST = lax.Precision.HIGHEST

D_MODEL = 2048
N_META = 16
DK = 128
H_A, DV_A = 8, 128
H_B, DV_B = 4, 256
D_HALF = H_A * DV_A
ALPHA_RANK = 16
GATE_TEMP = 16.0
D_FF = 5632
CHUNK = 64
EPS = 1e-6
IN_PROJ_WIDTHS = (D_HALF, D_HALF, D_HALF, D_HALF, H_B * DK, H_B * DK, D_HALF, D_HALF, ALPHA_RANK)

LANES = 128
SUBLANES = 8
VMEM_LIMIT_BYTES = 56 * 1024 * 1024

COL_QA, COL_IA, COL_GA, COL_QB, COL_KB, COL_VB, COL_RB = 0, 8, 16, 24, 28, 32, 40
P16_COLS = 3 * D_HALF + 2 * H_B * DK + 2 * D_HALF

TM = 512
TM_FFN = 1024
TF = 256
TF_CAST = 256
TN_PROJ = 512
TM_PROJ = 2048
N16_TILES = P16_COLS // TN_PROJ
DIAG_BLK = 16
SAMPLE_SEQS = 32
CHUNKS_PER_TRIP = 4
CHUNKS_PER_MXU_TRIP = 8

FAST_MIN_LOG_GATE = -60.0 / (DIAG_BLK - 1)
FAST_MAX_KEY = 1e10


def _rms(x, w):
    return x * lax.rsqrt(jnp.mean(x * x, axis=-1, keepdims=True) + EPS) * w


def _silu(x):
    return x * jax.nn.sigmoid(x)


def _params(*sem):
    return pltpu.CompilerParams(dimension_semantics=sem, vmem_limit_bytes=VMEM_LIMIT_BYTES)


def _ffn_kernel(h_ref, nw_ref, wg_ref, wu_ref, wo_ref, tnw_ref, out_ref, *rest,
                final_norm, emit_weights):
    u_ref = rest[-1]
    j = pl.program_id(1)
    last_j = pl.num_programs(1) - 1

    def step(first, last):
        if first:
            u = _rms(h_ref[...], nw_ref[...]).astype(BF16)
            u_ref[...] = u
        else:
            u = u_ref[...]
        wg, wu, wo = (r[...].astype(BF16) for r in (wg_ref, wu_ref, wo_ref))
        if emit_weights:
            for dst, w in zip(rest[-4:-1], (wg, wu, wo)):
                dst[...] = w
        g = jnp.dot(u, wg, preferred_element_type=F32)
        up = jnp.dot(u, wu, preferred_element_type=F32)
        a = (_silu(g) * up).astype(BF16)
        acc = jnp.dot(a, wo, preferred_element_type=F32)
        if not first:
            acc = out_ref[...] + acc
        if not last:
            out_ref[...] = acc
            return
        y = h_ref[...] + 0.5 * acc
        normed = _rms(y, tnw_ref[...])
        if final_norm:
            out_ref[...] = normed
        else:
            out_ref[...] = y
            rest[0][...] = normed.astype(BF16)

    pl.when(j == 0)(functools.partial(step, True, False))
    pl.when((j > 0) & (j < last_j))(functools.partial(step, False, False))
    pl.when(j == last_j)(functools.partial(step, False, True))


def _ffn(h, norm_w, weights, tail_norm_w, *, tm, tf, final_norm, emit_weights, name):
    m = h.shape[0]
    nf = D_FF // tf
    rows = pl.BlockSpec((tm, D_MODEL), lambda i, j: (i, 0),
                        pipeline_mode=pl.Buffered(1) if m == tm else None)
    vec = pl.BlockSpec((1, D_MODEL), lambda i, j: (0, 0))
    col_tile = pl.BlockSpec((D_MODEL, tf), lambda i, j: (0, j))
    row_tile = pl.BlockSpec((tf, D_MODEL), lambda i, j: (j, 0))
    out_specs = [rows]
    out_shape = [jax.ShapeDtypeStruct((m, D_MODEL), F32)]
    if not final_norm:
        out_specs += [rows]
        out_shape += [jax.ShapeDtypeStruct((m, D_MODEL), BF16)]
    if emit_weights:
        assert m == tm, "weight casts are written once, by a single row tile"
        up_tile = pl.BlockSpec((D_MODEL, tf), lambda i, j: (0, j + nf))
        out_specs += [col_tile, col_tile, row_tile]
        out_shape += [jax.ShapeDtypeStruct((D_MODEL, D_FF), BF16)] * 2
        out_shape += [jax.ShapeDtypeStruct((D_FF, D_MODEL), BF16)]
    else:
        up_tile = col_tile
    return pl.pallas_call(
        functools.partial(_ffn_kernel, final_norm=final_norm, emit_weights=emit_weights),
        grid=(m // tm, nf),
        in_specs=[rows, vec, col_tile, up_tile, row_tile, vec],
        out_specs=out_specs,
        out_shape=out_shape,
        scratch_shapes=[pltpu.VMEM((tm, D_MODEL), BF16)],
        compiler_params=_params("parallel", "arbitrary"),
        name=name,
    )(h, norm_w, *weights, tail_norm_w)


def _proj_kernel(u_ref, w_ref, wal_ref, p16_ref, pf_ref, pa_ref):
    j = pl.program_id(1)
    nt = (((1,), (1,)), ((), ()))

    @pl.when(j == 0)
    def _low_rank():
        wal = jnp.concatenate([wal_ref[...].astype(BF16),
                               jnp.zeros((LANES - ALPHA_RANK, D_MODEL), BF16)], axis=0)
        pa_ref[...] = lax.dot_general(u_ref[...], wal, nt, preferred_element_type=F32)

    def project():
        return lax.dot_general(u_ref[...], w_ref[...].astype(BF16), nt,
                               preferred_element_type=F32)

    @pl.when(j < N16_TILES)
    def _narrow():
        p16_ref[...] = project().astype(BF16)

    @pl.when(j >= N16_TILES)
    def _wide():
        pf_ref[...] = project()


def _w_in_tile(j):
    fa_lo, fa_hi = D_HALF // TN_PROJ, 2 * D_HALF // TN_PROJ
    moved = fa_hi - fa_lo
    return jnp.where(j < fa_lo, j, jnp.where(j < N16_TILES, j + moved, j - N16_TILES + fa_lo))


def _in_proj(u, w, *, tm, name):
    m = u.shape[0]
    return pl.pallas_call(
        _proj_kernel,
        grid=(m // tm, (P16_COLS + D_HALF) // TN_PROJ),
        in_specs=[
            pl.BlockSpec((tm, D_MODEL), lambda i, j: (i, 0)),
            pl.BlockSpec((TN_PROJ, D_MODEL), lambda i, j: (_w_in_tile(j), 0)),
            pl.BlockSpec((ALPHA_RANK, D_MODEL), lambda i, j: ((P16_COLS + D_HALF) // ALPHA_RANK, 0)),
        ],
        out_specs=[
            pl.BlockSpec((None, tm, TN_PROJ), lambda i, j: (jnp.minimum(j, N16_TILES - 1), i, 0)),
            pl.BlockSpec((None, tm, TN_PROJ), lambda i, j: (jnp.maximum(j - N16_TILES, 0), i, 0)),
            pl.BlockSpec((None, tm, LANES), lambda i, j: (0, i, 0)),
        ],
        out_shape=[jax.ShapeDtypeStruct((N16_TILES, m, TN_PROJ), BF16),
                   jax.ShapeDtypeStruct((D_HALF // TN_PROJ, m, TN_PROJ), F32),
                   jax.ShapeDtypeStruct((1, m, LANES), F32)],
        compiler_params=_params("parallel", "arbitrary"),
        name=name,
    )(u, w, w)


def _out_proj_kernel(oa_ref, ob_ref, h_ref, wa_ref, wb_ref, out_ref):
    out_ref[...] = (h_ref[...]
                    + jnp.dot(oa_ref[...], wa_ref[...], preferred_element_type=F32)
                    + jnp.dot(ob_ref[...], wb_ref[...], preferred_element_type=F32))


def _out_proj(oa, ob, h, w, *, tm, name):
    m = oa.shape[0]
    return pl.pallas_call(
        _out_proj_kernel,
        grid=(m // tm,),
        in_specs=[
            pl.BlockSpec((tm, D_HALF), lambda i: (i, 0)),
            pl.BlockSpec((tm, D_HALF), lambda i: (i, 0)),
            pl.BlockSpec((tm, D_MODEL), lambda i: (i, 0)),
            pl.BlockSpec((D_HALF, D_MODEL), lambda i: (0, 0)),
            pl.BlockSpec((D_HALF, D_MODEL), lambda i: (1, 0)),
        ],
        out_specs=pl.BlockSpec((tm, D_MODEL), lambda i: (i, 0)),
        out_shape=jax.ShapeDtypeStruct((m, D_MODEL), F32),
        compiler_params=_params("parallel"),
        name=name,
    )(oa, ob, h, w, w)


def _tri(c, seq_len):
    r = lax.broadcasted_iota(jnp.int32, (c, c), 0)
    s = lax.broadcasted_iota(jnp.int32, (c, c), 1)
    return jnp.where((r >= s) & ((r ^ s) < seq_len), 1.0, 0.0).astype(F32)


def _off_mask(c, blk):
    nb = c // blk
    rows = nb * (nb - 1) // 2 * blk
    r = lax.broadcasted_iota(jnp.int32, (c, rows), 0) >> (blk.bit_length() - 1)
    s = lax.broadcasted_iota(jnp.int32, (c, rows), 1)
    seg = jnp.zeros((c, rows), jnp.int32)
    for i in range(1, nb):
        seg = seg + jnp.where(s >= blk * i * (i - 1) // 2, 1, 0)
    return jnp.where(r == seg, 1.0, 0.0).astype(F32)


def _diag(q, k, v, bc, blk):
    c = q.shape[0]
    row = lax.broadcasted_iota(jnp.int32, (blk, 1), 0)
    outs = []
    for i in range(c // blk):
        sl = slice(i * blk, (i + 1) * blk)
        qi, ki, vi, bi = q[sl], k[sl], v[sl].astype(F32), bc[sl]
        acc = jnp.zeros((blk, v.shape[1]), F32)
        for s in range(blk):
            e = jnp.exp(jnp.minimum(bi - bi[s:s + 1, :], 0.0))
            col = jnp.sum(qi * ki[s:s + 1, :] * e, axis=-1, keepdims=True)
            col = jnp.where(row >= s, col, 0.0)
            acc = acc + col * vi[s:s + 1, :]
        outs.append(acc)
    return outs[0] if len(outs) == 1 else jnp.concatenate(outs, axis=0)


def _offdiag(q, k, v, bc, blk, mask):
    c = q.shape[0]
    nb = c // blk
    qs, ks, vs = [jnp.zeros((blk, DK), F32)], [], []
    for i in range(1, nb):
        lo = i * blk
        ref = bc[lo:lo + 1, :]
        qs.append(q[lo:lo + blk] * jnp.exp(bc[lo:lo + blk] - ref))
        ks.append(k[:lo] * jnp.exp(ref - bc[:lo]))
        vs.append(v[:lo])
    qt = jnp.concatenate(qs, axis=0).astype(BF16)
    kt = jnp.concatenate(ks, axis=0).astype(BF16)
    vt = jnp.concatenate(vs, axis=0).astype(BF16)
    a = lax.dot_general(qt, kt, (((1,), (1,)), ((), ())), preferred_element_type=F32)
    return jnp.dot((a * mask).astype(BF16), vt, preferred_element_type=F32)


def _hgrn_inputs(qa, fa, ia, lb):
    fg = lb + (1.0 - lb) * jax.nn.sigmoid(fa)
    return _silu(qa.astype(F32)) * (DK ** -0.5), 1.0 - fg, ia, jnp.log(fg)


def _gla_inputs(qb, kb, vb, al, wa, ba):
    alpha = jnp.dot(al.astype(BF16), wa, preferred_element_type=F32) + ba
    logsig = jnp.minimum(alpha, 0.0) - jnp.log1p(jnp.exp(-jnp.abs(alpha)))
    return qb.astype(F32) * (DK ** -0.5), kb.astype(F32), vb, logsig / GATE_TEMP


def _head_out(o, gn, gate):
    return (_rms(o, gn) * _silu(gate.astype(F32))).astype(BF16)


def _lower_bound(lbl_ref):
    x = lbl_ref[...]
    e = jnp.exp(x - jnp.max(x, axis=0, keepdims=True))
    return e[0:1, :] / jnp.sum(e, axis=0, keepdims=True)


def _stack_mask(c, blk):
    nb = c // blk
    rows = blk * nb * (nb + 1) // 2
    t = lax.broadcasted_iota(jnp.int32, (c, rows), 0)
    r = lax.broadcasted_iota(jnp.int32, (c, rows), 1)
    seg = jnp.zeros((c, rows), jnp.int32)
    off = jnp.zeros((c, rows), jnp.int32)
    for i in range(1, nb):
        start = blk * i * (i + 1) // 2
        seg = seg + jnp.where(r >= start, 1, 0)
        off = jnp.where(r >= start, start, off)
    keep = ((t >> (blk.bit_length() - 1)) == seg) & (r - off <= t)
    return jnp.where(keep, 1.0, 0.0).astype(F32)


def _state_update_t(k, vb, bc, st):
    c = k.shape[0]
    blast = bc[c - 1:c, :]
    kdec = (k * jnp.exp(blast - bc)).astype(BF16)
    upd = lax.dot_general(vb, kdec, (((0,), (0,)), ((), ())), preferred_element_type=F32)
    if st is None:
        return upd
    return st * jnp.exp(blast) + upd


def _scan_sublanes(p):
    row = lax.broadcasted_iota(jnp.int32, (SUBLANES, DK), 0)
    for s in (1, 2, 4):
        p = p + jnp.where(row >= s, pltpu.roll(p, s, axis=0), 0.0)
    return p


def _cumsum_chunk(g):
    out, carry = [], None
    for j in range(CHUNK // SUBLANES):
        p = _scan_sublanes(g[j * SUBLANES:(j + 1) * SUBLANES])
        if carry is not None:
            p = p + carry
        carry = p[SUBLANES - 1:SUBLANES, :]
        out.append(p)
    return jnp.concatenate(out, axis=0)


def _rescaled_operands(q, k, bc):
    c, blk = CHUNK, DIAG_BLK
    qs, ks = [], []
    for i in range(c // blk):
        lo, hi = i * blk, (i + 1) * blk
        ref = bc[lo:lo + 1, :]
        qs.append(q[lo:hi] * jnp.exp(bc[lo:hi] - ref))
        ks.append(k[:hi] * jnp.exp(ref - bc[:hi]))
    return ((q * jnp.exp(bc)).astype(BF16), jnp.concatenate(qs, axis=0).astype(BF16),
            jnp.concatenate(ks, axis=0).astype(BF16),
            (k * jnp.exp(bc[c - 1:c, :] - bc)).astype(BF16))


def _chunk_exact(q, k, v, bc, st, mask):
    o = lax.dot_general((q * jnp.exp(bc)).astype(BF16), st.astype(BF16),
                        (((1,), (1,)), ((), ())), preferred_element_type=F32)
    o = o + _offdiag(q, k, v, bc, DIAG_BLK, mask) + _diag(q, k, v, bc, DIAG_BLK)
    return o, _state_update_t(k, v.astype(BF16), bc, st)


def _prompt_scan_kernel(*refs, gla, seq, meta):
    n_in = 11 if gla else 8
    if gla:
        q_ref, k_ref, v_ref, gate_ref, al_ref, mk_ref, mv_ref, mal_ref, wa_ref, ba_ref, gn_ref = (
            refs[:n_in])
    else:
        q_ref, f_ref, v_ref, gate_ref, mf_ref, mv_ref, lbl_ref, gn_ref = refs[:n_in]
        lb = _lower_bound(lbl_ref)
    o_ref, sout_ref = refs[n_in:n_in + 2]
    (st_ref, qs_ref, ks_ref, gs_ref, qd_ref, qt_ref, kt_ref, kdec_ref, eb_ref,
     meta_st_ref) = refs[n_in + 2:]
    trips = seq // (CHUNK * CHUNKS_PER_TRIP)
    stack = kt_ref.shape[0] // (seq // CHUNK)
    nt = (((1,), (1,)), ((), ()))
    tn = (((0,), (0,)), ((), ()))

    def chunk_rows(ci, u, n=CHUNK):
        return pl.ds(pl.multiple_of((ci * CHUNKS_PER_TRIP + u) * n, n), n)

    @pl.when(pl.program_id(1) == 0)
    def _meta():
        if gla:
            _, mk, mv, mg = _gla_inputs(mk_ref[...], mk_ref[...], mv_ref[...], mal_ref[...],
                                        wa_ref[...], ba_ref[...])
        else:
            _, mk, mv, mg = _hgrn_inputs(mf_ref[...], mf_ref[...], mv_ref[...], lb)
        mbc = jnp.dot(_tri(meta, meta), mg, precision=HIGHEST, preferred_element_type=F32)
        meta_st_ref[...] = _state_update_t(mk, mv.astype(BF16), mbc, None)

    st_ref[...] = meta_st_ref[...]

    def gate_pass(ci, carry):
        gmin, kmax = carry
        trip_rows = CHUNK * CHUNKS_PER_TRIP
        rows = pl.ds(pl.multiple_of(ci * trip_rows, trip_rows), trip_rows)
        if gla:
            q, k, _, g = _gla_inputs(q_ref[rows, :], k_ref[rows, :], None, al_ref[rows, :],
                                     wa_ref[...], ba_ref[...])
        else:
            q, k, _, g = _hgrn_inputs(q_ref[rows, :], f_ref[rows, :], None, lb)
        qs_ref[rows, :] = q
        ks_ref[rows, :] = k
        gs_ref[rows, :] = g
        return (jnp.minimum(gmin, jnp.min(g, axis=0, keepdims=True)),
                jnp.maximum(kmax, jnp.max(jnp.abs(k), axis=0, keepdims=True)))

    gmin, kmax = lax.fori_loop(0, trips, gate_pass,
                               (jnp.zeros((1, DK), F32), jnp.zeros((1, DK), F32)))
    in_range = (jnp.min(gmin) >= FAST_MIN_LOG_GATE) & (jnp.max(kmax) <= FAST_MAX_KEY)

    def rows_of(c, n=CHUNK):
        return pl.ds(pl.multiple_of(c * n, n), n)

    @pl.when(in_range)
    def _rescaled():
        mask = _stack_mask(CHUNK, DIAG_BLK)
        units = range(CHUNKS_PER_MXU_TRIP)
        mxu_trips = seq // (CHUNK * CHUNKS_PER_MXU_TRIP)

        def prepare(ti):
            for u in units:
                c = ti * CHUNKS_PER_MXU_TRIP + u
                bc = _cumsum_chunk(gs_ref[rows_of(c), :])
                qd, qt, kt, kdec = _rescaled_operands(qs_ref[rows_of(c), :], ks_ref[rows_of(c), :],
                                                      bc)
                qd_ref[rows_of(c), :] = qd
                qt_ref[rows_of(c), :] = qt
                kt_ref[rows_of(c, stack), :] = kt
                kdec_ref[rows_of(c), :] = kdec
                eb_ref[c] = jnp.broadcast_to(jnp.exp(bc[CHUNK - 1:CHUNK, :]), (SUBLANES, DK))

        def contract(ti):
            cs = [ti * CHUNKS_PER_MXU_TRIP + u for u in units]
            vb = [v_ref[rows_of(c), :].astype(BF16) for c in cs]
            a = [lax.dot_general(qt_ref[rows_of(c), :], kt_ref[rows_of(c, stack), :], nt,
                                 preferred_element_type=F32) for c in cs]
            am = [(x * mask).astype(BF16) for x in a]
            vt = [jnp.concatenate([x[:(i + 1) * DIAG_BLK] for i in range(CHUNK // DIAG_BLK)], axis=0)
                  for x in vb]
            oi = [jnp.dot(am[u], vt[u], preferred_element_type=F32) for u in units]
            inc = [lax.dot_general(vb[u], kdec_ref[rows_of(cs[u]), :], tn,
                                   preferred_element_type=F32) for u in units]
            sts = [st_ref[...]]
            for u in units:
                sts.append(sts[-1] * eb_ref[cs[u]][0:1, :] + inc[u])
            o = [oi[u] + lax.dot_general(qd_ref[rows_of(cs[u]), :], sts[u].astype(BF16), nt,
                                         preferred_element_type=F32) for u in units]
            for u in units:
                o_ref[rows_of(cs[u]), :] = _head_out(o[u], gn_ref[...], gate_ref[rows_of(cs[u]), :])
            st_ref[...] = sts[-1]

        prepare(0)

        def body(ti, carry):
            contract(ti)
            prepare(ti + 1)
            return carry

        lax.fori_loop(0, mxu_trips - 1, body, 0)
        contract(mxu_trips - 1)

    @pl.when(jnp.logical_not(in_range))
    def _exact():
        mask = _off_mask(CHUNK, DIAG_BLK)

        def body(ci, carry):
            st = st_ref[...]
            for u in range(CHUNKS_PER_TRIP):
                rows = chunk_rows(ci, u)
                o, st = _chunk_exact(qs_ref[rows, :], ks_ref[rows, :], v_ref[rows, :],
                                     _cumsum_chunk(gs_ref[rows, :]), st, mask)
                o_ref[rows, :] = _head_out(o, gn_ref[...], gate_ref[rows, :])
            st_ref[...] = st
            return carry

        lax.fori_loop(0, trips, body, 0)

    sout_ref[...] = st_ref[...].T


def _sample_scan_kernel(*refs, gla, steps):
    if gla:
        (q_ref, k_ref, v_ref, gate_ref, al_ref, wa_ref, ba_ref, gn_ref, s0_ref,
         o_ref, sout_ref) = refs
        q, k, v, g = _gla_inputs(q_ref[...], k_ref[...], v_ref[...], al_ref[...],
                                 wa_ref[...], ba_ref[...])
    else:
        (q_ref, f_ref, v_ref, gate_ref, lbl_ref, gn_ref, s0_ref, o_ref, sout_ref) = refs
        q, k, v, g = _hgrn_inputs(q_ref[...], f_ref[...], v_ref[...], _lower_bound(lbl_ref))
    v32 = v.astype(F32)
    rows = SAMPLE_SEQS * steps
    groups = [slice(b * steps, (b + 1) * steps) for b in range(SAMPLE_SEQS)]
    tn = (((0,), (0,)), ((), ()))

    bc = jnp.concatenate([_scan_sublanes(g[sl]) for sl in groups], axis=0)

    def repeated(r):
        return jnp.concatenate([jnp.broadcast_to(bc[sl][r:r + 1, :], (steps, DK)) for sl in groups],
                               axis=0)

    qd = (q * jnp.exp(bc)).astype(BF16)
    kdec = (k * jnp.exp(repeated(steps - 1) - bc)).astype(BF16)
    r = lax.broadcasted_iota(jnp.int32, (rows, SAMPLE_SEQS), 0)
    c = lax.broadcasted_iota(jnp.int32, (rows, SAMPLE_SEQS), 1)
    selector = jnp.where((r >> (steps.bit_length() - 1)) == c, 1.0, 0.0).astype(F32)
    decay = jnp.exp(lax.dot_general(g, selector, tn, precision=HIGHEST,
                                    preferred_element_type=F32))

    inter = [jnp.dot(qd[sl], s0_ref[b].astype(BF16), preferred_element_type=F32)
             for b, sl in enumerate(groups)]
    upd = [lax.dot_general(kdec[sl], v32[sl].astype(BF16), tn, preferred_element_type=F32)
           for sl in groups]
    for b in range(SAMPLE_SEQS):
        sout_ref[b] = decay[:, b:b + 1] * s0_ref[b] + upd[b]
    o = jnp.concatenate(inter, axis=0) + _diag(q, k, v32, bc, steps)
    o_ref[...] = _head_out(o, gn_ref[...], gate_ref[...])


def _proj_spec(rows, width, c0, per_head, row_block, tile_width=TN_PROJ, head_major=False):
    per_tile = tile_width // width

    def index(i0, i1):
        b, h = (i1, i0) if head_major else (i0, i1)
        g = c0 * LANES // width + h * per_head
        return (g // per_tile, row_block(b), g % per_tile)

    return pl.BlockSpec((None, rows, width), index)


def _prompt_scan(p, pm, *, gla, batch, seq, meta_block, small, name):
    heads, dv = (H_B, DV_B) if gla else (H_A, DV_A)
    n_chunks = seq // CHUNK
    blocks = CHUNK // DIAG_BLK
    stack = DIAG_BLK * blocks * (blocks + 1) // 2
    p16, pf, pa = p
    pm16, pmf, pma = pm
    tok = functools.partial(_proj_spec, seq, row_block=lambda b: b, head_major=True)
    met = functools.partial(_proj_spec, N_META, row_block=lambda b: meta_block, head_major=True)
    vec = lambda width: pl.BlockSpec((1, width), lambda h, b: (0, h))
    if gla:
        wa, ba, gn = small
        in_specs = [tok(DK, COL_QB, 1), tok(DK, COL_KB, 1), tok(dv, COL_VB, 1), tok(dv, COL_RB, 1),
                    tok(LANES, 0, 0, tile_width=LANES),
                    met(DK, COL_KB, 1), met(dv, COL_VB, 1), met(LANES, 0, 0, tile_width=LANES),
                    pl.BlockSpec((LANES, DK), lambda h, b: (0, h)), vec(DK), vec(dv)]
        args = (p16, p16, p16, p16, pa, pm16, pm16, pma, wa, ba, gn)
    else:
        lbl, gn = small
        in_specs = [tok(DK, COL_QA, 1), tok(DK, 0, 1), tok(dv, COL_IA, 1), tok(dv, COL_GA, 1),
                    met(DK, 0, 1), met(dv, COL_IA, 1),
                    pl.BlockSpec((2, DK), lambda h, b: (0, h)), vec(dv)]
        args = (p16, pf, p16, p16, pmf, pm16, lbl, gn)
    return pl.pallas_call(
        functools.partial(_prompt_scan_kernel, gla=gla, seq=seq, meta=N_META),
        grid=(heads, batch),
        in_specs=in_specs,
        out_specs=[pl.BlockSpec((seq, dv), lambda h, b: (b, h)),
                   pl.BlockSpec((None, None, DK, dv), lambda h, b: (b, h, 0, 0))],
        out_shape=[jax.ShapeDtypeStruct((batch * seq, heads * dv), BF16),
                   jax.ShapeDtypeStruct((batch, heads, DK, dv), F32)],
        scratch_shapes=[
            pltpu.VMEM((dv, DK), F32),
            pltpu.VMEM((seq, DK), F32),
            pltpu.VMEM((seq, DK), F32),
            pltpu.VMEM((seq, DK), F32),
            pltpu.VMEM((seq, DK), BF16),
            pltpu.VMEM((seq, DK), BF16),
            pltpu.VMEM((n_chunks * stack, DK), BF16),
            pltpu.VMEM((seq, DK), BF16),
            pltpu.VMEM((n_chunks, SUBLANES, DK), F32),
            pltpu.VMEM((dv, DK), F32),
        ],
        compiler_params=_params("parallel", "arbitrary"),
        name=name,
    )(*args)


def _sample_scan(p, s0, *, gla, batch, steps, small, name):
    assert steps == SUBLANES, "the sample recurrence keeps one sequence per sublane group"
    heads, dv = (H_B, DV_B) if gla else (H_A, DV_A)
    rows = SAMPLE_SEQS * steps
    p16, pf, pa = p
    tok = functools.partial(_proj_spec, rows, row_block=lambda b: b)
    vec = lambda width: pl.BlockSpec((1, width), lambda b, h: (0, h))
    state = pl.BlockSpec((SAMPLE_SEQS, None, DK, dv), lambda b, h: (b, h, 0, 0))
    if gla:
        wa, ba, gn = small
        in_specs = [tok(DK, COL_QB, 1), tok(DK, COL_KB, 1), tok(dv, COL_VB, 1), tok(dv, COL_RB, 1),
                    tok(LANES, 0, 0, tile_width=LANES),
                    pl.BlockSpec((LANES, DK), lambda b, h: (0, h)), vec(DK), vec(dv), state]
        args = (p16, p16, p16, p16, pa, wa, ba, gn, s0)
    else:
        lbl, gn = small
        in_specs = [tok(DK, COL_QA, 1), tok(DK, 0, 1), tok(dv, COL_IA, 1), tok(dv, COL_GA, 1),
                    pl.BlockSpec((2, DK), lambda b, h: (0, h)), vec(dv), state]
        args = (p16, pf, p16, p16, lbl, gn, s0)
    return pl.pallas_call(
        functools.partial(_sample_scan_kernel, gla=gla, steps=steps),
        grid=(batch // SAMPLE_SEQS, heads),
        in_specs=in_specs,
        out_specs=[pl.BlockSpec((rows, dv), lambda b, h: (b, h)), state],
        out_shape=[jax.ShapeDtypeStruct((batch * steps, heads * dv), BF16),
                   jax.ShapeDtypeStruct((batch, heads, DK, dv), F32)],
        compiler_params=_params("parallel", "parallel"),
        name=name,
    )(*args)


def kernel(x_prompt, x_sample, state_hgrn, state_gla, meta_tokens, lb_logits, ffn1_norm, w_ffn1_in,
           w_ffn1_out, mix_norm, w_in, w_alpha_up, b_alpha, gnorm_a, gnorm_b, w_out, ffn2_norm,
           w_ffn2_in, w_ffn2_out, final_norm):
    batch, seq, _ = x_prompt.shape
    dec_batch, steps, _ = x_sample.shape

    w_in_t = w_in[0].T
    wop = w_out[0].astype(BF16)
    wa = jnp.pad(w_alpha_up[0], ((0, LANES - ALPHA_RANK), (0, 0))).astype(BF16)
    n1, nm, n2, nf = ffn1_norm, mix_norm, ffn2_norm, final_norm[None]
    small_a = (lb_logits, gnorm_a)
    small_b = (wa, b_alpha, gnorm_b)
    n_sample = dec_batch * steps

    xs = jnp.concatenate([x_sample.reshape(n_sample, D_MODEL), meta_tokens], axis=0)
    h1s, u1s, *w1 = _ffn(xs, n1, (w_ffn1_in[0], w_ffn1_in[0], w_ffn1_out[0]), nm, tm=xs.shape[0],
                         tf=TF_CAST, final_norm=False, emit_weights=True, name="ffn1_sample")
    ps = _in_proj(u1s, w_in_t, tm=xs.shape[0], name="inproj_sample")
    oa_s, sa_s = _sample_scan(ps, state_hgrn[0], gla=False, batch=dec_batch, steps=steps,
                              small=small_a, name="scan_hgrn_sample")
    ob_s, sb_s = _sample_scan(ps, state_gla[0], gla=True, batch=dec_batch, steps=steps,
                              small=small_b, name="scan_gla_sample")
    h2s = _out_proj(oa_s, ob_s, h1s, wop, tm=TM, name="outproj_sample")
    y_s, *w2 = _ffn(h2s, n2, (w_ffn2_in[0], w_ffn2_in[0], w_ffn2_out[0]), nf, tm=n_sample,
                    tf=TF_CAST, final_norm=True, emit_weights=True, name="ffn2_sample")

    h1p, u1p = _ffn(x_prompt.reshape(batch * seq, D_MODEL), n1, w1, nm, tm=TM_FFN, tf=TF,
                    final_norm=False, emit_weights=False, name="ffn1_prompt")
    pp = _in_proj(u1p, w_in_t, tm=TM_PROJ, name="inproj_prompt")
    meta_block = n_sample // N_META
    oa_p, sa_p = _prompt_scan(pp, ps, gla=False, batch=batch, seq=seq, meta_block=meta_block,
                              small=small_a, name="scan_hgrn_prompt")
    ob_p, sb_p = _prompt_scan(pp, ps, gla=True, batch=batch, seq=seq, meta_block=meta_block,
                              small=small_b, name="scan_gla_prompt")
    h2p = _out_proj(oa_p, ob_p, h1p, wop, tm=TM, name="outproj_prompt")
    (y_p,) = _ffn(h2p, n2, w2, nf, tm=TM_FFN, tf=TF, final_norm=True, emit_weights=False,
                  name="ffn2_prompt")

    return (y_p.reshape(batch, seq, D_MODEL), y_s.reshape(dec_batch, steps, D_MODEL),
            sa_p[None], sb_p[None], sa_s[None], sb_s[None])
```

```python
import functools

import jax
import jax.numpy as jnp
from jax import lax
from jax.experimental import pallas as pl
from jax.experimental.pallas import tpu as pltpu

F32 = jnp.float32
BF16 = jnp.bfloat16
HIGHEST = lax.Precision.HIGHEST

D_MODEL = 2048
N_META = 16
DK = 128
H_A, DV_A = 8, 128
H_B, DV_B = 4, 256
D_HALF = H_A * DV_A
ALPHA_RANK = 16
GATE_TEMP = 16.0
D_FF = 5632
CHUNK = 64
EPS = 1e-6
IN_PROJ_WIDTHS = (D_HALF, D_HALF, D_HALF, D_HALF, H_B * DK, H_B * DK, D_HALF, D_HALF, ALPHA_RANK)

LANES = 128
SUBLANES = 8
VMEM_LIMIT_BYTES = 56 * 1024 * 1024

COL_QA, COL_IA, COL_GA, COL_QB, COL_KB, COL_VB, COL_RB = 0, 8, 16, 24, 28, 32, 40
P16_COLS = 3 * D_HALF + 2 * H_B * DK + 2 * D_HALF

TM = 512
TM_FFN = 1024
TF = 256
TF_CAST = 256
TN_PROJ = 512
TM_PROJ = 2048
N16_TILES = P16_COLS // TN_PROJ
DIAG_BLK = 16
SAMPLE_SEQS = 32
CHUNKS_PER_TRIP = 4
CHUNKS_PER_MXU_TRIP = 8

FAST_MIN_LOG_GATE = -60.0 / (DIAG_BLK - 1)
FAST_MAX_KEY = 1e10


def _rms(x, w):
    return x * lax.rsqrt(jnp.mean(x * x, axis=-1, keepdims=True) + EPS) * w


def _silu(x):
    return x * jax.nn.sigmoid(x)


def _params(*sem):
    return pltpu.CompilerParams(dimension_semantics=sem, vmem_limit_bytes=VMEM_LIMIT_BYTES)


def _ffn_kernel(*refs, n_pieces, final_norm, emit_weights):
    h_refs = refs[:n_pieces]
    nw_ref, wg_ref, wu_ref, wo_ref, tnw_ref, out_ref, *rest = refs[n_pieces:]
    u_ref = rest[-1]
    j = pl.program_id(1)
    last_j = pl.num_programs(1) - 1

    def load_h():
        pieces = [r[...] for r in h_refs]
        return pieces[0] if n_pieces == 1 else jnp.concatenate(pieces, axis=0)

    def step(first, last):
        if first:
            u = _rms(load_h(), nw_ref[...]).astype(BF16)
            u_ref[...] = u
        else:
            u = u_ref[...]
        wg, wu, wo = (r[...].astype(BF16) for r in (wg_ref, wu_ref, wo_ref))
        if emit_weights:
            for dst, w in zip(rest[-4:-1], (wg, wu, wo)):
                dst[...] = w
        g = jnp.dot(u, wg, preferred_element_type=F32)
        up = jnp.dot(u, wu, preferred_element_type=F32)
        a = (_silu(g) * up).astype(BF16)
        acc = jnp.dot(a, wo, preferred_element_type=F32)
        if not first:
            acc = out_ref[...] + acc
        if not last:
            out_ref[...] = acc
            return
        y = load_h() + 0.5 * acc
        normed = _rms(y, tnw_ref[...])
        if final_norm:
            out_ref[...] = normed
        else:
            out_ref[...] = y
            rest[0][...] = normed.astype(BF16)

    pl.when(j == 0)(functools.partial(step, True, False))
    pl.when((j > 0) & (j < last_j))(functools.partial(step, False, False))
    pl.when(j == last_j)(functools.partial(step, False, True))


def _ffn(h, norm_w, weights, tail_norm_w, *, tm, tf, final_norm, emit_weights, name):
    pieces = h if isinstance(h, tuple) else (h,)
    m = sum(p.shape[0] for p in pieces)
    nf = D_FF // tf
    rows = pl.BlockSpec((tm, D_MODEL), lambda i, j: (i, 0),
                        pipeline_mode=pl.Buffered(1) if m == tm else None)
    if len(pieces) == 1:
        h_specs = [rows]
    else:
        assert m == tm, "stacked inputs form one row tile"
        h_specs = [pl.BlockSpec((p.shape[0], D_MODEL), lambda i, j: (0, 0),
                                pipeline_mode=pl.Buffered(1)) for p in pieces]
    vec = pl.BlockSpec((1, D_MODEL), lambda i, j: (0, 0))
    col_tile = pl.BlockSpec((D_MODEL, tf), lambda i, j: (0, j))
    row_tile = pl.BlockSpec((tf, D_MODEL), lambda i, j: (j, 0))
    out_specs = [rows]
    out_shape = [jax.ShapeDtypeStruct((m, D_MODEL), F32)]
    if not final_norm:
        out_specs += [rows]
        out_shape += [jax.ShapeDtypeStruct((m, D_MODEL), BF16)]
    if emit_weights:
        assert m == tm, "weight casts are written once, by a single row tile"
        up_tile = pl.BlockSpec((D_MODEL, tf), lambda i, j: (0, j + nf))
        out_specs += [col_tile, col_tile, row_tile]
        out_shape += [jax.ShapeDtypeStruct((D_MODEL, D_FF), BF16)] * 2
        out_shape += [jax.ShapeDtypeStruct((D_FF, D_MODEL), BF16)]
    else:
        up_tile = col_tile
    return pl.pallas_call(
        functools.partial(_ffn_kernel, n_pieces=len(pieces), final_norm=final_norm,
                          emit_weights=emit_weights),
        grid=(m // tm, nf),
        in_specs=h_specs + [vec, col_tile, up_tile, row_tile, vec],
        out_specs=out_specs,
        out_shape=out_shape,
        scratch_shapes=[pltpu.VMEM((tm, D_MODEL), BF16)],
        compiler_params=_params("parallel", "arbitrary"),
        name=name,
    )(*pieces, norm_w, *weights, tail_norm_w)


def _proj_kernel(u_ref, w_ref, wal_ref, p16_ref, pf_ref, pa_ref):
    j = pl.program_id(1)
    nt = (((1,), (1,)), ((), ()))

    @pl.when(j == 0)
    def _low_rank():
        wal = jnp.concatenate([wal_ref[...].astype(BF16),
                               jnp.zeros((LANES - ALPHA_RANK, D_MODEL), BF16)], axis=0)
        pa_ref[...] = lax.dot_general(u_ref[...], wal, nt, preferred_element_type=F32)

    def project():
        return lax.dot_general(u_ref[...], w_ref[...].astype(BF16), nt,
                               preferred_element_type=F32)

    @pl.when(j < N16_TILES)
    def _narrow():
        p16_ref[...] = project().astype(BF16)

    @pl.when(j >= N16_TILES)
    def _wide():
        pf_ref[...] = project()


def _w_in_tile(j):
    fa_lo, fa_hi = D_HALF // TN_PROJ, 2 * D_HALF // TN_PROJ
    moved = fa_hi - fa_lo
    return jnp.where(j < fa_lo, j, jnp.where(j < N16_TILES, j + moved, j - N16_TILES + fa_lo))


def _in_proj(u, w, *, tm, name):
    m = u.shape[0]
    return pl.pallas_call(
        _proj_kernel,
        grid=(m // tm, (P16_COLS + D_HALF) // TN_PROJ),
        in_specs=[
            pl.BlockSpec((tm, D_MODEL), lambda i, j: (i, 0)),
            pl.BlockSpec((TN_PROJ, D_MODEL), lambda i, j: (_w_in_tile(j), 0)),
            pl.BlockSpec((ALPHA_RANK, D_MODEL), lambda i, j: ((P16_COLS + D_HALF) // ALPHA_RANK, 0)),
        ],
        out_specs=[
            pl.BlockSpec((None, tm, TN_PROJ), lambda i, j: (jnp.minimum(j, N16_TILES - 1), i, 0)),
            pl.BlockSpec((None, tm, TN_PROJ), lambda i, j: (jnp.maximum(j - N16_TILES, 0), i, 0)),
            pl.BlockSpec((None, tm, LANES), lambda i, j: (0, i, 0)),
        ],
        out_shape=[jax.ShapeDtypeStruct((N16_TILES, m, TN_PROJ), BF16),
                   jax.ShapeDtypeStruct((D_HALF // TN_PROJ, m, TN_PROJ), F32),
                   jax.ShapeDtypeStruct((1, m, LANES), F32)],
        compiler_params=_params("parallel", "arbitrary"),
        name=name,
    )(u, w, w)


def _out_proj_kernel(oa_ref, ob_ref, h_ref, wa_ref, wb_ref, out_ref):
    out_ref[...] = (h_ref[...]
                    + jnp.dot(oa_ref[...], wa_ref[...].astype(BF16), preferred_element_type=F32)
                    + jnp.dot(ob_ref[...], wb_ref[...].astype(BF16), preferred_element_type=F32))


def _out_proj(oa, ob, h, w, *, tm, name):
    m = oa.shape[0]
    return pl.pallas_call(
        _out_proj_kernel,
        grid=(m // tm,),
        in_specs=[
            pl.BlockSpec((tm, D_HALF), lambda i: (i, 0)),
            pl.BlockSpec((tm, D_HALF), lambda i: (i, 0)),
            pl.BlockSpec((tm, D_MODEL), lambda i: (i, 0)),
            pl.BlockSpec((D_HALF, D_MODEL), lambda i: (0, 0), pipeline_mode=pl.Buffered(1)),
            pl.BlockSpec((D_HALF, D_MODEL), lambda i: (1, 0), pipeline_mode=pl.Buffered(1)),
        ],
        out_specs=pl.BlockSpec((tm, D_MODEL), lambda i: (i, 0)),
        out_shape=jax.ShapeDtypeStruct((m, D_MODEL), F32),
        compiler_params=_params("parallel"),
        name=name,
    )(oa, ob, h, w, w)


def _tri(c, seq_len):
    r = lax.broadcasted_iota(jnp.int32, (c, c), 0)
    s = lax.broadcasted_iota(jnp.int32, (c, c), 1)
    return jnp.where((r >= s) & ((r ^ s) < seq_len), 1.0, 0.0).astype(F32)


def _off_mask(c, blk):
    nb = c // blk
    rows = nb * (nb - 1) // 2 * blk
    r = lax.broadcasted_iota(jnp.int32, (c, rows), 0) >> (blk.bit_length() - 1)
    s = lax.broadcasted_iota(jnp.int32, (c, rows), 1)
    seg = jnp.zeros((c, rows), jnp.int32)
    for i in range(1, nb):
        seg = seg + jnp.where(s >= blk * i * (i - 1) // 2, 1, 0)
    return jnp.where(r == seg, 1.0, 0.0).astype(F32)


def _diag(q, k, v, bc, blk):
    c = q.shape[0]
    row = lax.broadcasted_iota(jnp.int32, (blk, 1), 0)
    outs = []
    for i in range(c // blk):
        sl = slice(i * blk, (i + 1) * blk)
        qi, ki, vi, bi = q[sl], k[sl], v[sl].astype(F32), bc[sl]
        acc = jnp.zeros((blk, v.shape[1]), F32)
        for s in range(blk):
            e = jnp.exp(jnp.minimum(bi - bi[s:s + 1, :], 0.0))
            col = jnp.sum(qi * ki[s:s + 1, :] * e, axis=-1, keepdims=True)
            col = jnp.where(row >= s, col, 0.0)
            acc = acc + col * vi[s:s + 1, :]
        outs.append(acc)
    return outs[0] if len(outs) == 1 else jnp.concatenate(outs, axis=0)


def _offdiag(q, k, v, bc, blk, mask):
    c = q.shape[0]
    nb = c // blk
    qs, ks, vs = [jnp.zeros((blk, DK), F32)], [], []
    for i in range(1, nb):
        lo = i * blk
        ref = bc[lo:lo + 1, :]
        qs.append(q[lo:lo + blk] * jnp.exp(bc[lo:lo + blk] - ref))
        ks.append(k[:lo] * jnp.exp(ref - bc[:lo]))
        vs.append(v[:lo])
    qt = jnp.concatenate(qs, axis=0).astype(BF16)
    kt = jnp.concatenate(ks, axis=0).astype(BF16)
    vt = jnp.concatenate(vs, axis=0).astype(BF16)
    a = lax.dot_general(qt, kt, (((1,), (1,)), ((), ())), preferred_element_type=F32)
    return jnp.dot((a * mask).astype(BF16), vt, preferred_element_type=F32)


def _hgrn_inputs(qa, fa, ia, lb):
    fg = lb + (1.0 - lb) * jax.nn.sigmoid(fa)
    return _silu(qa.astype(F32)) * (DK ** -0.5), 1.0 - fg, ia, jnp.log(fg)


def _gla_inputs(qb, kb, vb, al, wa, ba):
    alpha = jnp.dot(al.astype(BF16), wa, preferred_element_type=F32) + ba
    logsig = jnp.minimum(alpha, 0.0) - jnp.log1p(jnp.exp(-jnp.abs(alpha)))
    return qb.astype(F32) * (DK ** -0.5), kb.astype(F32), vb, logsig / GATE_TEMP


def _head_out(o, gn, gate):
    return (_rms(o, gn) * _silu(gate.astype(F32))).astype(BF16)


def _lower_bound(lbl_ref):
    x = lbl_ref[...]
    e = jnp.exp(x - jnp.max(x, axis=0, keepdims=True))
    return e[0:1, :] / jnp.sum(e, axis=0, keepdims=True)


def _stack_mask(c, blk):
    nb = c // blk
    rows = blk * nb * (nb + 1) // 2
    t = lax.broadcasted_iota(jnp.int32, (c, rows), 0)
    r = lax.broadcasted_iota(jnp.int32, (c, rows), 1)
    seg = jnp.zeros((c, rows), jnp.int32)
    off = jnp.zeros((c, rows), jnp.int32)
    for i in range(1, nb):
        start = blk * i * (i + 1) // 2
        seg = seg + jnp.where(r >= start, 1, 0)
        off = jnp.where(r >= start, start, off)
    keep = ((t >> (blk.bit_length() - 1)) == seg) & (r - off <= t)
    return jnp.where(keep, 1.0, 0.0).astype(F32)


def _state_update_t(k, vb, bc, st):
    c = k.shape[0]
    blast = bc[c - 1:c, :]
    kdec = (k * jnp.exp(blast - bc)).astype(BF16)
    upd = lax.dot_general(vb, kdec, (((0,), (0,)), ((), ())), preferred_element_type=F32)
    if st is None:
        return upd
    return st * jnp.exp(blast) + upd


def _scan_sublanes(p):
    row = lax.broadcasted_iota(jnp.int32, (SUBLANES, DK), 0)
    for s in (1, 2, 4):
        p = p + jnp.where(row >= s, pltpu.roll(p, s, axis=0), 0.0)
    return p


def _cumsum_chunk(g):
    out, carry = [], None
    for j in range(CHUNK // SUBLANES):
        p = _scan_sublanes(g[j * SUBLANES:(j + 1) * SUBLANES])
        if carry is not None:
            p = p + carry
        carry = p[SUBLANES - 1:SUBLANES, :]
        out.append(p)
    return jnp.concatenate(out, axis=0)


def _rescaled_operands(q, k, bc):
    c, blk = CHUNK, DIAG_BLK
    qs, ks = [], []
    for i in range(c // blk):
        lo, hi = i * blk, (i + 1) * blk
        ref = bc[lo:lo + 1, :]
        qs.append(q[lo:hi] * jnp.exp(bc[lo:hi] - ref))
        ks.append(k[:hi] * jnp.exp(ref - bc[:hi]))
    return ((q * jnp.exp(bc)).astype(BF16), jnp.concatenate(qs, axis=0).astype(BF16),
            jnp.concatenate(ks, axis=0).astype(BF16),
            (k * jnp.exp(bc[c - 1:c, :] - bc)).astype(BF16))


def _chunk_exact(q, k, v, bc, st, mask):
    o = lax.dot_general((q * jnp.exp(bc)).astype(BF16), st.astype(BF16),
                        (((1,), (1,)), ((), ())), preferred_element_type=F32)
    o = o + _offdiag(q, k, v, bc, DIAG_BLK, mask) + _diag(q, k, v, bc, DIAG_BLK)
    return o, _state_update_t(k, v.astype(BF16), bc, st)


def _prompt_scan_kernel(*refs, gla, seq, meta):
    n_in = 11 if gla else 8
    if gla:
        q_ref, k_ref, v_ref, gate_ref, al_ref, mk_ref, mv_ref, mal_ref, wa_ref, ba_ref, gn_ref = (
            refs[:n_in])
    else:
        q_ref, f_ref, v_ref, gate_ref, mf_ref, mv_ref, lbl_ref, gn_ref = refs[:n_in]
        lb = _lower_bound(lbl_ref)
    o_ref, sout_ref = refs[n_in:n_in + 2]
    (st_ref, qs_ref, ks_ref, gs_ref, qd_ref, qt_ref, kt_ref, kdec_ref, eb_ref,
     meta_st_ref) = refs[n_in + 2:]
    trips = seq // (CHUNK * CHUNKS_PER_TRIP)
    stack = kt_ref.shape[0] // (seq // CHUNK)
    nt = (((1,), (1,)), ((), ()))
    tn = (((0,), (0,)), ((), ()))

    def chunk_rows(ci, u, n=CHUNK):
        return pl.ds(pl.multiple_of((ci * CHUNKS_PER_TRIP + u) * n, n), n)

    @pl.when(pl.program_id(1) == 0)
    def _meta():
        if gla:
            _, mk, mv, mg = _gla_inputs(mk_ref[...], mk_ref[...], mv_ref[...], mal_ref[...],
                                        wa_ref[...], ba_ref[...])
        else:
            _, mk, mv, mg = _hgrn_inputs(mf_ref[...], mf_ref[...], mv_ref[...], lb)
        mbc = jnp.dot(_tri(meta, meta), mg, precision=HIGHEST, preferred_element_type=F32)
        meta_st_ref[...] = _state_update_t(mk, mv.astype(BF16), mbc, None)

    st_ref[...] = meta_st_ref[...]

    def rows_of(c, n=CHUNK):
        return pl.ds(pl.multiple_of(c * n, n), n)

    def rescaled_pass():
        mask = _stack_mask(CHUNK, DIAG_BLK)
        units = range(CHUNKS_PER_MXU_TRIP)
        mxu_trips = seq // (CHUNK * CHUNKS_PER_MXU_TRIP)

        def prepare(ti, extremes):
            gmin, kmax = extremes
            trip_rows = CHUNK * CHUNKS_PER_MXU_TRIP
            rows = pl.ds(pl.multiple_of(ti * trip_rows, trip_rows), trip_rows)
            if gla:
                q, k, _, g = _gla_inputs(q_ref[rows, :], k_ref[rows, :], None, al_ref[rows, :],
                                         wa_ref[...], ba_ref[...])
            else:
                q, k, _, g = _hgrn_inputs(q_ref[rows, :], f_ref[rows, :], None, lb)
            qs_ref[rows, :] = q
            ks_ref[rows, :] = k
            gs_ref[rows, :] = g
            for u in units:
                c = ti * CHUNKS_PER_MXU_TRIP + u
                sl = slice(u * CHUNK, (u + 1) * CHUNK)
                bc = _cumsum_chunk(g[sl])
                qd, qt, kt, kdec = _rescaled_operands(q[sl], k[sl], bc)
                qd_ref[rows_of(c), :] = qd
                qt_ref[rows_of(c), :] = qt
                kt_ref[rows_of(c, stack), :] = kt
                kdec_ref[rows_of(c), :] = kdec
                eb_ref[c] = jnp.broadcast_to(jnp.exp(bc[CHUNK - 1:CHUNK, :]), (SUBLANES, DK))
            return (jnp.minimum(gmin, jnp.min(g, axis=0, keepdims=True)),
                    jnp.maximum(kmax, jnp.max(jnp.abs(k), axis=0, keepdims=True)))

        def contract(ti):
            cs = [ti * CHUNKS_PER_MXU_TRIP + u for u in units]
            vb = [v_ref[rows_of(c), :].astype(BF16) for c in cs]
            a = [lax.dot_general(qt_ref[rows_of(c), :], kt_ref[rows_of(c, stack), :], nt,
                                 preferred_element_type=F32) for c in cs]
            am = [(x * mask).astype(BF16) for x in a]
            vt = [jnp.concatenate([x[:(i + 1) * DIAG_BLK] for i in range(CHUNK // DIAG_BLK)], axis=0)
                  for x in vb]
            oi = [jnp.dot(am[u], vt[u], preferred_element_type=F32) for u in units]
            inc = [lax.dot_general(vb[u], kdec_ref[rows_of(cs[u]), :], tn,
                                   preferred_element_type=F32) for u in units]
            sts = [st_ref[...]]
            for u in units:
                sts.append(sts[-1] * eb_ref[cs[u]][0:1, :] + inc[u])
            o = [oi[u] + lax.dot_general(qd_ref[rows_of(cs[u]), :], sts[u].astype(BF16), nt,
                                         preferred_element_type=F32) for u in units]
            for u in units:
                o_ref[rows_of(cs[u]), :] = _head_out(o[u], gn_ref[...], gate_ref[rows_of(cs[u]), :])
            st_ref[...] = sts[-1]

        extremes = prepare(0, (jnp.zeros((1, DK), F32), jnp.zeros((1, DK), F32)))

        def body(ti, extremes):
            contract(ti)
            return prepare(ti + 1, extremes)

        extremes = lax.fori_loop(0, mxu_trips - 1, body, extremes)
        contract(mxu_trips - 1)
        return extremes

    gmin, kmax = rescaled_pass()
    in_range = (jnp.min(gmin) >= FAST_MIN_LOG_GATE) & (jnp.max(kmax) <= FAST_MAX_KEY)

    @pl.when(jnp.logical_not(in_range))
    def _exact():
        mask = _off_mask(CHUNK, DIAG_BLK)
        st_ref[...] = meta_st_ref[...]

        def body(ci, carry):
            st = st_ref[...]
            for u in range(CHUNKS_PER_TRIP):
                rows = chunk_rows(ci, u)
                o, st = _chunk_exact(qs_ref[rows, :], ks_ref[rows, :], v_ref[rows, :],
                                     _cumsum_chunk(gs_ref[rows, :]), st, mask)
                o_ref[rows, :] = _head_out(o, gn_ref[...], gate_ref[rows, :])
            st_ref[...] = st
            return carry

        lax.fori_loop(0, trips, body, 0)

    sout_ref[...] = st_ref[...].T


def _sample_scan_kernel(*refs, gla, steps):
    if gla:
        (q_ref, k_ref, v_ref, gate_ref, al_ref, wa_ref, ba_ref, gn_ref, s0_ref,
         o_ref, sout_ref) = refs
        q, k, v, g = _gla_inputs(q_ref[...], k_ref[...], v_ref[...], al_ref[...],
                                 wa_ref[...], ba_ref[...])
    else:
        (q_ref, f_ref, v_ref, gate_ref, lbl_ref, gn_ref, s0_ref, o_ref, sout_ref) = refs
        q, k, v, g = _hgrn_inputs(q_ref[...], f_ref[...], v_ref[...], _lower_bound(lbl_ref))
    v32 = v.astype(F32)
    rows = SAMPLE_SEQS * steps
    groups = [slice(b * steps, (b + 1) * steps) for b in range(SAMPLE_SEQS)]
    tn = (((0,), (0,)), ((), ()))

    bc = jnp.concatenate([_scan_sublanes(g[sl]) for sl in groups], axis=0)

    def repeated(r):
        return jnp.concatenate([jnp.broadcast_to(bc[sl][r:r + 1, :], (steps, DK)) for sl in groups],
                               axis=0)

    qd = (q * jnp.exp(bc)).astype(BF16)
    kdec = (k * jnp.exp(repeated(steps - 1) - bc)).astype(BF16)
    r = lax.broadcasted_iota(jnp.int32, (rows, SAMPLE_SEQS), 0)
    c = lax.broadcasted_iota(jnp.int32, (rows, SAMPLE_SEQS), 1)
    selector = jnp.where((r >> (steps.bit_length() - 1)) == c, 1.0, 0.0).astype(F32)
    decay = jnp.exp(lax.dot_general(g, selector, tn, precision=HIGHEST,
                                    preferred_element_type=F32))

    inter = [jnp.dot(qd[sl], s0_ref[b].astype(BF16), preferred_element_type=F32)
             for b, sl in enumerate(groups)]
    upd = [lax.dot_general(kdec[sl], v32[sl].astype(BF16), tn, preferred_element_type=F32)
           for sl in groups]
    for b in range(SAMPLE_SEQS):
        sout_ref[b] = decay[:, b:b + 1] * s0_ref[b] + upd[b]
    o = jnp.concatenate(inter, axis=0) + _diag(q, k, v32, bc, steps)
    o_ref[...] = _head_out(o, gn_ref[...], gate_ref[...])


def _proj_spec(rows, width, c0, per_head, row_block, tile_width=TN_PROJ, head_major=False):
    per_tile = tile_width // width

    def index(i0, i1):
        b, h = (i1, i0) if head_major else (i0, i1)
        g = c0 * LANES // width + h * per_head
        return (g // per_tile, row_block(b), g % per_tile)

    return pl.BlockSpec((None, rows, width), index)


def _prompt_scan(p, pm, *, gla, batch, seq, meta_block, small, name):
    heads, dv = (H_B, DV_B) if gla else (H_A, DV_A)
    n_chunks = seq // CHUNK
    blocks = CHUNK // DIAG_BLK
    stack = DIAG_BLK * blocks * (blocks + 1) // 2
    p16, pf, pa = p
    pm16, pmf, pma = pm
    tok = functools.partial(_proj_spec, seq, row_block=lambda b: b, head_major=True)
    met = functools.partial(_proj_spec, N_META, row_block=lambda b: meta_block, head_major=True)
    vec = lambda width: pl.BlockSpec((1, width), lambda h, b: (0, h))
    if gla:
        wa, ba, gn = small
        in_specs = [tok(DK, COL_QB, 1), tok(DK, COL_KB, 1), tok(dv, COL_VB, 1), tok(dv, COL_RB, 1),
                    tok(LANES, 0, 0, tile_width=LANES),
                    met(DK, COL_KB, 1), met(dv, COL_VB, 1), met(LANES, 0, 0, tile_width=LANES),
                    pl.BlockSpec((LANES, DK), lambda h, b: (0, h)), vec(DK), vec(dv)]
        args = (p16, p16, p16, p16, pa, pm16, pm16, pma, wa, ba, gn)
    else:
        lbl, gn = small
        in_specs = [tok(DK, COL_QA, 1), tok(DK, 0, 1), tok(dv, COL_IA, 1), tok(dv, COL_GA, 1),
                    met(DK, 0, 1), met(dv, COL_IA, 1),
                    pl.BlockSpec((2, DK), lambda h, b: (0, h)), vec(dv)]
        args = (p16, pf, p16, p16, pmf, pm16, lbl, gn)
    return pl.pallas_call(
        functools.partial(_prompt_scan_kernel, gla=gla, seq=seq, meta=N_META),
        grid=(heads, batch),
        in_specs=in_specs,
        out_specs=[pl.BlockSpec((seq, dv), lambda h, b: (b, h)),
                   pl.BlockSpec((None, None, DK, dv), lambda h, b: (b, h, 0, 0))],
        out_shape=[jax.ShapeDtypeStruct((batch * seq, heads * dv), BF16),
                   jax.ShapeDtypeStruct((batch, heads, DK, dv), F32)],
        scratch_shapes=[
            pltpu.VMEM((dv, DK), F32),
            pltpu.VMEM((seq, DK), F32),
            pltpu.VMEM((seq, DK), F32),
            pltpu.VMEM((seq, DK), F32),
            pltpu.VMEM((seq, DK), BF16),
            pltpu.VMEM((seq, DK), BF16),
            pltpu.VMEM((n_chunks * stack, DK), BF16),
            pltpu.VMEM((seq, DK), BF16),
            pltpu.VMEM((n_chunks, SUBLANES, DK), F32),
            pltpu.VMEM((dv, DK), F32),
        ],
        compiler_params=_params("parallel", "arbitrary"),
        name=name,
    )(*args)


def _sample_scan(p, s0, *, gla, batch, steps, small, name):
    assert steps == SUBLANES, "the sample recurrence keeps one sequence per sublane group"
    heads, dv = (H_B, DV_B) if gla else (H_A, DV_A)
    rows = SAMPLE_SEQS * steps
    p16, pf, pa = p
    tok = functools.partial(_proj_spec, rows, row_block=lambda b: b)
    vec = lambda width: pl.BlockSpec((1, width), lambda b, h: (0, h))
    state = pl.BlockSpec((SAMPLE_SEQS, None, DK, dv), lambda b, h: (b, h, 0, 0))
    if gla:
        wa, ba, gn = small
        in_specs = [tok(DK, COL_QB, 1), tok(DK, COL_KB, 1), tok(dv, COL_VB, 1), tok(dv, COL_RB, 1),
                    tok(LANES, 0, 0, tile_width=LANES),
                    pl.BlockSpec((LANES, DK), lambda b, h: (0, h)), vec(DK), vec(dv), state]
        args = (p16, p16, p16, p16, pa, wa, ba, gn, s0)
    else:
        lbl, gn = small
        in_specs = [tok(DK, COL_QA, 1), tok(DK, 0, 1), tok(dv, COL_IA, 1), tok(dv, COL_GA, 1),
                    pl.BlockSpec((2, DK), lambda b, h: (0, h)), vec(dv), state]
        args = (p16, pf, p16, p16, lbl, gn, s0)
    return pl.pallas_call(
        functools.partial(_sample_scan_kernel, gla=gla, steps=steps),
        grid=(batch // SAMPLE_SEQS, heads),
        in_specs=in_specs,
        out_specs=[pl.BlockSpec((rows, dv), lambda b, h: (b, h)), state],
        out_shape=[jax.ShapeDtypeStruct((batch * steps, heads * dv), BF16),
                   jax.ShapeDtypeStruct((batch, heads, DK, dv), F32)],
        compiler_params=_params("parallel", "parallel"),
        name=name,
    )(*args)


def kernel(x_prompt, x_sample, state_hgrn, state_gla, meta_tokens, lb_logits, ffn1_norm, w_ffn1_in,
           w_ffn1_out, mix_norm, w_in, w_alpha_up, b_alpha, gnorm_a, gnorm_b, w_out, ffn2_norm,
           w_ffn2_in, w_ffn2_out, final_norm):
    batch, seq, _ = x_prompt.shape
    dec_batch, steps, _ = x_sample.shape

    w_in_t = w_in[0].T
    wop = w_out[0]
    wa = jnp.pad(w_alpha_up[0], ((0, LANES - ALPHA_RANK), (0, 0))).astype(BF16)
    n1, nm, n2, nf = ffn1_norm, mix_norm, ffn2_norm, final_norm[None]
    small_a = (lb_logits, gnorm_a)
    small_b = (wa, b_alpha, gnorm_b)
    n_sample = dec_batch * steps

    xs = (x_sample.reshape(n_sample, D_MODEL), meta_tokens)
    n_stream = n_sample + N_META
    h1s, u1s, *w1 = _ffn(xs, n1, (w_ffn1_in[0], w_ffn1_in[0], w_ffn1_out[0]), nm, tm=n_stream,
                         tf=TF_CAST, final_norm=False, emit_weights=True, name="ffn1_sample")
    ps = _in_proj(u1s, w_in_t, tm=n_stream, name="inproj_sample")
    oa_s, sa_s = _sample_scan(ps, state_hgrn[0], gla=False, batch=dec_batch, steps=steps,
                              small=small_a, name="scan_hgrn_sample")
    ob_s, sb_s = _sample_scan(ps, state_gla[0], gla=True, batch=dec_batch, steps=steps,
                              small=small_b, name="scan_gla_sample")
    h2s = _out_proj(oa_s, ob_s, h1s, wop, tm=TM, name="outproj_sample")
    y_s, *w2 = _ffn(h2s, n2, (w_ffn2_in[0], w_ffn2_in[0], w_ffn2_out[0]), nf, tm=n_sample,
                    tf=TF_CAST, final_norm=True, emit_weights=True, name="ffn2_sample")

    h1p, u1p = _ffn(x_prompt.reshape(batch * seq, D_MODEL), n1, w1, nm, tm=TM_FFN, tf=TF,
                    final_norm=False, emit_weights=False, name="ffn1_prompt")
    pp = _in_proj(u1p, w_in_t, tm=TM_PROJ, name="inproj_prompt")
    meta_block = n_sample // N_META
    oa_p, sa_p = _prompt_scan(pp, ps, gla=False, batch=batch, seq=seq, meta_block=meta_block,
                              small=small_a, name="scan_hgrn_prompt")
    ob_p, sb_p = _prompt_scan(pp, ps, gla=True, batch=batch, seq=seq, meta_block=meta_block,
                              small=small_b, name="scan_gla_prompt")
    h2p = _out_proj(oa_p, ob_p, h1p, wop, tm=TM, name="outproj_prompt")
    (y_p,) = _ffn(h2p, n2, w2, nf, tm=TM_FFN, tf=TF, final_norm=True, emit_weights=False,
                  name="ffn2_prompt")

    return (y_p.reshape(batch, seq, D_MODEL), y_s.reshape(dec_batch, steps, D_MODEL),
            sa_p[None], sb_p[None], sa_s[None], sb_s[None])
```

```python
import functools

import jax
import jax.numpy as jnp
from jax import lax
from jax.experimental import pallas as pl
from jax.experimental.pallas import tpu as pltpu

F32 = jnp.float32
BF16 = jnp.bfloat16
HIGHEST = lax.Precision.HIGHEST

D_MODEL = 2048
N_META = 16
DK = 128
H_A, DV_A = 8, 128
H_B, DV_B = 4, 256
D_HALF = H_A * DV_A
ALPHA_RANK = 16
GATE_TEMP = 16.0
D_FF = 5632
CHUNK = 64
EPS = 1e-6
IN_PROJ_WIDTHS = (D_HALF, D_HALF, D_HALF, D_HALF, H_B * DK, H_B * DK, D_HALF, D_HALF, ALPHA_RANK)

LANES = 128
SUBLANES = 8
VMEM_LIMIT_BYTES = 56 * 1024 * 1024

COL_QA, COL_IA, COL_GA, COL_QB, COL_KB, COL_VB, COL_RB = 0, 8, 16, 24, 28, 32, 40
P16_COLS = 3 * D_HALF + 2 * H_B * DK + 2 * D_HALF

TM = 512
TM_FFN = 1024
TF = 256
TF_CAST = 256
TN_PROJ = 512
TM_PROJ = 2048
N16_TILES = P16_COLS // TN_PROJ
DIAG_BLK = 16
SAMPLE_SEQS = 32
CHUNKS_PER_TRIP = 4
CHUNKS_PER_MXU_TRIP = 8

FAST_MIN_LOG_GATE = -60.0 / (DIAG_BLK - 1)
FAST_MAX_KEY = 1e10


def _rms(x, w):
    return x * lax.rsqrt(jnp.mean(x * x, axis=-1, keepdims=True) + EPS) * w


def _silu(x):
    return x * jax.nn.sigmoid(x)


def _params(*sem):
    return pltpu.CompilerParams(dimension_semantics=sem, vmem_limit_bytes=VMEM_LIMIT_BYTES)


def _ffn_kernel(*refs, n_pieces, final_norm, emit_weights):
    h_refs = refs[:n_pieces]
    nw_ref, wg_ref, wu_ref, wo_ref, tnw_ref, out_ref, *rest = refs[n_pieces:]
    u_ref = rest[-1]
    j = pl.program_id(1)
    last_j = pl.num_programs(1) - 1

    def load_h():
        pieces = [r[...] for r in h_refs]
        return pieces[0] if n_pieces == 1 else jnp.concatenate(pieces, axis=0)

    def step(first, last):
        if first:
            u = _rms(load_h(), nw_ref[...]).astype(BF16)
            u_ref[...] = u
        else:
            u = u_ref[...]
        wg, wu, wo = (r[...].astype(BF16) for r in (wg_ref, wu_ref, wo_ref))
        if emit_weights:
            for dst, w in zip(rest[-4:-1], (wg, wu, wo)):
                dst[...] = w
        g = jnp.dot(u, wg, preferred_element_type=F32)
        up = jnp.dot(u, wu, preferred_element_type=F32)
        a = (_silu(g) * up).astype(BF16)
        acc = jnp.dot(a, wo, preferred_element_type=F32)
        if not first:
            acc = out_ref[...] + acc
        if not last:
            out_ref[...] = acc
            return
        y = load_h() + 0.5 * acc
        normed = _rms(y, tnw_ref[...])
        if final_norm:
            out_ref[...] = normed
        else:
            out_ref[...] = y
            rest[0][...] = normed.astype(BF16)

    pl.when(j == 0)(functools.partial(step, True, False))
    pl.when((j > 0) & (j < last_j))(functools.partial(step, False, False))
    pl.when(j == last_j)(functools.partial(step, False, True))


def _ffn(h, norm_w, weights, tail_norm_w, *, tm, tf, final_norm, emit_weights, name):
    pieces = h if isinstance(h, tuple) else (h,)
    m = sum(p.shape[0] for p in pieces)
    nf = D_FF // tf
    rows = pl.BlockSpec((tm, D_MODEL), lambda i, j: (i, 0),
                        pipeline_mode=pl.Buffered(1) if m == tm else None)
    if len(pieces) == 1:
        h_specs = [rows]
    else:
        assert m == tm, "stacked inputs form one row tile"
        h_specs = [pl.BlockSpec((p.shape[0], D_MODEL), lambda i, j: (0, 0),
                                pipeline_mode=pl.Buffered(1)) for p in pieces]
    vec = pl.BlockSpec((1, D_MODEL), lambda i, j: (0, 0))
    col_tile = pl.BlockSpec((D_MODEL, tf), lambda i, j: (0, j))
    row_tile = pl.BlockSpec((tf, D_MODEL), lambda i, j: (j, 0))
    out_specs = [rows]
    out_shape = [jax.ShapeDtypeStruct((m, D_MODEL), F32)]
    if not final_norm:
        out_specs += [rows]
        out_shape += [jax.ShapeDtypeStruct((m, D_MODEL), BF16)]
    if emit_weights:
        assert m == tm, "weight casts are written once, by a single row tile"
        up_tile = pl.BlockSpec((D_MODEL, tf), lambda i, j: (0, j + nf))
        out_specs += [col_tile, col_tile, row_tile]
        out_shape += [jax.ShapeDtypeStruct((D_MODEL, D_FF), BF16)] * 2
        out_shape += [jax.ShapeDtypeStruct((D_FF, D_MODEL), BF16)]
    else:
        up_tile = col_tile
    return pl.pallas_call(
        functools.partial(_ffn_kernel, n_pieces=len(pieces), final_norm=final_norm,
                          emit_weights=emit_weights),
        grid=(m // tm, nf),
        in_specs=h_specs + [vec, col_tile, up_tile, row_tile, vec],
        out_specs=out_specs,
        out_shape=out_shape,
        scratch_shapes=[pltpu.VMEM((tm, D_MODEL), BF16)],
        compiler_params=_params("parallel", "arbitrary"),
        name=name,
    )(*pieces, norm_w, *weights, tail_norm_w)


def _proj_kernel(u_ref, w_ref, wal_ref, p16_ref, pf_ref, pa_ref):
    j = pl.program_id(1)
    nt = (((1,), (1,)), ((), ()))

    @pl.when(j == 0)
    def _low_rank():
        wal = jnp.concatenate([wal_ref[...].astype(BF16),
                               jnp.zeros((LANES - ALPHA_RANK, D_MODEL), BF16)], axis=0)
        pa_ref[...] = lax.dot_general(u_ref[...], wal, nt, preferred_element_type=F32)

    def project():
        return lax.dot_general(u_ref[...], w_ref[...].astype(BF16), nt,
                               preferred_element_type=F32)

    @pl.when(j < N16_TILES)
    def _narrow():
        p16_ref[...] = project().astype(BF16)

    @pl.when(j >= N16_TILES)
    def _wide():
        pf_ref[...] = project()


def _w_in_tile(j):
    fa_lo, fa_hi = D_HALF // TN_PROJ, 2 * D_HALF // TN_PROJ
    moved = fa_hi - fa_lo
    return jnp.where(j < fa_lo, j, jnp.where(j < N16_TILES, j + moved, j - N16_TILES + fa_lo))


def _in_proj(u, w, *, tm, name):
    m = u.shape[0]
    return pl.pallas_call(
        _proj_kernel,
        grid=(m // tm, (P16_COLS + D_HALF) // TN_PROJ),
        in_specs=[
            pl.BlockSpec((tm, D_MODEL), lambda i, j: (i, 0)),
            pl.BlockSpec((TN_PROJ, D_MODEL), lambda i, j: (_w_in_tile(j), 0)),
            pl.BlockSpec((ALPHA_RANK, D_MODEL), lambda i, j: ((P16_COLS + D_HALF) // ALPHA_RANK, 0)),
        ],
        out_specs=[
            pl.BlockSpec((None, tm, TN_PROJ), lambda i, j: (jnp.minimum(j, N16_TILES - 1), i, 0)),
            pl.BlockSpec((None, tm, TN_PROJ), lambda i, j: (jnp.maximum(j - N16_TILES, 0), i, 0)),
            pl.BlockSpec((None, tm, LANES), lambda i, j: (0, i, 0)),
        ],
        out_shape=[jax.ShapeDtypeStruct((N16_TILES, m, TN_PROJ), BF16),
                   jax.ShapeDtypeStruct((D_HALF // TN_PROJ, m, TN_PROJ), F32),
                   jax.ShapeDtypeStruct((1, m, LANES), F32)],
        compiler_params=_params("parallel", "arbitrary"),
        name=name,
    )(u, w, w)


def _out_proj_kernel(oa_ref, ob_ref, h_ref, wa_ref, wb_ref, out_ref):
    out_ref[...] = (h_ref[...]
                    + jnp.dot(oa_ref[...], wa_ref[...].astype(BF16), preferred_element_type=F32)
                    + jnp.dot(ob_ref[...], wb_ref[...].astype(BF16), preferred_element_type=F32))


def _out_proj(oa, ob, h, w, *, tm, name):
    m = oa.shape[0]
    return pl.pallas_call(
        _out_proj_kernel,
        grid=(m // tm,),
        in_specs=[
            pl.BlockSpec((tm, D_HALF), lambda i: (i, 0)),
            pl.BlockSpec((tm, D_HALF), lambda i: (i, 0)),
            pl.BlockSpec((tm, D_MODEL), lambda i: (i, 0)),
            pl.BlockSpec((D_HALF, D_MODEL), lambda i: (0, 0), pipeline_mode=pl.Buffered(1)),
            pl.BlockSpec((D_HALF, D_MODEL), lambda i: (1, 0), pipeline_mode=pl.Buffered(1)),
        ],
        out_specs=pl.BlockSpec((tm, D_MODEL), lambda i: (i, 0)),
        out_shape=jax.ShapeDtypeStruct((m, D_MODEL), F32),
        compiler_params=_params("parallel"),
        name=name,
    )(oa, ob, h, w, w)


def _tri(c, seq_len):
    r = lax.broadcasted_iota(jnp.int32, (c, c), 0)
    s = lax.broadcasted_iota(jnp.int32, (c, c), 1)
    return jnp.where((r >= s) & ((r ^ s) < seq_len), 1.0, 0.0).astype(F32)


def _off_mask(c, blk):
    nb = c // blk
    rows = nb * (nb - 1) // 2 * blk
    r = lax.broadcasted_iota(jnp.int32, (c, rows), 0) >> (blk.bit_length() - 1)
    s = lax.broadcasted_iota(jnp.int32, (c, rows), 1)
    seg = jnp.zeros((c, rows), jnp.int32)
    for i in range(1, nb):
        seg = seg + jnp.where(s >= blk * i * (i - 1) // 2, 1, 0)
    return jnp.where(r == seg, 1.0, 0.0).astype(F32)


def _diag(q, k, v, bc, blk):
    c = q.shape[0]
    row = lax.broadcasted_iota(jnp.int32, (blk, 1), 0)
    outs = []
    for i in range(c // blk):
        sl = slice(i * blk, (i + 1) * blk)
        qi, ki, vi, bi = q[sl], k[sl], v[sl].astype(F32), bc[sl]
        acc = jnp.zeros((blk, v.shape[1]), F32)
        for s in range(blk):
            e = jnp.exp(jnp.minimum(bi - bi[s:s + 1, :], 0.0))
            col = jnp.sum(qi * ki[s:s + 1, :] * e, axis=-1, keepdims=True)
            col = jnp.where(row >= s, col, 0.0)
            acc = acc + col * vi[s:s + 1, :]
        outs.append(acc)
    return outs[0] if len(outs) == 1 else jnp.concatenate(outs, axis=0)


def _offdiag(q, k, v, bc, blk, mask):
    c = q.shape[0]
    nb = c // blk
    qs, ks, vs = [jnp.zeros((blk, DK), F32)], [], []
    for i in range(1, nb):
        lo = i * blk
        ref = bc[lo:lo + 1, :]
        qs.append(q[lo:lo + blk] * jnp.exp(bc[lo:lo + blk] - ref))
        ks.append(k[:lo] * jnp.exp(ref - bc[:lo]))
        vs.append(v[:lo])
    qt = jnp.concatenate(qs, axis=0).astype(BF16)
    kt = jnp.concatenate(ks, axis=0).astype(BF16)
    vt = jnp.concatenate(vs, axis=0).astype(BF16)
    a = lax.dot_general(qt, kt, (((1,), (1,)), ((), ())), preferred_element_type=F32)
    return jnp.dot((a * mask).astype(BF16), vt, preferred_element_type=F32)


def _hgrn_inputs(qa, fa, ia, lb):
    fg = lb + (1.0 - lb) * jax.nn.sigmoid(fa)
    return _silu(qa.astype(F32)) * (DK ** -0.5), 1.0 - fg, ia, jnp.log(fg)


def _gla_inputs(qb, kb, vb, al, wa, ba):
    alpha = jnp.dot(al.astype(BF16), wa, preferred_element_type=F32) + ba
    logsig = jnp.minimum(alpha, 0.0) - jnp.log1p(jnp.exp(-jnp.abs(alpha)))
    return qb.astype(F32) * (DK ** -0.5), kb.astype(F32), vb, logsig / GATE_TEMP


def _head_out(o, gn, gate):
    return (_rms(o, gn) * _silu(gate.astype(F32))).astype(BF16)


def _lower_bound(lbl_ref):
    x = lbl_ref[...]
    e = jnp.exp(x - jnp.max(x, axis=0, keepdims=True))
    return e[0:1, :] / jnp.sum(e, axis=0, keepdims=True)


def _stack_mask(c, blk):
    nb = c // blk
    rows = blk * nb * (nb + 1) // 2
    t = lax.broadcasted_iota(jnp.int32, (c, rows), 0)
    r = lax.broadcasted_iota(jnp.int32, (c, rows), 1)
    seg = jnp.zeros((c, rows), jnp.int32)
    off = jnp.zeros((c, rows), jnp.int32)
    for i in range(1, nb):
        start = blk * i * (i + 1) // 2
        seg = seg + jnp.where(r >= start, 1, 0)
        off = jnp.where(r >= start, start, off)
    keep = ((t >> (blk.bit_length() - 1)) == seg) & (r - off <= t)
    return jnp.where(keep, 1.0, 0.0).astype(F32)


def _state_update_t(k, vb, bc, st):
    c = k.shape[0]
    blast = bc[c - 1:c, :]
    kdec = (k * jnp.exp(blast - bc)).astype(BF16)
    upd = lax.dot_general(vb, kdec, (((0,), (0,)), ((), ())), preferred_element_type=F32)
    if st is None:
        return upd
    return st * jnp.exp(blast) + upd


def _scan_sublanes(p):
    row = lax.broadcasted_iota(jnp.int32, (SUBLANES, DK), 0)
    for s in (1, 2, 4):
        p = p + jnp.where(row >= s, pltpu.roll(p, s, axis=0), 0.0)
    return p


def _cumsum_chunk(g):
    out, carry = [], None
    for j in range(CHUNK // SUBLANES):
        p = _scan_sublanes(g[j * SUBLANES:(j + 1) * SUBLANES])
        if carry is not None:
            p = p + carry
        carry = p[SUBLANES - 1:SUBLANES, :]
        out.append(p)
    return jnp.concatenate(out, axis=0)


def _rescaled_operands(q, k, bc):
    c, blk = CHUNK, DIAG_BLK
    qs, ks = [], []
    for i in range(c // blk):
        lo, hi = i * blk, (i + 1) * blk
        ref = bc[lo:lo + 1, :]
        qs.append(q[lo:hi] * jnp.exp(bc[lo:hi] - ref))
        ks.append(k[:hi] * jnp.exp(ref - bc[:hi]))
    return ((q * jnp.exp(bc)).astype(BF16), jnp.concatenate(qs, axis=0).astype(BF16),
            jnp.concatenate(ks, axis=0).T.astype(BF16),
            (k * jnp.exp(bc[c - 1:c, :] - bc)).astype(BF16))


def _chunk_exact(q, k, v, bc, st, mask):
    o = lax.dot_general((q * jnp.exp(bc)).astype(BF16), st.astype(BF16),
                        (((1,), (1,)), ((), ())), preferred_element_type=F32)
    o = o + _offdiag(q, k, v, bc, DIAG_BLK, mask) + _diag(q, k, v, bc, DIAG_BLK)
    return o, _state_update_t(k, v.astype(BF16), bc, st)


def _prompt_scan_kernel(*refs, gla, seq, meta):
    n_in = 11 if gla else 8
    if gla:
        q_ref, k_ref, v_ref, gate_ref, al_ref, mk_ref, mv_ref, mal_ref, wa_ref, ba_ref, gn_ref = (
            refs[:n_in])
    else:
        q_ref, f_ref, v_ref, gate_ref, mf_ref, mv_ref, lbl_ref, gn_ref = refs[:n_in]
        lb = _lower_bound(lbl_ref)
    o_ref, sout_ref = refs[n_in:n_in + 2]
    (st_ref, qs_ref, ks_ref, gs_ref, qd_ref, qt_ref, kt_ref, kdec_ref, eb_ref,
     meta_st_ref) = refs[n_in + 2:]
    trips = seq // (CHUNK * CHUNKS_PER_TRIP)
    nt = (((1,), (1,)), ((), ()))
    tn = (((0,), (0,)), ((), ()))

    def chunk_rows(ci, u, n=CHUNK):
        return pl.ds(pl.multiple_of((ci * CHUNKS_PER_TRIP + u) * n, n), n)

    @pl.when(pl.program_id(1) == 0)
    def _meta():
        if gla:
            _, mk, mv, mg = _gla_inputs(mk_ref[...], mk_ref[...], mv_ref[...], mal_ref[...],
                                        wa_ref[...], ba_ref[...])
        else:
            _, mk, mv, mg = _hgrn_inputs(mf_ref[...], mf_ref[...], mv_ref[...], lb)
        mbc = jnp.dot(_tri(meta, meta), mg, precision=HIGHEST, preferred_element_type=F32)
        meta_st_ref[...] = _state_update_t(mk, mv.astype(BF16), mbc, None)

    st_ref[...] = meta_st_ref[...]

    def rows_of(c, n=CHUNK):
        return pl.ds(pl.multiple_of(c * n, n), n)

    def rescaled_pass():
        mask = _stack_mask(CHUNK, DIAG_BLK)
        units = range(CHUNKS_PER_MXU_TRIP)
        mxu_trips = seq // (CHUNK * CHUNKS_PER_MXU_TRIP)

        def prepare(ti, extremes):
            gmin, kmax = extremes
            trip_rows = CHUNK * CHUNKS_PER_MXU_TRIP
            rows = pl.ds(pl.multiple_of(ti * trip_rows, trip_rows), trip_rows)
            if gla:
                q, k, _, g = _gla_inputs(q_ref[rows, :], k_ref[rows, :], None, al_ref[rows, :],
                                         wa_ref[...], ba_ref[...])
            else:
                q, k, _, g = _hgrn_inputs(q_ref[rows, :], f_ref[rows, :], None, lb)
            qs_ref[rows, :] = q
            ks_ref[rows, :] = k
            gs_ref[rows, :] = g
            for u in units:
                c = ti * CHUNKS_PER_MXU_TRIP + u
                sl = slice(u * CHUNK, (u + 1) * CHUNK)
                bc = _cumsum_chunk(g[sl])
                qd, qt, kt, kdec = _rescaled_operands(q[sl], k[sl], bc)
                qd_ref[rows_of(c), :] = qd
                qt_ref[rows_of(c), :] = qt
                kt_ref[rows_of(c, DK), :] = kt
                kdec_ref[rows_of(c), :] = kdec
                eb_ref[c] = jnp.broadcast_to(jnp.exp(bc[CHUNK - 1:CHUNK, :]), (SUBLANES, DK))
            return (jnp.minimum(gmin, jnp.min(g, axis=0, keepdims=True)),
                    jnp.maximum(kmax, jnp.max(jnp.abs(k), axis=0, keepdims=True)))

        def contract(ti):
            cs = [ti * CHUNKS_PER_MXU_TRIP + u for u in units]
            vb = [v_ref[rows_of(c), :].astype(BF16) for c in cs]
            a = [jnp.dot(qt_ref[rows_of(c), :], kt_ref[rows_of(c, DK), :],
                         preferred_element_type=F32) for c in cs]
            am = [(x * mask).astype(BF16) for x in a]
            vt = [jnp.concatenate([x[:(i + 1) * DIAG_BLK] for i in range(CHUNK // DIAG_BLK)], axis=0)
                  for x in vb]
            oi = [jnp.dot(am[u], vt[u], preferred_element_type=F32) for u in units]
            inc = [lax.dot_general(vb[u], kdec_ref[rows_of(cs[u]), :], tn,
                                   preferred_element_type=F32) for u in units]
            sts = [st_ref[...]]
            for u in units:
                sts.append(sts[-1] * eb_ref[cs[u]][0:1, :] + inc[u])
            o = [oi[u] + lax.dot_general(qd_ref[rows_of(cs[u]), :], sts[u].astype(BF16), nt,
                                         preferred_element_type=F32) for u in units]
            for u in units:
                o_ref[rows_of(cs[u]), :] = _head_out(o[u], gn_ref[...], gate_ref[rows_of(cs[u]), :])
            st_ref[...] = sts[-1]

        extremes = prepare(0, (jnp.zeros((1, DK), F32), jnp.zeros((1, DK), F32)))

        def body(ti, extremes):
            contract(ti)
            return prepare(ti + 1, extremes)

        extremes = lax.fori_loop(0, mxu_trips - 1, body, extremes)
        contract(mxu_trips - 1)
        return extremes

    gmin, kmax = rescaled_pass()
    in_range = (jnp.min(gmin) >= FAST_MIN_LOG_GATE) & (jnp.max(kmax) <= FAST_MAX_KEY)

    @pl.when(jnp.logical_not(in_range))
    def _exact():
        mask = _off_mask(CHUNK, DIAG_BLK)
        st_ref[...] = meta_st_ref[...]

        def body(ci, carry):
            st = st_ref[...]
            for u in range(CHUNKS_PER_TRIP):
                rows = chunk_rows(ci, u)
                o, st = _chunk_exact(qs_ref[rows, :], ks_ref[rows, :], v_ref[rows, :],
                                     _cumsum_chunk(gs_ref[rows, :]), st, mask)
                o_ref[rows, :] = _head_out(o, gn_ref[...], gate_ref[rows, :])
            st_ref[...] = st
            return carry

        lax.fori_loop(0, trips, body, 0)

    sout_ref[...] = st_ref[...].T


def _sample_scan_kernel(*refs, gla, steps):
    if gla:
        (q_ref, k_ref, v_ref, gate_ref, al_ref, wa_ref, ba_ref, gn_ref, s0_ref,
         o_ref, sout_ref) = refs
        q, k, v, g = _gla_inputs(q_ref[...], k_ref[...], v_ref[...], al_ref[...],
                                 wa_ref[...], ba_ref[...])
    else:
        (q_ref, f_ref, v_ref, gate_ref, lbl_ref, gn_ref, s0_ref, o_ref, sout_ref) = refs
        q, k, v, g = _hgrn_inputs(q_ref[...], f_ref[...], v_ref[...], _lower_bound(lbl_ref))
    v32 = v.astype(F32)
    rows = SAMPLE_SEQS * steps
    groups = [slice(b * steps, (b + 1) * steps) for b in range(SAMPLE_SEQS)]
    tn = (((0,), (0,)), ((), ()))

    bc = jnp.concatenate([_scan_sublanes(g[sl]) for sl in groups], axis=0)

    def repeated(r):
        return jnp.concatenate([jnp.broadcast_to(bc[sl][r:r + 1, :], (steps, DK)) for sl in groups],
                               axis=0)

    qd = (q * jnp.exp(bc)).astype(BF16)
    kdec = (k * jnp.exp(repeated(steps - 1) - bc)).astype(BF16)
    r = lax.broadcasted_iota(jnp.int32, (rows, SAMPLE_SEQS), 0)
    c = lax.broadcasted_iota(jnp.int32, (rows, SAMPLE_SEQS), 1)
    selector = jnp.where((r >> (steps.bit_length() - 1)) == c, 1.0, 0.0).astype(F32)
    decay = jnp.exp(lax.dot_general(g, selector, tn, precision=HIGHEST,
                                    preferred_element_type=F32))

    inter = [jnp.dot(qd[sl], s0_ref[b].astype(BF16), preferred_element_type=F32)
             for b, sl in enumerate(groups)]
    upd = [lax.dot_general(kdec[sl], v32[sl].astype(BF16), tn, preferred_element_type=F32)
           for sl in groups]
    for b in range(SAMPLE_SEQS):
        sout_ref[b] = decay[:, b:b + 1] * s0_ref[b] + upd[b]
    o = jnp.concatenate(inter, axis=0) + _diag(q, k, v32, bc, steps)
    o_ref[...] = _head_out(o, gn_ref[...], gate_ref[...])


def _proj_spec(rows, width, c0, per_head, row_block, tile_width=TN_PROJ, head_major=False):
    per_tile = tile_width // width

    def index(i0, i1):
        b, h = (i1, i0) if head_major else (i0, i1)
        g = c0 * LANES // width + h * per_head
        return (g // per_tile, row_block(b), g % per_tile)

    return pl.BlockSpec((None, rows, width), index)


def _prompt_scan(p, pm, *, gla, batch, seq, meta_block, small, name):
    heads, dv = (H_B, DV_B) if gla else (H_A, DV_A)
    n_chunks = seq // CHUNK
    blocks = CHUNK // DIAG_BLK
    stack = DIAG_BLK * blocks * (blocks + 1) // 2
    p16, pf, pa = p
    pm16, pmf, pma = pm
    tok = functools.partial(_proj_spec, seq, row_block=lambda b: b, head_major=True)
    met = functools.partial(_proj_spec, N_META, row_block=lambda b: meta_block, head_major=True)
    vec = lambda width: pl.BlockSpec((1, width), lambda h, b: (0, h))
    if gla:
        wa, ba, gn = small
        in_specs = [tok(DK, COL_QB, 1), tok(DK, COL_KB, 1), tok(dv, COL_VB, 1), tok(dv, COL_RB, 1),
                    tok(LANES, 0, 0, tile_width=LANES),
                    met(DK, COL_KB, 1), met(dv, COL_VB, 1), met(LANES, 0, 0, tile_width=LANES),
                    pl.BlockSpec((LANES, DK), lambda h, b: (0, h)), vec(DK), vec(dv)]
        args = (p16, p16, p16, p16, pa, pm16, pm16, pma, wa, ba, gn)
    else:
        lbl, gn = small
        in_specs = [tok(DK, COL_QA, 1), tok(DK, 0, 1), tok(dv, COL_IA, 1), tok(dv, COL_GA, 1),
                    met(DK, 0, 1), met(dv, COL_IA, 1),
                    pl.BlockSpec((2, DK), lambda h, b: (0, h)), vec(dv)]
        args = (p16, pf, p16, p16, pmf, pm16, lbl, gn)
    return pl.pallas_call(
        functools.partial(_prompt_scan_kernel, gla=gla, seq=seq, meta=N_META),
        grid=(heads, batch),
        in_specs=in_specs,
        out_specs=[pl.BlockSpec((seq, dv), lambda h, b: (b, h)),
                   pl.BlockSpec((None, None, DK, dv), lambda h, b: (b, h, 0, 0))],
        out_shape=[jax.ShapeDtypeStruct((batch * seq, heads * dv), BF16),
                   jax.ShapeDtypeStruct((batch, heads, DK, dv), F32)],
        scratch_shapes=[
            pltpu.VMEM((dv, DK), F32),
            pltpu.VMEM((seq, DK), F32),
            pltpu.VMEM((seq, DK), F32),
            pltpu.VMEM((seq, DK), F32),
            pltpu.VMEM((seq, DK), BF16),
            pltpu.VMEM((seq, DK), BF16),
            pltpu.VMEM((n_chunks * DK, stack), BF16),
            pltpu.VMEM((seq, DK), BF16),
            pltpu.VMEM((n_chunks, SUBLANES, DK), F32),
            pltpu.VMEM((dv, DK), F32),
        ],
        compiler_params=_params("parallel", "arbitrary"),
        name=name,
    )(*args)


def _sample_scan(p, s0, *, gla, batch, steps, small, name):
    assert steps == SUBLANES, "the sample recurrence keeps one sequence per sublane group"
    heads, dv = (H_B, DV_B) if gla else (H_A, DV_A)
    rows = SAMPLE_SEQS * steps
    p16, pf, pa = p
    tok = functools.partial(_proj_spec, rows, row_block=lambda b: b)
    vec = lambda width: pl.BlockSpec((1, width), lambda b, h: (0, h))
    state = pl.BlockSpec((SAMPLE_SEQS, None, DK, dv), lambda b, h: (b, h, 0, 0))
    if gla:
        wa, ba, gn = small
        in_specs = [tok(DK, COL_QB, 1), tok(DK, COL_KB, 1), tok(dv, COL_VB, 1), tok(dv, COL_RB, 1),
                    tok(LANES, 0, 0, tile_width=LANES),
                    pl.BlockSpec((LANES, DK), lambda b, h: (0, h)), vec(DK), vec(dv), state]
        args = (p16, p16, p16, p16, pa, wa, ba, gn, s0)
    else:
        lbl, gn = small
        in_specs = [tok(DK, COL_QA, 1), tok(DK, 0, 1), tok(dv, COL_IA, 1), tok(dv, COL_GA, 1),
                    pl.BlockSpec((2, DK), lambda b, h: (0, h)), vec(dv), state]
        args = (p16, pf, p16, p16, lbl, gn, s0)
    return pl.pallas_call(
        functools.partial(_sample_scan_kernel, gla=gla, steps=steps),
        grid=(batch // SAMPLE_SEQS, heads),
        in_specs=in_specs,
        out_specs=[pl.BlockSpec((rows, dv), lambda b, h: (b, h)), state],
        out_shape=[jax.ShapeDtypeStruct((batch * steps, heads * dv), BF16),
                   jax.ShapeDtypeStruct((batch, heads, DK, dv), F32)],
        compiler_params=_params("parallel", "parallel"),
        name=name,
    )(*args)


def kernel(x_prompt, x_sample, state_hgrn, state_gla, meta_tokens, lb_logits, ffn1_norm, w_ffn1_in,
           w_ffn1_out, mix_norm, w_in, w_alpha_up, b_alpha, gnorm_a, gnorm_b, w_out, ffn2_norm,
           w_ffn2_in, w_ffn2_out, final_norm):
    batch, seq, _ = x_prompt.shape
    dec_batch, steps, _ = x_sample.shape

    w_in_t = w_in[0].T
    wop = w_out[0]
    wa = jnp.pad(w_alpha_up[0], ((0, LANES - ALPHA_RANK), (0, 0))).astype(BF16)
    n1, nm, n2, nf = ffn1_norm, mix_norm, ffn2_norm, final_norm[None]
    small_a = (lb_logits, gnorm_a)
    small_b = (wa, b_alpha, gnorm_b)
    n_sample = dec_batch * steps

    xs = (x_sample.reshape(n_sample, D_MODEL), meta_tokens)
    n_stream = n_sample + N_META
    h1s, u1s, *w1 = _ffn(xs, n1, (w_ffn1_in[0], w_ffn1_in[0], w_ffn1_out[0]), nm, tm=n_stream,
                         tf=TF_CAST, final_norm=False, emit_weights=True, name="ffn1_sample")
    ps = _in_proj(u1s, w_in_t, tm=n_stream, name="inproj_sample")
    oa_s, sa_s = _sample_scan(ps, state_hgrn[0], gla=False, batch=dec_batch, steps=steps,
                              small=small_a, name="scan_hgrn_sample")
    ob_s, sb_s = _sample_scan(ps, state_gla[0], gla=True, batch=dec_batch, steps=steps,
                              small=small_b, name="scan_gla_sample")
    h2s = _out_proj(oa_s, ob_s, h1s, wop, tm=TM, name="outproj_sample")
    y_s, *w2 = _ffn(h2s, n2, (w_ffn2_in[0], w_ffn2_in[0], w_ffn2_out[0]), nf, tm=n_sample,
                    tf=TF_CAST, final_norm=True, emit_weights=True, name="ffn2_sample")

    h1p, u1p = _ffn(x_prompt.reshape(batch * seq, D_MODEL), n1, w1, nm, tm=TM_FFN, tf=TF,
                    final_norm=False, emit_weights=False, name="ffn1_prompt")
    pp = _in_proj(u1p, w_in_t, tm=TM_PROJ, name="inproj_prompt")
    meta_block = n_sample // N_META
    oa_p, sa_p = _prompt_scan(pp, ps, gla=False, batch=batch, seq=seq, meta_block=meta_block,
                              small=small_a, name="scan_hgrn_prompt")
    ob_p, sb_p = _prompt_scan(pp, ps, gla=True, batch=batch, seq=seq, meta_block=meta_block,
                              small=small_b, name="scan_gla_prompt")
    h2p = _out_proj(oa_p, ob_p, h1p, wop, tm=TM, name="outproj_prompt")
    (y_p,) = _ffn(h2p, n2, w2, nf, tm=TM_FFN, tf=TF, final_norm=True, emit_weights=False,
                  name="ffn2_prompt")

    return (y_p.reshape(batch, seq, D_MODEL), y_s.reshape(dec_batch, steps, D_MODEL),
            sa_p[None], sb_p[None], sa_s[None], sb_s[None])
```

```python
import functools

import jax
import jax.numpy as jnp
from jax import lax
from jax.experimental import pallas as pl
from jax.experimental.pallas import tpu as pltpu

F32 = jnp.float32
BF16 = jnp.bfloat16
HIGHEST = lax.Precision.HIGHEST

D_MODEL = 2048
N_META = 16
DK = 128
H_A, DV_A = 8, 128
H_B, DV_B = 4, 256
D_HALF = H_A * DV_A
ALPHA_RANK = 16
GATE_TEMP = 16.0
D_FF = 5632
CHUNK = 64
EPS = 1e-6
IN_PROJ_WIDTHS = (D_HALF, D_HALF, D_HALF, D_HALF, H_B * DK, H_B * DK, D_HALF, D_HALF, ALPHA_RANK)

LANES = 128
SUBLANES = 8
VMEM_LIMIT_BYTES = 56 * 1024 * 1024

COL_QA, COL_IA, COL_GA, COL_QB, COL_KB, COL_VB, COL_RB = 0, 8, 16, 24, 28, 32, 40
P16_COLS = 3 * D_HALF + 2 * H_B * DK + 2 * D_HALF

TM = 512
TM_FFN = 1024
TF = 256
TF_CAST = 256
TN_PROJ = 512
TM_PROJ = 2048
N16_TILES = P16_COLS // TN_PROJ
DIAG_BLK = 16
SAMPLE_SEQS = 32
CHUNKS_PER_TRIP = 4
CHUNKS_PER_MXU_TRIP = 8

LOG2_E = 1.4426950408889634

FAST_MIN_LOG_GATE = -60.0 * LOG2_E / (DIAG_BLK - 1)
FAST_MAX_KEY = 1e10


def _rms(x, w):
    return x * lax.rsqrt(jnp.mean(x * x, axis=-1, keepdims=True) + EPS) * w


def _silu(x):
    return x * jax.nn.sigmoid(x)


def _params(*sem):
    return pltpu.CompilerParams(dimension_semantics=sem, vmem_limit_bytes=VMEM_LIMIT_BYTES)


def _ffn_kernel(*refs, n_pieces, final_norm, emit_weights):
    h_refs = refs[:n_pieces]
    nw_ref, wg_ref, wu_ref, wo_ref, tnw_ref, out_ref, *rest = refs[n_pieces:]
    u_ref = rest[-1]
    j = pl.program_id(1)
    last_j = pl.num_programs(1) - 1

    def load_h():
        pieces = [r[...] for r in h_refs]
        return pieces[0] if n_pieces == 1 else jnp.concatenate(pieces, axis=0)

    def step(first, last):
        if first:
            u = _rms(load_h(), nw_ref[...]).astype(BF16)
            u_ref[...] = u
        else:
            u = u_ref[...]
        wg, wu, wo = (r[...].astype(BF16) for r in (wg_ref, wu_ref, wo_ref))
        if emit_weights:
            for dst, w in zip(rest[-4:-1], (wg, wu, wo)):
                dst[...] = w
        g = jnp.dot(u, wg, preferred_element_type=F32)
        up = jnp.dot(u, wu, preferred_element_type=F32)
        a = (_silu(g) * up).astype(BF16)
        acc = jnp.dot(a, wo, preferred_element_type=F32)
        if not first:
            acc = out_ref[...] + acc
        if not last:
            out_ref[...] = acc
            return
        y = load_h() + 0.5 * acc
        normed = _rms(y, tnw_ref[...])
        if final_norm:
            out_ref[...] = normed
        else:
            out_ref[...] = y
            rest[0][...] = normed.astype(BF16)

    pl.when(j == 0)(functools.partial(step, True, False))
    pl.when((j > 0) & (j < last_j))(functools.partial(step, False, False))
    pl.when(j == last_j)(functools.partial(step, False, True))


def _ffn(h, norm_w, weights, tail_norm_w, *, tm, tf, final_norm, emit_weights, name):
    pieces = h if isinstance(h, tuple) else (h,)
    m = sum(p.shape[0] for p in pieces)
    nf = D_FF // tf
    rows = pl.BlockSpec((tm, D_MODEL), lambda i, j: (i, 0),
                        pipeline_mode=pl.Buffered(1) if m == tm else None)
    if len(pieces) == 1:
        h_specs = [rows]
    else:
        assert m == tm, "stacked inputs form one row tile"
        h_specs = [pl.BlockSpec((p.shape[0], D_MODEL), lambda i, j: (0, 0),
                                pipeline_mode=pl.Buffered(1)) for p in pieces]
    vec = pl.BlockSpec((1, D_MODEL), lambda i, j: (0, 0))
    col_tile = pl.BlockSpec((D_MODEL, tf), lambda i, j: (0, j))
    row_tile = pl.BlockSpec((tf, D_MODEL), lambda i, j: (j, 0))
    out_specs = [rows]
    out_shape = [jax.ShapeDtypeStruct((m, D_MODEL), F32)]
    if not final_norm:
        out_specs += [rows]
        out_shape += [jax.ShapeDtypeStruct((m, D_MODEL), BF16)]
    if emit_weights:
        assert m == tm, "weight casts are written once, by a single row tile"
        up_tile = pl.BlockSpec((D_MODEL, tf), lambda i, j: (0, j + nf))
        out_specs += [col_tile, col_tile, row_tile]
        out_shape += [jax.ShapeDtypeStruct((D_MODEL, D_FF), BF16)] * 2
        out_shape += [jax.ShapeDtypeStruct((D_FF, D_MODEL), BF16)]
    else:
        up_tile = col_tile
    return pl.pallas_call(
        functools.partial(_ffn_kernel, n_pieces=len(pieces), final_norm=final_norm,
                          emit_weights=emit_weights),
        grid=(m // tm, nf),
        in_specs=h_specs + [vec, col_tile, up_tile, row_tile, vec],
        out_specs=out_specs,
        out_shape=out_shape,
        scratch_shapes=[pltpu.VMEM((tm, D_MODEL), BF16)],
        compiler_params=_params("parallel", "arbitrary"),
        name=name,
    )(*pieces, norm_w, *weights, tail_norm_w)


def _proj_kernel(u_ref, w_ref, wal_ref, p16_ref, pf_ref, pa_ref):
    j = pl.program_id(1)
    nt = (((1,), (1,)), ((), ()))

    @pl.when(j == 0)
    def _low_rank():
        wal = jnp.concatenate([wal_ref[...].astype(BF16),
                               jnp.zeros((LANES - ALPHA_RANK, D_MODEL), BF16)], axis=0)
        pa_ref[...] = lax.dot_general(u_ref[...], wal, nt, preferred_element_type=F32)

    def project():
        return lax.dot_general(u_ref[...], w_ref[...].astype(BF16), nt,
                               preferred_element_type=F32)

    @pl.when(j < N16_TILES)
    def _narrow():
        p16_ref[...] = project().astype(BF16)

    @pl.when(j >= N16_TILES)
    def _wide():
        pf_ref[...] = project()


def _w_in_tile(j):
    fa_lo, fa_hi = D_HALF // TN_PROJ, 2 * D_HALF // TN_PROJ
    moved = fa_hi - fa_lo
    return jnp.where(j < fa_lo, j, jnp.where(j < N16_TILES, j + moved, j - N16_TILES + fa_lo))


def _in_proj(u, w, *, tm, name):
    m = u.shape[0]
    return pl.pallas_call(
        _proj_kernel,
        grid=(m // tm, (P16_COLS + D_HALF) // TN_PROJ),
        in_specs=[
            pl.BlockSpec((tm, D_MODEL), lambda i, j: (i, 0)),
            pl.BlockSpec((TN_PROJ, D_MODEL), lambda i, j: (_w_in_tile(j), 0)),
            pl.BlockSpec((ALPHA_RANK, D_MODEL), lambda i, j: ((P16_COLS + D_HALF) // ALPHA_RANK, 0)),
        ],
        out_specs=[
            pl.BlockSpec((None, tm, TN_PROJ), lambda i, j: (jnp.minimum(j, N16_TILES - 1), i, 0)),
            pl.BlockSpec((None, tm, TN_PROJ), lambda i, j: (jnp.maximum(j - N16_TILES, 0), i, 0)),
            pl.BlockSpec((None, tm, LANES), lambda i, j: (0, i, 0)),
        ],
        out_shape=[jax.ShapeDtypeStruct((N16_TILES, m, TN_PROJ), BF16),
                   jax.ShapeDtypeStruct((D_HALF // TN_PROJ, m, TN_PROJ), F32),
                   jax.ShapeDtypeStruct((1, m, LANES), F32)],
        compiler_params=_params("parallel", "arbitrary"),
        name=name,
    )(u, w, w)


def _out_proj_kernel(oa_ref, ob_ref, h_ref, wa_ref, wb_ref, out_ref):
    out_ref[...] = (h_ref[...]
                    + jnp.dot(oa_ref[...], wa_ref[...].astype(BF16), preferred_element_type=F32)
                    + jnp.dot(ob_ref[...], wb_ref[...].astype(BF16), preferred_element_type=F32))


def _out_proj(oa, ob, h, w, *, tm, name):
    m = oa.shape[0]
    return pl.pallas_call(
        _out_proj_kernel,
        grid=(m // tm,),
        in_specs=[
            pl.BlockSpec((tm, D_HALF), lambda i: (i, 0)),
            pl.BlockSpec((tm, D_HALF), lambda i: (i, 0)),
            pl.BlockSpec((tm, D_MODEL), lambda i: (i, 0)),
            pl.BlockSpec((D_HALF, D_MODEL), lambda i: (0, 0), pipeline_mode=pl.Buffered(1)),
            pl.BlockSpec((D_HALF, D_MODEL), lambda i: (1, 0), pipeline_mode=pl.Buffered(1)),
        ],
        out_specs=pl.BlockSpec((tm, D_MODEL), lambda i: (i, 0)),
        out_shape=jax.ShapeDtypeStruct((m, D_MODEL), F32),
        compiler_params=_params("parallel"),
        name=name,
    )(oa, ob, h, w, w)


def _tri(c, seq_len):
    r = lax.broadcasted_iota(jnp.int32, (c, c), 0)
    s = lax.broadcasted_iota(jnp.int32, (c, c), 1)
    return jnp.where((r >= s) & ((r ^ s) < seq_len), 1.0, 0.0).astype(F32)


def _off_mask(c, blk):
    nb = c // blk
    rows = nb * (nb - 1) // 2 * blk
    r = lax.broadcasted_iota(jnp.int32, (c, rows), 0) >> (blk.bit_length() - 1)
    s = lax.broadcasted_iota(jnp.int32, (c, rows), 1)
    seg = jnp.zeros((c, rows), jnp.int32)
    for i in range(1, nb):
        seg = seg + jnp.where(s >= blk * i * (i - 1) // 2, 1, 0)
    return jnp.where(r == seg, 1.0, 0.0).astype(F32)


def _diag(q, k, v, bc, blk):
    c = q.shape[0]
    row = lax.broadcasted_iota(jnp.int32, (blk, 1), 0)
    outs = []
    for i in range(c // blk):
        sl = slice(i * blk, (i + 1) * blk)
        qi, ki, vi, bi = q[sl], k[sl], v[sl].astype(F32), bc[sl]
        acc = jnp.zeros((blk, v.shape[1]), F32)
        for s in range(blk):
            e = jnp.exp2(jnp.minimum(bi - bi[s:s + 1, :], 0.0))
            col = jnp.sum(qi * ki[s:s + 1, :] * e, axis=-1, keepdims=True)
            col = jnp.where(row >= s, col, 0.0)
            acc = acc + col * vi[s:s + 1, :]
        outs.append(acc)
    return outs[0] if len(outs) == 1 else jnp.concatenate(outs, axis=0)


def _offdiag(q, k, v, bc, blk, mask):
    c = q.shape[0]
    nb = c // blk
    qs, ks, vs = [jnp.zeros((blk, DK), F32)], [], []
    for i in range(1, nb):
        lo = i * blk
        ref = bc[lo:lo + 1, :]
        qs.append(q[lo:lo + blk] * jnp.exp2(bc[lo:lo + blk] - ref))
        ks.append(k[:lo] * jnp.exp2(ref - bc[:lo]))
        vs.append(v[:lo])
    qt = jnp.concatenate(qs, axis=0).astype(BF16)
    kt = jnp.concatenate(ks, axis=0).astype(BF16)
    vt = jnp.concatenate(vs, axis=0).astype(BF16)
    a = lax.dot_general(qt, kt, (((1,), (1,)), ((), ())), preferred_element_type=F32)
    return jnp.dot((a * mask).astype(BF16), vt, preferred_element_type=F32)


def _hgrn_inputs(qa, fa, ia, lb):
    fg = lb + (1.0 - lb) * jax.nn.sigmoid(fa)
    return _silu(qa.astype(F32)) * (DK ** -0.5), 1.0 - fg, ia, jnp.log2(fg)


def _gla_inputs(qb, kb, vb, al, wa, ba):
    alpha = jnp.dot(al.astype(BF16), wa, preferred_element_type=F32) + ba
    logsig = jnp.minimum(alpha, 0.0) - jnp.log1p(jnp.exp(-jnp.abs(alpha)))
    return qb.astype(F32) * (DK ** -0.5), kb.astype(F32), vb, logsig * (LOG2_E / GATE_TEMP)


def _head_out(o, gn, gate):
    return (_rms(o, gn) * _silu(gate.astype(F32))).astype(BF16)


def _lower_bound(lbl_ref):
    x = lbl_ref[...]
    e = jnp.exp(x - jnp.max(x, axis=0, keepdims=True))
    return e[0:1, :] / jnp.sum(e, axis=0, keepdims=True)


def _stack_mask(c, blk):
    nb = c // blk
    rows = blk * nb * (nb + 1) // 2
    t = lax.broadcasted_iota(jnp.int32, (c, rows), 0)
    r = lax.broadcasted_iota(jnp.int32, (c, rows), 1)
    seg = jnp.zeros((c, rows), jnp.int32)
    off = jnp.zeros((c, rows), jnp.int32)
    for i in range(1, nb):
        start = blk * i * (i + 1) // 2
        seg = seg + jnp.where(r >= start, 1, 0)
        off = jnp.where(r >= start, start, off)
    keep = ((t >> (blk.bit_length() - 1)) == seg) & (r - off <= t)
    return jnp.where(keep, 1.0, 0.0).astype(F32)


def _state_update_t(k, vb, bc, st):
    c = k.shape[0]
    blast = bc[c - 1:c, :]
    kdec = (k * jnp.exp2(blast - bc)).astype(BF16)
    upd = lax.dot_general(vb, kdec, (((0,), (0,)), ((), ())), preferred_element_type=F32)
    if st is None:
        return upd
    return st * jnp.exp2(blast) + upd


def _scan_sublanes(p):
    row = lax.broadcasted_iota(jnp.int32, (SUBLANES, DK), 0)
    for s in (1, 2, 4):
        p = p + jnp.where(row >= s, pltpu.roll(p, s, axis=0), 0.0)
    return p


def _cumsum_chunk(g):
    out, carry = [], None
    for j in range(CHUNK // SUBLANES):
        p = _scan_sublanes(g[j * SUBLANES:(j + 1) * SUBLANES])
        if carry is not None:
            p = p + carry
        carry = p[SUBLANES - 1:SUBLANES, :]
        out.append(p)
    return jnp.concatenate(out, axis=0)


def _rescaled_operands(q, k, bc):
    c, blk = CHUNK, DIAG_BLK
    qs, ks = [], []
    for i in range(c // blk):
        lo, hi = i * blk, (i + 1) * blk
        ref = bc[lo:lo + 1, :]
        qs.append(q[lo:hi] * jnp.exp2(bc[lo:hi] - ref))
        ks.append(k[:hi] * jnp.exp2(ref - bc[:hi]))
    return ((q * jnp.exp2(bc)).astype(BF16), jnp.concatenate(qs, axis=0).astype(BF16),
            jnp.concatenate(ks, axis=0).T.astype(BF16),
            (k * jnp.exp2(bc[c - 1:c, :] - bc)).astype(BF16))


def _chunk_exact(q, k, v, bc, st, mask):
    o = lax.dot_general((q * jnp.exp2(bc)).astype(BF16), st.astype(BF16),
                        (((1,), (1,)), ((), ())), preferred_element_type=F32)
    o = o + _offdiag(q, k, v, bc, DIAG_BLK, mask) + _diag(q, k, v, bc, DIAG_BLK)
    return o, _state_update_t(k, v.astype(BF16), bc, st)


def _prompt_scan_kernel(*refs, gla, seq, meta):
    n_in = 11 if gla else 8
    if gla:
        q_ref, k_ref, v_ref, gate_ref, al_ref, mk_ref, mv_ref, mal_ref, wa_ref, ba_ref, gn_ref = (
            refs[:n_in])
    else:
        q_ref, f_ref, v_ref, gate_ref, mf_ref, mv_ref, lbl_ref, gn_ref = refs[:n_in]
        lb = _lower_bound(lbl_ref)
    o_ref, sout_ref = refs[n_in:n_in + 2]
    (st_ref, qs_ref, ks_ref, gs_ref, qd_ref, qt_ref, kt_ref, kdec_ref, eb_ref,
     meta_st_ref) = refs[n_in + 2:]
    trips = seq // (CHUNK * CHUNKS_PER_TRIP)
    nt = (((1,), (1,)), ((), ()))
    tn = (((0,), (0,)), ((), ()))

    def chunk_rows(ci, u, n=CHUNK):
        return pl.ds(pl.multiple_of((ci * CHUNKS_PER_TRIP + u) * n, n), n)

    @pl.when(pl.program_id(1) == 0)
    def _meta():
        if gla:
            _, mk, mv, mg = _gla_inputs(mk_ref[...], mk_ref[...], mv_ref[...], mal_ref[...],
                                        wa_ref[...], ba_ref[...])
        else:
            _, mk, mv, mg = _hgrn_inputs(mf_ref[...], mf_ref[...], mv_ref[...], lb)
        mbc = jnp.dot(_tri(meta, meta), mg, precision=HIGHEST, preferred_element_type=F32)
        meta_st_ref[...] = _state_update_t(mk, mv.astype(BF16), mbc, None)

    st_ref[...] = meta_st_ref[...]

    def rows_of(c, n=CHUNK):
        return pl.ds(pl.multiple_of(c * n, n), n)

    def rescaled_pass():
        mask = _stack_mask(CHUNK, DIAG_BLK)
        units = range(CHUNKS_PER_MXU_TRIP)
        mxu_trips = seq // (CHUNK * CHUNKS_PER_MXU_TRIP)

        def prepare(ti, extremes):
            gmin, kmax = extremes
            trip_rows = CHUNK * CHUNKS_PER_MXU_TRIP
            rows = pl.ds(pl.multiple_of(ti * trip_rows, trip_rows), trip_rows)
            if gla:
                q, k, _, g = _gla_inputs(q_ref[rows, :], k_ref[rows, :], None, al_ref[rows, :],
                                         wa_ref[...], ba_ref[...])
            else:
                q, k, _, g = _hgrn_inputs(q_ref[rows, :], f_ref[rows, :], None, lb)
            qs_ref[rows, :] = q
            ks_ref[rows, :] = k
            gs_ref[rows, :] = g
            for u in units:
                c = ti * CHUNKS_PER_MXU_TRIP + u
                sl = slice(u * CHUNK, (u + 1) * CHUNK)
                bc = _cumsum_chunk(g[sl])
                qd, qt, kt, kdec = _rescaled_operands(q[sl], k[sl], bc)
                qd_ref[rows_of(c), :] = qd
                qt_ref[rows_of(c), :] = qt
                kt_ref[rows_of(c, DK), :] = kt
                kdec_ref[rows_of(c), :] = kdec
                eb_ref[c] = jnp.broadcast_to(jnp.exp2(bc[CHUNK - 1:CHUNK, :]), (SUBLANES, DK))
            return (jnp.minimum(gmin, jnp.min(g, axis=0, keepdims=True)),
                    jnp.maximum(kmax, jnp.max(jnp.abs(k), axis=0, keepdims=True)))

        def contract(ti):
            cs = [ti * CHUNKS_PER_MXU_TRIP + u for u in units]
            vb = [v_ref[rows_of(c), :].astype(BF16) for c in cs]
            a = [jnp.dot(qt_ref[rows_of(c), :], kt_ref[rows_of(c, DK), :],
                         preferred_element_type=F32) for c in cs]
            am = [(x * mask).astype(BF16) for x in a]
            vt = [jnp.concatenate([x[:(i + 1) * DIAG_BLK] for i in range(CHUNK // DIAG_BLK)], axis=0)
                  for x in vb]
            oi = [jnp.dot(am[u], vt[u], preferred_element_type=F32) for u in units]
            inc = [lax.dot_general(vb[u], kdec_ref[rows_of(cs[u]), :], tn,
                                   preferred_element_type=F32) for u in units]
            sts = [st_ref[...]]
            for u in units:
                sts.append(sts[-1] * eb_ref[cs[u]][0:1, :] + inc[u])
            o = [oi[u] + lax.dot_general(qd_ref[rows_of(cs[u]), :], sts[u].astype(BF16), nt,
                                         preferred_element_type=F32) for u in units]
            for u in units:
                o_ref[rows_of(cs[u]), :] = _head_out(o[u], gn_ref[...], gate_ref[rows_of(cs[u]), :])
            st_ref[...] = sts[-1]

        extremes = prepare(0, (jnp.zeros((1, DK), F32), jnp.zeros((1, DK), F32)))

        def body(ti, extremes):
            contract(ti)
            return prepare(ti + 1, extremes)

        extremes = lax.fori_loop(0, mxu_trips - 1, body, extremes)
        contract(mxu_trips - 1)
        return extremes

    gmin, kmax = rescaled_pass()
    in_range = (jnp.min(gmin) >= FAST_MIN_LOG_GATE) & (jnp.max(kmax) <= FAST_MAX_KEY)

    @pl.when(jnp.logical_not(in_range))
    def _exact():
        mask = _off_mask(CHUNK, DIAG_BLK)
        st_ref[...] = meta_st_ref[...]

        def body(ci, carry):
            st = st_ref[...]
            for u in range(CHUNKS_PER_TRIP):
                rows = chunk_rows(ci, u)
                o, st = _chunk_exact(qs_ref[rows, :], ks_ref[rows, :], v_ref[rows, :],
                                     _cumsum_chunk(gs_ref[rows, :]), st, mask)
                o_ref[rows, :] = _head_out(o, gn_ref[...], gate_ref[rows, :])
            st_ref[...] = st
            return carry

        lax.fori_loop(0, trips, body, 0)

    sout_ref[...] = st_ref[...].T


def _sample_scan_kernel(*refs, gla, steps):
    if gla:
        (q_ref, k_ref, v_ref, gate_ref, al_ref, wa_ref, ba_ref, gn_ref, s0_ref,
         o_ref, sout_ref) = refs
        q, k, v, g = _gla_inputs(q_ref[...], k_ref[...], v_ref[...], al_ref[...],
                                 wa_ref[...], ba_ref[...])
    else:
        (q_ref, f_ref, v_ref, gate_ref, lbl_ref, gn_ref, s0_ref, o_ref, sout_ref) = refs
        q, k, v, g = _hgrn_inputs(q_ref[...], f_ref[...], v_ref[...], _lower_bound(lbl_ref))
    v32 = v.astype(F32)
    rows = SAMPLE_SEQS * steps
    groups = [slice(b * steps, (b + 1) * steps) for b in range(SAMPLE_SEQS)]
    tn = (((0,), (0,)), ((), ()))

    bc = jnp.concatenate([_scan_sublanes(g[sl]) for sl in groups], axis=0)

    def repeated(r):
        return jnp.concatenate([jnp.broadcast_to(bc[sl][r:r + 1, :], (steps, DK)) for sl in groups],
                               axis=0)

    qd = (q * jnp.exp2(bc)).astype(BF16)
    kdec = (k * jnp.exp2(repeated(steps - 1) - bc)).astype(BF16)
    r = lax.broadcasted_iota(jnp.int32, (rows, SAMPLE_SEQS), 0)
    c = lax.broadcasted_iota(jnp.int32, (rows, SAMPLE_SEQS), 1)
    selector = jnp.where((r >> (steps.bit_length() - 1)) == c, 1.0, 0.0).astype(F32)
    decay = jnp.exp2(lax.dot_general(g, selector, tn, precision=HIGHEST,
                                    preferred_element_type=F32))

    inter = [jnp.dot(qd[sl], s0_ref[b].astype(BF16), preferred_element_type=F32)
             for b, sl in enumerate(groups)]
    upd = [lax.dot_general(kdec[sl], v32[sl].astype(BF16), tn, preferred_element_type=F32)
           for sl in groups]
    for b in range(SAMPLE_SEQS):
        sout_ref[b] = decay[:, b:b + 1] * s0_ref[b] + upd[b]
    o = jnp.concatenate(inter, axis=0) + _diag(q, k, v32, bc, steps)
    o_ref[...] = _head_out(o, gn_ref[...], gate_ref[...])


def _proj_spec(rows, width, c0, per_head, row_block, tile_width=TN_PROJ, head_major=False):
    per_tile = tile_width // width

    def index(i0, i1):
        b, h = (i1, i0) if head_major else (i0, i1)
        g = c0 * LANES // width + h * per_head
        return (g // per_tile, row_block(b), g % per_tile)

    return pl.BlockSpec((None, rows, width), index)


def _prompt_scan(p, pm, *, gla, batch, seq, meta_block, small, name):
    heads, dv = (H_B, DV_B) if gla else (H_A, DV_A)
    n_chunks = seq // CHUNK
    blocks = CHUNK // DIAG_BLK
    stack = DIAG_BLK * blocks * (blocks + 1) // 2
    p16, pf, pa = p
    pm16, pmf, pma = pm
    tok = functools.partial(_proj_spec, seq, row_block=lambda b: b, head_major=True)
    met = functools.partial(_proj_spec, N_META, row_block=lambda b: meta_block, head_major=True)
    vec = lambda width: pl.BlockSpec((1, width), lambda h, b: (0, h))
    if gla:
        wa, ba, gn = small
        in_specs = [tok(DK, COL_QB, 1), tok(DK, COL_KB, 1), tok(dv, COL_VB, 1), tok(dv, COL_RB, 1),
                    tok(LANES, 0, 0, tile_width=LANES),
                    met(DK, COL_KB, 1), met(dv, COL_VB, 1), met(LANES, 0, 0, tile_width=LANES),
                    pl.BlockSpec((LANES, DK), lambda h, b: (0, h)), vec(DK), vec(dv)]
        args = (p16, p16, p16, p16, pa, pm16, pm16, pma, wa, ba, gn)
    else:
        lbl, gn = small
        in_specs = [tok(DK, COL_QA, 1), tok(DK, 0, 1), tok(dv, COL_IA, 1), tok(dv, COL_GA, 1),
                    met(DK, 0, 1), met(dv, COL_IA, 1),
                    pl.BlockSpec((2, DK), lambda h, b: (0, h)), vec(dv)]
        args = (p16, pf, p16, p16, pmf, pm16, lbl, gn)
    return pl.pallas_call(
        functools.partial(_prompt_scan_kernel, gla=gla, seq=seq, meta=N_META),
        grid=(heads, batch),
        in_specs=in_specs,
        out_specs=[pl.BlockSpec((seq, dv), lambda h, b: (b, h)),
                   pl.BlockSpec((None, None, DK, dv), lambda h, b: (b, h, 0, 0))],
        out_shape=[jax.ShapeDtypeStruct((batch * seq, heads * dv), BF16),
                   jax.ShapeDtypeStruct((batch, heads, DK, dv), F32)],
        scratch_shapes=[
            pltpu.VMEM((dv, DK), F32),
            pltpu.VMEM((seq, DK), F32),
            pltpu.VMEM((seq, DK), F32),
            pltpu.VMEM((seq, DK), F32),
            pltpu.VMEM((seq, DK), BF16),
            pltpu.VMEM((seq, DK), BF16),
            pltpu.VMEM((n_chunks * DK, stack), BF16),
            pltpu.VMEM((seq, DK), BF16),
            pltpu.VMEM((n_chunks, SUBLANES, DK), F32),
            pltpu.VMEM((dv, DK), F32),
        ],
        compiler_params=_params("parallel", "arbitrary"),
        name=name,
    )(*args)


def _sample_scan(p, s0, *, gla, batch, steps, small, name):
    assert steps == SUBLANES, "the sample recurrence keeps one sequence per sublane group"
    heads, dv = (H_B, DV_B) if gla else (H_A, DV_A)
    rows = SAMPLE_SEQS * steps
    p16, pf, pa = p
    tok = functools.partial(_proj_spec, rows, row_block=lambda b: b)
    vec = lambda width: pl.BlockSpec((1, width), lambda b, h: (0, h))
    state = pl.BlockSpec((SAMPLE_SEQS, None, DK, dv), lambda b, h: (b, h, 0, 0))
    if gla:
        wa, ba, gn = small
        in_specs = [tok(DK, COL_QB, 1), tok(DK, COL_KB, 1), tok(dv, COL_VB, 1), tok(dv, COL_RB, 1),
                    tok(LANES, 0, 0, tile_width=LANES),
                    pl.BlockSpec((LANES, DK), lambda b, h: (0, h)), vec(DK), vec(dv), state]
        args = (p16, p16, p16, p16, pa, wa, ba, gn, s0)
    else:
        lbl, gn = small
        in_specs = [tok(DK, COL_QA, 1), tok(DK, 0, 1), tok(dv, COL_IA, 1), tok(dv, COL_GA, 1),
                    pl.BlockSpec((2, DK), lambda b, h: (0, h)), vec(dv), state]
        args = (p16, pf, p16, p16, lbl, gn, s0)
    return pl.pallas_call(
        functools.partial(_sample_scan_kernel, gla=gla, steps=steps),
        grid=(batch // SAMPLE_SEQS, heads),
        in_specs=in_specs,
        out_specs=[pl.BlockSpec((rows, dv), lambda b, h: (b, h)), state],
        out_shape=[jax.ShapeDtypeStruct((batch * steps, heads * dv), BF16),
                   jax.ShapeDtypeStruct((batch, heads, DK, dv), F32)],
        compiler_params=_params("parallel", "parallel"),
        name=name,
    )(*args)


def kernel(x_prompt, x_sample, state_hgrn, state_gla, meta_tokens, lb_logits, ffn1_norm, w_ffn1_in,
           w_ffn1_out, mix_norm, w_in, w_alpha_up, b_alpha, gnorm_a, gnorm_b, w_out, ffn2_norm,
           w_ffn2_in, w_ffn2_out, final_norm):
    batch, seq, _ = x_prompt.shape
    dec_batch, steps, _ = x_sample.shape

    w_in_t = w_in[0].T
    wop = w_out[0]
    wa = jnp.pad(w_alpha_up[0], ((0, LANES - ALPHA_RANK), (0, 0))).astype(BF16)
    n1, nm, n2, nf = ffn1_norm, mix_norm, ffn2_norm, final_norm[None]
    small_a = (lb_logits, gnorm_a)
    small_b = (wa, b_alpha, gnorm_b)
    n_sample = dec_batch * steps

    xs = (x_sample.reshape(n_sample, D_MODEL), meta_tokens)
    n_stream = n_sample + N_META
    h1s, u1s, *w1 = _ffn(xs, n1, (w_ffn1_in[0], w_ffn1_in[0], w_ffn1_out[0]), nm, tm=n_stream,
                         tf=TF_CAST, final_norm=False, emit_weights=True, name="ffn1_sample")
    ps = _in_proj(u1s, w_in_t, tm=n_stream, name="inproj_sample")
    oa_s, sa_s = _sample_scan(ps, state_hgrn[0], gla=False, batch=dec_batch, steps=steps,
                              small=small_a, name="scan_hgrn_sample")
    ob_s, sb_s = _sample_scan(ps, state_gla[0], gla=True, batch=dec_batch, steps=steps,
                              small=small_b, name="scan_gla_sample")
    h2s = _out_proj(oa_s, ob_s, h1s, wop, tm=TM, name="outproj_sample")
    y_s, *w2 = _ffn(h2s, n2, (w_ffn2_in[0], w_ffn2_in[0], w_ffn2_out[0]), nf, tm=n_sample,
                    tf=TF_CAST, final_norm=True, emit_weights=True, name="ffn2_sample")

    h1p, u1p = _ffn(x_prompt.reshape(batch * seq, D_MODEL), n1, w1, nm, tm=TM_FFN, tf=TF,
                    final_norm=False, emit_weights=False, name="ffn1_prompt")
    pp = _in_proj(u1p, w_in_t, tm=TM_PROJ, name="inproj_prompt")
    meta_block = n_sample // N_META
    oa_p, sa_p = _prompt_scan(pp, ps, gla=False, batch=batch, seq=seq, meta_block=meta_block,
                              small=small_a, name="scan_hgrn_prompt")
    ob_p, sb_p = _prompt_scan(pp, ps, gla=True, batch=batch, seq=seq, meta_block=meta_block,
                              small=small_b, name="scan_gla_prompt")
    h2p = _out_proj(oa_p, ob_p, h1p, wop, tm=TM, name="outproj_prompt")
    (y_p,) = _ffn(h2p, n2, w2, nf, tm=TM_FFN, tf=TF, final_norm=True, emit_weights=False,
                  name="ffn2_prompt")

    return (y_p.reshape(batch, seq, D_MODEL), y_s.reshape(dec_batch, steps, D_MODEL),
            sa_p[None], sb_p[None], sa_s[None], sb_s[None])
```

```python
import functools

import jax
import jax.numpy as jnp
from jax import lax
from jax.experimental import pallas as pl
from jax.experimental.pallas import tpu as pltpu

F32 = jnp.float32
BF16 = jnp.bfloat16
HIGHEST = lax.Precision.HIGHEST

D_MODEL = 2048
N_META = 16
DK = 128
H_A, DV_A = 8, 128
H_B, DV_B = 4, 256
D_HALF = H_A * DV_A
ALPHA_RANK = 16
GATE_TEMP = 16.0
D_FF = 5632
CHUNK = 64
EPS = 1e-6
IN_PROJ_WIDTHS = (D_HALF, D_HALF, D_HALF, D_HALF, H_B * DK, H_B * DK, D_HALF, D_HALF, ALPHA_RANK)

LANES = 128
SUBLANES = 8
VMEM_LIMIT_BYTES = 56 * 1024 * 1024

COL_QA, COL_IA, COL_GA, COL_QB, COL_KB, COL_VB, COL_RB = 0, 8, 16, 24, 28, 32, 40
P16_COLS = 3 * D_HALF + 2 * H_B * DK + 2 * D_HALF

TM = 512
TM_FFN = 1024
TF = 256
TF_CAST = 256
TN_PROJ = 512
TM_PROJ = 2048
N16_TILES = P16_COLS // TN_PROJ
DIAG_BLK = 16
SAMPLE_SEQS = 32
CHUNKS_PER_TRIP = 4
CHUNKS_PER_MXU_TRIP = 16

LOG2_E = 1.4426950408889634

FAST_MIN_LOG_GATE = -60.0 * LOG2_E / (DIAG_BLK - 1)
FAST_MAX_KEY = 1e10


def _rms(x, w):
    return x * lax.rsqrt(jnp.mean(x * x, axis=-1, keepdims=True) + EPS) * w


def _silu(x):
    return x * jax.nn.sigmoid(x)


def _params(*sem):
    return pltpu.CompilerParams(dimension_semantics=sem, vmem_limit_bytes=VMEM_LIMIT_BYTES)


def _ffn_kernel(*refs, n_pieces, final_norm, emit_weights):
    h_refs = refs[:n_pieces]
    nw_ref, wg_ref, wu_ref, wo_ref, tnw_ref, out_ref, *rest = refs[n_pieces:]
    u_ref = rest[-1]
    j = pl.program_id(1)
    last_j = pl.num_programs(1) - 1

    def load_h():
        pieces = [r[...] for r in h_refs]
        return pieces[0] if n_pieces == 1 else jnp.concatenate(pieces, axis=0)

    def step(first, last):
        if first:
            u = _rms(load_h(), nw_ref[...]).astype(BF16)
            u_ref[...] = u
        else:
            u = u_ref[...]
        wg, wu, wo = (r[...].astype(BF16) for r in (wg_ref, wu_ref, wo_ref))
        if emit_weights:
            for dst, w in zip(rest[-4:-1], (wg, wu, wo)):
                dst[...] = w
        g = jnp.dot(u, wg, preferred_element_type=F32)
        up = jnp.dot(u, wu, preferred_element_type=F32)
        a = (_silu(g) * up).astype(BF16)
        acc = jnp.dot(a, wo, preferred_element_type=F32)
        if not first:
            acc = out_ref[...] + acc
        if not last:
            out_ref[...] = acc
            return
        y = load_h() + 0.5 * acc
        normed = _rms(y, tnw_ref[...])
        if final_norm:
            out_ref[...] = normed
        else:
            out_ref[...] = y
            rest[0][...] = normed.astype(BF16)

    pl.when(j == 0)(functools.partial(step, True, False))
    pl.when((j > 0) & (j < last_j))(functools.partial(step, False, False))
    pl.when(j == last_j)(functools.partial(step, False, True))


def _ffn(h, norm_w, weights, tail_norm_w, *, tm, tf, final_norm, emit_weights, name):
    pieces = h if isinstance(h, tuple) else (h,)
    m = sum(p.shape[0] for p in pieces)
    nf = D_FF // tf
    rows = pl.BlockSpec((tm, D_MODEL), lambda i, j: (i, 0),
                        pipeline_mode=pl.Buffered(1) if m == tm else None)
    if len(pieces) == 1:
        h_specs = [rows]
    else:
        assert m == tm, "stacked inputs form one row tile"
        h_specs = [pl.BlockSpec((p.shape[0], D_MODEL), lambda i, j: (0, 0),
                                pipeline_mode=pl.Buffered(1)) for p in pieces]
    vec = pl.BlockSpec((1, D_MODEL), lambda i, j: (0, 0))
    col_tile = pl.BlockSpec((D_MODEL, tf), lambda i, j: (0, j))
    row_tile = pl.BlockSpec((tf, D_MODEL), lambda i, j: (j, 0))
    out_specs = [rows]
    out_shape = [jax.ShapeDtypeStruct((m, D_MODEL), F32)]
    if not final_norm:
        out_specs += [rows]
        out_shape += [jax.ShapeDtypeStruct((m, D_MODEL), BF16)]
    if emit_weights:
        assert m == tm, "weight casts are written once, by a single row tile"
        up_tile = pl.BlockSpec((D_MODEL, tf), lambda i, j: (0, j + nf))
        out_specs += [col_tile, col_tile, row_tile]
        out_shape += [jax.ShapeDtypeStruct((D_MODEL, D_FF), BF16)] * 2
        out_shape += [jax.ShapeDtypeStruct((D_FF, D_MODEL), BF16)]
    else:
        up_tile = col_tile
    return pl.pallas_call(
        functools.partial(_ffn_kernel, n_pieces=len(pieces), final_norm=final_norm,
                          emit_weights=emit_weights),
        grid=(m // tm, nf),
        in_specs=h_specs + [vec, col_tile, up_tile, row_tile, vec],
        out_specs=out_specs,
        out_shape=out_shape,
        scratch_shapes=[pltpu.VMEM((tm, D_MODEL), BF16)],
        compiler_params=_params("parallel", "arbitrary"),
        name=name,
    )(*pieces, norm_w, *weights, tail_norm_w)


def _proj_kernel(u_ref, w_ref, wal_ref, p16_ref, pf_ref, pa_ref):
    j = pl.program_id(1)
    nt = (((1,), (1,)), ((), ()))

    @pl.when(j == 0)
    def _low_rank():
        wal = jnp.concatenate([wal_ref[...].astype(BF16),
                               jnp.zeros((LANES - ALPHA_RANK, D_MODEL), BF16)], axis=0)
        pa_ref[...] = lax.dot_general(u_ref[...], wal, nt, preferred_element_type=F32)

    def project():
        return lax.dot_general(u_ref[...], w_ref[...].astype(BF16), nt,
                               preferred_element_type=F32)

    @pl.when(j < N16_TILES)
    def _narrow():
        p16_ref[...] = project().astype(BF16)

    @pl.when(j >= N16_TILES)
    def _wide():
        pf_ref[...] = project()


def _w_in_tile(j):
    fa_lo, fa_hi = D_HALF // TN_PROJ, 2 * D_HALF // TN_PROJ
    moved = fa_hi - fa_lo
    return jnp.where(j < fa_lo, j, jnp.where(j < N16_TILES, j + moved, j - N16_TILES + fa_lo))


def _in_proj(u, w, *, tm, name):
    m = u.shape[0]
    return pl.pallas_call(
        _proj_kernel,
        grid=(m // tm, (P16_COLS + D_HALF) // TN_PROJ),
        in_specs=[
            pl.BlockSpec((tm, D_MODEL), lambda i, j: (i, 0)),
            pl.BlockSpec((TN_PROJ, D_MODEL), lambda i, j: (_w_in_tile(j), 0)),
            pl.BlockSpec((ALPHA_RANK, D_MODEL), lambda i, j: ((P16_COLS + D_HALF) // ALPHA_RANK, 0)),
        ],
        out_specs=[
            pl.BlockSpec((None, tm, TN_PROJ), lambda i, j: (jnp.minimum(j, N16_TILES - 1), i, 0)),
            pl.BlockSpec((None, tm, TN_PROJ), lambda i, j: (jnp.maximum(j - N16_TILES, 0), i, 0)),
            pl.BlockSpec((None, tm, LANES), lambda i, j: (0, i, 0)),
        ],
        out_shape=[jax.ShapeDtypeStruct((N16_TILES, m, TN_PROJ), BF16),
                   jax.ShapeDtypeStruct((D_HALF // TN_PROJ, m, TN_PROJ), F32),
                   jax.ShapeDtypeStruct((1, m, LANES), F32)],
        compiler_params=_params("parallel", "arbitrary"),
        name=name,
    )(u, w, w)


def _out_proj_kernel(oa_ref, ob_ref, h_ref, wa_ref, wb_ref, out_ref):
    out_ref[...] = (h_ref[...]
                    + jnp.dot(oa_ref[...], wa_ref[...].astype(BF16), preferred_element_type=F32)
                    + jnp.dot(ob_ref[...], wb_ref[...].astype(BF16), preferred_element_type=F32))


def _out_proj(oa, ob, h, w, *, tm, name):
    m = oa.shape[0]
    return pl.pallas_call(
        _out_proj_kernel,
        grid=(m // tm,),
        in_specs=[
            pl.BlockSpec((tm, D_HALF), lambda i: (i, 0)),
            pl.BlockSpec((tm, D_HALF), lambda i: (i, 0)),
            pl.BlockSpec((tm, D_MODEL), lambda i: (i, 0)),
            pl.BlockSpec((D_HALF, D_MODEL), lambda i: (0, 0), pipeline_mode=pl.Buffered(1)),
            pl.BlockSpec((D_HALF, D_MODEL), lambda i: (1, 0), pipeline_mode=pl.Buffered(1)),
        ],
        out_specs=pl.BlockSpec((tm, D_MODEL), lambda i: (i, 0)),
        out_shape=jax.ShapeDtypeStruct((m, D_MODEL), F32),
        compiler_params=_params("parallel"),
        name=name,
    )(oa, ob, h, w, w)


def _tri(c, seq_len):
    r = lax.broadcasted_iota(jnp.int32, (c, c), 0)
    s = lax.broadcasted_iota(jnp.int32, (c, c), 1)
    return jnp.where((r >= s) & ((r ^ s) < seq_len), 1.0, 0.0).astype(F32)


def _off_mask(c, blk):
    nb = c // blk
    rows = nb * (nb - 1) // 2 * blk
    r = lax.broadcasted_iota(jnp.int32, (c, rows), 0) >> (blk.bit_length() - 1)
    s = lax.broadcasted_iota(jnp.int32, (c, rows), 1)
    seg = jnp.zeros((c, rows), jnp.int32)
    for i in range(1, nb):
        seg = seg + jnp.where(s >= blk * i * (i - 1) // 2, 1, 0)
    return jnp.where(r == seg, 1.0, 0.0).astype(F32)


def _diag(q, k, v, bc, blk):
    c = q.shape[0]
    row = lax.broadcasted_iota(jnp.int32, (blk, 1), 0)
    outs = []
    for i in range(c // blk):
        sl = slice(i * blk, (i + 1) * blk)
        qi, ki, vi, bi = q[sl], k[sl], v[sl].astype(F32), bc[sl]
        acc = jnp.zeros((blk, v.shape[1]), F32)
        for s in range(blk):
            e = jnp.exp2(jnp.minimum(bi - bi[s:s + 1, :], 0.0))
            col = jnp.sum(qi * ki[s:s + 1, :] * e, axis=-1, keepdims=True)
            col = jnp.where(row >= s, col, 0.0)
            acc = acc + col * vi[s:s + 1, :]
        outs.append(acc)
    return outs[0] if len(outs) == 1 else jnp.concatenate(outs, axis=0)


def _offdiag(q, k, v, bc, blk, mask):
    c = q.shape[0]
    nb = c // blk
    qs, ks, vs = [jnp.zeros((blk, DK), F32)], [], []
    for i in range(1, nb):
        lo = i * blk
        ref = bc[lo:lo + 1, :]
        qs.append(q[lo:lo + blk] * jnp.exp2(bc[lo:lo + blk] - ref))
        ks.append(k[:lo] * jnp.exp2(ref - bc[:lo]))
        vs.append(v[:lo])
    qt = jnp.concatenate(qs, axis=0).astype(BF16)
    kt = jnp.concatenate(ks, axis=0).astype(BF16)
    vt = jnp.concatenate(vs, axis=0).astype(BF16)
    a = lax.dot_general(qt, kt, (((1,), (1,)), ((), ())), preferred_element_type=F32)
    return jnp.dot((a * mask).astype(BF16), vt, preferred_element_type=F32)


def _hgrn_inputs(qa, fa, ia, lb):
    fg = lb + (1.0 - lb) * jax.nn.sigmoid(fa)
    return _silu(qa.astype(F32)) * (DK ** -0.5), 1.0 - fg, ia, jnp.log2(fg)


def _gla_inputs(qb, kb, vb, al, wa, ba):
    alpha = jnp.dot(al.astype(BF16), wa, preferred_element_type=F32) + ba
    logsig = jnp.minimum(alpha, 0.0) - jnp.log1p(jnp.exp(-jnp.abs(alpha)))
    return qb.astype(F32) * (DK ** -0.5), kb.astype(F32), vb, logsig * (LOG2_E / GATE_TEMP)


def _head_out(o, gn, gate):
    return (_rms(o, gn) * _silu(gate.astype(F32))).astype(BF16)


def _lower_bound(lbl_ref):
    x = lbl_ref[...]
    e = jnp.exp(x - jnp.max(x, axis=0, keepdims=True))
    return e[0:1, :] / jnp.sum(e, axis=0, keepdims=True)


def _stack_mask(c, blk):
    nb = c // blk
    rows = blk * nb * (nb + 1) // 2
    t = lax.broadcasted_iota(jnp.int32, (c, rows), 0)
    r = lax.broadcasted_iota(jnp.int32, (c, rows), 1)
    seg = jnp.zeros((c, rows), jnp.int32)
    off = jnp.zeros((c, rows), jnp.int32)
    for i in range(1, nb):
        start = blk * i * (i + 1) // 2
        seg = seg + jnp.where(r >= start, 1, 0)
        off = jnp.where(r >= start, start, off)
    keep = ((t >> (blk.bit_length() - 1)) == seg) & (r - off <= t)
    return jnp.where(keep, 1.0, 0.0).astype(F32)


def _state_update_t(k, vb, bc, st):
    c = k.shape[0]
    blast = bc[c - 1:c, :]
    kdec = (k * jnp.exp2(blast - bc)).astype(BF16)
    upd = lax.dot_general(vb, kdec, (((0,), (0,)), ((), ())), preferred_element_type=F32)
    if st is None:
        return upd
    return st * jnp.exp2(blast) + upd


def _scan_sublanes(p):
    row = lax.broadcasted_iota(jnp.int32, (SUBLANES, DK), 0)
    for s in (1, 2, 4):
        p = p + jnp.where(row >= s, pltpu.roll(p, s, axis=0), 0.0)
    return p


def _cumsum_chunk(g):
    out, carry = [], None
    for j in range(CHUNK // SUBLANES):
        p = _scan_sublanes(g[j * SUBLANES:(j + 1) * SUBLANES])
        if carry is not None:
            p = p + carry
        carry = p[SUBLANES - 1:SUBLANES, :]
        out.append(p)
    return jnp.concatenate(out, axis=0)


def _rescaled_operands(q, k, bc):
    c, blk = CHUNK, DIAG_BLK
    qs, ks = [], []
    for i in range(c // blk):
        lo, hi = i * blk, (i + 1) * blk
        ref = bc[lo:lo + 1, :]
        qs.append(q[lo:hi] * jnp.exp2(bc[lo:hi] - ref))
        ks.append(k[:hi] * jnp.exp2(ref - bc[:hi]))
    return ((q * jnp.exp2(bc)).astype(BF16), jnp.concatenate(qs, axis=0).astype(BF16),
            jnp.concatenate(ks, axis=0).T.astype(BF16),
            (k * jnp.exp2(bc[c - 1:c, :] - bc)).astype(BF16))


def _chunk_exact(q, k, v, bc, st, mask):
    o = lax.dot_general((q * jnp.exp2(bc)).astype(BF16), st.astype(BF16),
                        (((1,), (1,)), ((), ())), preferred_element_type=F32)
    o = o + _offdiag(q, k, v, bc, DIAG_BLK, mask) + _diag(q, k, v, bc, DIAG_BLK)
    return o, _state_update_t(k, v.astype(BF16), bc, st)


def _prompt_scan_kernel(*refs, gla, seq, meta):
    n_in = 11 if gla else 8
    if gla:
        q_ref, k_ref, v_ref, gate_ref, al_ref, mk_ref, mv_ref, mal_ref, wa_ref, ba_ref, gn_ref = (
            refs[:n_in])
    else:
        q_ref, f_ref, v_ref, gate_ref, mf_ref, mv_ref, lbl_ref, gn_ref = refs[:n_in]
        lb = _lower_bound(lbl_ref)
    o_ref, sout_ref = refs[n_in:n_in + 2]
    (st_ref, qs_ref, ks_ref, gs_ref, qd_ref, qt_ref, kt_ref, kdec_ref, eb_ref,
     meta_st_ref) = refs[n_in + 2:]
    trips = seq // (CHUNK * CHUNKS_PER_TRIP)
    nt = (((1,), (1,)), ((), ()))
    tn = (((0,), (0,)), ((), ()))

    def chunk_rows(ci, u, n=CHUNK):
        return pl.ds(pl.multiple_of((ci * CHUNKS_PER_TRIP + u) * n, n), n)

    @pl.when(pl.program_id(1) == 0)
    def _meta():
        if gla:
            _, mk, mv, mg = _gla_inputs(mk_ref[...], mk_ref[...], mv_ref[...], mal_ref[...],
                                        wa_ref[...], ba_ref[...])
        else:
            _, mk, mv, mg = _hgrn_inputs(mf_ref[...], mf_ref[...], mv_ref[...], lb)
        mbc = jnp.dot(_tri(meta, meta), mg, precision=HIGHEST, preferred_element_type=F32)
        meta_st_ref[...] = _state_update_t(mk, mv.astype(BF16), mbc, None)

    st_ref[...] = meta_st_ref[...]

    def rows_of(c, n=CHUNK):
        return pl.ds(pl.multiple_of(c * n, n), n)

    def rescaled_pass():
        mask = _stack_mask(CHUNK, DIAG_BLK)
        units = range(CHUNKS_PER_MXU_TRIP)
        mxu_trips = seq // (CHUNK * CHUNKS_PER_MXU_TRIP)

        def prepare(ti, extremes):
            gmin, kmax = extremes
            trip_rows = CHUNK * CHUNKS_PER_MXU_TRIP
            rows = pl.ds(pl.multiple_of(ti * trip_rows, trip_rows), trip_rows)
            if gla:
                q, k, _, g = _gla_inputs(q_ref[rows, :], k_ref[rows, :], None, al_ref[rows, :],
                                         wa_ref[...], ba_ref[...])
            else:
                q, k, _, g = _hgrn_inputs(q_ref[rows, :], f_ref[rows, :], None, lb)
            qs_ref[rows, :] = q
            ks_ref[rows, :] = k
            gs_ref[rows, :] = g
            for u in units:
                c = ti * CHUNKS_PER_MXU_TRIP + u
                sl = slice(u * CHUNK, (u + 1) * CHUNK)
                bc = _cumsum_chunk(g[sl])
                qd, qt, kt, kdec = _rescaled_operands(q[sl], k[sl], bc)
                qd_ref[rows_of(c), :] = qd
                qt_ref[rows_of(c), :] = qt
                kt_ref[rows_of(c, DK), :] = kt
                kdec_ref[rows_of(c), :] = kdec
                eb_ref[c] = jnp.broadcast_to(jnp.exp2(bc[CHUNK - 1:CHUNK, :]), (SUBLANES, DK))
            return (jnp.minimum(gmin, jnp.min(g, axis=0, keepdims=True)),
                    jnp.maximum(kmax, jnp.max(jnp.abs(k), axis=0, keepdims=True)))

        def contract(ti):
            cs = [ti * CHUNKS_PER_MXU_TRIP + u for u in units]
            vb = [v_ref[rows_of(c), :].astype(BF16) for c in cs]
            a = [jnp.dot(qt_ref[rows_of(c), :], kt_ref[rows_of(c, DK), :],
                         preferred_element_type=F32) for c in cs]
            am = [(x * mask).astype(BF16) for x in a]
            vt = [jnp.concatenate([x[:(i + 1) * DIAG_BLK] for i in range(CHUNK // DIAG_BLK)], axis=0)
                  for x in vb]
            oi = [jnp.dot(am[u], vt[u], preferred_element_type=F32) for u in units]
            inc = [lax.dot_general(vb[u], kdec_ref[rows_of(cs[u]), :], tn,
                                   preferred_element_type=F32) for u in units]
            sts = [st_ref[...]]
            for u in units:
                sts.append(sts[-1] * eb_ref[cs[u]][0:1, :] + inc[u])
            o = [oi[u] + lax.dot_general(qd_ref[rows_of(cs[u]), :], sts[u].astype(BF16), nt,
                                         preferred_element_type=F32) for u in units]
            for u in units:
                o_ref[rows_of(cs[u]), :] = _head_out(o[u], gn_ref[...], gate_ref[rows_of(cs[u]), :])
            st_ref[...] = sts[-1]

        extremes = prepare(0, (jnp.zeros((1, DK), F32), jnp.zeros((1, DK), F32)))

        def body(ti, extremes):
            contract(ti)
            return prepare(ti + 1, extremes)

        extremes = lax.fori_loop(0, mxu_trips - 1, body, extremes)
        contract(mxu_trips - 1)
        return extremes

    gmin, kmax = rescaled_pass()
    in_range = (jnp.min(gmin) >= FAST_MIN_LOG_GATE) & (jnp.max(kmax) <= FAST_MAX_KEY)

    @pl.when(jnp.logical_not(in_range))
    def _exact():
        mask = _off_mask(CHUNK, DIAG_BLK)
        st_ref[...] = meta_st_ref[...]

        def body(ci, carry):
            st = st_ref[...]
            for u in range(CHUNKS_PER_TRIP):
                rows = chunk_rows(ci, u)
                o, st = _chunk_exact(qs_ref[rows, :], ks_ref[rows, :], v_ref[rows, :],
                                     _cumsum_chunk(gs_ref[rows, :]), st, mask)
                o_ref[rows, :] = _head_out(o, gn_ref[...], gate_ref[rows, :])
            st_ref[...] = st
            return carry

        lax.fori_loop(0, trips, body, 0)

    sout_ref[...] = st_ref[...].T


def _sample_scan_kernel(*refs, gla, steps):
    if gla:
        (q_ref, k_ref, v_ref, gate_ref, al_ref, wa_ref, ba_ref, gn_ref, s0_ref,
         o_ref, sout_ref) = refs
        q, k, v, g = _gla_inputs(q_ref[...], k_ref[...], v_ref[...], al_ref[...],
                                 wa_ref[...], ba_ref[...])
    else:
        (q_ref, f_ref, v_ref, gate_ref, lbl_ref, gn_ref, s0_ref, o_ref, sout_ref) = refs
        q, k, v, g = _hgrn_inputs(q_ref[...], f_ref[...], v_ref[...], _lower_bound(lbl_ref))
    v32 = v.astype(F32)
    rows = SAMPLE_SEQS * steps
    groups = [slice(b * steps, (b + 1) * steps) for b in range(SAMPLE_SEQS)]
    tn = (((0,), (0,)), ((), ()))

    bc = jnp.concatenate([_scan_sublanes(g[sl]) for sl in groups], axis=0)

    def repeated(r):
        return jnp.concatenate([jnp.broadcast_to(bc[sl][r:r + 1, :], (steps, DK)) for sl in groups],
                               axis=0)

    qd = (q * jnp.exp2(bc)).astype(BF16)
    kdec = (k * jnp.exp2(repeated(steps - 1) - bc)).astype(BF16)
    r = lax.broadcasted_iota(jnp.int32, (rows, SAMPLE_SEQS), 0)
    c = lax.broadcasted_iota(jnp.int32, (rows, SAMPLE_SEQS), 1)
    selector = jnp.where((r >> (steps.bit_length() - 1)) == c, 1.0, 0.0).astype(F32)
    decay = jnp.exp2(lax.dot_general(g, selector, tn, precision=HIGHEST,
                                    preferred_element_type=F32))

    inter = [jnp.dot(qd[sl], s0_ref[b].astype(BF16), preferred_element_type=F32)
             for b, sl in enumerate(groups)]
    upd = [lax.dot_general(kdec[sl], v32[sl].astype(BF16), tn, preferred_element_type=F32)
           for sl in groups]
    for b in range(SAMPLE_SEQS):
        sout_ref[b] = decay[:, b:b + 1] * s0_ref[b] + upd[b]
    o = jnp.concatenate(inter, axis=0) + _diag(q, k, v32, bc, steps)
    o_ref[...] = _head_out(o, gn_ref[...], gate_ref[...])


def _proj_spec(rows, width, c0, per_head, row_block, tile_width=TN_PROJ, head_major=False):
    per_tile = tile_width // width

    def index(i0, i1):
        b, h = (i1, i0) if head_major else (i0, i1)
        g = c0 * LANES // width + h * per_head
        return (g // per_tile, row_block(b), g % per_tile)

    return pl.BlockSpec((None, rows, width), index)


def _prompt_scan(p, pm, *, gla, batch, seq, meta_block, small, name):
    heads, dv = (H_B, DV_B) if gla else (H_A, DV_A)
    n_chunks = seq // CHUNK
    blocks = CHUNK // DIAG_BLK
    stack = DIAG_BLK * blocks * (blocks + 1) // 2
    p16, pf, pa = p
    pm16, pmf, pma = pm
    tok = functools.partial(_proj_spec, seq, row_block=lambda b: b, head_major=True)
    met = functools.partial(_proj_spec, N_META, row_block=lambda b: meta_block, head_major=True)
    vec = lambda width: pl.BlockSpec((1, width), lambda h, b: (0, h))
    if gla:
        wa, ba, gn = small
        in_specs = [tok(DK, COL_QB, 1), tok(DK, COL_KB, 1), tok(dv, COL_VB, 1), tok(dv, COL_RB, 1),
                    tok(LANES, 0, 0, tile_width=LANES),
                    met(DK, COL_KB, 1), met(dv, COL_VB, 1), met(LANES, 0, 0, tile_width=LANES),
                    pl.BlockSpec((LANES, DK), lambda h, b: (0, h)), vec(DK), vec(dv)]
        args = (p16, p16, p16, p16, pa, pm16, pm16, pma, wa, ba, gn)
    else:
        lbl, gn = small
        in_specs = [tok(DK, COL_QA, 1), tok(DK, 0, 1), tok(dv, COL_IA, 1), tok(dv, COL_GA, 1),
                    met(DK, 0, 1), met(dv, COL_IA, 1),
                    pl.BlockSpec((2, DK), lambda h, b: (0, h)), vec(dv)]
        args = (p16, pf, p16, p16, pmf, pm16, lbl, gn)
    return pl.pallas_call(
        functools.partial(_prompt_scan_kernel, gla=gla, seq=seq, meta=N_META),
        grid=(heads, batch),
        in_specs=in_specs,
        out_specs=[pl.BlockSpec((seq, dv), lambda h, b: (b, h)),
                   pl.BlockSpec((None, None, DK, dv), lambda h, b: (b, h, 0, 0))],
        out_shape=[jax.ShapeDtypeStruct((batch * seq, heads * dv), BF16),
                   jax.ShapeDtypeStruct((batch, heads, DK, dv), F32)],
        scratch_shapes=[
            pltpu.VMEM((dv, DK), F32),
            pltpu.VMEM((seq, DK), F32),
            pltpu.VMEM((seq, DK), F32),
            pltpu.VMEM((seq, DK), F32),
            pltpu.VMEM((seq, DK), BF16),
            pltpu.VMEM((seq, DK), BF16),
            pltpu.VMEM((n_chunks * DK, stack), BF16),
            pltpu.VMEM((seq, DK), BF16),
            pltpu.VMEM((n_chunks, SUBLANES, DK), F32),
            pltpu.VMEM((dv, DK), F32),
        ],
        compiler_params=_params("parallel", "arbitrary"),
        name=name,
    )(*args)


def _sample_scan(p, s0, *, gla, batch, steps, small, name):
    assert steps == SUBLANES, "the sample recurrence keeps one sequence per sublane group"
    heads, dv = (H_B, DV_B) if gla else (H_A, DV_A)
    rows = SAMPLE_SEQS * steps
    p16, pf, pa = p
    tok = functools.partial(_proj_spec, rows, row_block=lambda b: b)
    vec = lambda width: pl.BlockSpec((1, width), lambda b, h: (0, h))
    state = pl.BlockSpec((SAMPLE_SEQS, None, DK, dv), lambda b, h: (b, h, 0, 0))
    if gla:
        wa, ba, gn = small
        in_specs = [tok(DK, COL_QB, 1), tok(DK, COL_KB, 1), tok(dv, COL_VB, 1), tok(dv, COL_RB, 1),
                    tok(LANES, 0, 0, tile_width=LANES),
                    pl.BlockSpec((LANES, DK), lambda b, h: (0, h)), vec(DK), vec(dv), state]
        args = (p16, p16, p16, p16, pa, wa, ba, gn, s0)
    else:
        lbl, gn = small
        in_specs = [tok(DK, COL_QA, 1), tok(DK, 0, 1), tok(dv, COL_IA, 1), tok(dv, COL_GA, 1),
                    pl.BlockSpec((2, DK), lambda b, h: (0, h)), vec(dv), state]
        args = (p16, pf, p16, p16, lbl, gn, s0)
    return pl.pallas_call(
        functools.partial(_sample_scan_kernel, gla=gla, steps=steps),
        grid=(batch // SAMPLE_SEQS, heads),
        in_specs=in_specs,
        out_specs=[pl.BlockSpec((rows, dv), lambda b, h: (b, h)), state],
        out_shape=[jax.ShapeDtypeStruct((batch * steps, heads * dv), BF16),
                   jax.ShapeDtypeStruct((batch, heads, DK, dv), F32)],
        compiler_params=_params("parallel", "parallel"),
        name=name,
    )(*args)


def kernel(x_prompt, x_sample, state_hgrn, state_gla, meta_tokens, lb_logits, ffn1_norm, w_ffn1_in,
           w_ffn1_out, mix_norm, w_in, w_alpha_up, b_alpha, gnorm_a, gnorm_b, w_out, ffn2_norm,
           w_ffn2_in, w_ffn2_out, final_norm):
    batch, seq, _ = x_prompt.shape
    dec_batch, steps, _ = x_sample.shape

    w_in_t = w_in[0].T
    wop = w_out[0]
    wa = jnp.pad(w_alpha_up[0], ((0, LANES - ALPHA_RANK), (0, 0))).astype(BF16)
    n1, nm, n2, nf = ffn1_norm, mix_norm, ffn2_norm, final_norm[None]
    small_a = (lb_logits, gnorm_a)
    small_b = (wa, b_alpha, gnorm_b)
    n_sample = dec_batch * steps

    xs = (x_sample.reshape(n_sample, D_MODEL), meta_tokens)
    n_stream = n_sample + N_META
    h1s, u1s, *w1 = _ffn(xs, n1, (w_ffn1_in[0], w_ffn1_in[0], w_ffn1_out[0]), nm, tm=n_stream,
                         tf=TF_CAST, final_norm=False, emit_weights=True, name="ffn1_sample")
    ps = _in_proj(u1s, w_in_t, tm=n_stream, name="inproj_sample")
    oa_s, sa_s = _sample_scan(ps, state_hgrn[0], gla=False, batch=dec_batch, steps=steps,
                              small=small_a, name="scan_hgrn_sample")
    ob_s, sb_s = _sample_scan(ps, state_gla[0], gla=True, batch=dec_batch, steps=steps,
                              small=small_b, name="scan_gla_sample")
    h2s = _out_proj(oa_s, ob_s, h1s, wop, tm=TM, name="outproj_sample")
    y_s, *w2 = _ffn(h2s, n2, (w_ffn2_in[0], w_ffn2_in[0], w_ffn2_out[0]), nf, tm=n_sample,
                    tf=TF_CAST, final_norm=True, emit_weights=True, name="ffn2_sample")

    h1p, u1p = _ffn(x_prompt.reshape(batch * seq, D_MODEL), n1, w1, nm, tm=TM_FFN, tf=TF,
                    final_norm=False, emit_weights=False, name="ffn1_prompt")
    pp = _in_proj(u1p, w_in_t, tm=TM_PROJ, name="inproj_prompt")
    meta_block = n_sample // N_META
    oa_p, sa_p = _prompt_scan(pp, ps, gla=False, batch=batch, seq=seq, meta_block=meta_block,
                              small=small_a, name="scan_hgrn_prompt")
    ob_p, sb_p = _prompt_scan(pp, ps, gla=True, batch=batch, seq=seq, meta_block=meta_block,
                              small=small_b, name="scan_gla_prompt")
    h2p = _out_proj(oa_p, ob_p, h1p, wop, tm=TM, name="outproj_prompt")
    (y_p,) = _ffn(h2p, n2, w2, nf, tm=TM_FFN, tf=TF, final_norm=True, emit_weights=False,
                  name="ffn2_prompt")

    return (y_p.reshape(batch, seq, D_MODEL), y_s.reshape(dec_batch, steps, D_MODEL),
            sa_p[None], sb_p[None], sa_s[None], sb_s[None])
```

```python
import functools

import jax
import jax.numpy as jnp
from jax import lax
from jax.experimental import pallas as pl
from jax.experimental.pallas import tpu as pltpu

F32 = jnp.float32
BF16 = jnp.bfloat16
HIGHEST = lax.Precision.HIGHEST

D_MODEL = 2048
N_META = 16
DK = 128
H_A, DV_A = 8, 128
H_B, DV_B = 4, 256
D_HALF = H_A * DV_A
ALPHA_RANK = 16
GATE_TEMP = 16.0
D_FF = 5632
CHUNK = 64
EPS = 1e-6
IN_PROJ_WIDTHS = (D_HALF, D_HALF, D_HALF, D_HALF, H_B * DK, H_B * DK, D_HALF, D_HALF, ALPHA_RANK)

LANES = 128
SUBLANES = 8
VMEM_LIMIT_BYTES = 56 * 1024 * 1024

COL_QA, COL_IA, COL_GA, COL_QB, COL_KB, COL_VB, COL_RB = 0, 8, 16, 24, 28, 32, 40
P16_COLS = 3 * D_HALF + 2 * H_B * DK + 2 * D_HALF

TM = 512
TM_FFN = 1024
TF = 256
TF_WIDE = 512
TF_CAST = 256
TN_PROJ = 512
TM_PROJ = 2048
N16_TILES = P16_COLS // TN_PROJ
DIAG_BLK = 16
SAMPLE_SEQS = 32
CHUNKS_PER_TRIP = 4
CHUNKS_PER_MXU_TRIP = 16

LOG2_E = 1.4426950408889634

FAST_MIN_LOG_GATE = -60.0 * LOG2_E / (DIAG_BLK - 1)
FAST_MAX_KEY = 1e10


def _rms(x, w):
    return x * lax.rsqrt(jnp.mean(x * x, axis=-1, keepdims=True) + EPS) * w


def _silu(x):
    return x * jax.nn.sigmoid(x)


def _params(*sem):
    return pltpu.CompilerParams(dimension_semantics=sem, vmem_limit_bytes=VMEM_LIMIT_BYTES)


def _ffn_kernel(*refs, n_pieces, final_norm, emit_weights, prefetch_h):
    h_refs = refs[:n_pieces]
    nw_ref, wg_ref, wu_ref, wo_ref, tnw_ref, out_ref, *rest = refs[n_pieces:]
    rest = list(rest)
    un_ref = None if final_norm else rest.pop(0)
    w16_refs = [rest.pop(0) for _ in range(3)] if emit_weights else []
    u_ref = rest.pop(0)
    i, j = pl.program_id(0), pl.program_id(1)
    last_j = pl.num_programs(1) - 1
    if prefetch_h:
        h_buf, h_sem = rest

        def h_copy(tile):
            rows = h_buf.shape[0]
            src = h_refs[0].at[pl.ds(pl.multiple_of(tile * rows, rows), rows), :]
            return pltpu.make_async_copy(src, h_buf, h_sem)

    def load_h():
        pieces = [r[...] for r in h_refs]
        return pieces[0] if n_pieces == 1 else jnp.concatenate(pieces, axis=0)

    def step(first, last):
        if first:
            if prefetch_h:
                pl.when(i == 0)(lambda: h_copy(0).start())
                h_copy(i).wait()
                h = h_buf[...]
            else:
                h = load_h()
            u = _rms(h, nw_ref[...]).astype(BF16)
            u_ref[...] = u
        else:
            u = u_ref[...]
        wg, wu, wo = (r[...].astype(BF16) for r in (wg_ref, wu_ref, wo_ref))
        for dst, w in zip(w16_refs, (wg, wu, wo)):
            dst[...] = w
        g = jnp.dot(u, wg, preferred_element_type=F32)
        up = jnp.dot(u, wu, preferred_element_type=F32)
        a = (_silu(g) * up).astype(BF16)
        acc = jnp.dot(a, wo, preferred_element_type=F32)
        if not first:
            acc = out_ref[...] + acc
        elif prefetch_h:
            acc = 2.0 * h + acc
        if not last:
            out_ref[...] = acc
            if first and prefetch_h:
                pl.when(i + 1 < pl.num_programs(0))(lambda: h_copy(i + 1).start())
            return
        y = 0.5 * acc if prefetch_h else load_h() + 0.5 * acc
        normed = _rms(y, tnw_ref[...])
        if final_norm:
            out_ref[...] = normed
        else:
            out_ref[...] = y
            un_ref[...] = normed.astype(BF16)

    pl.when(j == 0)(functools.partial(step, True, False))
    pl.when((j > 0) & (j < last_j))(functools.partial(step, False, False))
    pl.when(j == last_j)(functools.partial(step, False, True))


def _ffn(h, norm_w, weights, tail_norm_w, *, tm, tf, final_norm, emit_weights, name,
         prefetch_h=False):
    pieces = h if isinstance(h, tuple) else (h,)
    m = sum(p.shape[0] for p in pieces)
    nf = D_FF // tf
    rows = pl.BlockSpec((tm, D_MODEL), lambda i, j: (i, 0),
                        pipeline_mode=pl.Buffered(1) if m == tm else None)
    scratch = [pltpu.VMEM((tm, D_MODEL), BF16)]
    if prefetch_h:
        assert len(pieces) == 1 and m > tm
        h_specs = [pl.BlockSpec(memory_space=pl.ANY)]
        scratch += [pltpu.VMEM((tm, D_MODEL), F32), pltpu.SemaphoreType.DMA]
    elif len(pieces) == 1:
        h_specs = [rows]
    else:
        assert m == tm, "stacked inputs form one row tile"
        h_specs = [pl.BlockSpec((p.shape[0], D_MODEL), lambda i, j: (0, 0),
                                pipeline_mode=pl.Buffered(1)) for p in pieces]
    vec = pl.BlockSpec((1, D_MODEL), lambda i, j: (0, 0))
    col_tile = pl.BlockSpec((D_MODEL, tf), lambda i, j: (0, j))
    row_tile = pl.BlockSpec((tf, D_MODEL), lambda i, j: (j, 0))
    out_specs = [rows]
    out_shape = [jax.ShapeDtypeStruct((m, D_MODEL), F32)]
    if not final_norm:
        out_specs += [rows]
        out_shape += [jax.ShapeDtypeStruct((m, D_MODEL), BF16)]
    if emit_weights:
        assert m == tm, "weight casts are written once, by a single row tile"
        up_tile = pl.BlockSpec((D_MODEL, tf), lambda i, j: (0, j + nf))
        out_specs += [col_tile, col_tile, row_tile]
        out_shape += [jax.ShapeDtypeStruct((D_MODEL, D_FF), BF16)] * 2
        out_shape += [jax.ShapeDtypeStruct((D_FF, D_MODEL), BF16)]
    else:
        up_tile = col_tile
    return pl.pallas_call(
        functools.partial(_ffn_kernel, n_pieces=len(pieces), final_norm=final_norm,
                          emit_weights=emit_weights, prefetch_h=prefetch_h),
        grid=(m // tm, nf),
        in_specs=h_specs + [vec, col_tile, up_tile, row_tile, vec],
        out_specs=out_specs,
        out_shape=out_shape,
        scratch_shapes=scratch,
        compiler_params=_params("arbitrary" if prefetch_h else "parallel", "arbitrary"),
        name=name,
    )(*pieces, norm_w, *weights, tail_norm_w)


def _proj_kernel(u_ref, w_ref, wal_ref, p16_ref, pf_ref, pa_ref):
    j = pl.program_id(1)
    nt = (((1,), (1,)), ((), ()))

    @pl.when(j == 0)
    def _low_rank():
        wal = jnp.concatenate([wal_ref[...].astype(BF16),
                               jnp.zeros((LANES - ALPHA_RANK, D_MODEL), BF16)], axis=0)
        pa_ref[...] = lax.dot_general(u_ref[...], wal, nt, preferred_element_type=F32)

    def project():
        return lax.dot_general(u_ref[...], w_ref[...].astype(BF16), nt,
                               preferred_element_type=F32)

    @pl.when(j < N16_TILES)
    def _narrow():
        p16_ref[...] = project().astype(BF16)

    @pl.when(j >= N16_TILES)
    def _wide():
        pf_ref[...] = project()


def _w_in_tile(j):
    fa_lo, fa_hi = D_HALF // TN_PROJ, 2 * D_HALF // TN_PROJ
    moved = fa_hi - fa_lo
    return jnp.where(j < fa_lo, j, jnp.where(j < N16_TILES, j + moved, j - N16_TILES + fa_lo))


def _in_proj(u, w, *, tm, name):
    m = u.shape[0]
    return pl.pallas_call(
        _proj_kernel,
        grid=(m // tm, (P16_COLS + D_HALF) // TN_PROJ),
        in_specs=[
            pl.BlockSpec((tm, D_MODEL), lambda i, j: (i, 0)),
            pl.BlockSpec((TN_PROJ, D_MODEL), lambda i, j: (_w_in_tile(j), 0)),
            pl.BlockSpec((ALPHA_RANK, D_MODEL), lambda i, j: ((P16_COLS + D_HALF) // ALPHA_RANK, 0)),
        ],
        out_specs=[
            pl.BlockSpec((None, tm, TN_PROJ), lambda i, j: (jnp.minimum(j, N16_TILES - 1), i, 0)),
            pl.BlockSpec((None, tm, TN_PROJ), lambda i, j: (jnp.maximum(j - N16_TILES, 0), i, 0)),
            pl.BlockSpec((None, tm, LANES), lambda i, j: (0, i, 0)),
        ],
        out_shape=[jax.ShapeDtypeStruct((N16_TILES, m, TN_PROJ), BF16),
                   jax.ShapeDtypeStruct((D_HALF // TN_PROJ, m, TN_PROJ), F32),
                   jax.ShapeDtypeStruct((1, m, LANES), F32)],
        compiler_params=_params("parallel", "arbitrary"),
        name=name,
    )(u, w, w)


def _out_proj_kernel(oa_ref, ob_ref, h_ref, wa_ref, wb_ref, out_ref):
    out_ref[...] = (h_ref[...]
                    + jnp.dot(oa_ref[...], wa_ref[...].astype(BF16), preferred_element_type=F32)
                    + jnp.dot(ob_ref[...], wb_ref[...].astype(BF16), preferred_element_type=F32))


def _out_proj(oa, ob, h, w, *, tm, name):
    m = oa.shape[0]
    return pl.pallas_call(
        _out_proj_kernel,
        grid=(m // tm,),
        in_specs=[
            pl.BlockSpec((tm, D_HALF), lambda i: (i, 0)),
            pl.BlockSpec((tm, D_HALF), lambda i: (i, 0)),
            pl.BlockSpec((tm, D_MODEL), lambda i: (i, 0)),
            pl.BlockSpec((D_HALF, D_MODEL), lambda i: (0, 0), pipeline_mode=pl.Buffered(1)),
            pl.BlockSpec((D_HALF, D_MODEL), lambda i: (1, 0), pipeline_mode=pl.Buffered(1)),
        ],
        out_specs=pl.BlockSpec((tm, D_MODEL), lambda i: (i, 0)),
        out_shape=jax.ShapeDtypeStruct((m, D_MODEL), F32),
        compiler_params=_params("parallel"),
        name=name,
    )(oa, ob, h, w, w)


def _tri(c, seq_len):
    r = lax.broadcasted_iota(jnp.int32, (c, c), 0)
    s = lax.broadcasted_iota(jnp.int32, (c, c), 1)
    return jnp.where((r >= s) & ((r ^ s) < seq_len), 1.0, 0.0).astype(F32)


def _off_mask(c, blk):
    nb = c // blk
    rows = nb * (nb - 1) // 2 * blk
    r = lax.broadcasted_iota(jnp.int32, (c, rows), 0) >> (blk.bit_length() - 1)
    s = lax.broadcasted_iota(jnp.int32, (c, rows), 1)
    seg = jnp.zeros((c, rows), jnp.int32)
    for i in range(1, nb):
        seg = seg + jnp.where(s >= blk * i * (i - 1) // 2, 1, 0)
    return jnp.where(r == seg, 1.0, 0.0).astype(F32)


def _diag(q, k, v, bc, blk):
    c = q.shape[0]
    row = lax.broadcasted_iota(jnp.int32, (blk, 1), 0)
    outs = []
    for i in range(c // blk):
        sl = slice(i * blk, (i + 1) * blk)
        qi, ki, vi, bi = q[sl], k[sl], v[sl].astype(F32), bc[sl]
        acc = jnp.zeros((blk, v.shape[1]), F32)
        for s in range(blk):
            e = jnp.exp2(jnp.minimum(bi - bi[s:s + 1, :], 0.0))
            col = jnp.sum(qi * ki[s:s + 1, :] * e, axis=-1, keepdims=True)
            col = jnp.where(row >= s, col, 0.0)
            acc = acc + col * vi[s:s + 1, :]
        outs.append(acc)
    return outs[0] if len(outs) == 1 else jnp.concatenate(outs, axis=0)


def _offdiag(q, k, v, bc, blk, mask):
    c = q.shape[0]
    nb = c // blk
    qs, ks, vs = [jnp.zeros((blk, DK), F32)], [], []
    for i in range(1, nb):
        lo = i * blk
        ref = bc[lo:lo + 1, :]
        qs.append(q[lo:lo + blk] * jnp.exp2(bc[lo:lo + blk] - ref))
        ks.append(k[:lo] * jnp.exp2(ref - bc[:lo]))
        vs.append(v[:lo])
    qt = jnp.concatenate(qs, axis=0).astype(BF16)
    kt = jnp.concatenate(ks, axis=0).astype(BF16)
    vt = jnp.concatenate(vs, axis=0).astype(BF16)
    a = lax.dot_general(qt, kt, (((1,), (1,)), ((), ())), preferred_element_type=F32)
    return jnp.dot((a * mask).astype(BF16), vt, preferred_element_type=F32)


def _hgrn_inputs(qa, fa, ia, lb):
    fg = lb + (1.0 - lb) * jax.nn.sigmoid(fa)
    return _silu(qa.astype(F32)) * (DK ** -0.5), 1.0 - fg, ia, jnp.log2(fg)


def _gla_inputs(qb, kb, vb, al, wa, ba):
    alpha = jnp.dot(al.astype(BF16), wa, preferred_element_type=F32) + ba
    logsig = jnp.minimum(alpha, 0.0) - jnp.log1p(jnp.exp(-jnp.abs(alpha)))
    return qb.astype(F32) * (DK ** -0.5), kb.astype(F32), vb, logsig * (LOG2_E / GATE_TEMP)


def _head_out(o, gn, gate):
    return (_rms(o, gn) * _silu(gate.astype(F32))).astype(BF16)


def _lower_bound(lbl_ref):
    x = lbl_ref[...]
    e = jnp.exp(x - jnp.max(x, axis=0, keepdims=True))
    return e[0:1, :] / jnp.sum(e, axis=0, keepdims=True)


def _stack_mask(c, blk):
    nb = c // blk
    rows = blk * nb * (nb + 1) // 2
    t = lax.broadcasted_iota(jnp.int32, (c, rows), 0)
    r = lax.broadcasted_iota(jnp.int32, (c, rows), 1)
    seg = jnp.zeros((c, rows), jnp.int32)
    off = jnp.zeros((c, rows), jnp.int32)
    for i in range(1, nb):
        start = blk * i * (i + 1) // 2
        seg = seg + jnp.where(r >= start, 1, 0)
        off = jnp.where(r >= start, start, off)
    keep = ((t >> (blk.bit_length() - 1)) == seg) & (r - off <= t)
    return jnp.where(keep, 1.0, 0.0).astype(F32)


def _state_update_t(k, vb, bc, st):
    c = k.shape[0]
    blast = bc[c - 1:c, :]
    kdec = (k * jnp.exp2(blast - bc)).astype(BF16)
    upd = lax.dot_general(vb, kdec, (((0,), (0,)), ((), ())), preferred_element_type=F32)
    if st is None:
        return upd
    return st * jnp.exp2(blast) + upd


def _scan_sublanes(p):
    row = lax.broadcasted_iota(jnp.int32, (SUBLANES, DK), 0)
    for s in (1, 2, 4):
        p = p + jnp.where(row >= s, pltpu.roll(p, s, axis=0), 0.0)
    return p


def _cumsum_chunk(g):
    out, carry = [], None
    for j in range(CHUNK // SUBLANES):
        p = _scan_sublanes(g[j * SUBLANES:(j + 1) * SUBLANES])
        if carry is not None:
            p = p + carry
        carry = p[SUBLANES - 1:SUBLANES, :]
        out.append(p)
    return jnp.concatenate(out, axis=0)


def _rescaled_operands(q, k, bc):
    c, blk = CHUNK, DIAG_BLK
    qs, ks = [], []
    for i in range(c // blk):
        lo, hi = i * blk, (i + 1) * blk
        ref = bc[lo:lo + 1, :]
        qs.append(q[lo:hi] * jnp.exp2(bc[lo:hi] - ref))
        ks.append(k[:hi] * jnp.exp2(ref - bc[:hi]))
    return ((q * jnp.exp2(bc)).astype(BF16), jnp.concatenate(qs, axis=0).astype(BF16),
            jnp.concatenate(ks, axis=0).T.astype(BF16),
            (k * jnp.exp2(bc[c - 1:c, :] - bc)).astype(BF16))


def _chunk_exact(q, k, v, bc, st, mask):
    o = lax.dot_general((q * jnp.exp2(bc)).astype(BF16), st.astype(BF16),
                        (((1,), (1,)), ((), ())), preferred_element_type=F32)
    o = o + _offdiag(q, k, v, bc, DIAG_BLK, mask) + _diag(q, k, v, bc, DIAG_BLK)
    return o, _state_update_t(k, v.astype(BF16), bc, st)


def _prompt_scan_kernel(*refs, gla, seq, meta):
    n_in = 11 if gla else 8
    if gla:
        q_ref, k_ref, v_ref, gate_ref, al_ref, mk_ref, mv_ref, mal_ref, wa_ref, ba_ref, gn_ref = (
            refs[:n_in])
    else:
        q_ref, f_ref, v_ref, gate_ref, mf_ref, mv_ref, lbl_ref, gn_ref = refs[:n_in]
        lb = _lower_bound(lbl_ref)
    o_ref, sout_ref = refs[n_in:n_in + 2]
    (st_ref, qs_ref, ks_ref, gs_ref, qd_ref, qt_ref, kt_ref, kdec_ref, eb_ref,
     meta_st_ref) = refs[n_in + 2:]
    trips = seq // (CHUNK * CHUNKS_PER_TRIP)
    nt = (((1,), (1,)), ((), ()))
    tn = (((0,), (0,)), ((), ()))

    def chunk_rows(ci, u, n=CHUNK):
        return pl.ds(pl.multiple_of((ci * CHUNKS_PER_TRIP + u) * n, n), n)

    @pl.when(pl.program_id(1) == 0)
    def _meta():
        if gla:
            _, mk, mv, mg = _gla_inputs(mk_ref[...], mk_ref[...], mv_ref[...], mal_ref[...],
                                        wa_ref[...], ba_ref[...])
        else:
            _, mk, mv, mg = _hgrn_inputs(mf_ref[...], mf_ref[...], mv_ref[...], lb)
        mbc = jnp.dot(_tri(meta, meta), mg, precision=HIGHEST, preferred_element_type=F32)
        meta_st_ref[...] = _state_update_t(mk, mv.astype(BF16), mbc, None)

    st_ref[...] = meta_st_ref[...]

    def rows_of(c, n=CHUNK):
        return pl.ds(pl.multiple_of(c * n, n), n)

    def rescaled_pass():
        mask = _stack_mask(CHUNK, DIAG_BLK)
        units = range(CHUNKS_PER_MXU_TRIP)
        mxu_trips = seq // (CHUNK * CHUNKS_PER_MXU_TRIP)

        def prepare(ti, extremes):
            gmin, kmax = extremes
            trip_rows = CHUNK * CHUNKS_PER_MXU_TRIP
            rows = pl.ds(pl.multiple_of(ti * trip_rows, trip_rows), trip_rows)
            if gla:
                q, k, _, g = _gla_inputs(q_ref[rows, :], k_ref[rows, :], None, al_ref[rows, :],
                                         wa_ref[...], ba_ref[...])
            else:
                q, k, _, g = _hgrn_inputs(q_ref[rows, :], f_ref[rows, :], None, lb)
            qs_ref[rows, :] = q
            ks_ref[rows, :] = k
            gs_ref[rows, :] = g
            for u in units:
                c = ti * CHUNKS_PER_MXU_TRIP + u
                sl = slice(u * CHUNK, (u + 1) * CHUNK)
                bc = _cumsum_chunk(g[sl])
                qd, qt, kt, kdec = _rescaled_operands(q[sl], k[sl], bc)
                qd_ref[rows_of(c), :] = qd
                qt_ref[rows_of(c), :] = qt
                kt_ref[rows_of(c, DK), :] = kt
                kdec_ref[rows_of(c), :] = kdec
                eb_ref[c] = jnp.broadcast_to(jnp.exp2(bc[CHUNK - 1:CHUNK, :]), (SUBLANES, DK))
            return (jnp.minimum(gmin, jnp.min(g, axis=0, keepdims=True)),
                    jnp.maximum(kmax, jnp.max(jnp.abs(k), axis=0, keepdims=True)))

        def contract(ti):
            cs = [ti * CHUNKS_PER_MXU_TRIP + u for u in units]
            vb = [v_ref[rows_of(c), :].astype(BF16) for c in cs]
            a = [jnp.dot(qt_ref[rows_of(c), :], kt_ref[rows_of(c, DK), :],
                         preferred_element_type=F32) for c in cs]
            am = [(x * mask).astype(BF16) for x in a]
            vt = [jnp.concatenate([x[:(i + 1) * DIAG_BLK] for i in range(CHUNK // DIAG_BLK)], axis=0)
                  for x in vb]
            oi = [jnp.dot(am[u], vt[u], preferred_element_type=F32) for u in units]
            inc = [lax.dot_general(vb[u], kdec_ref[rows_of(cs[u]), :], tn,
                                   preferred_element_type=F32) for u in units]
            sts = [st_ref[...]]
            for u in units:
                sts.append(sts[-1] * eb_ref[cs[u]][0:1, :] + inc[u])
            o = [oi[u] + lax.dot_general(qd_ref[rows_of(cs[u]), :], sts[u].astype(BF16), nt,
                                         preferred_element_type=F32) for u in units]
            for u in units:
                o_ref[rows_of(cs[u]), :] = _head_out(o[u], gn_ref[...], gate_ref[rows_of(cs[u]), :])
            st_ref[...] = sts[-1]

        extremes = prepare(0, (jnp.zeros((1, DK), F32), jnp.zeros((1, DK), F32)))

        def body(ti, extremes):
            contract(ti)
            return prepare(ti + 1, extremes)

        extremes = lax.fori_loop(0, mxu_trips - 1, body, extremes)
        contract(mxu_trips - 1)
        return extremes

    gmin, kmax = rescaled_pass()
    in_range = (jnp.min(gmin) >= FAST_MIN_LOG_GATE) & (jnp.max(kmax) <= FAST_MAX_KEY)

    @pl.when(jnp.logical_not(in_range))
    def _exact():
        mask = _off_mask(CHUNK, DIAG_BLK)
        st_ref[...] = meta_st_ref[...]

        def body(ci, carry):
            st = st_ref[...]
            for u in range(CHUNKS_PER_TRIP):
                rows = chunk_rows(ci, u)
                o, st = _chunk_exact(qs_ref[rows, :], ks_ref[rows, :], v_ref[rows, :],
                                     _cumsum_chunk(gs_ref[rows, :]), st, mask)
                o_ref[rows, :] = _head_out(o, gn_ref[...], gate_ref[rows, :])
            st_ref[...] = st
            return carry

        lax.fori_loop(0, trips, body, 0)

    sout_ref[...] = st_ref[...].T


def _sample_scan_kernel(*refs, gla, steps):
    if gla:
        (q_ref, k_ref, v_ref, gate_ref, al_ref, wa_ref, ba_ref, gn_ref, s0_ref,
         o_ref, sout_ref) = refs
        q, k, v, g = _gla_inputs(q_ref[...], k_ref[...], v_ref[...], al_ref[...],
                                 wa_ref[...], ba_ref[...])
    else:
        (q_ref, f_ref, v_ref, gate_ref, lbl_ref, gn_ref, s0_ref, o_ref, sout_ref) = refs
        q, k, v, g = _hgrn_inputs(q_ref[...], f_ref[...], v_ref[...], _lower_bound(lbl_ref))
    v32 = v.astype(F32)
    rows = SAMPLE_SEQS * steps
    groups = [slice(b * steps, (b + 1) * steps) for b in range(SAMPLE_SEQS)]
    tn = (((0,), (0,)), ((), ()))

    bc = jnp.concatenate([_scan_sublanes(g[sl]) for sl in groups], axis=0)

    def repeated(r):
        return jnp.concatenate([jnp.broadcast_to(bc[sl][r:r + 1, :], (steps, DK)) for sl in groups],
                               axis=0)

    qd = (q * jnp.exp2(bc)).astype(BF16)
    kdec = (k * jnp.exp2(repeated(steps - 1) - bc)).astype(BF16)
    r = lax.broadcasted_iota(jnp.int32, (rows, SAMPLE_SEQS), 0)
    c = lax.broadcasted_iota(jnp.int32, (rows, SAMPLE_SEQS), 1)
    selector = jnp.where((r >> (steps.bit_length() - 1)) == c, 1.0, 0.0).astype(F32)
    decay = jnp.exp2(lax.dot_general(g, selector, tn, precision=HIGHEST,
                                    preferred_element_type=F32))

    inter = [jnp.dot(qd[sl], s0_ref[b].astype(BF16), preferred_element_type=F32)
             for b, sl in enumerate(groups)]
    upd = [lax.dot_general(kdec[sl], v32[sl].astype(BF16), tn, preferred_element_type=F32)
           for sl in groups]
    for b in range(SAMPLE_SEQS):
        sout_ref[b] = decay[:, b:b + 1] * s0_ref[b] + upd[b]
    o = jnp.concatenate(inter, axis=0) + _diag(q, k, v32, bc, steps)
    o_ref[...] = _head_out(o, gn_ref[...], gate_ref[...])


def _proj_spec(rows, width, c0, per_head, row_block, tile_width=TN_PROJ, head_major=False):
    per_tile = tile_width // width

    def index(i0, i1):
        b, h = (i1, i0) if head_major else (i0, i1)
        g = c0 * LANES // width + h * per_head
        return (g // per_tile, row_block(b), g % per_tile)

    return pl.BlockSpec((None, rows, width), index)


def _prompt_scan(p, pm, *, gla, batch, seq, meta_block, small, name):
    heads, dv = (H_B, DV_B) if gla else (H_A, DV_A)
    n_chunks = seq // CHUNK
    blocks = CHUNK // DIAG_BLK
    stack = DIAG_BLK * blocks * (blocks + 1) // 2
    p16, pf, pa = p
    pm16, pmf, pma = pm
    tok = functools.partial(_proj_spec, seq, row_block=lambda b: b, head_major=True)
    met = functools.partial(_proj_spec, N_META, row_block=lambda b: meta_block, head_major=True)
    vec = lambda width: pl.BlockSpec((1, width), lambda h, b: (0, h))
    if gla:
        wa, ba, gn = small
        in_specs = [tok(DK, COL_QB, 1), tok(DK, COL_KB, 1), tok(dv, COL_VB, 1), tok(dv, COL_RB, 1),
                    tok(LANES, 0, 0, tile_width=LANES),
                    met(DK, COL_KB, 1), met(dv, COL_VB, 1), met(LANES, 0, 0, tile_width=LANES),
                    pl.BlockSpec((LANES, DK), lambda h, b: (0, h)), vec(DK), vec(dv)]
        args = (p16, p16, p16, p16, pa, pm16, pm16, pma, wa, ba, gn)
    else:
        lbl, gn = small
        in_specs = [tok(DK, COL_QA, 1), tok(DK, 0, 1), tok(dv, COL_IA, 1), tok(dv, COL_GA, 1),
                    met(DK, 0, 1), met(dv, COL_IA, 1),
                    pl.BlockSpec((2, DK), lambda h, b: (0, h)), vec(dv)]
        args = (p16, pf, p16, p16, pmf, pm16, lbl, gn)
    return pl.pallas_call(
        functools.partial(_prompt_scan_kernel, gla=gla, seq=seq, meta=N_META),
        grid=(heads, batch),
        in_specs=in_specs,
        out_specs=[pl.BlockSpec((seq, dv), lambda h, b: (b, h)),
                   pl.BlockSpec((None, None, DK, dv), lambda h, b: (b, h, 0, 0))],
        out_shape=[jax.ShapeDtypeStruct((batch * seq, heads * dv), BF16),
                   jax.ShapeDtypeStruct((batch, heads, DK, dv), F32)],
        scratch_shapes=[
            pltpu.VMEM((dv, DK), F32),
            pltpu.VMEM((seq, DK), F32),
            pltpu.VMEM((seq, DK), F32),
            pltpu.VMEM((seq, DK), F32),
            pltpu.VMEM((seq, DK), BF16),
            pltpu.VMEM((seq, DK), BF16),
            pltpu.VMEM((n_chunks * DK, stack), BF16),
            pltpu.VMEM((seq, DK), BF16),
            pltpu.VMEM((n_chunks, SUBLANES, DK), F32),
            pltpu.VMEM((dv, DK), F32),
        ],
        compiler_params=_params("parallel", "arbitrary"),
        name=name,
    )(*args)


def _sample_scan(p, s0, *, gla, batch, steps, small, name):
    assert steps == SUBLANES, "the sample recurrence keeps one sequence per sublane group"
    heads, dv = (H_B, DV_B) if gla else (H_A, DV_A)
    rows = SAMPLE_SEQS * steps
    p16, pf, pa = p
    tok = functools.partial(_proj_spec, rows, row_block=lambda b: b)
    vec = lambda width: pl.BlockSpec((1, width), lambda b, h: (0, h))
    state = pl.BlockSpec((SAMPLE_SEQS, None, DK, dv), lambda b, h: (b, h, 0, 0))
    if gla:
        wa, ba, gn = small
        in_specs = [tok(DK, COL_QB, 1), tok(DK, COL_KB, 1), tok(dv, COL_VB, 1), tok(dv, COL_RB, 1),
                    tok(LANES, 0, 0, tile_width=LANES),
                    pl.BlockSpec((LANES, DK), lambda b, h: (0, h)), vec(DK), vec(dv), state]
        args = (p16, p16, p16, p16, pa, wa, ba, gn, s0)
    else:
        lbl, gn = small
        in_specs = [tok(DK, COL_QA, 1), tok(DK, 0, 1), tok(dv, COL_IA, 1), tok(dv, COL_GA, 1),
                    pl.BlockSpec((2, DK), lambda b, h: (0, h)), vec(dv), state]
        args = (p16, pf, p16, p16, lbl, gn, s0)
    return pl.pallas_call(
        functools.partial(_sample_scan_kernel, gla=gla, steps=steps),
        grid=(batch // SAMPLE_SEQS, heads),
        in_specs=in_specs,
        out_specs=[pl.BlockSpec((rows, dv), lambda b, h: (b, h)), state],
        out_shape=[jax.ShapeDtypeStruct((batch * steps, heads * dv), BF16),
                   jax.ShapeDtypeStruct((batch, heads, DK, dv), F32)],
        compiler_params=_params("parallel", "parallel"),
        name=name,
    )(*args)


def kernel(x_prompt, x_sample, state_hgrn, state_gla, meta_tokens, lb_logits, ffn1_norm, w_ffn1_in,
           w_ffn1_out, mix_norm, w_in, w_alpha_up, b_alpha, gnorm_a, gnorm_b, w_out, ffn2_norm,
           w_ffn2_in, w_ffn2_out, final_norm):
    batch, seq, _ = x_prompt.shape
    dec_batch, steps, _ = x_sample.shape

    w_in_t = w_in[0].T
    wop = w_out[0]
    wa = jnp.pad(w_alpha_up[0], ((0, LANES - ALPHA_RANK), (0, 0))).astype(BF16)
    n1, nm, n2, nf = ffn1_norm, mix_norm, ffn2_norm, final_norm[None]
    small_a = (lb_logits, gnorm_a)
    small_b = (wa, b_alpha, gnorm_b)
    n_sample = dec_batch * steps

    xs = (x_sample.reshape(n_sample, D_MODEL), meta_tokens)
    n_stream = n_sample + N_META
    h1s, u1s, *w1 = _ffn(xs, n1, (w_ffn1_in[0], w_ffn1_in[0], w_ffn1_out[0]), nm, tm=n_stream,
                         tf=TF_CAST, final_norm=False, emit_weights=True, name="ffn1_sample")
    ps = _in_proj(u1s, w_in_t, tm=n_stream, name="inproj_sample")
    oa_s, sa_s = _sample_scan(ps, state_hgrn[0], gla=False, batch=dec_batch, steps=steps,
                              small=small_a, name="scan_hgrn_sample")
    ob_s, sb_s = _sample_scan(ps, state_gla[0], gla=True, batch=dec_batch, steps=steps,
                              small=small_b, name="scan_gla_sample")
    h2s = _out_proj(oa_s, ob_s, h1s, wop, tm=TM, name="outproj_sample")
    y_s, *w2 = _ffn(h2s, n2, (w_ffn2_in[0], w_ffn2_in[0], w_ffn2_out[0]), nf, tm=n_sample,
                    tf=TF_CAST, final_norm=True, emit_weights=True, name="ffn2_sample")

    h1p, u1p = _ffn(x_prompt.reshape(batch * seq, D_MODEL), n1, w1, nm, tm=TM_FFN, tf=TF_WIDE,
                    final_norm=False, emit_weights=False, prefetch_h=True, name="ffn1_prompt")
    pp = _in_proj(u1p, w_in_t, tm=TM_PROJ, name="inproj_prompt")
    meta_block = n_sample // N_META
    oa_p, sa_p = _prompt_scan(pp, ps, gla=False, batch=batch, seq=seq, meta_block=meta_block,
                              small=small_a, name="scan_hgrn_prompt")
    ob_p, sb_p = _prompt_scan(pp, ps, gla=True, batch=batch, seq=seq, meta_block=meta_block,
                              small=small_b, name="scan_gla_prompt")
    h2p = _out_proj(oa_p, ob_p, h1p, wop, tm=TM, name="outproj_prompt")
    (y_p,) = _ffn(h2p, n2, w2, nf, tm=TM_FFN, tf=TF_WIDE, final_norm=True, emit_weights=False,
                  name="ffn2_prompt")

    return (y_p.reshape(batch, seq, D_MODEL), y_s.reshape(dec_batch, steps, D_MODEL),
            sa_p[None], sb_p[None], sa_s[None], sb_s[None])
```

```python
import functools

import jax
import jax.numpy as jnp
from jax import lax
from jax.experimental import pallas as pl
from jax.experimental.pallas import tpu as pltpu

F32 = jnp.float32
BF16 = jnp.bfloat16
HIGHEST = lax.Precision.HIGHEST

D_MODEL = 2048
N_META = 16
DK = 128
H_A, DV_A = 8, 128
H_B, DV_B = 4, 256
D_HALF = H_A * DV_A
ALPHA_RANK = 16
GATE_TEMP = 16.0
D_FF = 5632
CHUNK = 64
EPS = 1e-6
IN_PROJ_WIDTHS = (D_HALF, D_HALF, D_HALF, D_HALF, H_B * DK, H_B * DK, D_HALF, D_HALF, ALPHA_RANK)

LANES = 128
SUBLANES = 8
VMEM_LIMIT_BYTES = 56 * 1024 * 1024

COL_QA, COL_IA, COL_GA, COL_QB, COL_KB, COL_VB, COL_RB = 0, 8, 16, 24, 28, 32, 40
P16_COLS = 3 * D_HALF + 2 * H_B * DK + 2 * D_HALF

TM = 512
TM_FFN = 1024
TF = 256
TF_WIDE = 512
TF_CAST = 256
TN_PROJ = 512
TM_PROJ = 2048
N16_TILES = P16_COLS // TN_PROJ
DIAG_BLK = 16
SAMPLE_SEQS = 32
CHUNKS_PER_TRIP = 4
CHUNKS_PER_MXU_TRIP = 16

LOG2_E = 1.4426950408889634

FAST_MIN_LOG_GATE = -60.0 * LOG2_E / (DIAG_BLK - 1)
FAST_MAX_KEY = 1e10


def _rms(x, w):
    return x * lax.rsqrt(jnp.mean(x * x, axis=-1, keepdims=True) + EPS) * w


def _silu(x):
    return x * jax.nn.sigmoid(x)


def _params(*sem):
    return pltpu.CompilerParams(dimension_semantics=sem, vmem_limit_bytes=VMEM_LIMIT_BYTES)


def _ffn_kernel(*refs, n_pieces, final_norm, emit_weights, prefetch_h):
    h_refs = refs[:n_pieces]
    nw_ref, wg_ref, wu_ref, wo_ref, tnw_ref, out_ref, *rest = refs[n_pieces:]
    rest = list(rest)
    un_ref = None if final_norm else rest.pop(0)
    w16_refs = [rest.pop(0) for _ in range(3)] if emit_weights else []
    u_ref = rest.pop(0)
    i, j = pl.program_id(0), pl.program_id(1)
    last_j = pl.num_programs(1) - 1
    if prefetch_h:
        h_buf, h_sem = rest

        def h_copy(tile):
            rows = h_buf.shape[0]
            src = h_refs[0].at[pl.ds(pl.multiple_of(tile * rows, rows), rows), :]
            return pltpu.make_async_copy(src, h_buf, h_sem)

    def load_h():
        pieces = [r[...] for r in h_refs]
        return pieces[0] if n_pieces == 1 else jnp.concatenate(pieces, axis=0)

    def step(first, last):
        if first:
            if prefetch_h:
                pl.when(i == 0)(lambda: h_copy(0).start())
                h_copy(i).wait()
                h = h_buf[...]
            else:
                h = load_h()
            u = _rms(h, nw_ref[...]).astype(BF16)
            u_ref[...] = u
        else:
            u = u_ref[...]
        wg, wu, wo = (r[...].astype(BF16) for r in (wg_ref, wu_ref, wo_ref))
        for dst, w in zip(w16_refs, (wg, wu, wo)):
            dst[...] = w
        g = jnp.dot(u, wg, preferred_element_type=F32)
        up = jnp.dot(u, wu, preferred_element_type=F32)
        a = (_silu(g) * up).astype(BF16)
        acc = jnp.dot(a, wo, preferred_element_type=F32)
        if not first:
            acc = out_ref[...] + acc
        elif prefetch_h:
            acc = 2.0 * h + acc
        if not last:
            out_ref[...] = acc
            if first and prefetch_h:
                pl.when(i + 1 < pl.num_programs(0))(lambda: h_copy(i + 1).start())
            return
        y = 0.5 * acc if prefetch_h else load_h() + 0.5 * acc
        normed = _rms(y, tnw_ref[...])
        if final_norm:
            out_ref[...] = normed
        else:
            out_ref[...] = y
            un_ref[...] = normed.astype(BF16)

    pl.when(j == 0)(functools.partial(step, True, False))
    pl.when((j > 0) & (j < last_j))(functools.partial(step, False, False))
    pl.when(j == last_j)(functools.partial(step, False, True))


def _ffn(h, norm_w, weights, tail_norm_w, *, tm, tf, final_norm, emit_weights, name,
         prefetch_h=False):
    pieces = h if isinstance(h, tuple) else (h,)
    m = sum(p.shape[0] for p in pieces)
    nf = D_FF // tf
    rows = pl.BlockSpec((tm, D_MODEL), lambda i, j: (i, 0),
                        pipeline_mode=pl.Buffered(1) if m == tm else None)
    scratch = [pltpu.VMEM((tm, D_MODEL), BF16)]
    if prefetch_h:
        assert len(pieces) == 1 and m > tm
        h_specs = [pl.BlockSpec(memory_space=pl.ANY)]
        scratch += [pltpu.VMEM((tm, D_MODEL), F32), pltpu.SemaphoreType.DMA]
    elif len(pieces) == 1:
        h_specs = [rows]
    else:
        assert m == tm, "stacked inputs form one row tile"
        h_specs = [pl.BlockSpec((p.shape[0], D_MODEL), lambda i, j: (0, 0),
                                pipeline_mode=pl.Buffered(1)) for p in pieces]
    vec = pl.BlockSpec((1, D_MODEL), lambda i, j: (0, 0))
    col_tile = pl.BlockSpec((D_MODEL, tf), lambda i, j: (0, j))
    row_tile = pl.BlockSpec((tf, D_MODEL), lambda i, j: (j, 0))
    out_specs = [rows]
    out_shape = [jax.ShapeDtypeStruct((m, D_MODEL), F32)]
    if not final_norm:
        out_specs += [rows]
        out_shape += [jax.ShapeDtypeStruct((m, D_MODEL), BF16)]
    if emit_weights:
        assert m == tm, "weight casts are written once, by a single row tile"
        up_tile = pl.BlockSpec((D_MODEL, tf), lambda i, j: (0, j + nf))
        out_specs += [col_tile, col_tile, row_tile]
        out_shape += [jax.ShapeDtypeStruct((D_MODEL, D_FF), BF16)] * 2
        out_shape += [jax.ShapeDtypeStruct((D_FF, D_MODEL), BF16)]
    else:
        up_tile = col_tile
    return pl.pallas_call(
        functools.partial(_ffn_kernel, n_pieces=len(pieces), final_norm=final_norm,
                          emit_weights=emit_weights, prefetch_h=prefetch_h),
        grid=(m // tm, nf),
        in_specs=h_specs + [vec, col_tile, up_tile, row_tile, vec],
        out_specs=out_specs,
        out_shape=out_shape,
        scratch_shapes=scratch,
        compiler_params=_params("arbitrary" if prefetch_h else "parallel", "arbitrary"),
        name=name,
    )(*pieces, norm_w, *weights, tail_norm_w)


def _proj_kernel(u_ref, w_ref, wal_ref, p16_ref, pf_ref, pa_ref):
    j = pl.program_id(1)
    nt = (((1,), (1,)), ((), ()))

    @pl.when(j == 0)
    def _low_rank():
        wal = jnp.concatenate([wal_ref[...].astype(BF16),
                               jnp.zeros((LANES - ALPHA_RANK, D_MODEL), BF16)], axis=0)
        pa_ref[...] = lax.dot_general(u_ref[...], wal, nt, preferred_element_type=F32)

    def project():
        return lax.dot_general(u_ref[...], w_ref[...].astype(BF16), nt,
                               preferred_element_type=F32)

    @pl.when(j < N16_TILES)
    def _narrow():
        p16_ref[...] = project().astype(BF16)

    @pl.when(j >= N16_TILES)
    def _wide():
        pf_ref[...] = project()


def _w_in_tile(j):
    fa_lo, fa_hi = D_HALF // TN_PROJ, 2 * D_HALF // TN_PROJ
    moved = fa_hi - fa_lo
    return jnp.where(j < fa_lo, j, jnp.where(j < N16_TILES, j + moved, j - N16_TILES + fa_lo))


def _in_proj(u, w, *, tm, name):
    m = u.shape[0]
    return pl.pallas_call(
        _proj_kernel,
        grid=(m // tm, (P16_COLS + D_HALF) // TN_PROJ),
        in_specs=[
            pl.BlockSpec((tm, D_MODEL), lambda i, j: (i, 0)),
            pl.BlockSpec((TN_PROJ, D_MODEL), lambda i, j: (_w_in_tile(j), 0)),
            pl.BlockSpec((ALPHA_RANK, D_MODEL), lambda i, j: ((P16_COLS + D_HALF) // ALPHA_RANK, 0)),
        ],
        out_specs=[
            pl.BlockSpec((None, tm, TN_PROJ), lambda i, j: (jnp.minimum(j, N16_TILES - 1), i, 0)),
            pl.BlockSpec((None, tm, TN_PROJ), lambda i, j: (jnp.maximum(j - N16_TILES, 0), i, 0)),
            pl.BlockSpec((None, tm, LANES), lambda i, j: (0, i, 0)),
        ],
        out_shape=[jax.ShapeDtypeStruct((N16_TILES, m, TN_PROJ), BF16),
                   jax.ShapeDtypeStruct((D_HALF // TN_PROJ, m, TN_PROJ), F32),
                   jax.ShapeDtypeStruct((1, m, LANES), F32)],
        compiler_params=_params("parallel", "arbitrary"),
        name=name,
    )(u, w, w)


def _out_proj_kernel(oa_ref, ob_ref, h_ref, wa_ref, wb_ref, out_ref):
    out_ref[...] = (h_ref[...]
                    + jnp.dot(oa_ref[...], wa_ref[...].astype(BF16), preferred_element_type=F32)
                    + jnp.dot(ob_ref[...], wb_ref[...].astype(BF16), preferred_element_type=F32))


def _out_proj(oa, ob, h, w, *, tm, name):
    m = oa.shape[0]
    return pl.pallas_call(
        _out_proj_kernel,
        grid=(m // tm,),
        in_specs=[
            pl.BlockSpec((tm, D_HALF), lambda i: (i, 0)),
            pl.BlockSpec((tm, D_HALF), lambda i: (i, 0)),
            pl.BlockSpec((tm, D_MODEL), lambda i: (i, 0)),
            pl.BlockSpec((D_HALF, D_MODEL), lambda i: (0, 0), pipeline_mode=pl.Buffered(1)),
            pl.BlockSpec((D_HALF, D_MODEL), lambda i: (1, 0), pipeline_mode=pl.Buffered(1)),
        ],
        out_specs=pl.BlockSpec((tm, D_MODEL), lambda i: (i, 0)),
        out_shape=jax.ShapeDtypeStruct((m, D_MODEL), F32),
        compiler_params=_params("parallel"),
        name=name,
    )(oa, ob, h, w, w)


def _tri(c, seq_len):
    r = lax.broadcasted_iota(jnp.int32, (c, c), 0)
    s = lax.broadcasted_iota(jnp.int32, (c, c), 1)
    return jnp.where((r >= s) & ((r ^ s) < seq_len), 1.0, 0.0).astype(F32)


def _off_mask(c, blk):
    nb = c // blk
    rows = nb * (nb - 1) // 2 * blk
    r = lax.broadcasted_iota(jnp.int32, (c, rows), 0) >> (blk.bit_length() - 1)
    s = lax.broadcasted_iota(jnp.int32, (c, rows), 1)
    seg = jnp.zeros((c, rows), jnp.int32)
    for i in range(1, nb):
        seg = seg + jnp.where(s >= blk * i * (i - 1) // 2, 1, 0)
    return jnp.where(r == seg, 1.0, 0.0).astype(F32)


def _diag(q, k, v, bc, blk):
    c = q.shape[0]
    row = lax.broadcasted_iota(jnp.int32, (blk, 1), 0)
    outs = []
    for i in range(c // blk):
        sl = slice(i * blk, (i + 1) * blk)
        qi, ki, vi, bi = q[sl], k[sl], v[sl].astype(F32), bc[sl]
        acc = jnp.zeros((blk, v.shape[1]), F32)
        for s in range(blk):
            e = jnp.exp2(jnp.minimum(bi - bi[s:s + 1, :], 0.0))
            col = jnp.sum(qi * ki[s:s + 1, :] * e, axis=-1, keepdims=True)
            col = jnp.where(row >= s, col, 0.0)
            acc = acc + col * vi[s:s + 1, :]
        outs.append(acc)
    return outs[0] if len(outs) == 1 else jnp.concatenate(outs, axis=0)


def _offdiag(q, k, v, bc, blk, mask):
    c = q.shape[0]
    nb = c // blk
    qs, ks, vs = [jnp.zeros((blk, DK), F32)], [], []
    for i in range(1, nb):
        lo = i * blk
        ref = bc[lo:lo + 1, :]
        qs.append(q[lo:lo + blk] * jnp.exp2(bc[lo:lo + blk] - ref))
        ks.append(k[:lo] * jnp.exp2(ref - bc[:lo]))
        vs.append(v[:lo])
    qt = jnp.concatenate(qs, axis=0).astype(BF16)
    kt = jnp.concatenate(ks, axis=0).astype(BF16)
    vt = jnp.concatenate(vs, axis=0).astype(BF16)
    a = lax.dot_general(qt, kt, (((1,), (1,)), ((), ())), preferred_element_type=F32)
    return jnp.dot((a * mask).astype(BF16), vt, preferred_element_type=F32)


def _hgrn_inputs(qa, fa, ia, lb):
    fg = lb + (1.0 - lb) * jax.nn.sigmoid(fa)
    return _silu(qa.astype(F32)) * (DK ** -0.5), 1.0 - fg, ia, jnp.log2(fg)


def _gla_inputs(qb, kb, vb, al, wa, ba):
    alpha = jnp.dot(al.astype(BF16), wa, preferred_element_type=F32) + ba
    logsig = jnp.minimum(alpha, 0.0) - jnp.log1p(jnp.exp(-jnp.abs(alpha)))
    return qb.astype(F32) * (DK ** -0.5), kb.astype(F32), vb, logsig * (LOG2_E / GATE_TEMP)


def _head_out(o, gn, gate):
    return (_rms(o, gn) * _silu(gate.astype(F32))).astype(BF16)


def _lower_bound(lbl_ref):
    x = lbl_ref[...]
    e = jnp.exp(x - jnp.max(x, axis=0, keepdims=True))
    return e[0:1, :] / jnp.sum(e, axis=0, keepdims=True)


def _stack_mask(c, blk):
    nb = c // blk
    rows = blk * nb * (nb + 1) // 2
    t = lax.broadcasted_iota(jnp.int32, (c, rows), 0)
    r = lax.broadcasted_iota(jnp.int32, (c, rows), 1)
    seg = jnp.zeros((c, rows), jnp.int32)
    off = jnp.zeros((c, rows), jnp.int32)
    for i in range(1, nb):
        start = blk * i * (i + 1) // 2
        seg = seg + jnp.where(r >= start, 1, 0)
        off = jnp.where(r >= start, start, off)
    keep = ((t >> (blk.bit_length() - 1)) == seg) & (r - off <= t)
    return jnp.where(keep, 1.0, 0.0).astype(F32)


def _state_update_t(k, vb, bc, st):
    c = k.shape[0]
    blast = bc[c - 1:c, :]
    kdec = (k * jnp.exp2(blast - bc)).astype(BF16)
    upd = lax.dot_general(vb, kdec, (((0,), (0,)), ((), ())), preferred_element_type=F32)
    if st is None:
        return upd
    return st * jnp.exp2(blast) + upd


def _scan_sublanes(p):
    row = lax.broadcasted_iota(jnp.int32, (SUBLANES, DK), 0)
    for s in (1, 2, 4):
        p = p + jnp.where(row >= s, pltpu.roll(p, s, axis=0), 0.0)
    return p


def _cumsum_chunk(g):
    out, carry = [], None
    for j in range(CHUNK // SUBLANES):
        p = _scan_sublanes(g[j * SUBLANES:(j + 1) * SUBLANES])
        if carry is not None:
            p = p + carry
        carry = p[SUBLANES - 1:SUBLANES, :]
        out.append(p)
    return jnp.concatenate(out, axis=0)


def _rescaled_operands(q, k, bc):
    c, blk = CHUNK, DIAG_BLK
    qs, ks = [], []
    for i in range(c // blk):
        lo, hi = i * blk, (i + 1) * blk
        ref = bc[lo:lo + 1, :]
        qs.append(q[lo:hi] * jnp.exp2(bc[lo:hi] - ref))
        ks.append(k[:hi] * jnp.exp2(ref - bc[:hi]))
    return ((q * jnp.exp2(bc)).astype(BF16), jnp.concatenate(qs, axis=0).astype(BF16),
            jnp.concatenate(ks, axis=0).T.astype(BF16),
            (k * jnp.exp2(bc[c - 1:c, :] - bc)).astype(BF16))


def _chunk_exact(q, k, v, bc, st, mask):
    o = lax.dot_general((q * jnp.exp2(bc)).astype(BF16), st.astype(BF16),
                        (((1,), (1,)), ((), ())), preferred_element_type=F32)
    o = o + _offdiag(q, k, v, bc, DIAG_BLK, mask) + _diag(q, k, v, bc, DIAG_BLK)
    return o, _state_update_t(k, v.astype(BF16), bc, st)


def _prompt_scan_kernel(*refs, gla, seq, meta):
    n_in = 11 if gla else 8
    if gla:
        q_ref, k_ref, v_ref, gate_ref, al_ref, mk_ref, mv_ref, mal_ref, wa_ref, ba_ref, gn_ref = (
            refs[:n_in])
    else:
        q_ref, f_ref, v_ref, gate_ref, mf_ref, mv_ref, lbl_ref, gn_ref = refs[:n_in]
        lb = _lower_bound(lbl_ref)
    o_ref, sout_ref = refs[n_in:n_in + 2]
    (st_ref, qs_ref, ks_ref, gs_ref, qd_ref, qt_ref, kt_ref, kdec_ref, eb_ref,
     meta_st_ref) = refs[n_in + 2:]
    trips = seq // (CHUNK * CHUNKS_PER_TRIP)
    nt = (((1,), (1,)), ((), ()))
    tn = (((0,), (0,)), ((), ()))

    def chunk_rows(ci, u, n=CHUNK):
        return pl.ds(pl.multiple_of((ci * CHUNKS_PER_TRIP + u) * n, n), n)

    @pl.when(pl.program_id(1) == 0)
    def _meta():
        if gla:
            _, mk, mv, mg = _gla_inputs(mk_ref[...], mk_ref[...], mv_ref[...], mal_ref[...],
                                        wa_ref[...], ba_ref[...])
        else:
            _, mk, mv, mg = _hgrn_inputs(mf_ref[...], mf_ref[...], mv_ref[...], lb)
        mbc = jnp.dot(_tri(meta, meta), mg, precision=HIGHEST, preferred_element_type=F32)
        meta_st_ref[...] = _state_update_t(mk, mv.astype(BF16), mbc, None)

    st_ref[...] = meta_st_ref[...]

    def rows_of(c, n=CHUNK):
        return pl.ds(pl.multiple_of(c * n, n), n)

    def rescaled_pass():
        mask = _stack_mask(CHUNK, DIAG_BLK)
        units = range(CHUNKS_PER_MXU_TRIP)
        mxu_trips = seq // (CHUNK * CHUNKS_PER_MXU_TRIP)

        def prepare(ti, extremes):
            gmin, kmax = extremes
            trip_rows = CHUNK * CHUNKS_PER_MXU_TRIP
            rows = pl.ds(pl.multiple_of(ti * trip_rows, trip_rows), trip_rows)
            if gla:
                q, k, _, g = _gla_inputs(q_ref[rows, :], k_ref[rows, :], None, al_ref[rows, :],
                                         wa_ref[...], ba_ref[...])
            else:
                q, k, _, g = _hgrn_inputs(q_ref[rows, :], f_ref[rows, :], None, lb)
            qs_ref[rows, :] = q
            ks_ref[rows, :] = k
            gs_ref[rows, :] = g
            for u in units:
                c = ti * CHUNKS_PER_MXU_TRIP + u
                sl = slice(u * CHUNK, (u + 1) * CHUNK)
                bc = _cumsum_chunk(g[sl])
                qd, qt, kt, kdec = _rescaled_operands(q[sl], k[sl], bc)
                qd_ref[rows_of(c), :] = qd
                qt_ref[rows_of(c), :] = qt
                kt_ref[rows_of(c, DK), :] = kt
                kdec_ref[rows_of(c), :] = kdec
                eb_ref[c] = jnp.broadcast_to(jnp.exp2(bc[CHUNK - 1:CHUNK, :]), (SUBLANES, DK))
            return (jnp.minimum(gmin, jnp.min(g, axis=0, keepdims=True)),
                    jnp.maximum(kmax, jnp.max(jnp.abs(k), axis=0, keepdims=True)))

        def contract(ti):
            cs = [ti * CHUNKS_PER_MXU_TRIP + u for u in units]
            vb = [v_ref[rows_of(c), :].astype(BF16) for c in cs]
            a = [jnp.dot(qt_ref[rows_of(c), :], kt_ref[rows_of(c, DK), :],
                         preferred_element_type=F32) for c in cs]
            am = [(x * mask).astype(BF16) for x in a]
            vt = [jnp.concatenate([x[:(i + 1) * DIAG_BLK] for i in range(CHUNK // DIAG_BLK)], axis=0)
                  for x in vb]
            oi = [jnp.dot(am[u], vt[u], preferred_element_type=F32) for u in units]
            inc = [lax.dot_general(vb[u], kdec_ref[rows_of(cs[u]), :], tn,
                                   preferred_element_type=F32) for u in units]
            sts = [st_ref[...]]
            for u in units:
                sts.append(sts[-1] * eb_ref[cs[u]][0:1, :] + inc[u])
            o = [oi[u] + lax.dot_general(qd_ref[rows_of(cs[u]), :], sts[u].astype(BF16), nt,
                                         preferred_element_type=F32) for u in units]
            for u in units:
                o_ref[rows_of(cs[u]), :] = _head_out(o[u], gn_ref[...], gate_ref[rows_of(cs[u]), :])
            st_ref[...] = sts[-1]

        extremes = prepare(0, (jnp.zeros((1, DK), F32), jnp.zeros((1, DK), F32)))

        def body(ti, extremes):
            contract(ti)
            return prepare(ti + 1, extremes)

        extremes = lax.fori_loop(0, mxu_trips - 1, body, extremes)
        contract(mxu_trips - 1)
        return extremes

    gmin, kmax = rescaled_pass()
    in_range = (jnp.min(gmin) >= FAST_MIN_LOG_GATE) & (jnp.max(kmax) <= FAST_MAX_KEY)

    @pl.when(jnp.logical_not(in_range))
    def _exact():
        mask = _off_mask(CHUNK, DIAG_BLK)
        st_ref[...] = meta_st_ref[...]

        def body(ci, carry):
            st = st_ref[...]
            for u in range(CHUNKS_PER_TRIP):
                rows = chunk_rows(ci, u)
                o, st = _chunk_exact(qs_ref[rows, :], ks_ref[rows, :], v_ref[rows, :],
                                     _cumsum_chunk(gs_ref[rows, :]), st, mask)
                o_ref[rows, :] = _head_out(o, gn_ref[...], gate_ref[rows, :])
            st_ref[...] = st
            return carry

        lax.fori_loop(0, trips, body, 0)

    sout_ref[...] = st_ref[...].T


def _sample_scan_kernel(*refs, gla, steps):
    if gla:
        (q_ref, k_ref, v_ref, gate_ref, al_ref, wa_ref, ba_ref, gn_ref, s0_ref,
         o_ref, sout_ref) = refs
        q, k, v, g = _gla_inputs(q_ref[...], k_ref[...], v_ref[...], al_ref[...],
                                 wa_ref[...], ba_ref[...])
    else:
        (q_ref, f_ref, v_ref, gate_ref, lbl_ref, gn_ref, s0_ref, o_ref, sout_ref) = refs
        q, k, v, g = _hgrn_inputs(q_ref[...], f_ref[...], v_ref[...], _lower_bound(lbl_ref))
    v32 = v.astype(F32)
    rows = SAMPLE_SEQS * steps
    groups = [slice(b * steps, (b + 1) * steps) for b in range(SAMPLE_SEQS)]
    tn = (((0,), (0,)), ((), ()))

    bc = jnp.concatenate([_scan_sublanes(g[sl]) for sl in groups], axis=0)

    def repeated(r):
        return jnp.concatenate([jnp.broadcast_to(bc[sl][r:r + 1, :], (steps, DK)) for sl in groups],
                               axis=0)

    qd = (q * jnp.exp2(bc)).astype(BF16)
    kdec = (k * jnp.exp2(repeated(steps - 1) - bc)).astype(BF16)
    r = lax.broadcasted_iota(jnp.int32, (rows, SAMPLE_SEQS), 0)
    c = lax.broadcasted_iota(jnp.int32, (rows, SAMPLE_SEQS), 1)
    selector = jnp.where((r >> (steps.bit_length() - 1)) == c, 1.0, 0.0).astype(F32)
    decay = jnp.exp2(lax.dot_general(g, selector, tn, precision=HIGHEST,
                                    preferred_element_type=F32))

    inter = [jnp.dot(qd[sl], s0_ref[b].astype(BF16), preferred_element_type=F32)
             for b, sl in enumerate(groups)]
    upd = [lax.dot_general(kdec[sl], v32[sl].astype(BF16), tn, preferred_element_type=F32)
           for sl in groups]
    for b in range(SAMPLE_SEQS):
        sout_ref[b] = decay[:, b:b + 1] * s0_ref[b] + upd[b]
    o = jnp.concatenate(inter, axis=0) + _diag(q, k, v32, bc, steps)
    o_ref[...] = _head_out(o, gn_ref[...], gate_ref[...])


def _proj_spec(rows, width, c0, per_head, row_block, tile_width=TN_PROJ, head_major=False):
    per_tile = tile_width // width

    def index(i0, i1):
        b, h = (i1, i0) if head_major else (i0, i1)
        g = c0 * LANES // width + h * per_head
        return (g // per_tile, row_block(b), g % per_tile)

    return pl.BlockSpec((None, rows, width), index)


def _prompt_scan(p, pm, *, gla, batch, seq, meta_block, small, name):
    heads, dv = (H_B, DV_B) if gla else (H_A, DV_A)
    n_chunks = seq // CHUNK
    blocks = CHUNK // DIAG_BLK
    stack = DIAG_BLK * blocks * (blocks + 1) // 2
    p16, pf, pa = p
    pm16, pmf, pma = pm
    tok = functools.partial(_proj_spec, seq, row_block=lambda b: b, head_major=True)
    met = functools.partial(_proj_spec, N_META, row_block=lambda b: meta_block, head_major=True)
    vec = lambda width: pl.BlockSpec((1, width), lambda h, b: (0, h))
    if gla:
        wa, ba, gn = small
        in_specs = [tok(DK, COL_QB, 1), tok(DK, COL_KB, 1), tok(dv, COL_VB, 1), tok(dv, COL_RB, 1),
                    tok(LANES, 0, 0, tile_width=LANES),
                    met(DK, COL_KB, 1), met(dv, COL_VB, 1), met(LANES, 0, 0, tile_width=LANES),
                    pl.BlockSpec((LANES, DK), lambda h, b: (0, h)), vec(DK), vec(dv)]
        args = (p16, p16, p16, p16, pa, pm16, pm16, pma, wa, ba, gn)
    else:
        lbl, gn = small
        in_specs = [tok(DK, COL_QA, 1), tok(DK, 0, 1), tok(dv, COL_IA, 1), tok(dv, COL_GA, 1),
                    met(DK, 0, 1), met(dv, COL_IA, 1),
                    pl.BlockSpec((2, DK), lambda h, b: (0, h)), vec(dv)]
        args = (p16, pf, p16, p16, pmf, pm16, lbl, gn)
    return pl.pallas_call(
        functools.partial(_prompt_scan_kernel, gla=gla, seq=seq, meta=N_META),
        grid=(heads, batch),
        in_specs=in_specs,
        out_specs=[pl.BlockSpec((seq, dv), lambda h, b: (b, h)),
                   pl.BlockSpec((None, None, DK, dv), lambda h, b: (b, h, 0, 0))],
        out_shape=[jax.ShapeDtypeStruct((batch * seq, heads * dv), BF16),
                   jax.ShapeDtypeStruct((batch, heads, DK, dv), F32)],
        scratch_shapes=[
            pltpu.VMEM((dv, DK), F32),
            pltpu.VMEM((seq, DK), F32),
            pltpu.VMEM((seq, DK), F32),
            pltpu.VMEM((seq, DK), F32),
            pltpu.VMEM((seq, DK), BF16),
            pltpu.VMEM((seq, DK), BF16),
            pltpu.VMEM((n_chunks * DK, stack), BF16),
            pltpu.VMEM((seq, DK), BF16),
            pltpu.VMEM((n_chunks, SUBLANES, DK), F32),
            pltpu.VMEM((dv, DK), F32),
        ],
        compiler_params=_params("parallel", "arbitrary"),
        name=name,
    )(*args)


def _sample_scan(p, s0, *, gla, batch, steps, small, name):
    assert steps == SUBLANES, "the sample recurrence keeps one sequence per sublane group"
    heads, dv = (H_B, DV_B) if gla else (H_A, DV_A)
    rows = SAMPLE_SEQS * steps
    p16, pf, pa = p
    tok = functools.partial(_proj_spec, rows, row_block=lambda b: b)
    vec = lambda width: pl.BlockSpec((1, width), lambda b, h: (0, h))
    state = pl.BlockSpec((SAMPLE_SEQS, None, DK, dv), lambda b, h: (b, h, 0, 0))
    if gla:
        wa, ba, gn = small
        in_specs = [tok(DK, COL_QB, 1), tok(DK, COL_KB, 1), tok(dv, COL_VB, 1), tok(dv, COL_RB, 1),
                    tok(LANES, 0, 0, tile_width=LANES),
                    pl.BlockSpec((LANES, DK), lambda b, h: (0, h)), vec(DK), vec(dv), state]
        args = (p16, p16, p16, p16, pa, wa, ba, gn, s0)
    else:
        lbl, gn = small
        in_specs = [tok(DK, COL_QA, 1), tok(DK, 0, 1), tok(dv, COL_IA, 1), tok(dv, COL_GA, 1),
                    pl.BlockSpec((2, DK), lambda b, h: (0, h)), vec(dv), state]
        args = (p16, pf, p16, p16, lbl, gn, s0)
    return pl.pallas_call(
        functools.partial(_sample_scan_kernel, gla=gla, steps=steps),
        grid=(batch // SAMPLE_SEQS, heads),
        in_specs=in_specs,
        out_specs=[pl.BlockSpec((rows, dv), lambda b, h: (b, h)), state],
        out_shape=[jax.ShapeDtypeStruct((batch * steps, heads * dv), BF16),
                   jax.ShapeDtypeStruct((batch, heads, DK, dv), F32)],
        compiler_params=_params("parallel", "parallel"),
        name=name,
    )(*args)


def kernel(x_prompt, x_sample, state_hgrn, state_gla, meta_tokens, lb_logits, ffn1_norm, w_ffn1_in,
           w_ffn1_out, mix_norm, w_in, w_alpha_up, b_alpha, gnorm_a, gnorm_b, w_out, ffn2_norm,
           w_ffn2_in, w_ffn2_out, final_norm):
    batch, seq, _ = x_prompt.shape
    dec_batch, steps, _ = x_sample.shape

    w_in_t = w_in[0].T
    wop = w_out[0]
    wa = jnp.pad(w_alpha_up[0], ((0, LANES - ALPHA_RANK), (0, 0))).astype(BF16)
    n1, nm, n2, nf = ffn1_norm, mix_norm, ffn2_norm, final_norm[None]
    small_a = (lb_logits, gnorm_a)
    small_b = (wa, b_alpha, gnorm_b)
    n_sample = dec_batch * steps

    xs = (x_sample.reshape(n_sample, D_MODEL), meta_tokens)
    n_stream = n_sample + N_META
    h1s, u1s, *w1 = _ffn(xs, n1, (w_ffn1_in[0], w_ffn1_in[0], w_ffn1_out[0]), nm, tm=n_stream,
                         tf=TF_CAST, final_norm=False, emit_weights=True, name="ffn1_sample")
    ps = _in_proj(u1s, w_in_t, tm=n_stream, name="inproj_sample")
    oa_s, sa_s = _sample_scan(ps, state_hgrn[0], gla=False, batch=dec_batch, steps=steps,
                              small=small_a, name="scan_hgrn_sample")
    ob_s, sb_s = _sample_scan(ps, state_gla[0], gla=True, batch=dec_batch, steps=steps,
                              small=small_b, name="scan_gla_sample")
    h2s = _out_proj(oa_s, ob_s, h1s, wop, tm=TM, name="outproj_sample")
    y_s, *w2 = _ffn(h2s, n2, (w_ffn2_in[0], w_ffn2_in[0], w_ffn2_out[0]), nf, tm=n_sample,
                    tf=TF_CAST, final_norm=True, emit_weights=True, name="ffn2_sample")

    h1p, u1p = _ffn(x_prompt.reshape(batch * seq, D_MODEL), n1, w1, nm, tm=TM_FFN, tf=TF_WIDE,
                    final_norm=False, emit_weights=False, prefetch_h=True, name="ffn1_prompt")
    pp = _in_proj(u1p, w_in_t, tm=TM_PROJ, name="inproj_prompt")
    meta_block = n_sample // N_META
    oa_p, sa_p = _prompt_scan(pp, ps, gla=False, batch=batch, seq=seq, meta_block=meta_block,
                              small=small_a, name="scan_hgrn_prompt")
    ob_p, sb_p = _prompt_scan(pp, ps, gla=True, batch=batch, seq=seq, meta_block=meta_block,
                              small=small_b, name="scan_gla_prompt")
    h2p = _out_proj(oa_p, ob_p, h1p, wop, tm=TM, name="outproj_prompt")
    (y_p,) = _ffn(h2p, n2, w2, nf, tm=TM_FFN, tf=TF_WIDE, final_norm=True, emit_weights=False,
                  prefetch_h=True, name="ffn2_prompt")

    return (y_p.reshape(batch, seq, D_MODEL), y_s.reshape(dec_batch, steps, D_MODEL),
            sa_p[None], sb_p[None], sa_s[None], sb_s[None])
```

```python
import functools

import jax
import jax.numpy as jnp
from jax import lax
from jax.experimental import pallas as pl
from jax.experimental.pallas import tpu as pltpu

F32 = jnp.float32
BF16 = jnp.bfloat16
HIGHEST = lax.Precision.HIGHEST

D_MODEL = 2048
N_META = 16
DK = 128
H_A, DV_A = 8, 128
H_B, DV_B = 4, 256
D_HALF = H_A * DV_A
ALPHA_RANK = 16
GATE_TEMP = 16.0
D_FF = 5632
CHUNK = 64
EPS = 1e-6
IN_PROJ_WIDTHS = (D_HALF, D_HALF, D_HALF, D_HALF, H_B * DK, H_B * DK, D_HALF, D_HALF, ALPHA_RANK)

LANES = 128
SUBLANES = 8
VMEM_LIMIT_BYTES = 56 * 1024 * 1024

COL_QA, COL_IA, COL_GA, COL_QB, COL_KB, COL_VB, COL_RB = 0, 8, 16, 24, 28, 32, 40
P16_COLS = 3 * D_HALF + 2 * H_B * DK + 2 * D_HALF

TM = 512
TM_FFN = 1024
TF = 256
TF_WIDE = 512
TF_CAST = 256
TN_PROJ = 512
TM_PROJ = 2048
N16_TILES = P16_COLS // TN_PROJ
DIAG_BLK = 16
SAMPLE_SEQS = 64
CHUNKS_PER_TRIP = 4
CHUNKS_PER_MXU_TRIP = 16

LOG2_E = 1.4426950408889634

FAST_MIN_LOG_GATE = -60.0 * LOG2_E / (DIAG_BLK - 1)
FAST_MAX_KEY = 1e10


def _rms(x, w):
    return x * lax.rsqrt(jnp.mean(x * x, axis=-1, keepdims=True) + EPS) * w


def _silu(x):
    return x * jax.nn.sigmoid(x)


def _params(*sem):
    return pltpu.CompilerParams(dimension_semantics=sem, vmem_limit_bytes=VMEM_LIMIT_BYTES)


def _ffn_kernel(*refs, n_pieces, final_norm, emit_weights, prefetch_h):
    h_refs = refs[:n_pieces]
    nw_ref, wg_ref, wu_ref, wo_ref, tnw_ref, out_ref, *rest = refs[n_pieces:]
    rest = list(rest)
    un_ref = None if final_norm else rest.pop(0)
    w16_refs = [rest.pop(0) for _ in range(3)] if emit_weights else []
    u_ref = rest.pop(0)
    i, j = pl.program_id(0), pl.program_id(1)
    last_j = pl.num_programs(1) - 1
    if prefetch_h:
        h_buf, h_sem = rest

        def h_copy(tile):
            rows = h_buf.shape[0]
            src = h_refs[0].at[pl.ds(pl.multiple_of(tile * rows, rows), rows), :]
            return pltpu.make_async_copy(src, h_buf, h_sem)

    def load_h():
        pieces = [r[...] for r in h_refs]
        return pieces[0] if n_pieces == 1 else jnp.concatenate(pieces, axis=0)

    def step(first, last):
        if first:
            if prefetch_h:
                pl.when(i == 0)(lambda: h_copy(0).start())
                h_copy(i).wait()
                h = h_buf[...]
            else:
                h = load_h()
            u = _rms(h, nw_ref[...]).astype(BF16)
            u_ref[...] = u
        else:
            u = u_ref[...]
        wg, wu, wo = (r[...].astype(BF16) for r in (wg_ref, wu_ref, wo_ref))
        for dst, w in zip(w16_refs, (wg, wu, wo)):
            dst[...] = w
        g = jnp.dot(u, wg, preferred_element_type=F32)
        up = jnp.dot(u, wu, preferred_element_type=F32)
        a = (_silu(g) * up).astype(BF16)
        acc = jnp.dot(a, wo, preferred_element_type=F32)
        if not first:
            acc = out_ref[...] + acc
        elif prefetch_h:
            acc = 2.0 * h + acc
        if not last:
            out_ref[...] = acc
            if first and prefetch_h:
                pl.when(i + 1 < pl.num_programs(0))(lambda: h_copy(i + 1).start())
            return
        y = 0.5 * acc if prefetch_h else load_h() + 0.5 * acc
        normed = _rms(y, tnw_ref[...])
        if final_norm:
            out_ref[...] = normed
        else:
            out_ref[...] = y
            un_ref[...] = normed.astype(BF16)

    pl.when(j == 0)(functools.partial(step, True, False))
    pl.when((j > 0) & (j < last_j))(functools.partial(step, False, False))
    pl.when(j == last_j)(functools.partial(step, False, True))


def _ffn(h, norm_w, weights, tail_norm_w, *, tm, tf, final_norm, emit_weights, name,
         prefetch_h=False):
    pieces = h if isinstance(h, tuple) else (h,)
    m = sum(p.shape[0] for p in pieces)
    nf = D_FF // tf
    rows = pl.BlockSpec((tm, D_MODEL), lambda i, j: (i, 0),
                        pipeline_mode=pl.Buffered(1) if m == tm else None)
    scratch = [pltpu.VMEM((tm, D_MODEL), BF16)]
    if prefetch_h:
        assert len(pieces) == 1 and m > tm
        h_specs = [pl.BlockSpec(memory_space=pl.ANY)]
        scratch += [pltpu.VMEM((tm, D_MODEL), F32), pltpu.SemaphoreType.DMA]
    elif len(pieces) == 1:
        h_specs = [rows]
    else:
        assert m == tm, "stacked inputs form one row tile"
        h_specs = [pl.BlockSpec((p.shape[0], D_MODEL), lambda i, j: (0, 0),
                                pipeline_mode=pl.Buffered(1)) for p in pieces]
    vec = pl.BlockSpec((1, D_MODEL), lambda i, j: (0, 0))
    col_tile = pl.BlockSpec((D_MODEL, tf), lambda i, j: (0, j))
    row_tile = pl.BlockSpec((tf, D_MODEL), lambda i, j: (j, 0))
    out_specs = [rows]
    out_shape = [jax.ShapeDtypeStruct((m, D_MODEL), F32)]
    if not final_norm:
        out_specs += [rows]
        out_shape += [jax.ShapeDtypeStruct((m, D_MODEL), BF16)]
    if emit_weights:
        assert m == tm, "weight casts are written once, by a single row tile"
        up_tile = pl.BlockSpec((D_MODEL, tf), lambda i, j: (0, j + nf))
        out_specs += [col_tile, col_tile, row_tile]
        out_shape += [jax.ShapeDtypeStruct((D_MODEL, D_FF), BF16)] * 2
        out_shape += [jax.ShapeDtypeStruct((D_FF, D_MODEL), BF16)]
    else:
        up_tile = col_tile
    return pl.pallas_call(
        functools.partial(_ffn_kernel, n_pieces=len(pieces), final_norm=final_norm,
                          emit_weights=emit_weights, prefetch_h=prefetch_h),
        grid=(m // tm, nf),
        in_specs=h_specs + [vec, col_tile, up_tile, row_tile, vec],
        out_specs=out_specs,
        out_shape=out_shape,
        scratch_shapes=scratch,
        compiler_params=_params("arbitrary" if prefetch_h else "parallel", "arbitrary"),
        name=name,
    )(*pieces, norm_w, *weights, tail_norm_w)


def _proj_kernel(u_ref, w_ref, wal_ref, p16_ref, pf_ref, pa_ref):
    j = pl.program_id(1)
    nt = (((1,), (1,)), ((), ()))

    @pl.when(j == 0)
    def _low_rank():
        wal = jnp.concatenate([wal_ref[...].astype(BF16),
                               jnp.zeros((LANES - ALPHA_RANK, D_MODEL), BF16)], axis=0)
        pa_ref[...] = lax.dot_general(u_ref[...], wal, nt, preferred_element_type=F32)

    def project():
        return lax.dot_general(u_ref[...], w_ref[...].astype(BF16), nt,
                               preferred_element_type=F32)

    @pl.when(j < N16_TILES)
    def _narrow():
        p16_ref[...] = project().astype(BF16)

    @pl.when(j >= N16_TILES)
    def _wide():
        pf_ref[...] = project()


def _w_in_tile(j):
    fa_lo, fa_hi = D_HALF // TN_PROJ, 2 * D_HALF // TN_PROJ
    moved = fa_hi - fa_lo
    return jnp.where(j < fa_lo, j, jnp.where(j < N16_TILES, j + moved, j - N16_TILES + fa_lo))


def _in_proj(u, w, *, tm, name):
    m = u.shape[0]
    return pl.pallas_call(
        _proj_kernel,
        grid=(m // tm, (P16_COLS + D_HALF) // TN_PROJ),
        in_specs=[
            pl.BlockSpec((tm, D_MODEL), lambda i, j: (i, 0)),
            pl.BlockSpec((TN_PROJ, D_MODEL), lambda i, j: (_w_in_tile(j), 0)),
            pl.BlockSpec((ALPHA_RANK, D_MODEL), lambda i, j: ((P16_COLS + D_HALF) // ALPHA_RANK, 0)),
        ],
        out_specs=[
            pl.BlockSpec((None, tm, TN_PROJ), lambda i, j: (jnp.minimum(j, N16_TILES - 1), i, 0)),
            pl.BlockSpec((None, tm, TN_PROJ), lambda i, j: (jnp.maximum(j - N16_TILES, 0), i, 0)),
            pl.BlockSpec((None, tm, LANES), lambda i, j: (0, i, 0)),
        ],
        out_shape=[jax.ShapeDtypeStruct((N16_TILES, m, TN_PROJ), BF16),
                   jax.ShapeDtypeStruct((D_HALF // TN_PROJ, m, TN_PROJ), F32),
                   jax.ShapeDtypeStruct((1, m, LANES), F32)],
        compiler_params=_params("parallel", "arbitrary"),
        name=name,
    )(u, w, w)


def _out_proj_kernel(oa_ref, ob_ref, h_ref, wa_ref, wb_ref, out_ref):
    out_ref[...] = (h_ref[...]
                    + jnp.dot(oa_ref[...], wa_ref[...].astype(BF16), preferred_element_type=F32)
                    + jnp.dot(ob_ref[...], wb_ref[...].astype(BF16), preferred_element_type=F32))


def _out_proj(oa, ob, h, w, *, tm, name):
    m = oa.shape[0]
    return pl.pallas_call(
        _out_proj_kernel,
        grid=(m // tm,),
        in_specs=[
            pl.BlockSpec((tm, D_HALF), lambda i: (i, 0)),
            pl.BlockSpec((tm, D_HALF), lambda i: (i, 0)),
            pl.BlockSpec((tm, D_MODEL), lambda i: (i, 0)),
            pl.BlockSpec((D_HALF, D_MODEL), lambda i: (0, 0), pipeline_mode=pl.Buffered(1)),
            pl.BlockSpec((D_HALF, D_MODEL), lambda i: (1, 0), pipeline_mode=pl.Buffered(1)),
        ],
        out_specs=pl.BlockSpec((tm, D_MODEL), lambda i: (i, 0)),
        out_shape=jax.ShapeDtypeStruct((m, D_MODEL), F32),
        compiler_params=_params("parallel"),
        name=name,
    )(oa, ob, h, w, w)


def _tri(c, seq_len):
    r = lax.broadcasted_iota(jnp.int32, (c, c), 0)
    s = lax.broadcasted_iota(jnp.int32, (c, c), 1)
    return jnp.where((r >= s) & ((r ^ s) < seq_len), 1.0, 0.0).astype(F32)


def _off_mask(c, blk):
    nb = c // blk
    rows = nb * (nb - 1) // 2 * blk
    r = lax.broadcasted_iota(jnp.int32, (c, rows), 0) >> (blk.bit_length() - 1)
    s = lax.broadcasted_iota(jnp.int32, (c, rows), 1)
    seg = jnp.zeros((c, rows), jnp.int32)
    for i in range(1, nb):
        seg = seg + jnp.where(s >= blk * i * (i - 1) // 2, 1, 0)
    return jnp.where(r == seg, 1.0, 0.0).astype(F32)


def _diag(q, k, v, bc, blk):
    c = q.shape[0]
    row = lax.broadcasted_iota(jnp.int32, (blk, 1), 0)
    outs = []
    for i in range(c // blk):
        sl = slice(i * blk, (i + 1) * blk)
        qi, ki, vi, bi = q[sl], k[sl], v[sl].astype(F32), bc[sl]
        acc = jnp.zeros((blk, v.shape[1]), F32)
        for s in range(blk):
            e = jnp.exp2(jnp.minimum(bi - bi[s:s + 1, :], 0.0))
            col = jnp.sum(qi * ki[s:s + 1, :] * e, axis=-1, keepdims=True)
            col = jnp.where(row >= s, col, 0.0)
            acc = acc + col * vi[s:s + 1, :]
        outs.append(acc)
    return outs[0] if len(outs) == 1 else jnp.concatenate(outs, axis=0)


def _offdiag(q, k, v, bc, blk, mask):
    c = q.shape[0]
    nb = c // blk
    qs, ks, vs = [jnp.zeros((blk, DK), F32)], [], []
    for i in range(1, nb):
        lo = i * blk
        ref = bc[lo:lo + 1, :]
        qs.append(q[lo:lo + blk] * jnp.exp2(bc[lo:lo + blk] - ref))
        ks.append(k[:lo] * jnp.exp2(ref - bc[:lo]))
        vs.append(v[:lo])
    qt = jnp.concatenate(qs, axis=0).astype(BF16)
    kt = jnp.concatenate(ks, axis=0).astype(BF16)
    vt = jnp.concatenate(vs, axis=0).astype(BF16)
    a = lax.dot_general(qt, kt, (((1,), (1,)), ((), ())), preferred_element_type=F32)
    return jnp.dot((a * mask).astype(BF16), vt, preferred_element_type=F32)


def _hgrn_inputs(qa, fa, ia, lb):
    fg = lb + (1.0 - lb) * jax.nn.sigmoid(fa)
    return _silu(qa.astype(F32)) * (DK ** -0.5), 1.0 - fg, ia, jnp.log2(fg)


def _gla_inputs(qb, kb, vb, al, wa, ba):
    alpha = jnp.dot(al.astype(BF16), wa, preferred_element_type=F32) + ba
    logsig = jnp.minimum(alpha, 0.0) - jnp.log1p(jnp.exp(-jnp.abs(alpha)))
    return qb.astype(F32) * (DK ** -0.5), kb.astype(F32), vb, logsig * (LOG2_E / GATE_TEMP)


def _head_out(o, gn, gate):
    return (_rms(o, gn) * _silu(gate.astype(F32))).astype(BF16)


def _lower_bound(lbl_ref):
    x = lbl_ref[...]
    e = jnp.exp(x - jnp.max(x, axis=0, keepdims=True))
    return e[0:1, :] / jnp.sum(e, axis=0, keepdims=True)


def _stack_mask(c, blk):
    nb = c // blk
    rows = blk * nb * (nb + 1) // 2
    t = lax.broadcasted_iota(jnp.int32, (c, rows), 0)
    r = lax.broadcasted_iota(jnp.int32, (c, rows), 1)
    seg = jnp.zeros((c, rows), jnp.int32)
    off = jnp.zeros((c, rows), jnp.int32)
    for i in range(1, nb):
        start = blk * i * (i + 1) // 2
        seg = seg + jnp.where(r >= start, 1, 0)
        off = jnp.where(r >= start, start, off)
    keep = ((t >> (blk.bit_length() - 1)) == seg) & (r - off <= t)
    return jnp.where(keep, 1.0, 0.0).astype(F32)


def _state_update_t(k, vb, bc, st):
    c = k.shape[0]
    blast = bc[c - 1:c, :]
    kdec = (k * jnp.exp2(blast - bc)).astype(BF16)
    upd = lax.dot_general(vb, kdec, (((0,), (0,)), ((), ())), preferred_element_type=F32)
    if st is None:
        return upd
    return st * jnp.exp2(blast) + upd


def _scan_sublanes(p):
    row = lax.broadcasted_iota(jnp.int32, (SUBLANES, DK), 0)
    for s in (1, 2, 4):
        p = p + jnp.where(row >= s, pltpu.roll(p, s, axis=0), 0.0)
    return p


def _cumsum_chunk(g):
    out, carry = [], None
    for j in range(CHUNK // SUBLANES):
        p = _scan_sublanes(g[j * SUBLANES:(j + 1) * SUBLANES])
        if carry is not None:
            p = p + carry
        carry = p[SUBLANES - 1:SUBLANES, :]
        out.append(p)
    return jnp.concatenate(out, axis=0)


def _rescaled_operands(q, k, bc):
    c, blk = CHUNK, DIAG_BLK
    qs, ks = [], []
    for i in range(c // blk):
        lo, hi = i * blk, (i + 1) * blk
        ref = bc[lo:lo + 1, :]
        qs.append(q[lo:hi] * jnp.exp2(bc[lo:hi] - ref))
        ks.append(k[:hi] * jnp.exp2(ref - bc[:hi]))
    return ((q * jnp.exp2(bc)).astype(BF16), jnp.concatenate(qs, axis=0).astype(BF16),
            jnp.concatenate(ks, axis=0).T.astype(BF16),
            (k * jnp.exp2(bc[c - 1:c, :] - bc)).astype(BF16))


def _chunk_exact(q, k, v, bc, st, mask):
    o = lax.dot_general((q * jnp.exp2(bc)).astype(BF16), st.astype(BF16),
                        (((1,), (1,)), ((), ())), preferred_element_type=F32)
    o = o + _offdiag(q, k, v, bc, DIAG_BLK, mask) + _diag(q, k, v, bc, DIAG_BLK)
    return o, _state_update_t(k, v.astype(BF16), bc, st)


def _prompt_scan_kernel(*refs, gla, seq, meta):
    n_in = 11 if gla else 8
    if gla:
        q_ref, k_ref, v_ref, gate_ref, al_ref, mk_ref, mv_ref, mal_ref, wa_ref, ba_ref, gn_ref = (
            refs[:n_in])
    else:
        q_ref, f_ref, v_ref, gate_ref, mf_ref, mv_ref, lbl_ref, gn_ref = refs[:n_in]
        lb = _lower_bound(lbl_ref)
    o_ref, sout_ref = refs[n_in:n_in + 2]
    (st_ref, qs_ref, ks_ref, gs_ref, qd_ref, qt_ref, kt_ref, kdec_ref, eb_ref,
     meta_st_ref) = refs[n_in + 2:]
    trips = seq // (CHUNK * CHUNKS_PER_TRIP)
    nt = (((1,), (1,)), ((), ()))
    tn = (((0,), (0,)), ((), ()))

    def chunk_rows(ci, u, n=CHUNK):
        return pl.ds(pl.multiple_of((ci * CHUNKS_PER_TRIP + u) * n, n), n)

    @pl.when(pl.program_id(1) == 0)
    def _meta():
        if gla:
            _, mk, mv, mg = _gla_inputs(mk_ref[...], mk_ref[...], mv_ref[...], mal_ref[...],
                                        wa_ref[...], ba_ref[...])
        else:
            _, mk, mv, mg = _hgrn_inputs(mf_ref[...], mf_ref[...], mv_ref[...], lb)
        mbc = jnp.dot(_tri(meta, meta), mg, precision=HIGHEST, preferred_element_type=F32)
        meta_st_ref[...] = _state_update_t(mk, mv.astype(BF16), mbc, None)

    st_ref[...] = meta_st_ref[...]

    def rows_of(c, n=CHUNK):
        return pl.ds(pl.multiple_of(c * n, n), n)

    def rescaled_pass():
        mask = _stack_mask(CHUNK, DIAG_BLK)
        units = range(CHUNKS_PER_MXU_TRIP)
        mxu_trips = seq // (CHUNK * CHUNKS_PER_MXU_TRIP)

        def prepare(ti, extremes):
            gmin, kmax = extremes
            trip_rows = CHUNK * CHUNKS_PER_MXU_TRIP
            rows = pl.ds(pl.multiple_of(ti * trip_rows, trip_rows), trip_rows)
            if gla:
                q, k, _, g = _gla_inputs(q_ref[rows, :], k_ref[rows, :], None, al_ref[rows, :],
                                         wa_ref[...], ba_ref[...])
            else:
                q, k, _, g = _hgrn_inputs(q_ref[rows, :], f_ref[rows, :], None, lb)
            qs_ref[rows, :] = q
            ks_ref[rows, :] = k
            gs_ref[rows, :] = g
            for u in units:
                c = ti * CHUNKS_PER_MXU_TRIP + u
                sl = slice(u * CHUNK, (u + 1) * CHUNK)
                bc = _cumsum_chunk(g[sl])
                qd, qt, kt, kdec = _rescaled_operands(q[sl], k[sl], bc)
                qd_ref[rows_of(c), :] = qd
                qt_ref[rows_of(c), :] = qt
                kt_ref[rows_of(c, DK), :] = kt
                kdec_ref[rows_of(c), :] = kdec
                eb_ref[c] = jnp.broadcast_to(jnp.exp2(bc[CHUNK - 1:CHUNK, :]), (SUBLANES, DK))
            return (jnp.minimum(gmin, jnp.min(g, axis=0, keepdims=True)),
                    jnp.maximum(kmax, jnp.max(jnp.abs(k), axis=0, keepdims=True)))

        def contract(ti):
            cs = [ti * CHUNKS_PER_MXU_TRIP + u for u in units]
            vb = [v_ref[rows_of(c), :].astype(BF16) for c in cs]
            a = [jnp.dot(qt_ref[rows_of(c), :], kt_ref[rows_of(c, DK), :],
                         preferred_element_type=F32) for c in cs]
            am = [(x * mask).astype(BF16) for x in a]
            vt = [jnp.concatenate([x[:(i + 1) * DIAG_BLK] for i in range(CHUNK // DIAG_BLK)], axis=0)
                  for x in vb]
            oi = [jnp.dot(am[u], vt[u], preferred_element_type=F32) for u in units]
            inc = [lax.dot_general(vb[u], kdec_ref[rows_of(cs[u]), :], tn,
                                   preferred_element_type=F32) for u in units]
            sts = [st_ref[...]]
            for u in units:
                sts.append(sts[-1] * eb_ref[cs[u]][0:1, :] + inc[u])
            o = [oi[u] + lax.dot_general(qd_ref[rows_of(cs[u]), :], sts[u].astype(BF16), nt,
                                         preferred_element_type=F32) for u in units]
            for u in units:
                o_ref[rows_of(cs[u]), :] = _head_out(o[u], gn_ref[...], gate_ref[rows_of(cs[u]), :])
            st_ref[...] = sts[-1]

        extremes = prepare(0, (jnp.zeros((1, DK), F32), jnp.zeros((1, DK), F32)))

        def body(ti, extremes):
            contract(ti)
            return prepare(ti + 1, extremes)

        extremes = lax.fori_loop(0, mxu_trips - 1, body, extremes)
        contract(mxu_trips - 1)
        return extremes

    gmin, kmax = rescaled_pass()
    in_range = (jnp.min(gmin) >= FAST_MIN_LOG_GATE) & (jnp.max(kmax) <= FAST_MAX_KEY)

    @pl.when(jnp.logical_not(in_range))
    def _exact():
        mask = _off_mask(CHUNK, DIAG_BLK)
        st_ref[...] = meta_st_ref[...]

        def body(ci, carry):
            st = st_ref[...]
            for u in range(CHUNKS_PER_TRIP):
                rows = chunk_rows(ci, u)
                o, st = _chunk_exact(qs_ref[rows, :], ks_ref[rows, :], v_ref[rows, :],
                                     _cumsum_chunk(gs_ref[rows, :]), st, mask)
                o_ref[rows, :] = _head_out(o, gn_ref[...], gate_ref[rows, :])
            st_ref[...] = st
            return carry

        lax.fori_loop(0, trips, body, 0)

    sout_ref[...] = st_ref[...].T


def _sample_scan_kernel(*refs, gla, steps):
    if gla:
        (q_ref, k_ref, v_ref, gate_ref, al_ref, wa_ref, ba_ref, gn_ref, s0_ref,
         o_ref, sout_ref) = refs
        q, k, v, g = _gla_inputs(q_ref[...], k_ref[...], v_ref[...], al_ref[...],
                                 wa_ref[...], ba_ref[...])
    else:
        (q_ref, f_ref, v_ref, gate_ref, lbl_ref, gn_ref, s0_ref, o_ref, sout_ref) = refs
        q, k, v, g = _hgrn_inputs(q_ref[...], f_ref[...], v_ref[...], _lower_bound(lbl_ref))
    v32 = v.astype(F32)
    rows = SAMPLE_SEQS * steps
    groups = [slice(b * steps, (b + 1) * steps) for b in range(SAMPLE_SEQS)]
    tn = (((0,), (0,)), ((), ()))

    bc = jnp.concatenate([_scan_sublanes(g[sl]) for sl in groups], axis=0)

    def repeated(r):
        return jnp.concatenate([jnp.broadcast_to(bc[sl][r:r + 1, :], (steps, DK)) for sl in groups],
                               axis=0)

    qd = (q * jnp.exp2(bc)).astype(BF16)
    kdec = (k * jnp.exp2(repeated(steps - 1) - bc)).astype(BF16)
    r = lax.broadcasted_iota(jnp.int32, (rows, SAMPLE_SEQS), 0)
    c = lax.broadcasted_iota(jnp.int32, (rows, SAMPLE_SEQS), 1)
    selector = jnp.where((r >> (steps.bit_length() - 1)) == c, 1.0, 0.0).astype(F32)
    decay = jnp.exp2(lax.dot_general(g, selector, tn, precision=HIGHEST,
                                    preferred_element_type=F32))

    inter = [jnp.dot(qd[sl], s0_ref[b].astype(BF16), preferred_element_type=F32)
             for b, sl in enumerate(groups)]
    upd = [lax.dot_general(kdec[sl], v32[sl].astype(BF16), tn, preferred_element_type=F32)
           for sl in groups]
    for b in range(SAMPLE_SEQS):
        sout_ref[b] = decay[:, b:b + 1] * s0_ref[b] + upd[b]
    o = jnp.concatenate(inter, axis=0) + _diag(q, k, v32, bc, steps)
    o_ref[...] = _head_out(o, gn_ref[...], gate_ref[...])


def _proj_spec(rows, width, c0, per_head, row_block, tile_width=TN_PROJ, head_major=False):
    per_tile = tile_width // width

    def index(i0, i1):
        b, h = (i1, i0) if head_major else (i0, i1)
        g = c0 * LANES // width + h * per_head
        return (g // per_tile, row_block(b), g % per_tile)

    return pl.BlockSpec((None, rows, width), index)


def _prompt_scan(p, pm, *, gla, batch, seq, meta_block, small, name):
    heads, dv = (H_B, DV_B) if gla else (H_A, DV_A)
    n_chunks = seq // CHUNK
    blocks = CHUNK // DIAG_BLK
    stack = DIAG_BLK * blocks * (blocks + 1) // 2
    p16, pf, pa = p
    pm16, pmf, pma = pm
    tok = functools.partial(_proj_spec, seq, row_block=lambda b: b, head_major=True)
    met = functools.partial(_proj_spec, N_META, row_block=lambda b: meta_block, head_major=True)
    vec = lambda width: pl.BlockSpec((1, width), lambda h, b: (0, h))
    if gla:
        wa, ba, gn = small
        in_specs = [tok(DK, COL_QB, 1), tok(DK, COL_KB, 1), tok(dv, COL_VB, 1), tok(dv, COL_RB, 1),
                    tok(LANES, 0, 0, tile_width=LANES),
                    met(DK, COL_KB, 1), met(dv, COL_VB, 1), met(LANES, 0, 0, tile_width=LANES),
                    pl.BlockSpec((LANES, DK), lambda h, b: (0, h)), vec(DK), vec(dv)]
        args = (p16, p16, p16, p16, pa, pm16, pm16, pma, wa, ba, gn)
    else:
        lbl, gn = small
        in_specs = [tok(DK, COL_QA, 1), tok(DK, 0, 1), tok(dv, COL_IA, 1), tok(dv, COL_GA, 1),
                    met(DK, 0, 1), met(dv, COL_IA, 1),
                    pl.BlockSpec((2, DK), lambda h, b: (0, h)), vec(dv)]
        args = (p16, pf, p16, p16, pmf, pm16, lbl, gn)
    return pl.pallas_call(
        functools.partial(_prompt_scan_kernel, gla=gla, seq=seq, meta=N_META),
        grid=(heads, batch),
        in_specs=in_specs,
        out_specs=[pl.BlockSpec((seq, dv), lambda h, b: (b, h)),
                   pl.BlockSpec((None, None, DK, dv), lambda h, b: (b, h, 0, 0))],
        out_shape=[jax.ShapeDtypeStruct((batch * seq, heads * dv), BF16),
                   jax.ShapeDtypeStruct((batch, heads, DK, dv), F32)],
        scratch_shapes=[
            pltpu.VMEM((dv, DK), F32),
            pltpu.VMEM((seq, DK), F32),
            pltpu.VMEM((seq, DK), F32),
            pltpu.VMEM((seq, DK), F32),
            pltpu.VMEM((seq, DK), BF16),
            pltpu.VMEM((seq, DK), BF16),
            pltpu.VMEM((n_chunks * DK, stack), BF16),
            pltpu.VMEM((seq, DK), BF16),
            pltpu.VMEM((n_chunks, SUBLANES, DK), F32),
            pltpu.VMEM((dv, DK), F32),
        ],
        compiler_params=_params("parallel", "arbitrary"),
        name=name,
    )(*args)


def _sample_scan(p, s0, *, gla, batch, steps, small, name):
    assert steps == SUBLANES, "the sample recurrence keeps one sequence per sublane group"
    heads, dv = (H_B, DV_B) if gla else (H_A, DV_A)
    rows = SAMPLE_SEQS * steps
    p16, pf, pa = p
    tok = functools.partial(_proj_spec, rows, row_block=lambda b: b)
    vec = lambda width: pl.BlockSpec((1, width), lambda b, h: (0, h))
    state = pl.BlockSpec((SAMPLE_SEQS, None, DK, dv), lambda b, h: (b, h, 0, 0))
    if gla:
        wa, ba, gn = small
        in_specs = [tok(DK, COL_QB, 1), tok(DK, COL_KB, 1), tok(dv, COL_VB, 1), tok(dv, COL_RB, 1),
                    tok(LANES, 0, 0, tile_width=LANES),
                    pl.BlockSpec((LANES, DK), lambda b, h: (0, h)), vec(DK), vec(dv), state]
        args = (p16, p16, p16, p16, pa, wa, ba, gn, s0)
    else:
        lbl, gn = small
        in_specs = [tok(DK, COL_QA, 1), tok(DK, 0, 1), tok(dv, COL_IA, 1), tok(dv, COL_GA, 1),
                    pl.BlockSpec((2, DK), lambda b, h: (0, h)), vec(dv), state]
        args = (p16, pf, p16, p16, lbl, gn, s0)
    return pl.pallas_call(
        functools.partial(_sample_scan_kernel, gla=gla, steps=steps),
        grid=(batch // SAMPLE_SEQS, heads),
        in_specs=in_specs,
        out_specs=[pl.BlockSpec((rows, dv), lambda b, h: (b, h)), state],
        out_shape=[jax.ShapeDtypeStruct((batch * steps, heads * dv), BF16),
                   jax.ShapeDtypeStruct((batch, heads, DK, dv), F32)],
        compiler_params=_params("parallel", "parallel"),
        name=name,
    )(*args)


def kernel(x_prompt, x_sample, state_hgrn, state_gla, meta_tokens, lb_logits, ffn1_norm, w_ffn1_in,
           w_ffn1_out, mix_norm, w_in, w_alpha_up, b_alpha, gnorm_a, gnorm_b, w_out, ffn2_norm,
           w_ffn2_in, w_ffn2_out, final_norm):
    batch, seq, _ = x_prompt.shape
    dec_batch, steps, _ = x_sample.shape

    w_in_t = w_in[0].T
    wop = w_out[0]
    wa = jnp.pad(w_alpha_up[0], ((0, LANES - ALPHA_RANK), (0, 0))).astype(BF16)
    n1, nm, n2, nf = ffn1_norm, mix_norm, ffn2_norm, final_norm[None]
    small_a = (lb_logits, gnorm_a)
    small_b = (wa, b_alpha, gnorm_b)
    n_sample = dec_batch * steps

    xs = (x_sample.reshape(n_sample, D_MODEL), meta_tokens)
    n_stream = n_sample + N_META
    h1s, u1s, *w1 = _ffn(xs, n1, (w_ffn1_in[0], w_ffn1_in[0], w_ffn1_out[0]), nm, tm=n_stream,
                         tf=TF_CAST, final_norm=False, emit_weights=True, name="ffn1_sample")
    ps = _in_proj(u1s, w_in_t, tm=n_stream, name="inproj_sample")
    oa_s, sa_s = _sample_scan(ps, state_hgrn[0], gla=False, batch=dec_batch, steps=steps,
                              small=small_a, name="scan_hgrn_sample")
    ob_s, sb_s = _sample_scan(ps, state_gla[0], gla=True, batch=dec_batch, steps=steps,
                              small=small_b, name="scan_gla_sample")
    h2s = _out_proj(oa_s, ob_s, h1s, wop, tm=TM, name="outproj_sample")
    y_s, *w2 = _ffn(h2s, n2, (w_ffn2_in[0], w_ffn2_in[0], w_ffn2_out[0]), nf, tm=n_sample,
                    tf=TF_CAST, final_norm=True, emit_weights=True, name="ffn2_sample")

    h1p, u1p = _ffn(x_prompt.reshape(batch * seq, D_MODEL), n1, w1, nm, tm=TM_FFN, tf=TF_WIDE,
                    final_norm=False, emit_weights=False, prefetch_h=True, name="ffn1_prompt")
    pp = _in_proj(u1p, w_in_t, tm=TM_PROJ, name="inproj_prompt")
    meta_block = n_sample // N_META
    oa_p, sa_p = _prompt_scan(pp, ps, gla=False, batch=batch, seq=seq, meta_block=meta_block,
                              small=small_a, name="scan_hgrn_prompt")
    ob_p, sb_p = _prompt_scan(pp, ps, gla=True, batch=batch, seq=seq, meta_block=meta_block,
                              small=small_b, name="scan_gla_prompt")
    h2p = _out_proj(oa_p, ob_p, h1p, wop, tm=TM, name="outproj_prompt")
    (y_p,) = _ffn(h2p, n2, w2, nf, tm=TM_FFN, tf=TF_WIDE, final_norm=True, emit_weights=False,
                  prefetch_h=True, name="ffn2_prompt")

    return (y_p.reshape(batch, seq, D_MODEL), y_s.reshape(dec_batch, steps, D_MODEL),
            sa_p[None], sb_p[None], sa_s[None], sb_s[None])
```

```python
import functools

import jax
import jax.numpy as jnp
from jax import lax
from jax.experimental import pallas as pl
from jax.experimental.pallas import tpu as pltpu

F32 = jnp.float32
BF16 = jnp.bfloat16
HIGHEST = lax.Precision.HIGHEST

D_MODEL = 2048
N_META = 16
DK = 128
H_A, DV_A = 8, 128
H_B, DV_B = 4, 256
D_HALF = H_A * DV_A
ALPHA_RANK = 16
GATE_TEMP = 16.0
D_FF = 5632
CHUNK = 64
EPS = 1e-6
IN_PROJ_WIDTHS = (D_HALF, D_HALF, D_HALF, D_HALF, H_B * DK, H_B * DK, D_HALF, D_HALF, ALPHA_RANK)

LANES = 128
SUBLANES = 8
VMEM_LIMIT_BYTES = 56 * 1024 * 1024

COL_QA, COL_IA, COL_GA, COL_QB, COL_KB, COL_VB, COL_RB = 0, 8, 16, 24, 28, 32, 40
P16_COLS = 3 * D_HALF + 2 * H_B * DK + 2 * D_HALF

TM = 512
TM_FFN = 1024
TF = 256
TF_WIDE = 512
TF_CAST = 256
TN_PROJ = 512
TM_PROJ = 2048
N16_TILES = P16_COLS // TN_PROJ
DIAG_BLK = 16
SAMPLE_SEQS = 64
CHUNKS_PER_TRIP = 4
CHUNKS_PER_MXU_TRIP = 16

LOG2_E = 1.4426950408889634

FAST_MIN_LOG_GATE = -60.0 * LOG2_E / (DIAG_BLK - 1)
FAST_MAX_KEY = 1e10


def _rms(x, w):
    return x * lax.rsqrt(jnp.mean(x * x, axis=-1, keepdims=True) + EPS) * w


def _silu(x):
    return x * jax.nn.sigmoid(x)


def _params(*sem):
    return pltpu.CompilerParams(dimension_semantics=sem, vmem_limit_bytes=VMEM_LIMIT_BYTES)


def _ffn_kernel(*refs, n_pieces, final_norm, emit_weights, prefetch_h):
    h_refs = refs[:n_pieces]
    nw_ref, wg_ref, wu_ref, wo_ref, tnw_ref, out_ref, *rest = refs[n_pieces:]
    rest = list(rest)
    un_ref = None if final_norm else rest.pop(0)
    w16_refs = [rest.pop(0) for _ in range(3)] if emit_weights else []
    u_ref = rest.pop(0)
    i, j = pl.program_id(0), pl.program_id(1)
    last_j = pl.num_programs(1) - 1
    if prefetch_h:
        h_buf, h_sem = rest

        def h_copy(tile):
            rows = h_buf.shape[0]
            src = h_refs[0].at[pl.ds(pl.multiple_of(tile * rows, rows), rows), :]
            return pltpu.make_async_copy(src, h_buf, h_sem)

    def load_h():
        pieces = [r[...] for r in h_refs]
        return pieces[0] if n_pieces == 1 else jnp.concatenate(pieces, axis=0)

    def step(first, last):
        if first:
            if prefetch_h:
                pl.when(i == 0)(lambda: h_copy(0).start())
                h_copy(i).wait()
                h = h_buf[...]
            else:
                h = load_h()
            u = _rms(h, nw_ref[...]).astype(BF16)
            u_ref[...] = u
        else:
            u = u_ref[...]
        wg, wu, wo = (r[...].astype(BF16) for r in (wg_ref, wu_ref, wo_ref))
        for dst, w in zip(w16_refs, (wg, wu, wo)):
            dst[...] = w
        g = jnp.dot(u, wg, preferred_element_type=F32)
        up = jnp.dot(u, wu, preferred_element_type=F32)
        a = (_silu(g) * up).astype(BF16)
        acc = jnp.dot(a, wo, preferred_element_type=F32)
        if not first:
            acc = out_ref[...] + acc
        elif prefetch_h:
            acc = 2.0 * h + acc
        if not last:
            out_ref[...] = acc
            if first and prefetch_h:
                pl.when(i + 1 < pl.num_programs(0))(lambda: h_copy(i + 1).start())
            return
        y = 0.5 * acc if prefetch_h else load_h() + 0.5 * acc
        normed = _rms(y, tnw_ref[...])
        if final_norm:
            out_ref[...] = normed
        else:
            out_ref[...] = y
            un_ref[...] = normed.astype(BF16)

    pl.when(j == 0)(functools.partial(step, True, False))
    pl.when((j > 0) & (j < last_j))(functools.partial(step, False, False))
    pl.when(j == last_j)(functools.partial(step, False, True))


def _ffn(h, norm_w, weights, tail_norm_w, *, tm, tf, final_norm, emit_weights, name,
         prefetch_h=False):
    pieces = h if isinstance(h, tuple) else (h,)
    m = sum(p.shape[0] for p in pieces)
    nf = D_FF // tf
    rows = pl.BlockSpec((tm, D_MODEL), lambda i, j: (i, 0),
                        pipeline_mode=pl.Buffered(1) if m == tm else None)
    scratch = [pltpu.VMEM((tm, D_MODEL), BF16)]
    if prefetch_h:
        assert len(pieces) == 1 and m > tm
        h_specs = [pl.BlockSpec(memory_space=pl.ANY)]
        scratch += [pltpu.VMEM((tm, D_MODEL), F32), pltpu.SemaphoreType.DMA]
    elif len(pieces) == 1:
        h_specs = [rows]
    else:
        assert m == tm, "stacked inputs form one row tile"
        h_specs = [pl.BlockSpec((p.shape[0], D_MODEL), lambda i, j: (0, 0),
                                pipeline_mode=pl.Buffered(1)) for p in pieces]
    vec = pl.BlockSpec((1, D_MODEL), lambda i, j: (0, 0))
    col_tile = pl.BlockSpec((D_MODEL, tf), lambda i, j: (0, j))
    row_tile = pl.BlockSpec((tf, D_MODEL), lambda i, j: (j, 0))
    out_specs = [rows]
    out_shape = [jax.ShapeDtypeStruct((m, D_MODEL), F32)]
    if not final_norm:
        out_specs += [rows]
        out_shape += [jax.ShapeDtypeStruct((m, D_MODEL), BF16)]
    if emit_weights:
        assert m == tm, "weight casts are written once, by a single row tile"
        up_tile = pl.BlockSpec((D_MODEL, tf), lambda i, j: (0, j + nf))
        out_specs += [col_tile, col_tile, row_tile]
        out_shape += [jax.ShapeDtypeStruct((D_MODEL, D_FF), BF16)] * 2
        out_shape += [jax.ShapeDtypeStruct((D_FF, D_MODEL), BF16)]
    else:
        up_tile = col_tile
    return pl.pallas_call(
        functools.partial(_ffn_kernel, n_pieces=len(pieces), final_norm=final_norm,
                          emit_weights=emit_weights, prefetch_h=prefetch_h),
        grid=(m // tm, nf),
        in_specs=h_specs + [vec, col_tile, up_tile, row_tile, vec],
        out_specs=out_specs,
        out_shape=out_shape,
        scratch_shapes=scratch,
        compiler_params=_params("arbitrary" if prefetch_h else "parallel", "arbitrary"),
        name=name,
    )(*pieces, norm_w, *weights, tail_norm_w)


def _proj_kernel(u_ref, w_ref, wal_ref, p16_ref, pf_ref, pa_ref):
    j = pl.program_id(1)
    nt = (((1,), (1,)), ((), ()))

    @pl.when(j == 0)
    def _low_rank():
        wal = jnp.concatenate([wal_ref[...].astype(BF16),
                               jnp.zeros((LANES - ALPHA_RANK, D_MODEL), BF16)], axis=0)
        pa_ref[...] = lax.dot_general(u_ref[...], wal, nt, preferred_element_type=F32)

    def project():
        return lax.dot_general(u_ref[...], w_ref[...].astype(BF16), nt,
                               preferred_element_type=F32)

    @pl.when(j < N16_TILES)
    def _narrow():
        p16_ref[...] = project().astype(BF16)

    @pl.when(j >= N16_TILES)
    def _wide():
        pf_ref[...] = project()


def _w_in_tile(j):
    fa_lo, fa_hi = D_HALF // TN_PROJ, 2 * D_HALF // TN_PROJ
    moved = fa_hi - fa_lo
    return jnp.where(j < fa_lo, j, jnp.where(j < N16_TILES, j + moved, j - N16_TILES + fa_lo))


def _in_proj(u, w, *, tm, name):
    m = u.shape[0]
    return pl.pallas_call(
        _proj_kernel,
        grid=(m // tm, (P16_COLS + D_HALF) // TN_PROJ),
        in_specs=[
            pl.BlockSpec((tm, D_MODEL), lambda i, j: (i, 0)),
            pl.BlockSpec((TN_PROJ, D_MODEL), lambda i, j: (_w_in_tile(j), 0)),
            pl.BlockSpec((ALPHA_RANK, D_MODEL), lambda i, j: ((P16_COLS + D_HALF) // ALPHA_RANK, 0)),
        ],
        out_specs=[
            pl.BlockSpec((None, tm, TN_PROJ), lambda i, j: (jnp.minimum(j, N16_TILES - 1), i, 0)),
            pl.BlockSpec((None, tm, TN_PROJ), lambda i, j: (jnp.maximum(j - N16_TILES, 0), i, 0)),
            pl.BlockSpec((None, tm, LANES), lambda i, j: (0, i, 0)),
        ],
        out_shape=[jax.ShapeDtypeStruct((N16_TILES, m, TN_PROJ), BF16),
                   jax.ShapeDtypeStruct((D_HALF // TN_PROJ, m, TN_PROJ), F32),
                   jax.ShapeDtypeStruct((1, m, LANES), F32)],
        compiler_params=_params("parallel", "arbitrary"),
        name=name,
    )(u, w, w)


def _out_proj_kernel(oa_ref, ob_ref, h_ref, wa_ref, wb_ref, out_ref):
    out_ref[...] = (h_ref[...]
                    + jnp.dot(oa_ref[...], wa_ref[...].astype(BF16), preferred_element_type=F32)
                    + jnp.dot(ob_ref[...], wb_ref[...].astype(BF16), preferred_element_type=F32))


def _out_proj(oa, ob, h, w, *, tm, name):
    m = oa.shape[0]
    return pl.pallas_call(
        _out_proj_kernel,
        grid=(m // tm,),
        in_specs=[
            pl.BlockSpec((tm, D_HALF), lambda i: (i, 0)),
            pl.BlockSpec((tm, D_HALF), lambda i: (i, 0)),
            pl.BlockSpec((tm, D_MODEL), lambda i: (i, 0)),
            pl.BlockSpec((D_HALF, D_MODEL), lambda i: (0, 0), pipeline_mode=pl.Buffered(1)),
            pl.BlockSpec((D_HALF, D_MODEL), lambda i: (1, 0), pipeline_mode=pl.Buffered(1)),
        ],
        out_specs=pl.BlockSpec((tm, D_MODEL), lambda i: (i, 0)),
        out_shape=jax.ShapeDtypeStruct((m, D_MODEL), F32),
        compiler_params=_params("parallel"),
        name=name,
    )(oa, ob, h, w, w)


def _tri(c, seq_len):
    r = lax.broadcasted_iota(jnp.int32, (c, c), 0)
    s = lax.broadcasted_iota(jnp.int32, (c, c), 1)
    return jnp.where((r >= s) & ((r ^ s) < seq_len), 1.0, 0.0).astype(F32)


def _off_mask(c, blk):
    nb = c // blk
    rows = nb * (nb - 1) // 2 * blk
    r = lax.broadcasted_iota(jnp.int32, (c, rows), 0) >> (blk.bit_length() - 1)
    s = lax.broadcasted_iota(jnp.int32, (c, rows), 1)
    seg = jnp.zeros((c, rows), jnp.int32)
    for i in range(1, nb):
        seg = seg + jnp.where(s >= blk * i * (i - 1) // 2, 1, 0)
    return jnp.where(r == seg, 1.0, 0.0).astype(F32)


def _diag(q, k, v, bc, blk):
    c = q.shape[0]
    row = lax.broadcasted_iota(jnp.int32, (blk, 1), 0)
    outs = []
    for i in range(c // blk):
        sl = slice(i * blk, (i + 1) * blk)
        qi, ki, vi, bi = q[sl], k[sl], v[sl].astype(F32), bc[sl]
        acc = jnp.zeros((blk, v.shape[1]), F32)
        for s in range(blk):
            e = jnp.exp2(jnp.minimum(bi - bi[s:s + 1, :], 0.0))
            col = jnp.sum(qi * ki[s:s + 1, :] * e, axis=-1, keepdims=True)
            col = jnp.where(row >= s, col, 0.0)
            acc = acc + col * vi[s:s + 1, :]
        outs.append(acc)
    return outs[0] if len(outs) == 1 else jnp.concatenate(outs, axis=0)


def _offdiag(q, k, v, bc, blk, mask):
    c = q.shape[0]
    nb = c // blk
    qs, ks, vs = [jnp.zeros((blk, DK), F32)], [], []
    for i in range(1, nb):
        lo = i * blk
        ref = bc[lo:lo + 1, :]
        qs.append(q[lo:lo + blk] * jnp.exp2(bc[lo:lo + blk] - ref))
        ks.append(k[:lo] * jnp.exp2(ref - bc[:lo]))
        vs.append(v[:lo])
    qt = jnp.concatenate(qs, axis=0).astype(BF16)
    kt = jnp.concatenate(ks, axis=0).astype(BF16)
    vt = jnp.concatenate(vs, axis=0).astype(BF16)
    a = lax.dot_general(qt, kt, (((1,), (1,)), ((), ())), preferred_element_type=F32)
    return jnp.dot((a * mask).astype(BF16), vt, preferred_element_type=F32)


def _hgrn_inputs(qa, fa, ia, lb):
    fg = lb + (1.0 - lb) * jax.nn.sigmoid(fa)
    return _silu(qa.astype(F32)) * (DK ** -0.5), 1.0 - fg, ia, jnp.log2(fg)


def _gla_inputs(qb, kb, vb, al, wa, ba):
    alpha = jnp.dot(al.astype(BF16), wa, preferred_element_type=F32) + ba
    logsig = jnp.minimum(alpha, 0.0) - jnp.log1p(jnp.exp(-jnp.abs(alpha)))
    return qb.astype(F32) * (DK ** -0.5), kb.astype(F32), vb, logsig * (LOG2_E / GATE_TEMP)


def _head_out(o, gn, gate):
    return (_rms(o, gn) * _silu(gate.astype(F32))).astype(BF16)


def _lower_bound(lbl_ref):
    x = lbl_ref[...]
    e = jnp.exp(x - jnp.max(x, axis=0, keepdims=True))
    return e[0:1, :] / jnp.sum(e, axis=0, keepdims=True)


def _stack_mask(c, blk):
    nb = c // blk
    rows = blk * nb * (nb + 1) // 2
    t = lax.broadcasted_iota(jnp.int32, (c, rows), 0)
    r = lax.broadcasted_iota(jnp.int32, (c, rows), 1)
    seg = jnp.zeros((c, rows), jnp.int32)
    off = jnp.zeros((c, rows), jnp.int32)
    for i in range(1, nb):
        start = blk * i * (i + 1) // 2
        seg = seg + jnp.where(r >= start, 1, 0)
        off = jnp.where(r >= start, start, off)
    keep = ((t >> (blk.bit_length() - 1)) == seg) & (r - off <= t)
    return jnp.where(keep, 1.0, 0.0).astype(F32)


def _state_update_t(k, vb, bc, st):
    c = k.shape[0]
    blast = bc[c - 1:c, :]
    kdec = (k * jnp.exp2(blast - bc)).astype(BF16)
    upd = lax.dot_general(vb, kdec, (((0,), (0,)), ((), ())), preferred_element_type=F32)
    if st is None:
        return upd
    return st * jnp.exp2(blast) + upd


def _scan_sublanes(p):
    row = lax.broadcasted_iota(jnp.int32, (SUBLANES, DK), 0)
    for s in (1, 2, 4):
        p = p + jnp.where(row >= s, pltpu.roll(p, s, axis=0), 0.0)
    return p


def _cumsum_chunk(g):
    out, carry = [], None
    for j in range(CHUNK // SUBLANES):
        p = _scan_sublanes(g[j * SUBLANES:(j + 1) * SUBLANES])
        if carry is not None:
            p = p + carry
        carry = p[SUBLANES - 1:SUBLANES, :]
        out.append(p)
    return jnp.concatenate(out, axis=0)


def _rescaled_operands(q, k, bc):
    c, blk = CHUNK, DIAG_BLK
    qs, ks = [], []
    for i in range(c // blk):
        lo, hi = i * blk, (i + 1) * blk
        ref = bc[lo:lo + 1, :]
        qs.append(q[lo:hi] * jnp.exp2(bc[lo:hi] - ref))
        ks.append(k[:hi] * jnp.exp2(ref - bc[:hi]))
    return ((q * jnp.exp2(bc)).astype(BF16), jnp.concatenate(qs, axis=0).astype(BF16),
            jnp.concatenate(ks, axis=0).T.astype(BF16),
            (k * jnp.exp2(bc[c - 1:c, :] - bc)).astype(BF16))


def _chunk_exact(q, k, v, bc, st, mask):
    o = lax.dot_general((q * jnp.exp2(bc)).astype(BF16), st.astype(BF16),
                        (((1,), (1,)), ((), ())), preferred_element_type=F32)
    o = o + _offdiag(q, k, v, bc, DIAG_BLK, mask) + _diag(q, k, v, bc, DIAG_BLK)
    return o, _state_update_t(k, v.astype(BF16), bc, st)


def _prompt_scan_kernel(*refs, gla, seq, meta):
    n_in = 11 if gla else 8
    if gla:
        q_ref, k_ref, v_ref, gate_ref, al_ref, mk_ref, mv_ref, mal_ref, wa_ref, ba_ref, gn_ref = (
            refs[:n_in])
    else:
        q_ref, f_ref, v_ref, gate_ref, mf_ref, mv_ref, lbl_ref, gn_ref = refs[:n_in]
        lb = _lower_bound(lbl_ref)
    o_ref, sout_ref = refs[n_in:n_in + 2]
    (st_ref, qs_ref, ks_ref, gs_ref, qd_ref, qt_ref, kt_ref, kdec_ref, eb_ref,
     meta_st_ref) = refs[n_in + 2:]
    trips = seq // (CHUNK * CHUNKS_PER_TRIP)
    nt = (((1,), (1,)), ((), ()))
    tn = (((0,), (0,)), ((), ()))

    def chunk_rows(ci, u, n=CHUNK):
        return pl.ds(pl.multiple_of((ci * CHUNKS_PER_TRIP + u) * n, n), n)

    @pl.when(pl.program_id(1) == 0)
    def _meta():
        if gla:
            _, mk, mv, mg = _gla_inputs(mk_ref[...], mk_ref[...], mv_ref[...], mal_ref[...],
                                        wa_ref[...], ba_ref[...])
        else:
            _, mk, mv, mg = _hgrn_inputs(mf_ref[...], mf_ref[...], mv_ref[...], lb)
        mbc = jnp.dot(_tri(meta, meta), mg, precision=HIGHEST, preferred_element_type=F32)
        meta_st_ref[...] = _state_update_t(mk, mv.astype(BF16), mbc, None)

    st_ref[...] = meta_st_ref[...]

    def rows_of(c, n=CHUNK):
        return pl.ds(pl.multiple_of(c * n, n), n)

    def rescaled_pass():
        keep = _stack_mask(CHUNK, DIAG_BLK) != 0.0
        units = range(CHUNKS_PER_MXU_TRIP)
        mxu_trips = seq // (CHUNK * CHUNKS_PER_MXU_TRIP)

        def prepare(ti, extremes):
            gmin, kmax = extremes
            trip_rows = CHUNK * CHUNKS_PER_MXU_TRIP
            rows = pl.ds(pl.multiple_of(ti * trip_rows, trip_rows), trip_rows)
            if gla:
                q, k, _, g = _gla_inputs(q_ref[rows, :], k_ref[rows, :], None, al_ref[rows, :],
                                         wa_ref[...], ba_ref[...])
            else:
                q, k, _, g = _hgrn_inputs(q_ref[rows, :], f_ref[rows, :], None, lb)
            qs_ref[rows, :] = q
            ks_ref[rows, :] = k
            gs_ref[rows, :] = g
            for u in units:
                c = ti * CHUNKS_PER_MXU_TRIP + u
                sl = slice(u * CHUNK, (u + 1) * CHUNK)
                bc = _cumsum_chunk(g[sl])
                qd, qt, kt, kdec = _rescaled_operands(q[sl], k[sl], bc)
                qd_ref[rows_of(c), :] = qd
                qt_ref[rows_of(c), :] = qt
                kt_ref[rows_of(c, DK), :] = kt
                kdec_ref[rows_of(c), :] = kdec
                eb_ref[c] = jnp.broadcast_to(jnp.exp2(bc[CHUNK - 1:CHUNK, :]), (SUBLANES, DK))
            return (jnp.minimum(gmin, jnp.min(g, axis=0, keepdims=True)),
                    jnp.maximum(kmax, jnp.max(jnp.abs(k), axis=0, keepdims=True)))

        def contract(ti):
            cs = [ti * CHUNKS_PER_MXU_TRIP + u for u in units]
            vb = [v_ref[rows_of(c), :].astype(BF16) for c in cs]
            a = [jnp.dot(qt_ref[rows_of(c), :], kt_ref[rows_of(c, DK), :],
                         preferred_element_type=F32) for c in cs]
            am = [jnp.where(keep, x, 0.0).astype(BF16) for x in a]
            vt = [jnp.concatenate([x[:(i + 1) * DIAG_BLK] for i in range(CHUNK // DIAG_BLK)], axis=0)
                  for x in vb]
            oi = [jnp.dot(am[u], vt[u], preferred_element_type=F32) for u in units]
            inc = [lax.dot_general(vb[u], kdec_ref[rows_of(cs[u]), :], tn,
                                   preferred_element_type=F32) for u in units]
            sts = [st_ref[...]]
            for u in units:
                sts.append(sts[-1] * eb_ref[cs[u]][0:1, :] + inc[u])
            o = [oi[u] + lax.dot_general(qd_ref[rows_of(cs[u]), :], sts[u].astype(BF16), nt,
                                         preferred_element_type=F32) for u in units]
            for u in units:
                o_ref[rows_of(cs[u]), :] = _head_out(o[u], gn_ref[...], gate_ref[rows_of(cs[u]), :])
            st_ref[...] = sts[-1]

        extremes = prepare(0, (jnp.zeros((1, DK), F32), jnp.zeros((1, DK), F32)))

        def body(ti, extremes):
            contract(ti)
            return prepare(ti + 1, extremes)

        extremes = lax.fori_loop(0, mxu_trips - 1, body, extremes)
        contract(mxu_trips - 1)
        return extremes

    gmin, kmax = rescaled_pass()
    in_range = (jnp.min(gmin) >= FAST_MIN_LOG_GATE) & (jnp.max(kmax) <= FAST_MAX_KEY)

    @pl.when(jnp.logical_not(in_range))
    def _exact():
        mask = _off_mask(CHUNK, DIAG_BLK)
        st_ref[...] = meta_st_ref[...]

        def body(ci, carry):
            st = st_ref[...]
            for u in range(CHUNKS_PER_TRIP):
                rows = chunk_rows(ci, u)
                o, st = _chunk_exact(qs_ref[rows, :], ks_ref[rows, :], v_ref[rows, :],
                                     _cumsum_chunk(gs_ref[rows, :]), st, mask)
                o_ref[rows, :] = _head_out(o, gn_ref[...], gate_ref[rows, :])
            st_ref[...] = st
            return carry

        lax.fori_loop(0, trips, body, 0)

    sout_ref[...] = st_ref[...].T


def _sample_scan_kernel(*refs, gla, steps):
    if gla:
        (q_ref, k_ref, v_ref, gate_ref, al_ref, wa_ref, ba_ref, gn_ref, s0_ref,
         o_ref, sout_ref) = refs
        q, k, v, g = _gla_inputs(q_ref[...], k_ref[...], v_ref[...], al_ref[...],
                                 wa_ref[...], ba_ref[...])
    else:
        (q_ref, f_ref, v_ref, gate_ref, lbl_ref, gn_ref, s0_ref, o_ref, sout_ref) = refs
        q, k, v, g = _hgrn_inputs(q_ref[...], f_ref[...], v_ref[...], _lower_bound(lbl_ref))
    v32 = v.astype(F32)
    rows = SAMPLE_SEQS * steps
    groups = [slice(b * steps, (b + 1) * steps) for b in range(SAMPLE_SEQS)]
    tn = (((0,), (0,)), ((), ()))

    bc = jnp.concatenate([_scan_sublanes(g[sl]) for sl in groups], axis=0)

    def repeated(r):
        return jnp.concatenate([jnp.broadcast_to(bc[sl][r:r + 1, :], (steps, DK)) for sl in groups],
                               axis=0)

    qd = (q * jnp.exp2(bc)).astype(BF16)
    kdec = (k * jnp.exp2(repeated(steps - 1) - bc)).astype(BF16)
    r = lax.broadcasted_iota(jnp.int32, (rows, SAMPLE_SEQS), 0)
    c = lax.broadcasted_iota(jnp.int32, (rows, SAMPLE_SEQS), 1)
    selector = jnp.where((r >> (steps.bit_length() - 1)) == c, 1.0, 0.0).astype(F32)
    decay = jnp.exp2(lax.dot_general(g, selector, tn, precision=HIGHEST,
                                    preferred_element_type=F32))

    inter = [jnp.dot(qd[sl], s0_ref[b].astype(BF16), preferred_element_type=F32)
             for b, sl in enumerate(groups)]
    upd = [lax.dot_general(kdec[sl], v32[sl].astype(BF16), tn, preferred_element_type=F32)
           for sl in groups]
    for b in range(SAMPLE_SEQS):
        sout_ref[b] = decay[:, b:b + 1] * s0_ref[b] + upd[b]
    o = jnp.concatenate(inter, axis=0) + _diag(q, k, v32, bc, steps)
    o_ref[...] = _head_out(o, gn_ref[...], gate_ref[...])


def _proj_spec(rows, width, c0, per_head, row_block, tile_width=TN_PROJ, head_major=False):
    per_tile = tile_width // width

    def index(i0, i1):
        b, h = (i1, i0) if head_major else (i0, i1)
        g = c0 * LANES // width + h * per_head
        return (g // per_tile, row_block(b), g % per_tile)

    return pl.BlockSpec((None, rows, width), index)


def _prompt_scan(p, pm, *, gla, batch, seq, meta_block, small, name):
    heads, dv = (H_B, DV_B) if gla else (H_A, DV_A)
    n_chunks = seq // CHUNK
    blocks = CHUNK // DIAG_BLK
    stack = DIAG_BLK * blocks * (blocks + 1) // 2
    p16, pf, pa = p
    pm16, pmf, pma = pm
    tok = functools.partial(_proj_spec, seq, row_block=lambda b: b, head_major=True)
    met = functools.partial(_proj_spec, N_META, row_block=lambda b: meta_block, head_major=True)
    vec = lambda width: pl.BlockSpec((1, width), lambda h, b: (0, h))
    if gla:
        wa, ba, gn = small
        in_specs = [tok(DK, COL_QB, 1), tok(DK, COL_KB, 1), tok(dv, COL_VB, 1), tok(dv, COL_RB, 1),
                    tok(LANES, 0, 0, tile_width=LANES),
                    met(DK, COL_KB, 1), met(dv, COL_VB, 1), met(LANES, 0, 0, tile_width=LANES),
                    pl.BlockSpec((LANES, DK), lambda h, b: (0, h)), vec(DK), vec(dv)]
        args = (p16, p16, p16, p16, pa, pm16, pm16, pma, wa, ba, gn)
    else:
        lbl, gn = small
        in_specs = [tok(DK, COL_QA, 1), tok(DK, 0, 1), tok(dv, COL_IA, 1), tok(dv, COL_GA, 1),
                    met(DK, 0, 1), met(dv, COL_IA, 1),
                    pl.BlockSpec((2, DK), lambda h, b: (0, h)), vec(dv)]
        args = (p16, pf, p16, p16, pmf, pm16, lbl, gn)
    return pl.pallas_call(
        functools.partial(_prompt_scan_kernel, gla=gla, seq=seq, meta=N_META),
        grid=(heads, batch),
        in_specs=in_specs,
        out_specs=[pl.BlockSpec((seq, dv), lambda h, b: (b, h)),
                   pl.BlockSpec((None, None, DK, dv), lambda h, b: (b, h, 0, 0))],
        out_shape=[jax.ShapeDtypeStruct((batch * seq, heads * dv), BF16),
                   jax.ShapeDtypeStruct((batch, heads, DK, dv), F32)],
        scratch_shapes=[
            pltpu.VMEM((dv, DK), F32),
            pltpu.VMEM((seq, DK), F32),
            pltpu.VMEM((seq, DK), F32),
            pltpu.VMEM((seq, DK), F32),
            pltpu.VMEM((seq, DK), BF16),
            pltpu.VMEM((seq, DK), BF16),
            pltpu.VMEM((n_chunks * DK, stack), BF16),
            pltpu.VMEM((seq, DK), BF16),
            pltpu.VMEM((n_chunks, SUBLANES, DK), F32),
            pltpu.VMEM((dv, DK), F32),
        ],
        compiler_params=_params("parallel", "arbitrary"),
        name=name,
    )(*args)


def _sample_scan(p, s0, *, gla, batch, steps, small, name):
    assert steps == SUBLANES, "the sample recurrence keeps one sequence per sublane group"
    heads, dv = (H_B, DV_B) if gla else (H_A, DV_A)
    rows = SAMPLE_SEQS * steps
    p16, pf, pa = p
    tok = functools.partial(_proj_spec, rows, row_block=lambda b: b)
    vec = lambda width: pl.BlockSpec((1, width), lambda b, h: (0, h))
    state = pl.BlockSpec((SAMPLE_SEQS, None, DK, dv), lambda b, h: (b, h, 0, 0))
    if gla:
        wa, ba, gn = small
        in_specs = [tok(DK, COL_QB, 1), tok(DK, COL_KB, 1), tok(dv, COL_VB, 1), tok(dv, COL_RB, 1),
                    tok(LANES, 0, 0, tile_width=LANES),
                    pl.BlockSpec((LANES, DK), lambda b, h: (0, h)), vec(DK), vec(dv), state]
        args = (p16, p16, p16, p16, pa, wa, ba, gn, s0)
    else:
        lbl, gn = small
        in_specs = [tok(DK, COL_QA, 1), tok(DK, 0, 1), tok(dv, COL_IA, 1), tok(dv, COL_GA, 1),
                    pl.BlockSpec((2, DK), lambda b, h: (0, h)), vec(dv), state]
        args = (p16, pf, p16, p16, lbl, gn, s0)
    return pl.pallas_call(
        functools.partial(_sample_scan_kernel, gla=gla, steps=steps),
        grid=(batch // SAMPLE_SEQS, heads),
        in_specs=in_specs,
        out_specs=[pl.BlockSpec((rows, dv), lambda b, h: (b, h)), state],
        out_shape=[jax.ShapeDtypeStruct((batch * steps, heads * dv), BF16),
                   jax.ShapeDtypeStruct((batch, heads, DK, dv), F32)],
        compiler_params=_params("parallel", "parallel"),
        name=name,
    )(*args)


def kernel(x_prompt, x_sample, state_hgrn, state_gla, meta_tokens, lb_logits, ffn1_norm, w_ffn1_in,
           w_ffn1_out, mix_norm, w_in, w_alpha_up, b_alpha, gnorm_a, gnorm_b, w_out, ffn2_norm,
           w_ffn2_in, w_ffn2_out, final_norm):
    batch, seq, _ = x_prompt.shape
    dec_batch, steps, _ = x_sample.shape

    w_in_t = w_in[0].T
    wop = w_out[0]
    wa = jnp.pad(w_alpha_up[0], ((0, LANES - ALPHA_RANK), (0, 0))).astype(BF16)
    n1, nm, n2, nf = ffn1_norm, mix_norm, ffn2_norm, final_norm[None]
    small_a = (lb_logits, gnorm_a)
    small_b = (wa, b_alpha, gnorm_b)
    n_sample = dec_batch * steps

    xs = (x_sample.reshape(n_sample, D_MODEL), meta_tokens)
    n_stream = n_sample + N_META
    h1s, u1s, *w1 = _ffn(xs, n1, (w_ffn1_in[0], w_ffn1_in[0], w_ffn1_out[0]), nm, tm=n_stream,
                         tf=TF_CAST, final_norm=False, emit_weights=True, name="ffn1_sample")
    ps = _in_proj(u1s, w_in_t, tm=n_stream, name="inproj_sample")
    oa_s, sa_s = _sample_scan(ps, state_hgrn[0], gla=False, batch=dec_batch, steps=steps,
                              small=small_a, name="scan_hgrn_sample")
    ob_s, sb_s = _sample_scan(ps, state_gla[0], gla=True, batch=dec_batch, steps=steps,
                              small=small_b, name="scan_gla_sample")
    h2s = _out_proj(oa_s, ob_s, h1s, wop, tm=TM, name="outproj_sample")
    y_s, *w2 = _ffn(h2s, n2, (w_ffn2_in[0], w_ffn2_in[0], w_ffn2_out[0]), nf, tm=n_sample,
                    tf=TF_CAST, final_norm=True, emit_weights=True, name="ffn2_sample")

    h1p, u1p = _ffn(x_prompt.reshape(batch * seq, D_MODEL), n1, w1, nm, tm=TM_FFN, tf=TF_WIDE,
                    final_norm=False, emit_weights=False, prefetch_h=True, name="ffn1_prompt")
    pp = _in_proj(u1p, w_in_t, tm=TM_PROJ, name="inproj_prompt")
    meta_block = n_sample // N_META
    oa_p, sa_p = _prompt_scan(pp, ps, gla=False, batch=batch, seq=seq, meta_block=meta_block,
                              small=small_a, name="scan_hgrn_prompt")
    ob_p, sb_p = _prompt_scan(pp, ps, gla=True, batch=batch, seq=seq, meta_block=meta_block,
                              small=small_b, name="scan_gla_prompt")
    h2p = _out_proj(oa_p, ob_p, h1p, wop, tm=TM, name="outproj_prompt")
    (y_p,) = _ffn(h2p, n2, w2, nf, tm=TM_FFN, tf=TF_WIDE, final_norm=True, emit_weights=False,
                  prefetch_h=True, name="ffn2_prompt")

    return (y_p.reshape(batch, seq, D_MODEL), y_s.reshape(dec_batch, steps, D_MODEL),
            sa_p[None], sb_p[None], sa_s[None], sb_s[None])
```
